```python
import jax, jax.numpy as jnp
from jax import lax
import numpy as np

D_MODEL = 1024
BATCH = 16
SEQ = 256
DEPTH = 1
DEC_BATCH = 4
DEC_SEQ = 2048
PAST_LEN = 256

GRID_W = 64
LRU_HEADS = 8
LRU_WIDTH = D_MODEL // 2
LRU_HEAD_DIM = LRU_WIDTH // LRU_HEADS
LRU_CONV_W = 4
LRU_CONV_LEFT = 2
LRU_C = 8.0
SGU_GROUPS = 4
SGU_WIDTH = D_MODEL // 2
SGU_GROUP_DIM = SGU_WIDTH // SGU_GROUPS
CHUNK = 128
D_IN = 2 * LRU_WIDTH + 2 * SGU_WIDTH
D_MIX = LRU_WIDTH + SGU_WIDTH
D_FF = 3 * D_MODEL
FFN_CONV_W = 3
FFN_CONV_LEFT = 1
N_MOD = 6
EPS = 1e-6

kernel_name = "hybrid_rglru_sgu_prefix_diffusion_step"


def rms_norm(x, gain):
    xf = x.astype(jnp.float32)
    y = xf * lax.rsqrt(jnp.mean(xf * xf, axis=-1, keepdims=True) + EPS)
    return (y * gain.astype(jnp.float32)).astype(x.dtype)


def dw_conv(x, w, b, left):
    k_w = w.shape[0]
    n = x.shape[-2]
    pad = [(0, 0)] * (x.ndim - 2) + [(left, k_w - 1 - left), (0, 0)]
    xp = jnp.pad(x, pad)
    out = b + xp[..., 0:n, :] * w[0]
    for k in range(1, k_w):
        out = out + xp[..., k:k + n, :] * w[k]
    return out


def _lin_combine(earlier, later):
    a1, b1 = earlier
    a2, b2 = later
    return a1 * a2, a2 * b1 + b2


def rglru_direction(xc, h0, w_a, b_a, w_x, b_x, lam, reverse):
    bsz, n, _ = xc.shape
    xh = xc.reshape(bsz, n, LRU_HEADS, LRU_HEAD_DIM)
    r = jax.nn.sigmoid(jnp.einsum("bshi,hij->bshj", xh, w_a).reshape(bsz, n, LRU_WIDTH) + b_a)
    i = jax.nn.sigmoid(jnp.einsum("bshi,hij->bshj", xh, w_x).reshape(bsz, n, LRU_WIDTH) + b_x)
    log_a = -LRU_C * r * jax.nn.softplus(-lam)
    a = jnp.exp(log_a)
    b = jnp.sqrt(-jnp.expm1(2.0 * log_a)) * (i * xc)
    edge = n - 1 if reverse else 0
    b = b.at[:, edge].add(a[:, edge] * h0.astype(b.dtype))
    _, h = lax.associative_scan(_lin_combine, (a, b), axis=1, reverse=reverse)
    return h


def mixer(h, h0_fwd, h0_bwd, p):
    z = h @ p["w_in"]
    xr, yr, u, v = jnp.split(z, [LRU_WIDTH, 2 * LRU_WIDTH, 2 * LRU_WIDTH + SGU_WIDTH], axis=-1)
    xc = dw_conv(xr, p["lru_conv_w"], p["lru_conv_b"], LRU_CONV_LEFT)
    hf = rglru_direction(xc, h0_fwd, p["lru_wa"][0], p["lru_ba"][0], p["lru_wx"][0],
                         p["lru_bx"][0], p["lru_lam"][0], False)
    hb = rglru_direction(xc, h0_bwd, p["lru_wa"][1], p["lru_ba"][1], p["lru_wx"][1],
                         p["lru_bx"][1], p["lru_lam"][1], True)
    o_lru = jax.nn.gelu(yr) * (hf + hb)
    bsz, n, _ = v.shape
    u = jax.nn.gelu(u)
    vc = jax.nn.gelu(v).reshape(bsz, n // CHUNK, CHUNK, SGU_GROUPS, SGU_GROUP_DIM)
    s = jnp.einsum("gpq,bnqgc->bnpgc", p["sgu_ws"], vc) + p["sgu_bs"].T[:, :, None]
    o_sgu = u * s.reshape(bsz, n, SGU_WIDTH)
    o = jnp.concatenate([rms_norm(o_lru, p["g_lru"]), rms_norm(o_sgu, p["g_sgu"])], axis=-1)
    return o @ p["w_out"], hf[:, -1], hb[:, 0]


def conv_ffn(h, p, grid):
    z = h @ p["ffn_up"]
    bsz, n, f2 = z.shape
    if grid:
        rows = n // GRID_W
        z = dw_conv(z.reshape(bsz, rows, GRID_W, f2), p["ffn_conv_w"], p["ffn_conv_b"],
                    FFN_CONV_LEFT).reshape(bsz, n, f2)
    else:
        z = dw_conv(z, p["ffn_conv_w"], p["ffn_conv_b"], FFN_CONV_LEFT)
    g, val = jnp.split(z, 2, axis=-1)
    return (jax.nn.silu(g) * val) @ p["ffn_down"]


def layer(x, mod, h0_fwd, h0_bwd, p, grid):
    sh1, sc1, g1, sh2, sc2, g2 = jnp.split(mod, N_MOD, axis=-1)
    h = rms_norm(x, p["norm1"]) * (1 + sc1) + sh1
    o, hf, hb = mixer(h, h0_fwd, h0_bwd, p)
    x = x + g1 * o
    h = rms_norm(x, p["norm2"]) * (1 + sc2) + sh2
    x = x + g2 * conv_ffn(h, p, grid)
    return x, hf, hb


def setup_inputs(seed: int = 0) -> dict:
    key = jax.random.key(seed)
    ks = jax.random.split(key, 32)
    nrm = jax.random.normal
    u = jax.random.uniform(ks[0], (DEPTH, 2, LRU_WIDTH), minval=0.9, maxval=0.999)
    s = u ** (1.0 / LRU_C)
    lam = jnp.log(s) - jnp.log1p(-s)
    return {
        "x_prompt": nrm(ks[1], (BATCH, SEQ, D_MODEL), jnp.float32),
        "x_sample": nrm(ks[2], (DEC_BATCH, DEC_SEQ, D_MODEL), jnp.float32),
        "state_lru": 0.5 * nrm(ks[3], (DEC_BATCH, DEPTH, 2, LRU_WIDTH), jnp.float32),
        "c": nrm(ks[4], (DEC_BATCH, D_MODEL), jnp.float32),
        "c_ctx": nrm(ks[5], (D_MODEL,), jnp.float32),
        "norm1": 1.0 + 0.05 * nrm(ks[6], (DEPTH, D_MODEL), jnp.float32),
        "norm2": 1.0 + 0.05 * nrm(ks[7], (DEPTH, D_MODEL), jnp.float32),
        "w_ada": 0.3 * D_MODEL ** -0.5 * nrm(ks[8], (DEPTH, D_MODEL, N_MOD * D_MODEL), jnp.float32),
        "b_ada": 0.01 * nrm(ks[9], (DEPTH, N_MOD * D_MODEL), jnp.float32),
        "w_in": D_MODEL ** -0.5 * nrm(ks[10], (DEPTH, D_MODEL, D_IN), jnp.float32),
        "lru_conv_w": LRU_CONV_W ** -0.5 * nrm(ks[11], (DEPTH, LRU_CONV_W, LRU_WIDTH), jnp.float32),
        "lru_conv_b": 0.01 * nrm(ks[12], (DEPTH, LRU_WIDTH), jnp.float32),
        "lru_wa": LRU_HEAD_DIM ** -0.5 * nrm(ks[13], (DEPTH, 2, LRU_HEADS, LRU_HEAD_DIM, LRU_HEAD_DIM), jnp.float32),
        "lru_ba": 0.01 * nrm(ks[14], (DEPTH, 2, LRU_WIDTH), jnp.float32),
        "lru_wx": LRU_HEAD_DIM ** -0.5 * nrm(ks[15], (DEPTH, 2, LRU_HEADS, LRU_HEAD_DIM, LRU_HEAD_DIM), jnp.float32),
        "lru_bx": 0.01 * nrm(ks[16], (DEPTH, 2, LRU_WIDTH), jnp.float32),
        "lru_lam": lam.astype(jnp.float32),
        "sgu_ws": CHUNK ** -0.5 * nrm(ks[17], (DEPTH, SGU_GROUPS, CHUNK, CHUNK), jnp.float32),
        "sgu_bs": 1.0 + 0.1 * nrm(ks[18], (DEPTH, SGU_GROUPS, CHUNK), jnp.float32),
        "g_lru": 1.0 + 0.05 * nrm(ks[19], (DEPTH, LRU_WIDTH), jnp.float32),
        "g_sgu": 1.0 + 0.05 * nrm(ks[20], (DEPTH, SGU_WIDTH), jnp.float32),
        "w_out": D_MIX ** -0.5 * nrm(ks[21], (DEPTH, D_MIX, D_MODEL), jnp.float32),
        "ffn_up": D_MODEL ** -0.5 * nrm(ks[22], (DEPTH, D_MODEL, 2 * D_FF), jnp.float32),
        "ffn_conv_w": FFN_CONV_W ** -0.5 * nrm(ks[23], (DEPTH, FFN_CONV_W, 2 * D_FF), jnp.float32),
        "ffn_conv_b": 0.01 * nrm(ks[24], (DEPTH, 2 * D_FF), jnp.float32),
        "ffn_down": D_FF ** -0.5 * nrm(ks[25], (DEPTH, D_FF, D_MODEL), jnp.float32),
        "final_norm": 1.0 + 0.05 * nrm(ks[26], (D_MODEL,), jnp.float32),
    }


def reference(x_prompt, x_sample, state_lru, c, c_ctx, norm1, norm2, w_ada, b_ada, w_in,
              lru_conv_w, lru_conv_b, lru_wa, lru_ba, lru_wx, lru_bx, lru_lam, sgu_ws, sgu_bs,
              g_lru, g_sgu, w_out, ffn_up, ffn_conv_w, ffn_conv_b, ffn_down, final_norm):
    xp = x_prompt
    xs = x_sample
    zeros = jnp.zeros((x_prompt.shape[0], LRU_WIDTH), x_prompt.dtype)
    new_states = []
    for l in range(DEPTH):
        p = {
            "norm1": norm1[l], "norm2": norm2[l], "w_in": w_in[l],
            "lru_conv_w": lru_conv_w[l], "lru_conv_b": lru_conv_b[l],
            "lru_wa": lru_wa[l], "lru_ba": lru_ba[l], "lru_wx": lru_wx[l], "lru_bx": lru_bx[l],
            "lru_lam": lru_lam[l], "sgu_ws": sgu_ws[l], "sgu_bs": sgu_bs[l],
            "g_lru": g_lru[l], "g_sgu": g_sgu[l], "w_out": w_out[l],
            "ffn_up": ffn_up[l], "ffn_conv_w": ffn_conv_w[l], "ffn_conv_b": ffn_conv_b[l],
            "ffn_down": ffn_down[l],
        }
        mod_ctx = (jax.nn.silu(c_ctx) @ w_ada[l] + b_ada[l])[None, None, :]
        mod_lat = (jax.nn.silu(c) @ w_ada[l] + b_ada[l])[:, None, :]
        xp, hf, hb = layer(xp, mod_ctx, zeros, zeros, p, False)
        new_states.append(jnp.stack([hf, hb], axis=1))
        xs, _, _ = layer(xs, mod_lat, state_lru[:, l, 0], state_lru[:, l, 1], p, True)
    y_prompt = rms_norm(xp, final_norm)
    y_sample = rms_norm(xs, final_norm)
    new_state_lru = jnp.stack(new_states, axis=1)
    return (y_prompt, y_sample, new_state_lru)
```

```python
import functools

import jax
import jax.numpy as jnp
from jax import lax
from jax.experimental import pallas as pl
from jax.experimental.pallas import tpu as pltpu

D_MODEL = 1024
LRU_HEADS = 8
LRU_WIDTH = 512
LRU_HEAD_DIM = 64
LRU_C = 8.0
SGU_GROUPS = 4
SGU_WIDTH = 512
CHUNK = 128
D_FF = 3072
N_MOD = 6
EPS = 1e-6
GRID_W = 64

LANES = 128
SUBLANES = 8
MOD_ROWS = 8
BF16 = jnp.bfloat16
F32 = jnp.float32


def _rms(x, gain):
    return x * lax.rsqrt(jnp.mean(x * x, axis=-1, keepdims=True) + EPS) * gain


def _mod_row(mod_ref, k, row):
    return mod_ref[k, pl.ds(row, 1), :]


def _mod_kernel(c_ref, w_ref, b_ref, o_ref):
    s = jax.nn.silu(c_ref[...]).astype(BF16)
    o_ref[...] = jnp.dot(s, w_ref[...].astype(BF16), preferred_element_type=F32) + b_ref[...]


def _modulation(cc, w_ada, b_ada):
    return pl.pallas_call(
        _mod_kernel,
        grid=(N_MOD,),
        in_specs=[
            pl.BlockSpec((MOD_ROWS, D_MODEL), lambda j: (0, 0)),
            pl.BlockSpec((D_MODEL, D_MODEL), lambda j: (0, j)),
            pl.BlockSpec((1, D_MODEL), lambda j: (0, j)),
        ],
        out_specs=pl.BlockSpec((None, MOD_ROWS, D_MODEL), lambda j: (j, 0, 0)),
        out_shape=jax.ShapeDtypeStruct((N_MOD, MOD_ROWS, D_MODEL), F32),
        compiler_params=pltpu.CompilerParams(dimension_semantics=("arbitrary",)),
        name="modulation",
    )(cc, w_ada, b_ada)


def _inproj_kernel(x_ref, mod_ref, n1_ref, win_ref, ws_ref, bs_ref, xy_ref, osgu_ref, *,
                   tm, row0, rows_per_mod):
    row = row0 + (pl.program_id(0) * tm) // rows_per_mod
    sh1 = _mod_row(mod_ref, 0, row)
    sc1 = _mod_row(mod_ref, 1, row)
    h = _rms(x_ref[...], n1_ref[...]) * (1.0 + sc1) + sh1
    hb = h.astype(BF16)
    xy_ref[...] = jnp.dot(hb, win_ref[:, 0:2 * LRU_WIDTH], preferred_element_type=F32)
    u0 = 2 * LRU_WIDTH
    v0 = u0 + SGU_WIDTH
    for g in range(SGU_GROUPS):
        lo = g * LANES
        u = jnp.dot(hb, win_ref[:, u0 + lo:u0 + lo + LANES], preferred_element_type=F32)
        v = jnp.dot(hb, win_ref[:, v0 + lo:v0 + lo + LANES], preferred_element_type=F32)
        gv = jax.nn.gelu(v).astype(BF16)
        bias = jnp.broadcast_to(bs_ref[:, g:g + 1], (CHUNK, LANES))
        for ck in range(tm // CHUNK):
            r0 = ck * CHUNK
            s = jnp.dot(ws_ref[g], gv[r0:r0 + CHUNK], preferred_element_type=F32) + bias
            osgu_ref[r0:r0 + CHUNK, lo:lo + LANES] = jax.nn.gelu(u[r0:r0 + CHUNK]) * s


def _inproj(x, mod, norm1, w_in, ws, bs_t, *, tm, row0, rows_per_mod):
    m = x.shape[0]
    const = lambda i: (0, 0)
    return pl.pallas_call(
        functools.partial(_inproj_kernel, tm=tm, row0=row0, rows_per_mod=rows_per_mod),
        grid=(m // tm,),
        in_specs=[
            pl.BlockSpec((tm, D_MODEL), lambda i: (i, 0)),
            pl.BlockSpec((N_MOD, MOD_ROWS, D_MODEL), lambda i: (0, 0, 0)),
            pl.BlockSpec((1, D_MODEL), const),
            pl.BlockSpec((D_MODEL, 4 * LRU_WIDTH), const),
            pl.BlockSpec((SGU_GROUPS, CHUNK, CHUNK), lambda i: (0, 0, 0)),
            pl.BlockSpec((CHUNK, SGU_GROUPS), const),
        ],
        out_specs=[
            pl.BlockSpec((tm, 2 * LRU_WIDTH), lambda i: (i, 0)),
            pl.BlockSpec((tm, SGU_WIDTH), lambda i: (i, 0)),
        ],
        out_shape=[
            jax.ShapeDtypeStruct((m, 2 * LRU_WIDTH), F32),
            jax.ShapeDtypeStruct((m, SGU_WIDTH), F32),
        ],
        compiler_params=pltpu.CompilerParams(
            dimension_semantics=("arbitrary",), vmem_limit_bytes=40 * 1024 * 1024),
        name="inproj_sgu",
    )(x, mod, norm1, w_in, ws, bs_t)


def _scan_pitch(seq):
    pitch = seq // SUBLANES
    return pitch + (4 - pitch % 8) % 8


def _lru_kernel(xr_ref, yr_ref, cw_ref, cb_ref, wg_ref, bg_ref, lam_ref, h0_ref,
                o_ref, st_ref, a_sc, b_sc, hs_sc, *, seq, pitch, rc):
    sp_rows = SUBLANES * pitch
    n_rc = seq // rc
    sp = jax.nn.softplus(-lam_ref[...])
    cw = cw_ref[...]
    cb = cb_ref[...]
    bg = bg_ref[...]

    for d in range(2):
        a_sc[d, seq:sp_rows, :] = jnp.ones((sp_rows - seq, LANES), F32)
        b_sc[d, seq:sp_rows, :] = jnp.zeros((sp_rows - seq, LANES), F32)

    def gates(ci, _):
        base = pl.multiple_of(ci * rc, rc)
        xm = xr_ref[pl.ds(base, rc), :]
        prev = xr_ref[pl.ds(pl.multiple_of(jnp.maximum(base - SUBLANES, 0), SUBLANES), SUBLANES), :]
        nxt = xr_ref[pl.ds(pl.multiple_of(jnp.minimum(base + rc, seq - SUBLANES), SUBLANES), SUBLANES), :]
        prev = jnp.where(ci == 0, 0.0, prev)
        nxt = jnp.where(ci == n_rc - 1, 0.0, nxt)
        ext = jnp.concatenate([prev, xm, nxt], axis=0)
        n_ext = rc + 2 * SUBLANES
        xc = cb + cw[2:3] * xm
        xc = xc + cw[0:1] * pltpu.roll(ext, 2, axis=0)[SUBLANES:SUBLANES + rc]
        xc = xc + cw[1:2] * pltpu.roll(ext, 1, axis=0)[SUBLANES:SUBLANES + rc]
        xc = xc + cw[3:4] * pltpu.roll(ext, n_ext - 1, axis=0)[SUBLANES:SUBLANES + rc]
        g = jnp.dot(xc.astype(BF16), wg_ref[...], preferred_element_type=F32) + bg
        for d in range(2):
            r = jax.nn.sigmoid(g[:, (2 * d) * LANES:(2 * d + 1) * LANES])
            i = jax.nn.sigmoid(g[:, (2 * d + 1) * LANES:(2 * d + 2) * LANES])
            log_a = (-LRU_C) * r * sp[d:d + 1]
            a = jnp.exp(log_a)
            one_minus_a2 = -jnp.tanh(log_a) * (1.0 + a * a)
            a_sc[d, pl.ds(base, rc), :] = a
            b_sc[d, pl.ds(base, rc), :] = jnp.sqrt(one_minus_a2) * (i * xc)
        return 0

    lax.fori_loop(0, n_rc, gates, 0)

    def rows(t):
        return pl.ds(t, SUBLANES, stride=pitch)

    def local(t, carry):
        hf, pf, hb, pb = carry
        af = a_sc[0, rows(t), :]
        hf = af * hf + b_sc[0, rows(t), :]
        pf = af * pf
        b_sc[0, rows(t), :] = hf
        a_sc[0, rows(t), :] = pf
        tb = pitch - 1 - t
        ab = a_sc[1, rows(tb), :]
        hb = ab * hb + b_sc[1, rows(tb), :]
        pb = ab * pb
        b_sc[1, rows(tb), :] = hb
        a_sc[1, rows(tb), :] = pb
        return hf, pf, hb, pb

    zero = jnp.zeros((SUBLANES, LANES), F32)
    one = jnp.ones((SUBLANES, LANES), F32)
    hf, pf, hb, pb = lax.fori_loop(0, pitch, local, (zero, one, zero, one))

    h0 = h0_ref[...]
    c = h0[0:1]
    cf = []
    for j in range(SUBLANES):
        cf.append(c)
        c = hf[j:j + 1] + pf[j:j + 1] * c
    st_f = c
    c = h0[1:2]
    cbk = [None] * SUBLANES
    for j in reversed(range(SUBLANES)):
        cbk[j] = c
        c = hb[j:j + 1] + pb[j:j + 1] * c
    st_ref[...] = jnp.concatenate([st_f, c], axis=0)
    cf = jnp.concatenate(cf, axis=0)
    cbk = jnp.concatenate(cbk, axis=0)

    def fix(t, _):
        hs_sc[rows(t), :] = (b_sc[0, rows(t), :] + a_sc[0, rows(t), :] * cf
                             + b_sc[1, rows(t), :] + a_sc[1, rows(t), :] * cbk)
        return 0

    lax.fori_loop(0, pitch, fix, 0)

    def gate_out(ci, _):
        base = pl.multiple_of(ci * rc, rc)
        o_ref[pl.ds(base, rc), :] = jax.nn.gelu(yr_ref[pl.ds(base, rc), :]) * hs_sc[pl.ds(base, rc), :]
        return 0

    lax.fori_loop(0, n_rc, gate_out, 0)


def _lru(xy, conv_w, conv_b, wg, bg, lam, h0, *, seq):
    m = xy.shape[0]
    nseq = m // seq
    nlb = LRU_WIDTH // LANES
    pitch = _scan_pitch(seq)
    sp_rows = SUBLANES * pitch
    rc = min(seq, 256)
    return pl.pallas_call(
        functools.partial(_lru_kernel, seq=seq, pitch=pitch, rc=rc),
        grid=(nseq, nlb),
        in_specs=[
            pl.BlockSpec((seq, LANES), lambda b, j: (b, j)),
            pl.BlockSpec((seq, LANES), lambda b, j: (b, nlb + j)),
            pl.BlockSpec((4, LANES), lambda b, j: (0, j)),
            pl.BlockSpec((1, LANES), lambda b, j: (0, j)),
            pl.BlockSpec((None, LANES, 4 * LANES), lambda b, j: (j, 0, 0)),
            pl.BlockSpec((None, 1, 4 * LANES), lambda b, j: (j, 0, 0)),
            pl.BlockSpec((2, LANES), lambda b, j: (0, j)),
            pl.BlockSpec((None, 2, LANES), lambda b, j: (b, 0, j)),
        ],
        out_specs=[
            pl.BlockSpec((seq, LANES), lambda b, j: (b, j)),
            pl.BlockSpec((None, 2, LANES), lambda b, j: (b, 0, j)),
        ],
        out_shape=[
            jax.ShapeDtypeStruct((m, LRU_WIDTH), F32),
            jax.ShapeDtypeStruct((nseq, 2, LRU_WIDTH), F32),
        ],
        scratch_shapes=[
            pltpu.VMEM((2, sp_rows, LANES), F32),
            pltpu.VMEM((2, sp_rows, LANES), F32),
            pltpu.VMEM((sp_rows, LANES), F32),
        ],
        compiler_params=pltpu.CompilerParams(dimension_semantics=("arbitrary", "arbitrary")),
        name="rglru",
    )(xy, xy, conv_w, conv_b, wg, bg, lam, h0)


def _ffn_kernel(x_ref, ol_ref, os_ref, mod_ref, gl_ref, gs_ref, wout_ref, n2_ref, up_ref,
                cw_ref, cb_ref, down_ref, fn_ref, y_ref, *, tm, fc, period, row0, rows_per_mod):
    row = row0 + (pl.program_id(0) * tm) // rows_per_mod
    g1 = _mod_row(mod_ref, 2, row)
    sh2 = _mod_row(mod_ref, 3, row)
    sc2 = _mod_row(mod_ref, 4, row)
    g2 = _mod_row(mod_ref, 5, row)
    nl = _rms(ol_ref[...], gl_ref[...]).astype(BF16)
    ns = _rms(os_ref[...], gs_ref[...]).astype(BF16)
    o = (jnp.dot(nl, wout_ref[0:LRU_WIDTH, :], preferred_element_type=F32)
         + jnp.dot(ns, wout_ref[LRU_WIDTH:LRU_WIDTH + SGU_WIDTH, :], preferred_element_type=F32))
    x1 = x_ref[...] + g1 * o
    h2 = (_rms(x1, n2_ref[...]) * (1.0 + sc2) + sh2).astype(BF16)

    pos = lax.broadcasted_iota(jnp.int32, (tm, fc), 0) % period
    first = pos == 0
    last = pos == period - 1

    def conv(z, c0):
        w = cw_ref[:, c0:c0 + fc]
        zm = jnp.where(first, 0.0, pltpu.roll(z, 1, axis=0))
        zp = jnp.where(last, 0.0, pltpu.roll(z, tm - 1, axis=0))
        return cb_ref[:, c0:c0 + fc] + w[0:1] * zm + w[1:2] * z + w[2:3] * zp

    acc = jnp.zeros((tm, D_MODEL), F32)
    for f0 in range(0, D_FF, fc):
        g = conv(jnp.dot(h2, up_ref[:, f0:f0 + fc], preferred_element_type=F32), f0)
        v = conv(jnp.dot(h2, up_ref[:, D_FF + f0:D_FF + f0 + fc], preferred_element_type=F32), D_FF + f0)
        act = (jax.nn.silu(g) * v).astype(BF16)
        acc = acc + jnp.dot(act, down_ref[f0:f0 + fc, :], preferred_element_type=F32)
    x2 = x1 + g2 * acc
    y_ref[...] = _rms(x2, fn_ref[...])


def _ffn(x, olru, osgu, mod, g_lru, g_sgu, w_out, norm2, ffn_up, conv_w, conv_b, ffn_down,
         final_norm, *, tm, period, row0, rows_per_mod):
    m = x.shape[0]
    const = lambda i: (0, 0)
    resident = lambda shape: pl.BlockSpec(shape, const, pipeline_mode=pl.Buffered(1))
    return pl.pallas_call(
        functools.partial(_ffn_kernel, tm=tm, fc=512, period=period, row0=row0,
                          rows_per_mod=rows_per_mod),
        grid=(m // tm,),
        in_specs=[
            pl.BlockSpec((tm, D_MODEL), lambda i: (i, 0)),
            pl.BlockSpec((tm, LRU_WIDTH), lambda i: (i, 0)),
            pl.BlockSpec((tm, SGU_WIDTH), lambda i: (i, 0)),
            pl.BlockSpec((N_MOD, MOD_ROWS, D_MODEL), lambda i: (0, 0, 0)),
            pl.BlockSpec((1, LRU_WIDTH), const),
            pl.BlockSpec((1, SGU_WIDTH), const),
            resident((LRU_WIDTH + SGU_WIDTH, D_MODEL)),
            pl.BlockSpec((1, D_MODEL), const),
            resident((D_MODEL, 2 * D_FF)),
            pl.BlockSpec((3, 2 * D_FF), const),
            pl.BlockSpec((1, 2 * D_FF), const),
            resident((D_FF, D_MODEL)),
            pl.BlockSpec((1, D_MODEL), const),
        ],
        out_specs=pl.BlockSpec((tm, D_MODEL), lambda i: (i, 0)),
        out_shape=jax.ShapeDtypeStruct((m, D_MODEL), F32),
        compiler_params=pltpu.CompilerParams(
            dimension_semantics=("arbitrary",), vmem_limit_bytes=48 * 1024 * 1024),
        name="outproj_ffn",
    )(x, olru, osgu, mod, g_lru, g_sgu, w_out, norm2, ffn_up, conv_w, conv_b, ffn_down, final_norm)


def _block_diag_pairs(w):
    nd = w.shape[0]
    wp = w.reshape(nd, LRU_HEADS // 2, 2, LRU_HEAD_DIM, LRU_HEAD_DIM)
    z = jnp.zeros_like(wp[:, :, 0])
    top = jnp.concatenate([wp[:, :, 0], z], axis=-1)
    bot = jnp.concatenate([z, wp[:, :, 1]], axis=-1)
    return jnp.concatenate([top, bot], axis=-2)


def kernel(x_prompt, x_sample, state_lru, c, c_ctx, norm1, norm2, w_ada, b_ada, w_in, lru_conv_w,
           lru_conv_b, lru_wa, lru_ba, lru_wx, lru_bx, lru_lam, sgu_ws, sgu_bs, g_lru, g_sgu, w_out,
           ffn_up, ffn_conv_w, ffn_conv_b, ffn_down, final_norm):
    batch, seq, _ = x_prompt.shape
    dec_batch, dec_seq, _ = x_sample.shape
    depth = norm1.shape[0]
    assert depth == 1, "the final norm is fused into the (single) layer's last kernel"
    nlb = LRU_WIDTH // LANES

    groups = [
        dict(x=x_prompt.reshape(batch * seq, D_MODEL), seq=seq, period=seq, row0=0,
             rows_per_mod=batch * seq),
        dict(x=x_sample.reshape(dec_batch * dec_seq, D_MODEL), seq=dec_seq, period=GRID_W, row0=1,
             rows_per_mod=dec_seq),
    ]
    cc = jnp.concatenate(
        [c_ctx[None, :], c, jnp.zeros((MOD_ROWS - 1 - dec_batch, D_MODEL), F32)], axis=0)
    zeros_state = jnp.zeros((batch, 2, LRU_WIDTH), x_prompt.dtype)

    new_states = []
    for l in range(depth):
        mod = _modulation(cc, w_ada[l], b_ada[l][None, :])
        w_in_b = w_in[l].astype(BF16)
        ws_b = sgu_ws[l].astype(BF16)
        bs_t = sgu_bs[l].T
        bd_a = _block_diag_pairs(lru_wa[l])
        bd_x = _block_diag_pairs(lru_wx[l])
        wg = jnp.concatenate([bd_a[0], bd_x[0], bd_a[1], bd_x[1]], axis=-1).astype(BF16)
        ba = lru_ba[l].reshape(2, nlb, 1, LANES)
        bx = lru_bx[l].reshape(2, nlb, 1, LANES)
        bg = jnp.concatenate([ba[0], bx[0], ba[1], bx[1]], axis=-1)
        w_out_b = w_out[l].astype(BF16)
        up_b = ffn_up[l].astype(BF16)
        down_b = ffn_down[l].astype(BF16)
        h0s = [zeros_state, state_lru[:, l]]
        for gi, grp in enumerate(groups):
            xy, osgu = _inproj(grp["x"], mod, norm1[l][None, :], w_in_b, ws_b, bs_t, tm=256,
                               row0=grp["row0"], rows_per_mod=grp["rows_per_mod"])
            olru, st = _lru(xy, lru_conv_w[l], lru_conv_b[l][None, :], wg, bg, lru_lam[l], h0s[gi],
                            seq=grp["seq"])
            grp["x"] = _ffn(grp["x"], olru, osgu, mod, g_lru[l][None, :], g_sgu[l][None, :], w_out_b,
                            norm2[l][None, :], up_b, ffn_conv_w[l], ffn_conv_b[l][None, :], down_b,
                            final_norm[None, :], tm=256, period=grp["period"], row0=grp["row0"],
                            rows_per_mod=grp["rows_per_mod"])
            if gi == 0:
                new_states.append(st)
    y_prompt = groups[0]["x"].reshape(batch, seq, D_MODEL)
    y_sample = groups[1]["x"].reshape(dec_batch, dec_seq, D_MODEL)
    return (y_prompt, y_sample, jnp.stack(new_states, axis=1))
```

```python
import functools
import math

import jax
import jax.numpy as jnp
from jax import lax
from jax.experimental import pallas as pl
from jax.experimental.pallas import tpu as pltpu

D_MODEL = 1024
LRU_HEADS = 8
LRU_WIDTH = 512
LRU_HEAD_DIM = 64
LRU_C = 8.0
SGU_GROUPS = 4
SGU_WIDTH = 512
CHUNK = 128
D_FF = 3072
N_MOD = 6
EPS = 1e-6
GRID_W = 64

LANES = 128
SUBLANES = 8
MOD_ROWS = 8
BF16 = jnp.bfloat16
F32 = jnp.float32

_GELU_C0 = math.sqrt(2.0 / math.pi)
_GELU_C1 = _GELU_C0 * 0.044715


def _rms(x, gain):
    return x * lax.rsqrt(jnp.mean(x * x, axis=-1, keepdims=True) + EPS) * gain


def _gelu(x):
    return (0.5 * x) * (1.0 + jnp.tanh(x * (_GELU_C0 + _GELU_C1 * (x * x))))


def _mod_row(mod_ref, k, row):
    return mod_ref[k, pl.ds(row, 1), :]


def _mod_kernel(c_ref, w_ref, b_ref, o_ref):
    s = jax.nn.silu(c_ref[...]).astype(BF16)
    o_ref[...] = jnp.dot(s, w_ref[...].astype(BF16), preferred_element_type=F32) + b_ref[...]


def _modulation(cc, w_ada, b_ada):
    return pl.pallas_call(
        _mod_kernel,
        grid=(N_MOD,),
        in_specs=[
            pl.BlockSpec((MOD_ROWS, D_MODEL), lambda j: (0, 0)),
            pl.BlockSpec((D_MODEL, D_MODEL), lambda j: (0, j)),
            pl.BlockSpec((1, D_MODEL), lambda j: (0, j)),
        ],
        out_specs=pl.BlockSpec((None, MOD_ROWS, D_MODEL), lambda j: (j, 0, 0)),
        out_shape=jax.ShapeDtypeStruct((N_MOD, MOD_ROWS, D_MODEL), F32),
        compiler_params=pltpu.CompilerParams(dimension_semantics=("arbitrary",)),
        name="modulation",
    )(cc, w_ada, b_ada)


def _inproj_kernel(x_ref, mod_ref, n1_ref, win_ref, ws_ref, bs_ref, xy_ref, osgu_ref, *,
                   tm, row0, rows_per_mod):
    row = row0 + (pl.program_id(0) * tm) // rows_per_mod
    sh1 = _mod_row(mod_ref, 0, row)
    sc1 = _mod_row(mod_ref, 1, row)
    hb = (_rms(x_ref[...], n1_ref[...] * (1.0 + sc1)) + sh1).astype(BF16)
    xy_ref[...] = jnp.dot(hb, win_ref[:, 0:2 * LRU_WIDTH], preferred_element_type=F32)
    uv = jnp.dot(hb, win_ref[:, 2 * LRU_WIDTH:], preferred_element_type=F32)
    for g in range(SGU_GROUPS):
        lo = g * LANES
        gu = _gelu(uv[:, lo:lo + LANES])
        gv = _gelu(uv[:, SGU_WIDTH + lo:SGU_WIDTH + lo + LANES]).astype(BF16)
        bias = jnp.broadcast_to(bs_ref[:, g:g + 1], (CHUNK, LANES))
        for ck in range(tm // CHUNK):
            r0 = ck * CHUNK
            s = jnp.dot(ws_ref[g], gv[r0:r0 + CHUNK], preferred_element_type=F32) + bias
            osgu_ref[r0:r0 + CHUNK, lo:lo + LANES] = gu[r0:r0 + CHUNK] * s


def _inproj(x, mod, norm1, w_in, ws, bs_t, *, tm, row0, rows_per_mod):
    m = x.shape[0]
    const = lambda i: (0, 0)
    return pl.pallas_call(
        functools.partial(_inproj_kernel, tm=tm, row0=row0, rows_per_mod=rows_per_mod),
        grid=(m // tm,),
        in_specs=[
            pl.BlockSpec((tm, D_MODEL), lambda i: (i, 0)),
            pl.BlockSpec((N_MOD, MOD_ROWS, D_MODEL), lambda i: (0, 0, 0)),
            pl.BlockSpec((1, D_MODEL), const),
            pl.BlockSpec((D_MODEL, 4 * LRU_WIDTH), const),
            pl.BlockSpec((SGU_GROUPS, CHUNK, CHUNK), lambda i: (0, 0, 0)),
            pl.BlockSpec((CHUNK, SGU_GROUPS), const),
        ],
        out_specs=[
            pl.BlockSpec((tm, 2 * LRU_WIDTH), lambda i: (i, 0)),
            pl.BlockSpec((tm, SGU_WIDTH), lambda i: (i, 0)),
        ],
        out_shape=[
            jax.ShapeDtypeStruct((m, 2 * LRU_WIDTH), F32),
            jax.ShapeDtypeStruct((m, SGU_WIDTH), F32),
        ],
        compiler_params=pltpu.CompilerParams(
            dimension_semantics=("arbitrary",), vmem_limit_bytes=48 * 1024 * 1024),
        name="inproj_sgu",
    )(x, mod, norm1, w_in, ws, bs_t)


def _scan_pitch(seq):
    pitch = seq // SUBLANES
    return pitch + (4 - pitch % 8) % 8


def _lru_kernel(xr_ref, yr_ref, cw_ref, cb_ref, wg_ref, bg_ref, lam_ref, h0_ref,
                o_ref, st_ref, a_sc, b_sc, p_sc, h_sc, *, seq, nsb, pitch, rc, unroll):
    sp_rows = SUBLANES * pitch
    n_rc = (seq * nsb) // rc
    seg = min(seq, rc)
    segs = rc // seg
    chunks_per_seq = seq // seg
    c4 = (-0.5 * LRU_C) * jax.nn.softplus(-lam_ref[...])
    cw = cw_ref[...]
    cb = cb_ref[...]
    bg = bg_ref[...]

    for s in range(nsb):
        lo, hi = s * sp_rows + seq, (s + 1) * sp_rows
        for d in range(2):
            a_sc[d, lo:hi, :] = jnp.ones((hi - lo, LANES), F32)
            b_sc[d, lo:hi, :] = jnp.zeros((hi - lo, LANES), F32)

    def scratch_row(ci, k):
        if chunks_per_seq > 1:
            return pl.multiple_of((ci // chunks_per_seq) * sp_rows + (ci % chunks_per_seq) * seg,
                                  SUBLANES)
        return pl.multiple_of((ci * segs + k) * sp_rows, SUBLANES)

    def conv_segment(xs, prev, nxt):
        ext = jnp.concatenate([prev, xs, nxt], axis=0)
        n_ext = seg + 2 * SUBLANES
        xc = cb + cw[2:3] * xs
        xc = xc + cw[0:1] * pltpu.roll(ext, 2, axis=0)[SUBLANES:SUBLANES + seg]
        xc = xc + cw[1:2] * pltpu.roll(ext, 1, axis=0)[SUBLANES:SUBLANES + seg]
        return xc + cw[3:4] * pltpu.roll(ext, n_ext - 1, axis=0)[SUBLANES:SUBLANES + seg]

    def gates(ci, _):
        base = pl.multiple_of(ci * rc, rc)
        xm = xr_ref[pl.ds(base, rc), :]
        if chunks_per_seq > 1:
            cs = ci % chunks_per_seq
            p0 = pl.multiple_of(jnp.maximum(base - SUBLANES, 0), SUBLANES)
            n0 = pl.multiple_of(jnp.minimum(base + rc, seq * nsb - SUBLANES), SUBLANES)
            prev = jnp.where(cs == 0, 0.0, xr_ref[pl.ds(p0, SUBLANES), :])
            nxt = jnp.where(cs == chunks_per_seq - 1, 0.0, xr_ref[pl.ds(n0, SUBLANES), :])
            xc = conv_segment(xm, prev, nxt)
        else:
            zeros = jnp.zeros((SUBLANES, LANES), F32)
            xc = jnp.concatenate(
                [conv_segment(xm[k * seg:(k + 1) * seg], zeros, zeros) for k in range(segs)], axis=0)
        g = jnp.dot(xc.astype(BF16), wg_ref[...], preferred_element_type=F32) + bg
        hxc = 0.5 * xc
        for d in range(2):
            th_r = jnp.tanh(g[:, (2 * d) * LANES:(2 * d + 1) * LANES])
            th_i = jnp.tanh(g[:, (2 * d + 1) * LANES:(2 * d + 2) * LANES])
            log_a = c4[d:d + 1] * th_r + c4[d:d + 1]
            a = jnp.exp(log_a)
            om = jnp.tanh(log_a) * (-1.0 - a * a)
            root = jnp.where(om > 0.0, om * lax.rsqrt(om), 0.0)
            b = root * ((th_i + 1.0) * hxc)
            for k in range(segs):
                dst = scratch_row(ci, k)
                a_sc[d, pl.ds(dst, seg), :] = a[k * seg:(k + 1) * seg]
                b_sc[d, pl.ds(dst, seg), :] = b[k * seg:(k + 1) * seg]
        return 0

    lax.fori_loop(0, n_rc, gates, 0)

    def scan_sequence(s, _):
        off = s * sp_rows

        def rows(t):
            return pl.ds(off + t, SUBLANES, stride=pitch)

        def local(t, carry):
            hf, pf, hb, pb = carry
            af = a_sc[0, rows(t), :]
            hf = af * hf + b_sc[0, rows(t), :]
            pf = af * pf
            h_sc[0, rows(t), :] = hf
            p_sc[0, rows(t), :] = pf
            tb = pitch - 1 - t
            ab = a_sc[1, rows(tb), :]
            hb = ab * hb + b_sc[1, rows(tb), :]
            pb = ab * pb
            h_sc[1, rows(tb), :] = hb
            p_sc[1, rows(tb), :] = pb
            return hf, pf, hb, pb

        zero = jnp.zeros((SUBLANES, LANES), F32)
        one = jnp.ones((SUBLANES, LANES), F32)
        hf, pf, hb, pb = lax.fori_loop(0, pitch, local, (zero, one, zero, one), unroll=unroll)

        h0 = h0_ref[s]
        c = h0[0:1]
        cf = []
        for j in range(SUBLANES):
            cf.append(c)
            c = hf[j:j + 1] + pf[j:j + 1] * c
        st_f = c
        c = h0[1:2]
        cbk = [None] * SUBLANES
        for j in reversed(range(SUBLANES)):
            cbk[j] = c
            c = hb[j:j + 1] + pb[j:j + 1] * c
        st_ref[s] = jnp.concatenate([st_f, c], axis=0)
        cf = jnp.concatenate(cf, axis=0)
        cbk = jnp.concatenate(cbk, axis=0)

        def fix(t, _):
            a_sc[0, rows(t), :] = (h_sc[0, rows(t), :] + p_sc[0, rows(t), :] * cf
                                   + h_sc[1, rows(t), :] + p_sc[1, rows(t), :] * cbk)
            return 0

        lax.fori_loop(0, pitch, fix, 0, unroll=unroll)
        return 0

    if nsb == 1:
        scan_sequence(0, 0)
    else:
        lax.fori_loop(0, nsb, scan_sequence, 0)

    def gate_out(ci, _):
        base = pl.multiple_of(ci * rc, rc)
        gy = _gelu(yr_ref[pl.ds(base, rc), :])
        for k in range(segs):
            src = scratch_row(ci, k)
            o_ref[pl.ds(base + k * seg, seg), :] = gy[k * seg:(k + 1) * seg] * a_sc[0, pl.ds(src, seg), :]
        return 0

    lax.fori_loop(0, n_rc, gate_out, 0)


def _lru(xy, conv_w, conv_b, wg, bg, lam, h0, *, seq, nsb):
    m = xy.shape[0]
    nblk = m // (seq * nsb)
    nlb = LRU_WIDTH // LANES
    pitch = _scan_pitch(seq)
    sc_rows = nsb * SUBLANES * pitch
    rows = seq * nsb
    return pl.pallas_call(
        functools.partial(_lru_kernel, seq=seq, nsb=nsb, pitch=pitch, rc=512, unroll=4),
        grid=(nlb, nblk),
        in_specs=[
            pl.BlockSpec((rows, LANES), lambda j, b: (b, j)),
            pl.BlockSpec((rows, LANES), lambda j, b: (b, nlb + j)),
            pl.BlockSpec((4, LANES), lambda j, b: (0, j)),
            pl.BlockSpec((1, LANES), lambda j, b: (0, j)),
            pl.BlockSpec((None, LANES, 4 * LANES), lambda j, b: (j, 0, 0)),
            pl.BlockSpec((None, 1, 4 * LANES), lambda j, b: (j, 0, 0)),
            pl.BlockSpec((2, LANES), lambda j, b: (0, j)),
            pl.BlockSpec((nsb, 2, LANES), lambda j, b: (b, 0, j)),
        ],
        out_specs=[
            pl.BlockSpec((rows, LANES), lambda j, b: (b, j)),
            pl.BlockSpec((nsb, 2, LANES), lambda j, b: (b, 0, j)),
        ],
        out_shape=[
            jax.ShapeDtypeStruct((m, LRU_WIDTH), F32),
            jax.ShapeDtypeStruct((m // seq, 2, LRU_WIDTH), F32),
        ],
        scratch_shapes=[pltpu.VMEM((2, sc_rows, LANES), F32)] * 4,
        compiler_params=pltpu.CompilerParams(
            dimension_semantics=("arbitrary", "arbitrary"), vmem_limit_bytes=40 * 1024 * 1024),
        name="rglru",
    )(xy, xy, conv_w, conv_b, wg, bg, lam, h0)


def _ffn_kernel(x_ref, ol_ref, os_ref, mod_ref, gl_ref, gs_ref, wout_ref, n2_ref, up_ref,
                cw_ref, cb_ref, down_ref, fn_ref, y_ref, *, tm, fc, period, row0, rows_per_mod):
    row = row0 + (pl.program_id(0) * tm) // rows_per_mod
    g1 = _mod_row(mod_ref, 2, row)
    sh2 = _mod_row(mod_ref, 3, row)
    sc2 = _mod_row(mod_ref, 4, row)
    g2 = _mod_row(mod_ref, 5, row)
    nl = _rms(ol_ref[...], gl_ref[...]).astype(BF16)
    ns = _rms(os_ref[...], gs_ref[...]).astype(BF16)
    o = (jnp.dot(nl, wout_ref[0:LRU_WIDTH, :], preferred_element_type=F32)
         + jnp.dot(ns, wout_ref[LRU_WIDTH:LRU_WIDTH + SGU_WIDTH, :], preferred_element_type=F32))
    x1 = x_ref[...] + g1 * o
    h2 = (_rms(x1, n2_ref[...] * (1.0 + sc2)) + sh2).astype(BF16)

    sub = lax.broadcasted_iota(jnp.int32, (SUBLANES, fc), 0)

    def conv(z, c0):
        w = cw_ref[:, c0:c0 + fc]
        b = cb_ref[:, c0:c0 + fc]
        outs = []
        for s0 in range(0, tm, period):
            seg = z[s0:s0 + period]
            zm = pltpu.roll(seg, 1, axis=0)
            zp = pltpu.roll(seg, period - 1, axis=0)
            zm = jnp.concatenate([jnp.where(sub == 0, 0.0, zm[:SUBLANES]), zm[SUBLANES:]], axis=0)
            zp = jnp.concatenate(
                [zp[:period - SUBLANES], jnp.where(sub == SUBLANES - 1, 0.0, zp[period - SUBLANES:])],
                axis=0)
            outs.append(b + w[0:1] * zm + w[1:2] * seg + w[2:3] * zp)
        return jnp.concatenate(outs, axis=0) if len(outs) > 1 else outs[0]

    def up(f0):
        zg = jnp.dot(h2, up_ref[:, f0:f0 + fc], preferred_element_type=F32)
        zv = jnp.dot(h2, up_ref[:, D_FF + f0:D_FF + f0 + fc], preferred_element_type=F32)
        return zg, zv

    acc = jnp.zeros((tm, D_MODEL), F32)
    nxt = up(0)
    for f0 in range(0, D_FF, fc):
        zg, zv = nxt
        if f0 + fc < D_FF:
            nxt = up(f0 + fc)
        hg = 0.5 * conv(zg, f0)
        act = (hg * (jnp.tanh(hg) + 1.0) * conv(zv, D_FF + f0)).astype(BF16)
        acc = acc + jnp.dot(act, down_ref[f0:f0 + fc, :], preferred_element_type=F32)
    x2 = x1 + g2 * acc
    y_ref[...] = _rms(x2, fn_ref[...])


def _ffn(x, olru, osgu, mod, g_lru, g_sgu, w_out, norm2, ffn_up, conv_w, conv_b, ffn_down,
         final_norm, *, tm, period, row0, rows_per_mod):
    m = x.shape[0]
    const = lambda i: (0, 0)
    resident = lambda shape: pl.BlockSpec(shape, const, pipeline_mode=pl.Buffered(1))
    return pl.pallas_call(
        functools.partial(_ffn_kernel, tm=tm, fc=512, period=period, row0=row0,
                          rows_per_mod=rows_per_mod),
        grid=(m // tm,),
        in_specs=[
            pl.BlockSpec((tm, D_MODEL), lambda i: (i, 0)),
            pl.BlockSpec((tm, LRU_WIDTH), lambda i: (i, 0)),
            pl.BlockSpec((tm, SGU_WIDTH), lambda i: (i, 0)),
            pl.BlockSpec((N_MOD, MOD_ROWS, D_MODEL), lambda i: (0, 0, 0)),
            pl.BlockSpec((1, LRU_WIDTH), const),
            pl.BlockSpec((1, SGU_WIDTH), const),
            resident((LRU_WIDTH + SGU_WIDTH, D_MODEL)),
            pl.BlockSpec((1, D_MODEL), const),
            resident((D_MODEL, 2 * D_FF)),
            pl.BlockSpec((3, 2 * D_FF), const),
            pl.BlockSpec((1, 2 * D_FF), const),
            resident((D_FF, D_MODEL)),
            pl.BlockSpec((1, D_MODEL), const),
        ],
        out_specs=pl.BlockSpec((tm, D_MODEL), lambda i: (i, 0)),
        out_shape=jax.ShapeDtypeStruct((m, D_MODEL), F32),
        compiler_params=pltpu.CompilerParams(
            dimension_semantics=("arbitrary",), vmem_limit_bytes=56 * 1024 * 1024),
        name="outproj_ffn",
    )(x, olru, osgu, mod, g_lru, g_sgu, w_out, norm2, ffn_up, conv_w, conv_b, ffn_down, final_norm)


def _block_diag_pairs(w):
    nd = w.shape[0]
    wp = w.reshape(nd, LRU_HEADS // 2, 2, LRU_HEAD_DIM, LRU_HEAD_DIM)
    z = jnp.zeros_like(wp[:, :, 0])
    top = jnp.concatenate([wp[:, :, 0], z], axis=-1)
    bot = jnp.concatenate([z, wp[:, :, 1]], axis=-1)
    return jnp.concatenate([top, bot], axis=-2)


def kernel(x_prompt, x_sample, state_lru, c, c_ctx, norm1, norm2, w_ada, b_ada, w_in, lru_conv_w,
           lru_conv_b, lru_wa, lru_ba, lru_wx, lru_bx, lru_lam, sgu_ws, sgu_bs, g_lru, g_sgu, w_out,
           ffn_up, ffn_conv_w, ffn_conv_b, ffn_down, final_norm):
    batch, seq, _ = x_prompt.shape
    dec_batch, dec_seq, _ = x_sample.shape
    depth = norm1.shape[0]
    assert depth == 1, "the final norm is fused into the (single) layer's last kernel"
    nlb = LRU_WIDTH // LANES

    groups = [
        dict(x=x_prompt.reshape(batch * seq, D_MODEL), seq=seq, nsb=8, period=seq, row0=0,
             rows_per_mod=batch * seq),
        dict(x=x_sample.reshape(dec_batch * dec_seq, D_MODEL), seq=dec_seq, nsb=1, period=GRID_W,
             row0=1, rows_per_mod=dec_seq),
    ]
    cc = jnp.concatenate(
        [c_ctx[None, :], c, jnp.zeros((MOD_ROWS - 1 - dec_batch, D_MODEL), F32)], axis=0)
    zeros_state = jnp.zeros((batch, 2, LRU_WIDTH), x_prompt.dtype)

    new_states = []
    for l in range(depth):
        mod = _modulation(cc, w_ada[l], b_ada[l][None, :])
        w_in_b = w_in[l].astype(BF16)
        ws_b = sgu_ws[l].astype(BF16)
        bs_t = sgu_bs[l].T
        bd_a = _block_diag_pairs(0.5 * lru_wa[l])
        bd_x = _block_diag_pairs(0.5 * lru_wx[l])
        wg = jnp.concatenate([bd_a[0], bd_x[0], bd_a[1], bd_x[1]], axis=-1).astype(BF16)
        ba = (0.5 * lru_ba[l]).reshape(2, nlb, 1, LANES)
        bx = (0.5 * lru_bx[l]).reshape(2, nlb, 1, LANES)
        bg = jnp.concatenate([ba[0], bx[0], ba[1], bx[1]], axis=-1)
        w_out_b = w_out[l].astype(BF16)
        up_b = ffn_up[l].astype(BF16)
        down_b = ffn_down[l].astype(BF16)
        h0s = [zeros_state, state_lru[:, l]]
        for gi, grp in enumerate(groups):
            xy, osgu = _inproj(grp["x"], mod, norm1[l][None, :], w_in_b, ws_b, bs_t, tm=512,
                               row0=grp["row0"], rows_per_mod=grp["rows_per_mod"])
            olru, st = _lru(xy, lru_conv_w[l], lru_conv_b[l][None, :], wg, bg, lru_lam[l], h0s[gi],
                            seq=grp["seq"], nsb=grp["nsb"])
            grp["x"] = _ffn(grp["x"], olru, osgu, mod, g_lru[l][None, :], g_sgu[l][None, :], w_out_b,
                            norm2[l][None, :], up_b, ffn_conv_w[l], ffn_conv_b[l][None, :], down_b,
                            final_norm[None, :], tm=512, period=grp["period"], row0=grp["row0"],
                            rows_per_mod=grp["rows_per_mod"])
            if gi == 0:
                new_states.append(st)
    y_prompt = groups[0]["x"].reshape(batch, seq, D_MODEL)
    y_sample = groups[1]["x"].reshape(dec_batch, dec_seq, D_MODEL)
    return (y_prompt, y_sample, jnp.stack(new_states, axis=1))
```

```python
import functools
import math

import jax
import jax.numpy as jnp
from jax import lax
from jax.experimental import pallas as pl
from jax.experimental.pallas import tpu as pltpu

D_MODEL = 1024
LRU_HEADS = 8
LRU_WIDTH = 512
LRU_HEAD_DIM = 64
LRU_C = 8.0
SGU_GROUPS = 4
SGU_WIDTH = 512
CHUNK = 128
D_FF = 3072
N_MOD = 6
EPS = 1e-6
GRID_W = 64

LANES = 128
SUBLANES = 8
MOD_ROWS = 8
BF16 = jnp.bfloat16
F32 = jnp.float32

_GELU_C0 = math.sqrt(2.0 / math.pi)
_GELU_C1 = _GELU_C0 * 0.044715


def _rms(x, gain):
    return x * lax.rsqrt(jnp.mean(x * x, axis=-1, keepdims=True) + EPS) * gain


def _gelu(x):
    return (0.5 * x) * (1.0 + jnp.tanh(x * (_GELU_C0 + _GELU_C1 * (x * x))))


def _mod_row(mod_ref, k, row):
    return mod_ref[k, pl.ds(row, 1), :]


def _mod_kernel(c_ref, w_ref, b_ref, o_ref):
    s = jax.nn.silu(c_ref[...]).astype(BF16)
    o_ref[...] = jnp.dot(s, w_ref[...].astype(BF16), preferred_element_type=F32) + b_ref[...]


def _modulation(cc, w_ada, b_ada):
    return pl.pallas_call(
        _mod_kernel,
        grid=(N_MOD,),
        in_specs=[
            pl.BlockSpec((MOD_ROWS, D_MODEL), lambda j: (0, 0)),
            pl.BlockSpec((D_MODEL, D_MODEL), lambda j: (0, j)),
            pl.BlockSpec((1, D_MODEL), lambda j: (0, j)),
        ],
        out_specs=pl.BlockSpec((None, MOD_ROWS, D_MODEL), lambda j: (j, 0, 0)),
        out_shape=jax.ShapeDtypeStruct((N_MOD, MOD_ROWS, D_MODEL), F32),
        compiler_params=pltpu.CompilerParams(dimension_semantics=("arbitrary",)),
        name="modulation",
    )(cc, w_ada, b_ada)


def _inproj_kernel(x_ref, mod_ref, n1_ref, win_ref, ws_ref, bs_ref, xy_ref, osgu_ref, *,
                   tm, row0, rows_per_mod):
    row = row0 + (pl.program_id(0) * tm) // rows_per_mod
    sh1 = _mod_row(mod_ref, 0, row)
    sc1 = _mod_row(mod_ref, 1, row)
    hb = (_rms(x_ref[...], n1_ref[...] * (1.0 + sc1)) + sh1).astype(BF16)
    xy_ref[...] = jnp.dot(hb, win_ref[:, 0:2 * LRU_WIDTH], preferred_element_type=F32)
    uv = jnp.dot(hb, win_ref[:, 2 * LRU_WIDTH:], preferred_element_type=F32)
    for g in range(SGU_GROUPS):
        lo = g * LANES
        gu = _gelu(uv[:, lo:lo + LANES])
        gv = _gelu(uv[:, SGU_WIDTH + lo:SGU_WIDTH + lo + LANES]).astype(BF16)
        bias = jnp.broadcast_to(bs_ref[:, g:g + 1], (CHUNK, LANES))
        for ck in range(tm // CHUNK):
            r0 = ck * CHUNK
            s = jnp.dot(ws_ref[g], gv[r0:r0 + CHUNK], preferred_element_type=F32) + bias
            osgu_ref[r0:r0 + CHUNK, lo:lo + LANES] = gu[r0:r0 + CHUNK] * s


def _inproj(x, mod, norm1, w_in, ws, bs_t, *, tm, row0, rows_per_mod):
    m = x.shape[0]
    const = lambda i: (0, 0)
    return pl.pallas_call(
        functools.partial(_inproj_kernel, tm=tm, row0=row0, rows_per_mod=rows_per_mod),
        grid=(m // tm,),
        in_specs=[
            pl.BlockSpec((tm, D_MODEL), lambda i: (i, 0)),
            pl.BlockSpec((N_MOD, MOD_ROWS, D_MODEL), lambda i: (0, 0, 0)),
            pl.BlockSpec((1, D_MODEL), const),
            pl.BlockSpec((D_MODEL, 4 * LRU_WIDTH), const),
            pl.BlockSpec((SGU_GROUPS, CHUNK, CHUNK), lambda i: (0, 0, 0)),
            pl.BlockSpec((CHUNK, SGU_GROUPS), const),
        ],
        out_specs=[
            pl.BlockSpec((tm, 2 * LRU_WIDTH), lambda i: (i, 0)),
            pl.BlockSpec((tm, SGU_WIDTH), lambda i: (i, 0)),
        ],
        out_shape=[
            jax.ShapeDtypeStruct((m, 2 * LRU_WIDTH), F32),
            jax.ShapeDtypeStruct((m, SGU_WIDTH), F32),
        ],
        compiler_params=pltpu.CompilerParams(
            dimension_semantics=("arbitrary",), vmem_limit_bytes=48 * 1024 * 1024),
        name="inproj_sgu",
    )(x, mod, norm1, w_in, ws, bs_t)


def _scan_pitch(seq):
    pitch = seq // SUBLANES
    return pitch + (4 - pitch % 8) % 8


def _lru_kernel(xr_ref, yr_ref, cw_ref, cb_ref, wg_ref, bg_ref, lam_ref, h0_ref,
                o_ref, st_ref, a_sc, b_sc, p_sc, h_sc, *, seq, nsb, pitch, rc, unroll):
    sp_rows = SUBLANES * pitch
    n_rc = (seq * nsb) // rc
    seg = min(seq, rc)
    segs = rc // seg
    chunks_per_seq = seq // seg
    c4 = (-0.5 * LRU_C) * jax.nn.softplus(-lam_ref[...])
    cw = cw_ref[...]
    cb = cb_ref[...]
    bg = bg_ref[...]

    for s in range(nsb):
        lo, hi = s * sp_rows + seq, (s + 1) * sp_rows
        for d in range(2):
            a_sc[d, lo:hi, :] = jnp.ones((hi - lo, LANES), F32)
            b_sc[d, lo:hi, :] = jnp.zeros((hi - lo, LANES), F32)

    def scratch_row(ci, k):
        if chunks_per_seq > 1:
            return pl.multiple_of((ci // chunks_per_seq) * sp_rows + (ci % chunks_per_seq) * seg,
                                  SUBLANES)
        return pl.multiple_of((ci * segs + k) * sp_rows, SUBLANES)

    def conv_segment(xs, prev, nxt):
        ext = jnp.concatenate([prev, xs, nxt], axis=0)
        n_ext = seg + 2 * SUBLANES
        xc = cb + cw[2:3] * xs
        xc = xc + cw[0:1] * pltpu.roll(ext, 2, axis=0)[SUBLANES:SUBLANES + seg]
        xc = xc + cw[1:2] * pltpu.roll(ext, 1, axis=0)[SUBLANES:SUBLANES + seg]
        return xc + cw[3:4] * pltpu.roll(ext, n_ext - 1, axis=0)[SUBLANES:SUBLANES + seg]

    def gates(ci, _):
        base = pl.multiple_of(ci * rc, rc)
        xm = xr_ref[pl.ds(base, rc), :]
        if chunks_per_seq > 1:
            cs = ci % chunks_per_seq
            p0 = pl.multiple_of(jnp.maximum(base - SUBLANES, 0), SUBLANES)
            n0 = pl.multiple_of(jnp.minimum(base + rc, seq * nsb - SUBLANES), SUBLANES)
            prev = jnp.where(cs == 0, 0.0, xr_ref[pl.ds(p0, SUBLANES), :])
            nxt = jnp.where(cs == chunks_per_seq - 1, 0.0, xr_ref[pl.ds(n0, SUBLANES), :])
            xc = conv_segment(xm, prev, nxt)
        else:
            zeros = jnp.zeros((SUBLANES, LANES), F32)
            xc = jnp.concatenate(
                [conv_segment(xm[k * seg:(k + 1) * seg], zeros, zeros) for k in range(segs)], axis=0)
        g = jnp.dot(xc.astype(BF16), wg_ref[...], preferred_element_type=F32) + bg
        hxc = 0.5 * xc
        for d in range(2):
            th_r = jnp.tanh(g[:, (2 * d) * LANES:(2 * d + 1) * LANES])
            th_i = jnp.tanh(g[:, (2 * d + 1) * LANES:(2 * d + 2) * LANES])
            log_a = c4[d:d + 1] * th_r + c4[d:d + 1]
            a = jnp.exp(log_a)
            om = jnp.tanh(log_a) * (-1.0 - a * a)
            root = jnp.where(om > 0.0, om * lax.rsqrt(om), 0.0)
            b = root * ((th_i + 1.0) * hxc)
            for k in range(segs):
                dst = scratch_row(ci, k)
                a_sc[d, pl.ds(dst, seg), :] = a[k * seg:(k + 1) * seg]
                b_sc[d, pl.ds(dst, seg), :] = b[k * seg:(k + 1) * seg]
        return 0

    lax.fori_loop(0, n_rc, gates, 0)

    def scan_sequence(s, _):
        off = s * sp_rows

        def rows(t):
            return pl.ds(off + t, SUBLANES, stride=pitch)

        def local(t, carry):
            hf, pf, hb, pb = carry
            af = a_sc[0, rows(t), :]
            hf = af * hf + b_sc[0, rows(t), :]
            pf = af * pf
            h_sc[0, rows(t), :] = hf
            p_sc[0, rows(t), :] = pf
            tb = pitch - 1 - t
            ab = a_sc[1, rows(tb), :]
            hb = ab * hb + b_sc[1, rows(tb), :]
            pb = ab * pb
            h_sc[1, rows(tb), :] = hb
            p_sc[1, rows(tb), :] = pb
            return hf, pf, hb, pb

        zero = jnp.zeros((SUBLANES, LANES), F32)
        one = jnp.ones((SUBLANES, LANES), F32)
        hf, pf, hb, pb = lax.fori_loop(0, pitch, local, (zero, one, zero, one), unroll=unroll)

        h0 = h0_ref[s]
        c = h0[0:1]
        cf = []
        for j in range(SUBLANES):
            cf.append(c)
            c = hf[j:j + 1] + pf[j:j + 1] * c
        st_f = c
        c = h0[1:2]
        cbk = [None] * SUBLANES
        for j in reversed(range(SUBLANES)):
            cbk[j] = c
            c = hb[j:j + 1] + pb[j:j + 1] * c
        st_ref[s] = jnp.concatenate([st_f, c], axis=0)
        cf = jnp.concatenate(cf, axis=0)
        cbk = jnp.concatenate(cbk, axis=0)

        def fix(t, _):
            a_sc[0, rows(t), :] = (h_sc[0, rows(t), :] + p_sc[0, rows(t), :] * cf
                                   + h_sc[1, rows(t), :] + p_sc[1, rows(t), :] * cbk)
            return 0

        lax.fori_loop(0, pitch, fix, 0, unroll=unroll)
        return 0

    if nsb == 1:
        scan_sequence(0, 0)
    else:
        lax.fori_loop(0, nsb, scan_sequence, 0)

    def gate_out(ci, _):
        base = pl.multiple_of(ci * rc, rc)
        gy = _gelu(yr_ref[pl.ds(base, rc), :])
        for k in range(segs):
            src = scratch_row(ci, k)
            o_ref[pl.ds(base + k * seg, seg), :] = gy[k * seg:(k + 1) * seg] * a_sc[0, pl.ds(src, seg), :]
        return 0

    lax.fori_loop(0, n_rc, gate_out, 0)


def _lru(xy, conv_w, conv_b, wg, bg, lam, h0, *, seq, nsb):
    m = xy.shape[0]
    nblk = m // (seq * nsb)
    nlb = LRU_WIDTH // LANES
    pitch = _scan_pitch(seq)
    sc_rows = nsb * SUBLANES * pitch
    rows = seq * nsb
    return pl.pallas_call(
        functools.partial(_lru_kernel, seq=seq, nsb=nsb, pitch=pitch, rc=512, unroll=4),
        grid=(nlb, nblk),
        in_specs=[
            pl.BlockSpec((rows, LANES), lambda j, b: (b, j)),
            pl.BlockSpec((rows, LANES), lambda j, b: (b, nlb + j)),
            pl.BlockSpec((4, LANES), lambda j, b: (0, j)),
            pl.BlockSpec((1, LANES), lambda j, b: (0, j)),
            pl.BlockSpec((None, LANES, 4 * LANES), lambda j, b: (j, 0, 0)),
            pl.BlockSpec((None, 1, 4 * LANES), lambda j, b: (j, 0, 0)),
            pl.BlockSpec((2, LANES), lambda j, b: (0, j)),
            pl.BlockSpec((nsb, 2, LANES), lambda j, b: (b, 0, j)),
        ],
        out_specs=[
            pl.BlockSpec((rows, LANES), lambda j, b: (b, j)),
            pl.BlockSpec((nsb, 2, LANES), lambda j, b: (b, 0, j)),
        ],
        out_shape=[
            jax.ShapeDtypeStruct((m, LRU_WIDTH), F32),
            jax.ShapeDtypeStruct((m // seq, 2, LRU_WIDTH), F32),
        ],
        scratch_shapes=[pltpu.VMEM((2, sc_rows, LANES), F32)] * 4,
        compiler_params=pltpu.CompilerParams(
            dimension_semantics=("arbitrary", "arbitrary"), vmem_limit_bytes=40 * 1024 * 1024),
        name="rglru",
    )(xy, xy, conv_w, conv_b, wg, bg, lam, h0)


GROUP = SUBLANES * SUBLANES


def _swap_rows(slab_ref, val, r0):
    n = val.shape[0]
    for k in range(D_MODEL // LANES):
        slab_ref[k, r0:r0 + n, :] = val[:, k * LANES:(k + 1) * LANES]
    cols = []
    for k in range(D_MODEL // LANES):
        rows = [slab_ref[k, pl.ds(r0 + g * GROUP + t, SUBLANES, stride=SUBLANES), :]
                for g in range(n // GROUP) for t in range(SUBLANES)]
        cols.append(jnp.concatenate(rows, axis=0))
    return jnp.concatenate(cols, axis=1)


def _ffn_kernel(x_ref, ol_ref, os_ref, mod_ref, gl_ref, gs_ref, wout_ref, n2_ref, up_ref,
                cw_ref, cb_ref, down_ref, fn_ref, y_ref, slab_sc, *, tm, fc, es, period, row0,
                rows_per_mod):
    row = row0 + (pl.program_id(0) * tm) // rows_per_mod
    g1 = _mod_row(mod_ref, 2, row)
    sh2 = _mod_row(mod_ref, 3, row)
    sc2 = _mod_row(mod_ref, 4, row)
    g2 = _mod_row(mod_ref, 5, row)
    nl = _rms(ol_ref[...], gl_ref[...]).astype(BF16)
    ns = _rms(os_ref[...], gs_ref[...]).astype(BF16)
    o = (jnp.dot(nl, wout_ref[0:LRU_WIDTH, :], preferred_element_type=F32)
         + jnp.dot(ns, wout_ref[LRU_WIDTH:LRU_WIDTH + SGU_WIDTH, :], preferred_element_type=F32))
    x1 = x_ref[...] + g1 * o
    h2f = _rms(x1, n2_ref[...] * (1.0 + sc2)) + sh2
    h2 = _swap_rows(slab_sc, h2f, 0).astype(BF16)

    sub = lax.broadcasted_iota(jnp.int32, (SUBLANES, fc), 0)
    groups_per_period = period // GROUP
    n_groups = tm // GROUP

    def conv(z, c0):
        w = cw_ref[:, c0:c0 + fc]
        b = cb_ref[:, c0:c0 + fc]
        w0, w1, w2 = w[0:1], w[1:2], w[2:3]
        z8 = [[z[g * GROUP + t * SUBLANES:g * GROUP + (t + 1) * SUBLANES] for t in range(SUBLANES)]
              for g in range(n_groups)]
        outs = []
        for g in range(n_groups):
            cur = z8[g]
            lo = pltpu.roll(cur[SUBLANES - 1], 1, axis=0)
            if g % groups_per_period == 0:
                lo = jnp.where(sub == 0, 0.0, lo)
            else:
                lo = jnp.where(sub == 0, pltpu.roll(z8[g - 1][SUBLANES - 1], 1, axis=0), lo)
            hi = pltpu.roll(cur[0], SUBLANES - 1, axis=0)
            if g % groups_per_period == groups_per_period - 1:
                hi = jnp.where(sub == SUBLANES - 1, 0.0, hi)
            else:
                hi = jnp.where(sub == SUBLANES - 1, pltpu.roll(z8[g + 1][0], SUBLANES - 1, axis=0), hi)
            zm = [lo] + cur[:SUBLANES - 1]
            zp = cur[1:] + [hi]
            for t in range(SUBLANES):
                outs.append(b + w0 * zm[t] + w1 * cur[t] + w2 * zp[t])
        return jnp.concatenate(outs, axis=0)

    def up(f0):
        zg = jnp.dot(h2, up_ref[:, f0:f0 + fc], preferred_element_type=F32)
        zv = jnp.dot(h2, up_ref[:, D_FF + f0:D_FF + f0 + fc], preferred_element_type=F32)
        return zg, zv

    acc = jnp.zeros((tm, D_MODEL), F32)
    nxt = up(0)
    for f0 in range(0, D_FF, fc):
        zg, zv = nxt
        if f0 + fc < D_FF:
            nxt = up(f0 + fc)
        hg = 0.5 * conv(zg, f0)
        act = (hg * (jnp.tanh(hg) + 1.0) * conv(zv, D_FF + f0)).astype(BF16)
        if f0 + fc < D_FF:
            acc = acc + jnp.dot(act, down_ref[f0:f0 + fc, :], preferred_element_type=F32)
        else:
            for r0 in range(0, tm, es):
                a = acc[r0:r0 + es] + jnp.dot(act[r0:r0 + es], down_ref[f0:f0 + fc, :],
                                              preferred_element_type=F32)
                x2 = x1[r0:r0 + es] + g2 * _swap_rows(slab_sc, a, r0)
                y_ref[r0:r0 + es, :] = _rms(x2, fn_ref[...])


def _ffn(x, olru, osgu, mod, g_lru, g_sgu, w_out, norm2, ffn_up, conv_w, conv_b, ffn_down,
         final_norm, *, tm, period, row0, rows_per_mod):
    m = x.shape[0]
    const = lambda i: (0, 0)
    resident = lambda shape: pl.BlockSpec(shape, const, pipeline_mode=pl.Buffered(1))
    return pl.pallas_call(
        functools.partial(_ffn_kernel, tm=tm, fc=512, es=128, period=period, row0=row0,
                          rows_per_mod=rows_per_mod),
        grid=(m // tm,),
        in_specs=[
            pl.BlockSpec((tm, D_MODEL), lambda i: (i, 0)),
            pl.BlockSpec((tm, LRU_WIDTH), lambda i: (i, 0)),
            pl.BlockSpec((tm, SGU_WIDTH), lambda i: (i, 0)),
            pl.BlockSpec((N_MOD, MOD_ROWS, D_MODEL), lambda i: (0, 0, 0)),
            pl.BlockSpec((1, LRU_WIDTH), const),
            pl.BlockSpec((1, SGU_WIDTH), const),
            resident((LRU_WIDTH + SGU_WIDTH, D_MODEL)),
            pl.BlockSpec((1, D_MODEL), const),
            resident((D_MODEL, 2 * D_FF)),
            pl.BlockSpec((3, 2 * D_FF), const),
            pl.BlockSpec((1, 2 * D_FF), const),
            resident((D_FF, D_MODEL)),
            pl.BlockSpec((1, D_MODEL), const),
        ],
        out_specs=pl.BlockSpec((tm, D_MODEL), lambda i: (i, 0)),
        out_shape=jax.ShapeDtypeStruct((m, D_MODEL), F32),
        scratch_shapes=[pltpu.VMEM((D_MODEL // LANES, tm, LANES), F32)],
        compiler_params=pltpu.CompilerParams(
            dimension_semantics=("arbitrary",), vmem_limit_bytes=56 * 1024 * 1024),
        name="outproj_ffn",
    )(x, olru, osgu, mod, g_lru, g_sgu, w_out, norm2, ffn_up, conv_w, conv_b, ffn_down, final_norm)


def _block_diag_pairs(w):
    nd = w.shape[0]
    wp = w.reshape(nd, LRU_HEADS // 2, 2, LRU_HEAD_DIM, LRU_HEAD_DIM)
    z = jnp.zeros_like(wp[:, :, 0])
    top = jnp.concatenate([wp[:, :, 0], z], axis=-1)
    bot = jnp.concatenate([z, wp[:, :, 1]], axis=-1)
    return jnp.concatenate([top, bot], axis=-2)


def kernel(x_prompt, x_sample, state_lru, c, c_ctx, norm1, norm2, w_ada, b_ada, w_in, lru_conv_w,
           lru_conv_b, lru_wa, lru_ba, lru_wx, lru_bx, lru_lam, sgu_ws, sgu_bs, g_lru, g_sgu, w_out,
           ffn_up, ffn_conv_w, ffn_conv_b, ffn_down, final_norm):
    batch, seq, _ = x_prompt.shape
    dec_batch, dec_seq, _ = x_sample.shape
    depth = norm1.shape[0]
    assert depth == 1, "the final norm is fused into the (single) layer's last kernel"
    nlb = LRU_WIDTH // LANES

    groups = [
        dict(x=x_prompt.reshape(batch * seq, D_MODEL), seq=seq, nsb=8, period=seq, row0=0,
             rows_per_mod=batch * seq),
        dict(x=x_sample.reshape(dec_batch * dec_seq, D_MODEL), seq=dec_seq, nsb=1, period=GRID_W,
             row0=1, rows_per_mod=dec_seq),
    ]
    cc = jnp.concatenate(
        [c_ctx[None, :], c, jnp.zeros((MOD_ROWS - 1 - dec_batch, D_MODEL), F32)], axis=0)
    zeros_state = jnp.zeros((batch, 2, LRU_WIDTH), x_prompt.dtype)

    new_states = []
    for l in range(depth):
        mod = _modulation(cc, w_ada[l], b_ada[l][None, :])
        w_in_b = w_in[l].astype(BF16)
        ws_b = sgu_ws[l].astype(BF16)
        bs_t = sgu_bs[l].T
        bd_a = _block_diag_pairs(0.5 * lru_wa[l])
        bd_x = _block_diag_pairs(0.5 * lru_wx[l])
        wg = jnp.concatenate([bd_a[0], bd_x[0], bd_a[1], bd_x[1]], axis=-1).astype(BF16)
        ba = (0.5 * lru_ba[l]).reshape(2, nlb, 1, LANES)
        bx = (0.5 * lru_bx[l]).reshape(2, nlb, 1, LANES)
        bg = jnp.concatenate([ba[0], bx[0], ba[1], bx[1]], axis=-1)
        w_out_b = w_out[l].astype(BF16)
        up_b = ffn_up[l].astype(BF16)
        down_b = ffn_down[l].astype(BF16)
        h0s = [zeros_state, state_lru[:, l]]
        for gi, grp in enumerate(groups):
            xy, osgu = _inproj(grp["x"], mod, norm1[l][None, :], w_in_b, ws_b, bs_t, tm=512,
                               row0=grp["row0"], rows_per_mod=grp["rows_per_mod"])
            olru, st = _lru(xy, lru_conv_w[l], lru_conv_b[l][None, :], wg, bg, lru_lam[l], h0s[gi],
                            seq=grp["seq"], nsb=grp["nsb"])
            grp["x"] = _ffn(grp["x"], olru, osgu, mod, g_lru[l][None, :], g_sgu[l][None, :], w_out_b,
                            norm2[l][None, :], up_b, ffn_conv_w[l], ffn_conv_b[l][None, :], down_b,
                            final_norm[None, :], tm=512, period=grp["period"], row0=grp["row0"],
                            rows_per_mod=grp["rows_per_mod"])
            if gi == 0:
                new_states.append(st)
    y_prompt = groups[0]["x"].reshape(batch, seq, D_MODEL)
    y_sample = groups[1]["x"].reshape(dec_batch, dec_seq, D_MODEL)
    return (y_prompt, y_sample, jnp.stack(new_states, axis=1))
```

```python
import functools
import math

import jax
import jax.numpy as jnp
from jax import lax
from jax.experimental import pallas as pl
from jax.experimental.pallas import tpu as pltpu

D_MODEL = 1024
LRU_HEADS = 8
LRU_WIDTH = 512
LRU_HEAD_DIM = 64
LRU_C = 8.0
SGU_GROUPS = 4
SGU_WIDTH = 512
CHUNK = 128
D_FF = 3072
N_MOD = 6
EPS = 1e-6
GRID_W = 64

LANES = 128
SUBLANES = 8
MOD_ROWS = 8
BF16 = jnp.bfloat16
F32 = jnp.float32

_GELU_C0 = math.sqrt(2.0 / math.pi)
_GELU_C1 = _GELU_C0 * 0.044715


def _rms(x, gain):
    return x * lax.rsqrt(jnp.mean(x * x, axis=-1, keepdims=True) + EPS) * gain


def _gelu(x):
    return (0.5 * x) * (1.0 + jnp.tanh(x * (_GELU_C0 + _GELU_C1 * (x * x))))


def _mod_row(mod_ref, k, row):
    return mod_ref[k, pl.ds(row, 1), :]


def _mod_kernel(c_ref, w_ref, b_ref, o_ref):
    s = jax.nn.silu(c_ref[...]).astype(BF16)
    o_ref[...] = jnp.dot(s, w_ref[...].astype(BF16), preferred_element_type=F32) + b_ref[...]


def _modulation(cc, w_ada, b_ada):
    return pl.pallas_call(
        _mod_kernel,
        grid=(N_MOD,),
        in_specs=[
            pl.BlockSpec((MOD_ROWS, D_MODEL), lambda j: (0, 0)),
            pl.BlockSpec((D_MODEL, D_MODEL), lambda j: (0, j)),
            pl.BlockSpec((1, D_MODEL), lambda j: (0, j)),
        ],
        out_specs=pl.BlockSpec((None, MOD_ROWS, D_MODEL), lambda j: (j, 0, 0)),
        out_shape=jax.ShapeDtypeStruct((N_MOD, MOD_ROWS, D_MODEL), F32),
        compiler_params=pltpu.CompilerParams(dimension_semantics=("arbitrary",)),
        name="modulation",
    )(cc, w_ada, b_ada)


def _inproj_kernel(x_ref, mod_ref, n1_ref, win_ref, ws_ref, bs_ref, xy_ref, osgu_ref, *,
                   tm, row0, rows_per_mod):
    row = row0 + (pl.program_id(0) * tm) // rows_per_mod
    sh1 = _mod_row(mod_ref, 0, row)
    sc1 = _mod_row(mod_ref, 1, row)
    hb = (_rms(x_ref[...], n1_ref[...] * (1.0 + sc1)) + sh1).astype(BF16)
    xy_ref[...] = jnp.dot(hb, win_ref[:, 0:2 * LRU_WIDTH], preferred_element_type=F32)
    uv = jnp.dot(hb, win_ref[:, 2 * LRU_WIDTH:], preferred_element_type=F32)
    for g in range(SGU_GROUPS):
        lo = g * LANES
        gu = _gelu(uv[:, lo:lo + LANES])
        gv = _gelu(uv[:, SGU_WIDTH + lo:SGU_WIDTH + lo + LANES]).astype(BF16)
        bias = jnp.broadcast_to(bs_ref[:, g:g + 1], (CHUNK, LANES))
        for ck in range(tm // CHUNK):
            r0 = ck * CHUNK
            s = jnp.dot(ws_ref[g], gv[r0:r0 + CHUNK], preferred_element_type=F32) + bias
            osgu_ref[r0:r0 + CHUNK, lo:lo + LANES] = gu[r0:r0 + CHUNK] * s


def _inproj(x, mod, norm1, w_in, ws, bs_t, *, tm, row0, rows_per_mod):
    m = x.shape[0]
    const = lambda i: (0, 0)
    return pl.pallas_call(
        functools.partial(_inproj_kernel, tm=tm, row0=row0, rows_per_mod=rows_per_mod),
        grid=(m // tm,),
        in_specs=[
            pl.BlockSpec((tm, D_MODEL), lambda i: (i, 0)),
            pl.BlockSpec((N_MOD, MOD_ROWS, D_MODEL), lambda i: (0, 0, 0)),
            pl.BlockSpec((1, D_MODEL), const),
            pl.BlockSpec((D_MODEL, 4 * LRU_WIDTH), const),
            pl.BlockSpec((SGU_GROUPS, CHUNK, CHUNK), lambda i: (0, 0, 0)),
            pl.BlockSpec((CHUNK, SGU_GROUPS), const),
        ],
        out_specs=[
            pl.BlockSpec((tm, 2 * LRU_WIDTH), lambda i: (i, 0)),
            pl.BlockSpec((tm, SGU_WIDTH), lambda i: (i, 0)),
        ],
        out_shape=[
            jax.ShapeDtypeStruct((m, 2 * LRU_WIDTH), F32),
            jax.ShapeDtypeStruct((m, SGU_WIDTH), F32),
        ],
        compiler_params=pltpu.CompilerParams(
            dimension_semantics=("arbitrary",), vmem_limit_bytes=48 * 1024 * 1024),
        name="inproj_sgu",
    )(x, mod, norm1, w_in, ws, bs_t)


def _scan_pitch(seq):
    pitch = seq // SUBLANES
    return pitch + (4 - pitch % 8) % 8


def _lru_kernel(xr_ref, yr_ref, cw_ref, cb_ref, wg_ref, bg_ref, lam_ref, h0_ref,
                o_ref, st_ref, a_sc, b_sc, p_sc, h_sc, *, seq, nsb, nk, pitch, rc, unroll):
    sp_rows = SUBLANES * pitch
    n_rc = (seq * nsb) // rc
    seg = min(seq, rc)
    segs = rc // seg
    chunks_per_seq = seq // seg
    c4_all = (-0.5 * LRU_C) * jax.nn.softplus(-lam_ref[...])
    cw_all = cw_ref[...]
    cb_all = cb_ref[...]

    for s in range(nsb):
        lo, hi = s * sp_rows + seq, (s + 1) * sp_rows
        for slab in range(2 * nk):
            a_sc[slab, lo:hi, :] = jnp.ones((hi - lo, LANES), F32)
            b_sc[slab, lo:hi, :] = jnp.zeros((hi - lo, LANES), F32)

    def scratch_row(ci, k):
        if chunks_per_seq > 1:
            return pl.multiple_of((ci // chunks_per_seq) * sp_rows + (ci % chunks_per_seq) * seg,
                                  SUBLANES)
        return pl.multiple_of((ci * segs + k) * sp_rows, SUBLANES)

    def conv_segment(xs, prev, nxt, cw, cb):
        ext = jnp.concatenate([prev, xs, nxt], axis=0)
        n_ext = seg + 2 * SUBLANES
        xc = cb + cw[2:3] * xs
        xc = xc + cw[0:1] * pltpu.roll(ext, 2, axis=0)[SUBLANES:SUBLANES + seg]
        xc = xc + cw[1:2] * pltpu.roll(ext, 1, axis=0)[SUBLANES:SUBLANES + seg]
        return xc + cw[3:4] * pltpu.roll(ext, n_ext - 1, axis=0)[SUBLANES:SUBLANES + seg]

    def gates(ci, _):
        base = pl.multiple_of(ci * rc, rc)
        for k in range(nk):
            lanes = slice(k * LANES, (k + 1) * LANES)
            cw, cb, c4 = cw_all[:, lanes], cb_all[:, lanes], c4_all[:, lanes]
            xm = xr_ref[pl.ds(base, rc), lanes]
            if chunks_per_seq > 1:
                cs = ci % chunks_per_seq
                p0 = pl.multiple_of(jnp.maximum(base - SUBLANES, 0), SUBLANES)
                n0 = pl.multiple_of(jnp.minimum(base + rc, seq * nsb - SUBLANES), SUBLANES)
                prev = jnp.where(cs == 0, 0.0, xr_ref[pl.ds(p0, SUBLANES), lanes])
                nxt = jnp.where(cs == chunks_per_seq - 1, 0.0, xr_ref[pl.ds(n0, SUBLANES), lanes])
                xc = conv_segment(xm, prev, nxt, cw, cb)
            else:
                zeros = jnp.zeros((SUBLANES, LANES), F32)
                xc = jnp.concatenate(
                    [conv_segment(xm[j * seg:(j + 1) * seg], zeros, zeros, cw, cb) for j in range(segs)],
                    axis=0)
            g = jnp.dot(xc.astype(BF16), wg_ref[k], preferred_element_type=F32) + bg_ref[k]
            hxc = 0.5 * xc
            for d in range(2):
                th_r = jnp.tanh(g[:, (2 * d) * LANES:(2 * d + 1) * LANES])
                th_i = jnp.tanh(g[:, (2 * d + 1) * LANES:(2 * d + 2) * LANES])
                log_a = c4[d:d + 1] * th_r + c4[d:d + 1]
                a = jnp.exp(log_a)
                om = jnp.tanh(log_a) * (-1.0 - a * a)
                root = jnp.where(om > 0.0, om * lax.rsqrt(om), 0.0)
                b = root * ((th_i + 1.0) * hxc)
                for j in range(segs):
                    dst = scratch_row(ci, j)
                    a_sc[d * nk + k, pl.ds(dst, seg), :] = a[j * seg:(j + 1) * seg]
                    b_sc[d * nk + k, pl.ds(dst, seg), :] = b[j * seg:(j + 1) * seg]
        return 0

    lax.fori_loop(0, n_rc, gates, 0)

    def scan_sequence(s, _):
        off = s * sp_rows

        def rows(t):
            return pl.ds(off + t, SUBLANES, stride=pitch)

        def local(t, carry):
            out = []
            for slab, (h, p) in enumerate(carry):
                tt = t if slab < nk else pitch - 1 - t
                a = a_sc[slab, rows(tt), :]
                h = a * h + b_sc[slab, rows(tt), :]
                p = a * p
                h_sc[slab, rows(tt), :] = h
                p_sc[slab, rows(tt), :] = p
                out.append((h, p))
            return tuple(out)

        zero = jnp.zeros((SUBLANES, LANES), F32)
        one = jnp.ones((SUBLANES, LANES), F32)
        ends = lax.fori_loop(0, pitch, local, ((zero, one),) * (2 * nk), unroll=unroll)

        h0 = h0_ref[s]
        starts = []
        finals = [[None] * nk, [None] * nk]
        for slab, (h, p) in enumerate(ends):
            d, k = divmod(slab, nk)
            c = h0[d:d + 1, k * LANES:(k + 1) * LANES]
            cs = [None] * SUBLANES
            for j in (range(SUBLANES) if d == 0 else reversed(range(SUBLANES))):
                cs[j] = c
                c = h[j:j + 1] + p[j:j + 1] * c
            finals[d][k] = c
            starts.append(jnp.concatenate(cs, axis=0))
        st_ref[s] = jnp.concatenate(
            [jnp.concatenate(finals[0], axis=1), jnp.concatenate(finals[1], axis=1)], axis=0)

        def fix(t, _):
            for k in range(nk):
                a_sc[k, rows(t), :] = (h_sc[k, rows(t), :] + p_sc[k, rows(t), :] * starts[k]
                                       + h_sc[nk + k, rows(t), :] + p_sc[nk + k, rows(t), :] * starts[nk + k])
            return 0

        lax.fori_loop(0, pitch, fix, 0, unroll=unroll)
        return 0

    if nsb == 1:
        scan_sequence(0, 0)
    else:
        lax.fori_loop(0, nsb, scan_sequence, 0)

    def gate_out(ci, _):
        base = pl.multiple_of(ci * rc, rc)
        for k in range(nk):
            lanes = slice(k * LANES, (k + 1) * LANES)
            gy = _gelu(yr_ref[pl.ds(base, rc), lanes])
            for j in range(segs):
                src = scratch_row(ci, j)
                o_ref[pl.ds(base + j * seg, seg), lanes] = (gy[j * seg:(j + 1) * seg]
                                                            * a_sc[k, pl.ds(src, seg), :])
        return 0

    lax.fori_loop(0, n_rc, gate_out, 0)


def _lru(xy, conv_w, conv_b, wg, bg, lam, h0, *, seq, nsb):
    m = xy.shape[0]
    nblk = m // (seq * nsb)
    nk = 2
    width = nk * LANES
    ncb = LRU_WIDTH // width
    pitch = _scan_pitch(seq)
    sc_rows = nsb * SUBLANES * pitch
    rows = seq * nsb
    return pl.pallas_call(
        functools.partial(_lru_kernel, seq=seq, nsb=nsb, nk=nk, pitch=pitch, rc=512, unroll=4),
        grid=(ncb, nblk),
        in_specs=[
            pl.BlockSpec((rows, width), lambda j, b: (b, j)),
            pl.BlockSpec((rows, width), lambda j, b: (b, ncb + j)),
            pl.BlockSpec((4, width), lambda j, b: (0, j)),
            pl.BlockSpec((1, width), lambda j, b: (0, j)),
            pl.BlockSpec((nk, LANES, 4 * LANES), lambda j, b: (j, 0, 0)),
            pl.BlockSpec((nk, 1, 4 * LANES), lambda j, b: (j, 0, 0)),
            pl.BlockSpec((2, width), lambda j, b: (0, j)),
            pl.BlockSpec((nsb, 2, width), lambda j, b: (b, 0, j)),
        ],
        out_specs=[
            pl.BlockSpec((rows, width), lambda j, b: (b, j)),
            pl.BlockSpec((nsb, 2, width), lambda j, b: (b, 0, j)),
        ],
        out_shape=[
            jax.ShapeDtypeStruct((m, LRU_WIDTH), F32),
            jax.ShapeDtypeStruct((m // seq, 2, LRU_WIDTH), F32),
        ],
        scratch_shapes=[pltpu.VMEM((2 * nk, sc_rows, LANES), F32)] * 4,
        compiler_params=pltpu.CompilerParams(
            dimension_semantics=("arbitrary", "arbitrary"), vmem_limit_bytes=48 * 1024 * 1024),
        name="rglru",
    )(xy, xy, conv_w, conv_b, wg, bg, lam, h0)


GROUP = SUBLANES * SUBLANES


def _swap_rows(slab_ref, val, r0):
    n = val.shape[0]
    for k in range(D_MODEL // LANES):
        slab_ref[k, r0:r0 + n, :] = val[:, k * LANES:(k + 1) * LANES]
    cols = []
    for k in range(D_MODEL // LANES):
        rows = [slab_ref[k, pl.ds(r0 + g * GROUP + t, SUBLANES, stride=SUBLANES), :]
                for g in range(n // GROUP) for t in range(SUBLANES)]
        cols.append(jnp.concatenate(rows, axis=0))
    return jnp.concatenate(cols, axis=1)


def _ffn_kernel(x_ref, ol_ref, os_ref, mod_ref, gl_ref, gs_ref, wout_ref, n2_ref, up_ref,
                cw_ref, cb_ref, down_ref, fn_ref, y_ref, slab_sc, *, tm, fc, es, period, row0,
                rows_per_mod):
    row = row0 + (pl.program_id(0) * tm) // rows_per_mod
    g1 = _mod_row(mod_ref, 2, row)
    sh2 = _mod_row(mod_ref, 3, row)
    sc2 = _mod_row(mod_ref, 4, row)
    g2 = _mod_row(mod_ref, 5, row)
    nl = _rms(ol_ref[...], gl_ref[...]).astype(BF16)
    ns = _rms(os_ref[...], gs_ref[...]).astype(BF16)
    o = (jnp.dot(nl, wout_ref[0:LRU_WIDTH, :], preferred_element_type=F32)
         + jnp.dot(ns, wout_ref[LRU_WIDTH:LRU_WIDTH + SGU_WIDTH, :], preferred_element_type=F32))
    x1 = x_ref[...] + g1 * o
    h2f = _rms(x1, n2_ref[...] * (1.0 + sc2)) + sh2
    h2 = _swap_rows(slab_sc, h2f, 0).astype(BF16)

    sub = lax.broadcasted_iota(jnp.int32, (SUBLANES, fc), 0)
    groups_per_period = period // GROUP
    n_groups = tm // GROUP

    def conv(z, c0):
        w = cw_ref[:, c0:c0 + fc]
        b = cb_ref[:, c0:c0 + fc]
        w0, w1, w2 = w[0:1], w[1:2], w[2:3]
        z8 = [[z[g * GROUP + t * SUBLANES:g * GROUP + (t + 1) * SUBLANES] for t in range(SUBLANES)]
              for g in range(n_groups)]
        outs = []
        for g in range(n_groups):
            cur = z8[g]
            lo = pltpu.roll(cur[SUBLANES - 1], 1, axis=0)
            if g % groups_per_period == 0:
                lo = jnp.where(sub == 0, 0.0, lo)
            else:
                lo = jnp.where(sub == 0, pltpu.roll(z8[g - 1][SUBLANES - 1], 1, axis=0), lo)
            hi = pltpu.roll(cur[0], SUBLANES - 1, axis=0)
            if g % groups_per_period == groups_per_period - 1:
                hi = jnp.where(sub == SUBLANES - 1, 0.0, hi)
            else:
                hi = jnp.where(sub == SUBLANES - 1, pltpu.roll(z8[g + 1][0], SUBLANES - 1, axis=0), hi)
            zm = [lo] + cur[:SUBLANES - 1]
            zp = cur[1:] + [hi]
            for t in range(SUBLANES):
                outs.append(b + w0 * zm[t] + w1 * cur[t] + w2 * zp[t])
        return jnp.concatenate(outs, axis=0)

    def up(f0):
        zg = jnp.dot(h2, up_ref[:, f0:f0 + fc], preferred_element_type=F32)
        zv = jnp.dot(h2, up_ref[:, D_FF + f0:D_FF + f0 + fc], preferred_element_type=F32)
        return zg, zv

    acc = jnp.zeros((tm, D_MODEL), F32)
    nxt = up(0)
    for f0 in range(0, D_FF, fc):
        zg, zv = nxt
        if f0 + fc < D_FF:
            nxt = up(f0 + fc)
        hg = 0.5 * conv(zg, f0)
        act = (hg * (jnp.tanh(hg) + 1.0) * conv(zv, D_FF + f0)).astype(BF16)
        if f0 + fc < D_FF:
            acc = acc + jnp.dot(act, down_ref[f0:f0 + fc, :], preferred_element_type=F32)
        else:
            for r0 in range(0, tm, es):
                a = acc[r0:r0 + es] + jnp.dot(act[r0:r0 + es], down_ref[f0:f0 + fc, :],
                                              preferred_element_type=F32)
                x2 = x1[r0:r0 + es] + g2 * _swap_rows(slab_sc, a, r0)
                y_ref[r0:r0 + es, :] = _rms(x2, fn_ref[...])


def _ffn(x, olru, osgu, mod, g_lru, g_sgu, w_out, norm2, ffn_up, conv_w, conv_b, ffn_down,
         final_norm, *, tm, period, row0, rows_per_mod):
    m = x.shape[0]
    const = lambda i: (0, 0)
    resident = lambda shape: pl.BlockSpec(shape, const, pipeline_mode=pl.Buffered(1))
    return pl.pallas_call(
        functools.partial(_ffn_kernel, tm=tm, fc=512, es=128, period=period, row0=row0,
                          rows_per_mod=rows_per_mod),
        grid=(m // tm,),
        in_specs=[
            pl.BlockSpec((tm, D_MODEL), lambda i: (i, 0)),
            pl.BlockSpec((tm, LRU_WIDTH), lambda i: (i, 0)),
            pl.BlockSpec((tm, SGU_WIDTH), lambda i: (i, 0)),
            pl.BlockSpec((N_MOD, MOD_ROWS, D_MODEL), lambda i: (0, 0, 0)),
            pl.BlockSpec((1, LRU_WIDTH), const),
            pl.BlockSpec((1, SGU_WIDTH), const),
            resident((LRU_WIDTH + SGU_WIDTH, D_MODEL)),
            pl.BlockSpec((1, D_MODEL), const),
            resident((D_MODEL, 2 * D_FF)),
            pl.BlockSpec((3, 2 * D_FF), const),
            pl.BlockSpec((1, 2 * D_FF), const),
            resident((D_FF, D_MODEL)),
            pl.BlockSpec((1, D_MODEL), const),
        ],
        out_specs=pl.BlockSpec((tm, D_MODEL), lambda i: (i, 0)),
        out_shape=jax.ShapeDtypeStruct((m, D_MODEL), F32),
        scratch_shapes=[pltpu.VMEM((D_MODEL // LANES, tm, LANES), F32)],
        compiler_params=pltpu.CompilerParams(
            dimension_semantics=("arbitrary",), vmem_limit_bytes=56 * 1024 * 1024),
        name="outproj_ffn",
    )(x, olru, osgu, mod, g_lru, g_sgu, w_out, norm2, ffn_up, conv_w, conv_b, ffn_down, final_norm)


def _block_diag_pairs(w):
    nd = w.shape[0]
    wp = w.reshape(nd, LRU_HEADS // 2, 2, LRU_HEAD_DIM, LRU_HEAD_DIM)
    z = jnp.zeros_like(wp[:, :, 0])
    top = jnp.concatenate([wp[:, :, 0], z], axis=-1)
    bot = jnp.concatenate([z, wp[:, :, 1]], axis=-1)
    return jnp.concatenate([top, bot], axis=-2)


def kernel(x_prompt, x_sample, state_lru, c, c_ctx, norm1, norm2, w_ada, b_ada, w_in, lru_conv_w,
           lru_conv_b, lru_wa, lru_ba, lru_wx, lru_bx, lru_lam, sgu_ws, sgu_bs, g_lru, g_sgu, w_out,
           ffn_up, ffn_conv_w, ffn_conv_b, ffn_down, final_norm):
    batch, seq, _ = x_prompt.shape
    dec_batch, dec_seq, _ = x_sample.shape
    depth = norm1.shape[0]
    assert depth == 1, "the final norm is fused into the (single) layer's last kernel"
    nlb = LRU_WIDTH // LANES

    groups = [
        dict(x=x_prompt.reshape(batch * seq, D_MODEL), seq=seq, nsb=8, period=seq, row0=0,
             rows_per_mod=batch * seq),
        dict(x=x_sample.reshape(dec_batch * dec_seq, D_MODEL), seq=dec_seq, nsb=1, period=GRID_W,
             row0=1, rows_per_mod=dec_seq),
    ]
    cc = jnp.concatenate(
        [c_ctx[None, :], c, jnp.zeros((MOD_ROWS - 1 - dec_batch, D_MODEL), F32)], axis=0)
    zeros_state = jnp.zeros((batch, 2, LRU_WIDTH), x_prompt.dtype)

    new_states = []
    for l in range(depth):
        mod = _modulation(cc, w_ada[l], b_ada[l][None, :])
        w_in_b = w_in[l].astype(BF16)
        ws_b = sgu_ws[l].astype(BF16)
        bs_t = sgu_bs[l].T
        bd_a = _block_diag_pairs(0.5 * lru_wa[l])
        bd_x = _block_diag_pairs(0.5 * lru_wx[l])
        wg = jnp.concatenate([bd_a[0], bd_x[0], bd_a[1], bd_x[1]], axis=-1).astype(BF16)
        ba = (0.5 * lru_ba[l]).reshape(2, nlb, 1, LANES)
        bx = (0.5 * lru_bx[l]).reshape(2, nlb, 1, LANES)
        bg = jnp.concatenate([ba[0], bx[0], ba[1], bx[1]], axis=-1)
        w_out_b = w_out[l].astype(BF16)
        up_b = ffn_up[l].astype(BF16)
        down_b = ffn_down[l].astype(BF16)
        h0s = [zeros_state, state_lru[:, l]]
        for gi, grp in enumerate(groups):
            xy, osgu = _inproj(grp["x"], mod, norm1[l][None, :], w_in_b, ws_b, bs_t, tm=512,
                               row0=grp["row0"], rows_per_mod=grp["rows_per_mod"])
            olru, st = _lru(xy, lru_conv_w[l], lru_conv_b[l][None, :], wg, bg, lru_lam[l], h0s[gi],
                            seq=grp["seq"], nsb=grp["nsb"])
            grp["x"] = _ffn(grp["x"], olru, osgu, mod, g_lru[l][None, :], g_sgu[l][None, :], w_out_b,
                            norm2[l][None, :], up_b, ffn_conv_w[l], ffn_conv_b[l][None, :], down_b,
                            final_norm[None, :], tm=512, period=grp["period"], row0=grp["row0"],
                            rows_per_mod=grp["rows_per_mod"])
            if gi == 0:
                new_states.append(st)
    y_prompt = groups[0]["x"].reshape(batch, seq, D_MODEL)
    y_sample = groups[1]["x"].reshape(dec_batch, dec_seq, D_MODEL)
    return (y_prompt, y_sample, jnp.stack(new_states, axis=1))
```

```python
import functools
import math

import jax
import jax.numpy as jnp
from jax import lax
from jax.experimental import pallas as pl
from jax.experimental.pallas import tpu as pltpu

D_MODEL = 1024
LRU_HEADS = 8
LRU_WIDTH = 512
LRU_HEAD_DIM = 64
LRU_C = 8.0
SGU_GROUPS = 4
SGU_WIDTH = 512
CHUNK = 128
D_FF = 3072
N_MOD = 6
EPS = 1e-6
GRID_W = 64

LANES = 128
SUBLANES = 8
MOD_ROWS = 8
BF16 = jnp.bfloat16
F32 = jnp.float32

_GELU_C0 = math.sqrt(2.0 / math.pi)
_GELU_C1 = _GELU_C0 * 0.044715


def _rms(x, gain):
    return x * lax.rsqrt(jnp.mean(x * x, axis=-1, keepdims=True) + EPS) * gain


def _gelu(x):
    return (0.5 * x) * (1.0 + jnp.tanh(x * (_GELU_C0 + _GELU_C1 * (x * x))))


def _mod_row(mod_ref, k, row):
    return mod_ref[k, pl.ds(row, 1), :]


def _mod_kernel(c_ref, w_ref, b_ref, o_ref):
    s = jax.nn.silu(c_ref[...]).astype(BF16)
    o_ref[...] = jnp.dot(s, w_ref[...].astype(BF16), preferred_element_type=F32) + b_ref[...]


def _modulation(cc, w_ada, b_ada):
    return pl.pallas_call(
        _mod_kernel,
        grid=(N_MOD,),
        in_specs=[
            pl.BlockSpec((MOD_ROWS, D_MODEL), lambda j: (0, 0)),
            pl.BlockSpec((D_MODEL, D_MODEL), lambda j: (0, j)),
            pl.BlockSpec((1, D_MODEL), lambda j: (0, j)),
        ],
        out_specs=pl.BlockSpec((None, MOD_ROWS, D_MODEL), lambda j: (j, 0, 0)),
        out_shape=jax.ShapeDtypeStruct((N_MOD, MOD_ROWS, D_MODEL), F32),
        compiler_params=pltpu.CompilerParams(dimension_semantics=("arbitrary",)),
        name="modulation",
    )(cc, w_ada, b_ada)


def _token_groups(tm, n_ctx, n_lat, lat_seq):
    assert n_ctx % tm == 0 and n_lat % tm == 0 and lat_seq % tm == 0
    nc = n_ctx // tm
    ctx_map = lambda i: (jnp.minimum(i, nc - 1), 0)
    lat_map = lambda i: (jnp.maximum(i - nc, 0), 0)
    lat_mod_row = lambda i: 1 + ((i - nc) * tm) // lat_seq
    return nc, nc + n_lat // tm, ctx_map, lat_map, lat_mod_row


def _inproj_kernel(xc_ref, xl_ref, mod_ref, n1_ref, win_ref, ws_ref, bs_ref, xy_ref, osgu_ref, *,
                   tm, nc, lat_mod_row):
    i = pl.program_id(0)

    def body(x_ref, row):
        sh1 = _mod_row(mod_ref, 0, row)
        sc1 = _mod_row(mod_ref, 1, row)
        hb = (_rms(x_ref[...], n1_ref[...] * (1.0 + sc1)) + sh1).astype(BF16)
        uv = jnp.dot(hb, win_ref[:, 2 * LRU_WIDTH:], preferred_element_type=F32)
        xy_ref[...] = jnp.dot(hb, win_ref[:, 0:2 * LRU_WIDTH], preferred_element_type=F32)
        for g in range(SGU_GROUPS):
            lo = g * LANES
            gu = _gelu(uv[:, lo:lo + LANES])
            gv = _gelu(uv[:, SGU_WIDTH + lo:SGU_WIDTH + lo + LANES]).astype(BF16)
            bias = jnp.broadcast_to(bs_ref[:, g:g + 1], (CHUNK, LANES))
            for ck in range(tm // CHUNK):
                r0 = ck * CHUNK
                s = jnp.dot(ws_ref[g], gv[r0:r0 + CHUNK], preferred_element_type=F32) + bias
                osgu_ref[r0:r0 + CHUNK, lo:lo + LANES] = gu[r0:r0 + CHUNK] * s

    @pl.when(i < nc)
    def _():
        body(xc_ref, 0)

    @pl.when(i >= nc)
    def _():
        body(xl_ref, lat_mod_row(i))


def _inproj(xc, xl, mod, norm1, w_in, ws, bs_t, *, tm, lat_seq):
    nc, n, ctx_map, lat_map, lat_mod_row = _token_groups(tm, xc.shape[0], xl.shape[0], lat_seq)
    m = n * tm
    const = lambda i: (0, 0)
    return pl.pallas_call(
        functools.partial(_inproj_kernel, tm=tm, nc=nc, lat_mod_row=lat_mod_row),
        grid=(n,),
        in_specs=[
            pl.BlockSpec((tm, D_MODEL), ctx_map),
            pl.BlockSpec((tm, D_MODEL), lat_map),
            pl.BlockSpec((N_MOD, MOD_ROWS, D_MODEL), lambda i: (0, 0, 0)),
            pl.BlockSpec((1, D_MODEL), const),
            pl.BlockSpec((D_MODEL, 4 * LRU_WIDTH), const),
            pl.BlockSpec((SGU_GROUPS, CHUNK, CHUNK), lambda i: (0, 0, 0)),
            pl.BlockSpec((CHUNK, SGU_GROUPS), const),
        ],
        out_specs=[
            pl.BlockSpec((tm, 2 * LRU_WIDTH), lambda i: (i, 0)),
            pl.BlockSpec((tm, SGU_WIDTH), lambda i: (i, 0)),
        ],
        out_shape=[
            jax.ShapeDtypeStruct((m, 2 * LRU_WIDTH), F32),
            jax.ShapeDtypeStruct((m, SGU_WIDTH), F32),
        ],
        compiler_params=pltpu.CompilerParams(
            dimension_semantics=("arbitrary",), vmem_limit_bytes=48 * 1024 * 1024),
        name="inproj_sgu",
    )(xc, xl, mod, norm1, w_in, ws, bs_t)


def _scan_pitch(seq):
    pitch = seq // SUBLANES
    return pitch + (4 - pitch % 8) % 8


def _lru_kernel(xr_ref, yr_ref, cw_ref, cb_ref, wg_ref, bg_ref, lam_ref, h0_ref,
                o_ref, st_ref, a_sc, b_sc, p_sc, h_sc, *, seq, nsb, nk, pitch, rc, unroll):
    sp_rows = SUBLANES * pitch
    n_rc = (seq * nsb) // rc
    seg = min(seq, rc)
    segs = rc // seg
    chunks_per_seq = seq // seg
    c4_all = (-0.5 * LRU_C) * jax.nn.softplus(-lam_ref[...])
    cw_all = cw_ref[...]
    cb_all = cb_ref[...]

    for s in range(nsb):
        lo, hi = s * sp_rows + seq, (s + 1) * sp_rows
        for slab in range(2 * nk):
            a_sc[slab, lo:hi, :] = jnp.ones((hi - lo, LANES), F32)
            b_sc[slab, lo:hi, :] = jnp.zeros((hi - lo, LANES), F32)

    def scratch_row(ci, k):
        if chunks_per_seq > 1:
            return pl.multiple_of((ci // chunks_per_seq) * sp_rows + (ci % chunks_per_seq) * seg,
                                  SUBLANES)
        return pl.multiple_of((ci * segs + k) * sp_rows, SUBLANES)

    def conv_segment(xs, prev, nxt, cw, cb):
        ext = jnp.concatenate([prev, xs, nxt], axis=0)
        n_ext = seg + 2 * SUBLANES
        xc = cb + cw[2:3] * xs
        xc = xc + cw[0:1] * pltpu.roll(ext, 2, axis=0)[SUBLANES:SUBLANES + seg]
        xc = xc + cw[1:2] * pltpu.roll(ext, 1, axis=0)[SUBLANES:SUBLANES + seg]
        return xc + cw[3:4] * pltpu.roll(ext, n_ext - 1, axis=0)[SUBLANES:SUBLANES + seg]

    def gates(ci, _):
        base = pl.multiple_of(ci * rc, rc)
        for k in range(nk):
            lanes = slice(k * LANES, (k + 1) * LANES)
            cw, cb, c4 = cw_all[:, lanes], cb_all[:, lanes], c4_all[:, lanes]
            xm = xr_ref[pl.ds(base, rc), lanes]
            if chunks_per_seq > 1:
                cs = ci % chunks_per_seq
                p0 = pl.multiple_of(jnp.maximum(base - SUBLANES, 0), SUBLANES)
                n0 = pl.multiple_of(jnp.minimum(base + rc, seq * nsb - SUBLANES), SUBLANES)
                prev = jnp.where(cs == 0, 0.0, xr_ref[pl.ds(p0, SUBLANES), lanes])
                nxt = jnp.where(cs == chunks_per_seq - 1, 0.0, xr_ref[pl.ds(n0, SUBLANES), lanes])
                xc = conv_segment(xm, prev, nxt, cw, cb)
            else:
                zeros = jnp.zeros((SUBLANES, LANES), F32)
                xc = jnp.concatenate(
                    [conv_segment(xm[j * seg:(j + 1) * seg], zeros, zeros, cw, cb) for j in range(segs)],
                    axis=0)
            g = jnp.dot(xc.astype(BF16), wg_ref[k], preferred_element_type=F32) + bg_ref[k]
            hxc = 0.5 * xc
            for d in range(2):
                th_r = jnp.tanh(g[:, (2 * d) * LANES:(2 * d + 1) * LANES])
                th_i = jnp.tanh(g[:, (2 * d + 1) * LANES:(2 * d + 2) * LANES])
                log_a = c4[d:d + 1] * th_r + c4[d:d + 1]
                a = jnp.exp(log_a)
                om = jnp.tanh(log_a) * (-1.0 - a * a)
                root = jnp.where(om > 0.0, om * lax.rsqrt(om), 0.0)
                b = root * ((th_i + 1.0) * hxc)
                for j in range(segs):
                    dst = scratch_row(ci, j)
                    a_sc[d * nk + k, pl.ds(dst, seg), :] = a[j * seg:(j + 1) * seg]
                    b_sc[d * nk + k, pl.ds(dst, seg), :] = b[j * seg:(j + 1) * seg]
        return 0

    lax.fori_loop(0, n_rc, gates, 0)

    def scan_sequence(s, _):
        off = s * sp_rows

        def rows(t):
            return pl.ds(off + t, SUBLANES, stride=pitch)

        def local(t, carry):
            out = []
            for slab, (h, p) in enumerate(carry):
                tt = t if slab < nk else pitch - 1 - t
                a = a_sc[slab, rows(tt), :]
                h = a * h + b_sc[slab, rows(tt), :]
                p = a * p
                h_sc[slab, rows(tt), :] = h
                p_sc[slab, rows(tt), :] = p
                out.append((h, p))
            return tuple(out)

        zero = jnp.zeros((SUBLANES, LANES), F32)
        one = jnp.ones((SUBLANES, LANES), F32)
        ends = lax.fori_loop(0, pitch, local, ((zero, one),) * (2 * nk), unroll=unroll)

        h0 = h0_ref[s]
        starts = []
        finals = [[None] * nk, [None] * nk]
        for slab, (h, p) in enumerate(ends):
            d, k = divmod(slab, nk)
            c = h0[d:d + 1, k * LANES:(k + 1) * LANES]
            cs = [None] * SUBLANES
            for j in (range(SUBLANES) if d == 0 else reversed(range(SUBLANES))):
                cs[j] = c
                c = h[j:j + 1] + p[j:j + 1] * c
            finals[d][k] = c
            starts.append(jnp.concatenate(cs, axis=0))
        st_ref[s] = jnp.concatenate(
            [jnp.concatenate(finals[0], axis=1), jnp.concatenate(finals[1], axis=1)], axis=0)

        def fix(t, _):
            for k in range(nk):
                a_sc[k, rows(t), :] = (h_sc[k, rows(t), :] + p_sc[k, rows(t), :] * starts[k]
                                       + h_sc[nk + k, rows(t), :] + p_sc[nk + k, rows(t), :] * starts[nk + k])
            return 0

        lax.fori_loop(0, pitch, fix, 0, unroll=unroll)
        return 0

    if nsb == 1:
        scan_sequence(0, 0)
    else:
        lax.fori_loop(0, nsb, scan_sequence, 0)

    def gate_out(ci, _):
        base = pl.multiple_of(ci * rc, rc)
        for k in range(nk):
            lanes = slice(k * LANES, (k + 1) * LANES)
            gy = _gelu(yr_ref[pl.ds(base, rc), lanes])
            for j in range(segs):
                src = scratch_row(ci, j)
                o_ref[pl.ds(base + j * seg, seg), lanes] = (gy[j * seg:(j + 1) * seg]
                                                            * a_sc[k, pl.ds(src, seg), :])
        return 0

    lax.fori_loop(0, n_rc, gate_out, 0)


def _lru(xy, conv_w, conv_b, wg, bg, lam, h0, *, row_start, m, seq, nsb):
    nblk = m // (seq * nsb)
    assert row_start % (seq * nsb) == 0
    blk0 = row_start // (seq * nsb)
    nk = 2
    width = nk * LANES
    ncb = LRU_WIDTH // width
    pitch = _scan_pitch(seq)
    sc_rows = nsb * SUBLANES * pitch
    rows = seq * nsb
    return pl.pallas_call(
        functools.partial(_lru_kernel, seq=seq, nsb=nsb, nk=nk, pitch=pitch, rc=512, unroll=4),
        grid=(ncb, nblk),
        in_specs=[
            pl.BlockSpec((rows, width), lambda j, b: (blk0 + b, j)),
            pl.BlockSpec((rows, width), lambda j, b: (blk0 + b, ncb + j)),
            pl.BlockSpec((4, width), lambda j, b: (0, j)),
            pl.BlockSpec((1, width), lambda j, b: (0, j)),
            pl.BlockSpec((nk, LANES, 4 * LANES), lambda j, b: (j, 0, 0)),
            pl.BlockSpec((nk, 1, 4 * LANES), lambda j, b: (j, 0, 0)),
            pl.BlockSpec((2, width), lambda j, b: (0, j)),
            pl.BlockSpec((nsb, 2, width), lambda j, b: (b, 0, j)),
        ],
        out_specs=[
            pl.BlockSpec((rows, width), lambda j, b: (b, j)),
            pl.BlockSpec((nsb, 2, width), lambda j, b: (b, 0, j)),
        ],
        out_shape=[
            jax.ShapeDtypeStruct((m, LRU_WIDTH), F32),
            jax.ShapeDtypeStruct((m // seq, 2, LRU_WIDTH), F32),
        ],
        scratch_shapes=[pltpu.VMEM((2 * nk, sc_rows, LANES), F32)] * 4,
        compiler_params=pltpu.CompilerParams(
            dimension_semantics=("arbitrary", "arbitrary"), vmem_limit_bytes=48 * 1024 * 1024),
        name="rglru",
    )(xy, xy, conv_w, conv_b, wg, bg, lam, h0)


GROUP = SUBLANES * SUBLANES


def _swap_rows(slab_ref, val, r0):
    n = val.shape[0]
    for k in range(D_MODEL // LANES):
        slab_ref[k, r0:r0 + n, :] = val[:, k * LANES:(k + 1) * LANES]
    cols = []
    for k in range(D_MODEL // LANES):
        rows = [slab_ref[k, pl.ds(r0 + g * GROUP + t, SUBLANES, stride=SUBLANES), :]
                for g in range(n // GROUP) for t in range(SUBLANES)]
        cols.append(jnp.concatenate(rows, axis=0))
    return jnp.concatenate(cols, axis=1)


def _ffn_tile(x_ref, ol_ref, os_ref, mod_ref, gl_ref, gs_ref, wout_ref, n2_ref, up_ref,
              cw_ref, cb_ref, down_ref, fn_ref, y_ref, slab_sc, *, tm, fc, es, period, row):
    g1 = _mod_row(mod_ref, 2, row)
    sh2 = _mod_row(mod_ref, 3, row)
    sc2 = _mod_row(mod_ref, 4, row)
    g2 = _mod_row(mod_ref, 5, row)
    nl = _rms(ol_ref[...], gl_ref[...]).astype(BF16)
    ns = _rms(os_ref[...], gs_ref[...]).astype(BF16)
    o = (jnp.dot(nl, wout_ref[0:LRU_WIDTH, :], preferred_element_type=F32)
         + jnp.dot(ns, wout_ref[LRU_WIDTH:LRU_WIDTH + SGU_WIDTH, :], preferred_element_type=F32))
    x1 = x_ref[...] + g1 * o
    h2f = _rms(x1, n2_ref[...] * (1.0 + sc2)) + sh2
    h2 = _swap_rows(slab_sc, h2f, 0).astype(BF16)

    sub = lax.broadcasted_iota(jnp.int32, (SUBLANES, fc), 0)
    groups_per_period = period // GROUP
    n_groups = tm // GROUP

    def conv(z, c0):
        w = cw_ref[:, c0:c0 + fc]
        b = cb_ref[:, c0:c0 + fc]
        w0, w1, w2 = w[0:1], w[1:2], w[2:3]
        z8 = [[z[g * GROUP + t * SUBLANES:g * GROUP + (t + 1) * SUBLANES] for t in range(SUBLANES)]
              for g in range(n_groups)]
        outs = []
        for g in range(n_groups):
            cur = z8[g]
            lo = pltpu.roll(cur[SUBLANES - 1], 1, axis=0)
            if g % groups_per_period == 0:
                lo = jnp.where(sub == 0, 0.0, lo)
            else:
                lo = jnp.where(sub == 0, pltpu.roll(z8[g - 1][SUBLANES - 1], 1, axis=0), lo)
            hi = pltpu.roll(cur[0], SUBLANES - 1, axis=0)
            if g % groups_per_period == groups_per_period - 1:
                hi = jnp.where(sub == SUBLANES - 1, 0.0, hi)
            else:
                hi = jnp.where(sub == SUBLANES - 1, pltpu.roll(z8[g + 1][0], SUBLANES - 1, axis=0), hi)
            zm = [lo] + cur[:SUBLANES - 1]
            zp = cur[1:] + [hi]
            for t in range(SUBLANES):
                outs.append(b + w0 * zm[t] + w1 * cur[t] + w2 * zp[t])
        return jnp.concatenate(outs, axis=0)

    def up(f0):
        zg = jnp.dot(h2, up_ref[:, f0:f0 + fc], preferred_element_type=F32)
        zv = jnp.dot(h2, up_ref[:, D_FF + f0:D_FF + f0 + fc], preferred_element_type=F32)
        return zg, zv

    acc = jnp.zeros((tm, D_MODEL), F32)
    nxt = up(0)
    for f0 in range(0, D_FF, fc):
        zg, zv = nxt
        if f0 + fc < D_FF:
            nxt = up(f0 + fc)
        hg = 0.5 * conv(zg, f0)
        act = (hg * (jnp.tanh(hg) + 1.0) * conv(zv, D_FF + f0)).astype(BF16)
        if f0 + fc < D_FF:
            acc = acc + jnp.dot(act, down_ref[f0:f0 + fc, :], preferred_element_type=F32)
        else:
            for r0 in range(0, tm, es):
                a = acc[r0:r0 + es] + jnp.dot(act[r0:r0 + es], down_ref[f0:f0 + fc, :],
                                              preferred_element_type=F32)
                x2 = x1[r0:r0 + es] + g2 * _swap_rows(slab_sc, a, r0)
                y_ref[r0:r0 + es, :] = _rms(x2, fn_ref[...])


def _ffn_kernel(xc_ref, xl_ref, olc_ref, oll_ref, os_ref, mod_ref, gl_ref, gs_ref, wout_ref, n2_ref,
                up_ref, cw_ref, cb_ref, down_ref, fn_ref, yc_ref, yl_ref, slab_sc, *,
                nc, lat_mod_row, ctx_period, lat_period, **tile):
    i = pl.program_id(0)
    shared = (mod_ref, gl_ref, gs_ref, wout_ref, n2_ref, up_ref, cw_ref, cb_ref, down_ref, fn_ref)

    @pl.when(i < nc)
    def _():
        _ffn_tile(xc_ref, olc_ref, os_ref, *shared, yc_ref, slab_sc, period=ctx_period, row=0, **tile)

    @pl.when(i >= nc)
    def _():
        _ffn_tile(xl_ref, oll_ref, os_ref, *shared, yl_ref, slab_sc, period=lat_period,
                  row=lat_mod_row(i), **tile)


def _ffn(xc, xl, olru_c, olru_l, osgu, mod, g_lru, g_sgu, w_out, norm2, ffn_up, conv_w, conv_b,
         ffn_down, final_norm, *, tm, ctx_period, lat_period, lat_seq):
    nc, n, ctx_map, lat_map, lat_mod_row = _token_groups(tm, xc.shape[0], xl.shape[0], lat_seq)
    const = lambda i: (0, 0)
    resident = lambda shape: pl.BlockSpec(shape, const, pipeline_mode=pl.Buffered(1))
    return pl.pallas_call(
        functools.partial(_ffn_kernel, nc=nc, lat_mod_row=lat_mod_row, ctx_period=ctx_period,
                          lat_period=lat_period, tm=tm, fc=512, es=128),
        grid=(n,),
        in_specs=[
            pl.BlockSpec((tm, D_MODEL), ctx_map),
            pl.BlockSpec((tm, D_MODEL), lat_map),
            pl.BlockSpec((tm, LRU_WIDTH), ctx_map),
            pl.BlockSpec((tm, LRU_WIDTH), lat_map),
            pl.BlockSpec((tm, SGU_WIDTH), lambda i: (i, 0)),
            pl.BlockSpec((N_MOD, MOD_ROWS, D_MODEL), lambda i: (0, 0, 0)),
            pl.BlockSpec((1, LRU_WIDTH), const),
            pl.BlockSpec((1, SGU_WIDTH), const),
            resident((LRU_WIDTH + SGU_WIDTH, D_MODEL)),
            pl.BlockSpec((1, D_MODEL), const),
            resident((D_MODEL, 2 * D_FF)),
            pl.BlockSpec((3, 2 * D_FF), const),
            pl.BlockSpec((1, 2 * D_FF), const),
            resident((D_FF, D_MODEL)),
            pl.BlockSpec((1, D_MODEL), const),
        ],
        out_specs=[pl.BlockSpec((tm, D_MODEL), ctx_map), pl.BlockSpec((tm, D_MODEL), lat_map)],
        out_shape=[jax.ShapeDtypeStruct(xc.shape, F32), jax.ShapeDtypeStruct(xl.shape, F32)],
        scratch_shapes=[pltpu.VMEM((D_MODEL // LANES, tm, LANES), F32)],
        compiler_params=pltpu.CompilerParams(
            dimension_semantics=("arbitrary",), vmem_limit_bytes=60 * 1024 * 1024),
        name="outproj_ffn",
    )(xc, xl, olru_c, olru_l, osgu, mod, g_lru, g_sgu, w_out, norm2, ffn_up, conv_w, conv_b, ffn_down,
      final_norm)


def _block_diag_pairs(w):
    nd = w.shape[0]
    wp = w.reshape(nd, LRU_HEADS // 2, 2, LRU_HEAD_DIM, LRU_HEAD_DIM)
    z = jnp.zeros_like(wp[:, :, 0])
    top = jnp.concatenate([wp[:, :, 0], z], axis=-1)
    bot = jnp.concatenate([z, wp[:, :, 1]], axis=-1)
    return jnp.concatenate([top, bot], axis=-2)


def kernel(x_prompt, x_sample, state_lru, c, c_ctx, norm1, norm2, w_ada, b_ada, w_in, lru_conv_w,
           lru_conv_b, lru_wa, lru_ba, lru_wx, lru_bx, lru_lam, sgu_ws, sgu_bs, g_lru, g_sgu, w_out,
           ffn_up, ffn_conv_w, ffn_conv_b, ffn_down, final_norm):
    batch, seq, _ = x_prompt.shape
    dec_batch, dec_seq, _ = x_sample.shape
    depth = norm1.shape[0]
    assert depth == 1, "the final norm is fused into the (single) layer's last kernel"
    nlb = LRU_WIDTH // LANES
    l = 0
    tm = 512

    xc = x_prompt.reshape(batch * seq, D_MODEL)
    xl = x_sample.reshape(dec_batch * dec_seq, D_MODEL)
    n_ctx, n_lat = xc.shape[0], xl.shape[0]
    cc = jnp.concatenate(
        [c_ctx[None, :], c, jnp.zeros((MOD_ROWS - 1 - dec_batch, D_MODEL), F32)], axis=0)
    zeros_state = jnp.zeros((batch, 2, LRU_WIDTH), x_prompt.dtype)

    mod = _modulation(cc, w_ada[l], b_ada[l][None, :])
    bd_a = _block_diag_pairs(0.5 * lru_wa[l])
    bd_x = _block_diag_pairs(0.5 * lru_wx[l])
    wg = jnp.concatenate([bd_a[0], bd_x[0], bd_a[1], bd_x[1]], axis=-1).astype(BF16)
    ba = (0.5 * lru_ba[l]).reshape(2, nlb, 1, LANES)
    bx = (0.5 * lru_bx[l]).reshape(2, nlb, 1, LANES)
    bg = jnp.concatenate([ba[0], bx[0], ba[1], bx[1]], axis=-1)
    lru_params = (lru_conv_w[l], lru_conv_b[l][None, :], wg, bg, lru_lam[l])

    xy, osgu = _inproj(xc, xl, mod, norm1[l][None, :], w_in[l].astype(BF16), sgu_ws[l].astype(BF16),
                       sgu_bs[l].T, tm=tm, lat_seq=dec_seq)
    olru_c, new_state = _lru(xy, *lru_params, zeros_state, row_start=0, m=n_ctx, seq=seq, nsb=8)
    olru_l, _ = _lru(xy, *lru_params, state_lru[:, l], row_start=n_ctx, m=n_lat, seq=dec_seq, nsb=1)
    yc, yl = _ffn(xc, xl, olru_c, olru_l, osgu, mod, g_lru[l][None, :], g_sgu[l][None, :],
                  w_out[l].astype(BF16), norm2[l][None, :], ffn_up[l].astype(BF16), ffn_conv_w[l],
                  ffn_conv_b[l][None, :], ffn_down[l].astype(BF16), final_norm[None, :], tm=tm,
                  ctx_period=seq, lat_period=GRID_W, lat_seq=dec_seq)
    return (yc.reshape(batch, seq, D_MODEL), yl.reshape(dec_batch, dec_seq, D_MODEL),
            new_state[:, None])
```

```python
import functools
import math

import jax
import jax.numpy as jnp
from jax import lax
from jax.experimental import pallas as pl
from jax.experimental.pallas import tpu as pltpu

D_MODEL = 1024
LRU_HEADS = 8
LRU_WIDTH = 512
LRU_HEAD_DIM = 64
LRU_C = 8.0
SGU_GROUPS = 4
SGU_WIDTH = 512
CHUNK = 128
D_FF = 3072
N_MOD = 6
EPS = 1e-6
GRID_W = 64

LANES = 128
SUBLANES = 8
MOD_ROWS = 8
BF16 = jnp.bfloat16
F32 = jnp.float32

_GELU_C0 = math.sqrt(2.0 / math.pi)
_GELU_C1 = _GELU_C0 * 0.044715


def _rms(x, gain):
    return x * lax.rsqrt(jnp.mean(x * x, axis=-1, keepdims=True) + EPS) * gain


def _gelu(x):
    return (0.5 * x) * (1.0 + jnp.tanh(x * (_GELU_C0 + _GELU_C1 * (x * x))))


def _mod_row(mod_ref, k, row):
    return mod_ref[k, pl.ds(row, 1), :]


def _mod_kernel(c_ref, w_ref, b_ref, o_ref):
    s = jax.nn.silu(c_ref[...]).astype(BF16)
    o_ref[...] = jnp.dot(s, w_ref[...].astype(BF16), preferred_element_type=F32) + b_ref[...]


def _modulation(cc, w_ada, b_ada):
    return pl.pallas_call(
        _mod_kernel,
        grid=(N_MOD,),
        in_specs=[
            pl.BlockSpec((MOD_ROWS, D_MODEL), lambda j: (0, 0)),
            pl.BlockSpec((D_MODEL, D_MODEL), lambda j: (0, j)),
            pl.BlockSpec((1, D_MODEL), lambda j: (0, j)),
        ],
        out_specs=pl.BlockSpec((None, MOD_ROWS, D_MODEL), lambda j: (j, 0, 0)),
        out_shape=jax.ShapeDtypeStruct((N_MOD, MOD_ROWS, D_MODEL), F32),
        compiler_params=pltpu.CompilerParams(dimension_semantics=("arbitrary",)),
        name="modulation",
    )(cc, w_ada, b_ada)


def _token_groups(tm, n_ctx, n_lat, lat_seq):
    assert n_ctx % tm == 0 and n_lat % tm == 0 and lat_seq % tm == 0
    nc = n_ctx // tm
    ctx_map = lambda i: (jnp.minimum(i, nc - 1), 0)
    lat_map = lambda i: (jnp.maximum(i - nc, 0), 0)
    lat_mod_row = lambda i: 1 + ((i - nc) * tm) // lat_seq
    return nc, nc + n_lat // tm, ctx_map, lat_map, lat_mod_row


def _inproj_kernel(xc_ref, xl_ref, mod_ref, n1_ref, win_ref, ws_ref, bs_ref, wo_ref, up_ref, dn_ref,
                   xy_ref, osgu_ref, wo_b_ref, up_b_ref, dn_b_ref, *, tm, nc, lat_mod_row):
    i = pl.program_id(0)
    wo_b_ref[...] = wo_ref[...].astype(BF16)
    up_b_ref[...] = up_ref[...].astype(BF16)
    dn_b_ref[...] = dn_ref[...].astype(BF16)

    def body(x_ref, row):
        sh1 = _mod_row(mod_ref, 0, row)
        sc1 = _mod_row(mod_ref, 1, row)
        hb = (_rms(x_ref[...], n1_ref[...] * (1.0 + sc1)) + sh1).astype(BF16)
        uv = jnp.dot(hb, win_ref[:, 2 * LRU_WIDTH:], preferred_element_type=F32)
        xy_ref[...] = jnp.dot(hb, win_ref[:, 0:2 * LRU_WIDTH], preferred_element_type=F32)
        for g in range(SGU_GROUPS):
            lo = g * LANES
            gu = _gelu(uv[:, lo:lo + LANES])
            gv = _gelu(uv[:, SGU_WIDTH + lo:SGU_WIDTH + lo + LANES]).astype(BF16)
            bias = jnp.broadcast_to(bs_ref[:, g:g + 1], (CHUNK, LANES))
            for ck in range(tm // CHUNK):
                r0 = ck * CHUNK
                s = jnp.dot(ws_ref[g], gv[r0:r0 + CHUNK], preferred_element_type=F32) + bias
                osgu_ref[r0:r0 + CHUNK, lo:lo + LANES] = gu[r0:r0 + CHUNK] * s

    @pl.when(i < nc)
    def _():
        body(xc_ref, 0)

    @pl.when(i >= nc)
    def _():
        body(xl_ref, lat_mod_row(i))


BF16_SUBLANES = 16


def _cast_slice_spec(shape, n_steps):
    rows = shape[0]
    r = BF16_SUBLANES
    while rows % r or rows // r > n_steps:
        r += BF16_SUBLANES
    last = rows // r - 1
    return pl.BlockSpec((r, shape[1]), lambda i: (jnp.minimum(i, last), 0))


def _inproj(xc, xl, mod, norm1, w_in, ws, bs_t, later_weights, *, tm, lat_seq):
    nc, n, ctx_map, lat_map, lat_mod_row = _token_groups(tm, xc.shape[0], xl.shape[0], lat_seq)
    m = n * tm
    const = lambda i: (0, 0)
    cast_specs = [_cast_slice_spec(w.shape, n) for w in later_weights]
    return pl.pallas_call(
        functools.partial(_inproj_kernel, tm=tm, nc=nc, lat_mod_row=lat_mod_row),
        grid=(n,),
        in_specs=[
            pl.BlockSpec((tm, D_MODEL), ctx_map),
            pl.BlockSpec((tm, D_MODEL), lat_map),
            pl.BlockSpec((N_MOD, MOD_ROWS, D_MODEL), lambda i: (0, 0, 0)),
            pl.BlockSpec((1, D_MODEL), const),
            pl.BlockSpec((D_MODEL, 4 * LRU_WIDTH), const),
            pl.BlockSpec((SGU_GROUPS, CHUNK, CHUNK), lambda i: (0, 0, 0)),
            pl.BlockSpec((CHUNK, SGU_GROUPS), const),
        ] + cast_specs,
        out_specs=[
            pl.BlockSpec((tm, 2 * LRU_WIDTH), lambda i: (i, 0)),
            pl.BlockSpec((tm, SGU_WIDTH), lambda i: (i, 0)),
        ] + cast_specs,
        out_shape=[
            jax.ShapeDtypeStruct((m, 2 * LRU_WIDTH), F32),
            jax.ShapeDtypeStruct((m, SGU_WIDTH), F32),
        ] + [jax.ShapeDtypeStruct(w.shape, BF16) for w in later_weights],
        compiler_params=pltpu.CompilerParams(
            dimension_semantics=("arbitrary",), vmem_limit_bytes=48 * 1024 * 1024),
        name="inproj_sgu",
    )(xc, xl, mod, norm1, w_in, ws, bs_t, *later_weights)


def _scan_pitch(seq):
    pitch = seq // SUBLANES
    return pitch + (4 - pitch % 8) % 8


def _lru_kernel(xr_ref, yr_ref, cw_ref, cb_ref, wg_ref, bg_ref, lam_ref, h0_ref,
                o_ref, st_ref, a_sc, b_sc, p_sc, h_sc, *, seq, nsb, nk, pitch, rc, unroll):
    sp_rows = SUBLANES * pitch
    n_rc = (seq * nsb) // rc
    seg = min(seq, rc)
    segs = rc // seg
    chunks_per_seq = seq // seg
    c4_all = (-0.5 * LRU_C) * jax.nn.softplus(-lam_ref[...])
    cw_all = cw_ref[...]
    cb_all = cb_ref[...]

    for s in range(nsb):
        lo, hi = s * sp_rows + seq, (s + 1) * sp_rows
        for slab in range(2 * nk):
            a_sc[slab, lo:hi, :] = jnp.ones((hi - lo, LANES), F32)
            b_sc[slab, lo:hi, :] = jnp.zeros((hi - lo, LANES), F32)

    def scratch_row(ci, k):
        if chunks_per_seq > 1:
            return pl.multiple_of((ci // chunks_per_seq) * sp_rows + (ci % chunks_per_seq) * seg,
                                  SUBLANES)
        return pl.multiple_of((ci * segs + k) * sp_rows, SUBLANES)

    def conv_segment(xs, prev, nxt, cw, cb):
        ext = jnp.concatenate([prev, xs, nxt], axis=0)
        n_ext = seg + 2 * SUBLANES
        xc = cb + cw[2:3] * xs
        xc = xc + cw[0:1] * pltpu.roll(ext, 2, axis=0)[SUBLANES:SUBLANES + seg]
        xc = xc + cw[1:2] * pltpu.roll(ext, 1, axis=0)[SUBLANES:SUBLANES + seg]
        return xc + cw[3:4] * pltpu.roll(ext, n_ext - 1, axis=0)[SUBLANES:SUBLANES + seg]

    def gates(ci, _):
        base = pl.multiple_of(ci * rc, rc)
        for k in range(nk):
            lanes = slice(k * LANES, (k + 1) * LANES)
            cw, cb, c4 = cw_all[:, lanes], cb_all[:, lanes], c4_all[:, lanes]
            xm = xr_ref[pl.ds(base, rc), lanes]
            if chunks_per_seq > 1:
                cs = ci % chunks_per_seq
                p0 = pl.multiple_of(jnp.maximum(base - SUBLANES, 0), SUBLANES)
                n0 = pl.multiple_of(jnp.minimum(base + rc, seq * nsb - SUBLANES), SUBLANES)
                prev = jnp.where(cs == 0, 0.0, xr_ref[pl.ds(p0, SUBLANES), lanes])
                nxt = jnp.where(cs == chunks_per_seq - 1, 0.0, xr_ref[pl.ds(n0, SUBLANES), lanes])
                xc = conv_segment(xm, prev, nxt, cw, cb)
            else:
                zeros = jnp.zeros((SUBLANES, LANES), F32)
                xc = jnp.concatenate(
                    [conv_segment(xm[j * seg:(j + 1) * seg], zeros, zeros, cw, cb) for j in range(segs)],
                    axis=0)
            g = jnp.dot(xc.astype(BF16), wg_ref[k], preferred_element_type=F32) + bg_ref[k]
            hxc = 0.5 * xc
            for d in range(2):
                th_r = jnp.tanh(g[:, (2 * d) * LANES:(2 * d + 1) * LANES])
                th_i = jnp.tanh(g[:, (2 * d + 1) * LANES:(2 * d + 2) * LANES])
                log_a = c4[d:d + 1] * th_r + c4[d:d + 1]
                a = jnp.exp(log_a)
                om = jnp.tanh(log_a) * (-1.0 - a * a)
                root = jnp.where(om > 0.0, om * lax.rsqrt(om), 0.0)
                b = root * ((th_i + 1.0) * hxc)
                for j in range(segs):
                    dst = scratch_row(ci, j)
                    a_sc[d * nk + k, pl.ds(dst, seg), :] = a[j * seg:(j + 1) * seg]
                    b_sc[d * nk + k, pl.ds(dst, seg), :] = b[j * seg:(j + 1) * seg]
        return 0

    lax.fori_loop(0, n_rc, gates, 0)

    def scan_sequence(s, _):
        off = s * sp_rows

        def rows(t):
            return pl.ds(off + t, SUBLANES, stride=pitch)

        def local(t, carry):
            out = []
            for slab, (h, p) in enumerate(carry):
                tt = t if slab < nk else pitch - 1 - t
                a = a_sc[slab, rows(tt), :]
                h = a * h + b_sc[slab, rows(tt), :]
                p = a * p
                h_sc[slab, rows(tt), :] = h
                p_sc[slab, rows(tt), :] = p
                out.append((h, p))
            return tuple(out)

        zero = jnp.zeros((SUBLANES, LANES), F32)
        one = jnp.ones((SUBLANES, LANES), F32)
        ends = lax.fori_loop(0, pitch, local, ((zero, one),) * (2 * nk), unroll=unroll)

        h0 = h0_ref[s]
        starts = []
        finals = [[None] * nk, [None] * nk]
        for slab, (h, p) in enumerate(ends):
            d, k = divmod(slab, nk)
            c = h0[d:d + 1, k * LANES:(k + 1) * LANES]
            cs = [None] * SUBLANES
            for j in (range(SUBLANES) if d == 0 else reversed(range(SUBLANES))):
                cs[j] = c
                c = h[j:j + 1] + p[j:j + 1] * c
            finals[d][k] = c
            starts.append(jnp.concatenate(cs, axis=0))
        st_ref[s] = jnp.concatenate(
            [jnp.concatenate(finals[0], axis=1), jnp.concatenate(finals[1], axis=1)], axis=0)

        def fix(t, _):
            for k in range(nk):
                a_sc[k, rows(t), :] = (h_sc[k, rows(t), :] + p_sc[k, rows(t), :] * starts[k]
                                       + h_sc[nk + k, rows(t), :] + p_sc[nk + k, rows(t), :] * starts[nk + k])
            return 0

        lax.fori_loop(0, pitch, fix, 0, unroll=unroll)
        return 0

    if nsb == 1:
        scan_sequence(0, 0)
    else:
        lax.fori_loop(0, nsb, scan_sequence, 0)

    def gate_out(ci, _):
        base = pl.multiple_of(ci * rc, rc)
        for k in range(nk):
            lanes = slice(k * LANES, (k + 1) * LANES)
            gy = _gelu(yr_ref[pl.ds(base, rc), lanes])
            for j in range(segs):
                src = scratch_row(ci, j)
                o_ref[pl.ds(base + j * seg, seg), lanes] = (gy[j * seg:(j + 1) * seg]
                                                            * a_sc[k, pl.ds(src, seg), :])
        return 0

    lax.fori_loop(0, n_rc, gate_out, 0)


def _lru(xy, conv_w, conv_b, wg, bg, lam, h0, *, row_start, m, seq, nsb):
    nblk = m // (seq * nsb)
    assert row_start % (seq * nsb) == 0
    blk0 = row_start // (seq * nsb)
    nk = 2
    width = nk * LANES
    ncb = LRU_WIDTH // width
    pitch = _scan_pitch(seq)
    sc_rows = nsb * SUBLANES * pitch
    rows = seq * nsb
    return pl.pallas_call(
        functools.partial(_lru_kernel, seq=seq, nsb=nsb, nk=nk, pitch=pitch, rc=512, unroll=4),
        grid=(ncb, nblk),
        in_specs=[
            pl.BlockSpec((rows, width), lambda j, b: (blk0 + b, j)),
            pl.BlockSpec((rows, width), lambda j, b: (blk0 + b, ncb + j)),
            pl.BlockSpec((4, width), lambda j, b: (0, j)),
            pl.BlockSpec((1, width), lambda j, b: (0, j)),
            pl.BlockSpec((nk, LANES, 4 * LANES), lambda j, b: (j, 0, 0)),
            pl.BlockSpec((nk, 1, 4 * LANES), lambda j, b: (j, 0, 0)),
            pl.BlockSpec((2, width), lambda j, b: (0, j)),
            pl.BlockSpec((nsb, 2, width), lambda j, b: (b, 0, j)),
        ],
        out_specs=[
            pl.BlockSpec((rows, width), lambda j, b: (b, j)),
            pl.BlockSpec((nsb, 2, width), lambda j, b: (b, 0, j)),
        ],
        out_shape=[
            jax.ShapeDtypeStruct((m, LRU_WIDTH), F32),
            jax.ShapeDtypeStruct((m // seq, 2, LRU_WIDTH), F32),
        ],
        scratch_shapes=[pltpu.VMEM((2 * nk, sc_rows, LANES), F32)] * 4,
        compiler_params=pltpu.CompilerParams(
            dimension_semantics=("arbitrary", "arbitrary"), vmem_limit_bytes=48 * 1024 * 1024),
        name="rglru",
    )(xy, xy, conv_w, conv_b, wg, bg, lam, h0)


GROUP = SUBLANES * SUBLANES


def _swap_rows(slab_ref, val, r0):
    n = val.shape[0]
    for k in range(D_MODEL // LANES):
        slab_ref[k, r0:r0 + n, :] = val[:, k * LANES:(k + 1) * LANES]
    cols = []
    for k in range(D_MODEL // LANES):
        rows = [slab_ref[k, pl.ds(r0 + g * GROUP + t, SUBLANES, stride=SUBLANES), :]
                for g in range(n // GROUP) for t in range(SUBLANES)]
        cols.append(jnp.concatenate(rows, axis=0))
    return jnp.concatenate(cols, axis=1)


def _ffn_tile(x_ref, ol_ref, os_ref, mod_ref, gl_ref, gs_ref, wout_ref, n2_ref, up_ref,
              cw_ref, cb_ref, down_ref, fn_ref, y_ref, slab_sc, *, tm, fc, es, period, row):
    g1 = _mod_row(mod_ref, 2, row)
    sh2 = _mod_row(mod_ref, 3, row)
    sc2 = _mod_row(mod_ref, 4, row)
    g2 = _mod_row(mod_ref, 5, row)
    nl = _rms(ol_ref[...], gl_ref[...]).astype(BF16)
    ns = _rms(os_ref[...], gs_ref[...]).astype(BF16)
    o = (jnp.dot(nl, wout_ref[0:LRU_WIDTH, :], preferred_element_type=F32)
         + jnp.dot(ns, wout_ref[LRU_WIDTH:LRU_WIDTH + SGU_WIDTH, :], preferred_element_type=F32))
    x1 = x_ref[...] + g1 * o
    h2f = _rms(x1, n2_ref[...] * (1.0 + sc2)) + sh2
    h2 = _swap_rows(slab_sc, h2f, 0).astype(BF16)

    sub = lax.broadcasted_iota(jnp.int32, (SUBLANES, fc), 0)
    groups_per_period = period // GROUP
    n_groups = tm // GROUP

    def conv(z, c0):
        w = cw_ref[:, c0:c0 + fc]
        b = cb_ref[:, c0:c0 + fc]
        w0, w1, w2 = w[0:1], w[1:2], w[2:3]
        z8 = [[z[g * GROUP + t * SUBLANES:g * GROUP + (t + 1) * SUBLANES] for t in range(SUBLANES)]
              for g in range(n_groups)]
        outs = []
        for g in range(n_groups):
            cur = z8[g]
            lo = pltpu.roll(cur[SUBLANES - 1], 1, axis=0)
            if g % groups_per_period == 0:
                lo = jnp.where(sub == 0, 0.0, lo)
            else:
                lo = jnp.where(sub == 0, pltpu.roll(z8[g - 1][SUBLANES - 1], 1, axis=0), lo)
            hi = pltpu.roll(cur[0], SUBLANES - 1, axis=0)
            if g % groups_per_period == groups_per_period - 1:
                hi = jnp.where(sub == SUBLANES - 1, 0.0, hi)
            else:
                hi = jnp.where(sub == SUBLANES - 1, pltpu.roll(z8[g + 1][0], SUBLANES - 1, axis=0), hi)
            zm = [lo] + cur[:SUBLANES - 1]
            zp = cur[1:] + [hi]
            for t in range(SUBLANES):
                outs.append(b + w0 * zm[t] + w1 * cur[t] + w2 * zp[t])
        return jnp.concatenate(outs, axis=0)

    def up(f0):
        zg = jnp.dot(h2, up_ref[:, f0:f0 + fc], preferred_element_type=F32)
        zv = jnp.dot(h2, up_ref[:, D_FF + f0:D_FF + f0 + fc], preferred_element_type=F32)
        return zg, zv

    acc = jnp.zeros((tm, D_MODEL), F32)
    nxt = up(0)
    for f0 in range(0, D_FF, fc):
        zg, zv = nxt
        if f0 + fc < D_FF:
            nxt = up(f0 + fc)
        hg = 0.5 * conv(zg, f0)
        act = (hg * (jnp.tanh(hg) + 1.0) * conv(zv, D_FF + f0)).astype(BF16)
        if f0 + fc < D_FF:
            acc = acc + jnp.dot(act, down_ref[f0:f0 + fc, :], preferred_element_type=F32)
        else:
            for r0 in range(0, tm, es):
                a = acc[r0:r0 + es] + jnp.dot(act[r0:r0 + es], down_ref[f0:f0 + fc, :],
                                              preferred_element_type=F32)
                x2 = x1[r0:r0 + es] + g2 * _swap_rows(slab_sc, a, r0)
                y_ref[r0:r0 + es, :] = _rms(x2, fn_ref[...])


def _ffn_kernel(xc_ref, xl_ref, olc_ref, oll_ref, os_ref, mod_ref, gl_ref, gs_ref, wout_ref, n2_ref,
                up_ref, cw_ref, cb_ref, down_ref, fn_ref, yc_ref, yl_ref, slab_sc, *,
                nc, lat_mod_row, ctx_period, lat_period, **tile):
    i = pl.program_id(0)
    shared = (mod_ref, gl_ref, gs_ref, wout_ref, n2_ref, up_ref, cw_ref, cb_ref, down_ref, fn_ref)

    @pl.when(i < nc)
    def _():
        _ffn_tile(xc_ref, olc_ref, os_ref, *shared, yc_ref, slab_sc, period=ctx_period, row=0, **tile)

    @pl.when(i >= nc)
    def _():
        _ffn_tile(xl_ref, oll_ref, os_ref, *shared, yl_ref, slab_sc, period=lat_period,
                  row=lat_mod_row(i), **tile)


def _ffn(xc, xl, olru_c, olru_l, osgu, mod, g_lru, g_sgu, w_out, norm2, ffn_up, conv_w, conv_b,
         ffn_down, final_norm, *, tm, ctx_period, lat_period, lat_seq):
    nc, n, ctx_map, lat_map, lat_mod_row = _token_groups(tm, xc.shape[0], xl.shape[0], lat_seq)
    const = lambda i: (0, 0)
    resident = lambda shape: pl.BlockSpec(shape, const, pipeline_mode=pl.Buffered(1))
    return pl.pallas_call(
        functools.partial(_ffn_kernel, nc=nc, lat_mod_row=lat_mod_row, ctx_period=ctx_period,
                          lat_period=lat_period, tm=tm, fc=512, es=128),
        grid=(n,),
        in_specs=[
            pl.BlockSpec((tm, D_MODEL), ctx_map),
            pl.BlockSpec((tm, D_MODEL), lat_map),
            pl.BlockSpec((tm, LRU_WIDTH), ctx_map),
            pl.BlockSpec((tm, LRU_WIDTH), lat_map),
            pl.BlockSpec((tm, SGU_WIDTH), lambda i: (i, 0)),
            pl.BlockSpec((N_MOD, MOD_ROWS, D_MODEL), lambda i: (0, 0, 0)),
            pl.BlockSpec((1, LRU_WIDTH), const),
            pl.BlockSpec((1, SGU_WIDTH), const),
            resident((LRU_WIDTH + SGU_WIDTH, D_MODEL)),
            pl.BlockSpec((1, D_MODEL), const),
            resident((D_MODEL, 2 * D_FF)),
            pl.BlockSpec((3, 2 * D_FF), const),
            pl.BlockSpec((1, 2 * D_FF), const),
            resident((D_FF, D_MODEL)),
            pl.BlockSpec((1, D_MODEL), const),
        ],
        out_specs=[pl.BlockSpec((tm, D_MODEL), ctx_map), pl.BlockSpec((tm, D_MODEL), lat_map)],
        out_shape=[jax.ShapeDtypeStruct(xc.shape, F32), jax.ShapeDtypeStruct(xl.shape, F32)],
        scratch_shapes=[pltpu.VMEM((D_MODEL // LANES, tm, LANES), F32)],
        compiler_params=pltpu.CompilerParams(
            dimension_semantics=("arbitrary",), vmem_limit_bytes=60 * 1024 * 1024),
        name="outproj_ffn",
    )(xc, xl, olru_c, olru_l, osgu, mod, g_lru, g_sgu, w_out, norm2, ffn_up, conv_w, conv_b, ffn_down,
      final_norm)


def _block_diag_pairs(w):
    nd = w.shape[0]
    wp = w.reshape(nd, LRU_HEADS // 2, 2, LRU_HEAD_DIM, LRU_HEAD_DIM)
    z = jnp.zeros_like(wp[:, :, 0])
    top = jnp.concatenate([wp[:, :, 0], z], axis=-1)
    bot = jnp.concatenate([z, wp[:, :, 1]], axis=-1)
    return jnp.concatenate([top, bot], axis=-2)


def kernel(x_prompt, x_sample, state_lru, c, c_ctx, norm1, norm2, w_ada, b_ada, w_in, lru_conv_w,
           lru_conv_b, lru_wa, lru_ba, lru_wx, lru_bx, lru_lam, sgu_ws, sgu_bs, g_lru, g_sgu, w_out,
           ffn_up, ffn_conv_w, ffn_conv_b, ffn_down, final_norm):
    batch, seq, _ = x_prompt.shape
    dec_batch, dec_seq, _ = x_sample.shape
    depth = norm1.shape[0]
    assert depth == 1, "the final norm is fused into the (single) layer's last kernel"
    nlb = LRU_WIDTH // LANES
    l = 0
    tm = 512

    xc = x_prompt.reshape(batch * seq, D_MODEL)
    xl = x_sample.reshape(dec_batch * dec_seq, D_MODEL)
    n_ctx, n_lat = xc.shape[0], xl.shape[0]
    cc = jnp.concatenate(
        [c_ctx[None, :], c, jnp.zeros((MOD_ROWS - 1 - dec_batch, D_MODEL), F32)], axis=0)
    zeros_state = jnp.zeros((batch, 2, LRU_WIDTH), x_prompt.dtype)

    mod = _modulation(cc, w_ada[l], b_ada[l][None, :])
    bd_a = _block_diag_pairs(0.5 * lru_wa[l])
    bd_x = _block_diag_pairs(0.5 * lru_wx[l])
    wg = jnp.concatenate([bd_a[0], bd_x[0], bd_a[1], bd_x[1]], axis=-1).astype(BF16)
    ba = (0.5 * lru_ba[l]).reshape(2, nlb, 1, LANES)
    bx = (0.5 * lru_bx[l]).reshape(2, nlb, 1, LANES)
    bg = jnp.concatenate([ba[0], bx[0], ba[1], bx[1]], axis=-1)
    lru_params = (lru_conv_w[l], lru_conv_b[l][None, :], wg, bg, lru_lam[l])

    xy, osgu, w_out_b, up_b, down_b = _inproj(
        xc, xl, mod, norm1[l][None, :], w_in[l].astype(BF16), sgu_ws[l].astype(BF16), sgu_bs[l].T,
        (w_out[l], ffn_up[l], ffn_down[l]), tm=tm, lat_seq=dec_seq)
    olru_c, new_state = _lru(xy, *lru_params, zeros_state, row_start=0, m=n_ctx, seq=seq, nsb=8)
    olru_l, _ = _lru(xy, *lru_params, state_lru[:, l], row_start=n_ctx, m=n_lat, seq=dec_seq, nsb=1)
    yc, yl = _ffn(xc, xl, olru_c, olru_l, osgu, mod, g_lru[l][None, :], g_sgu[l][None, :],
                  w_out_b, norm2[l][None, :], up_b, ffn_conv_w[l],
                  ffn_conv_b[l][None, :], down_b, final_norm[None, :], tm=tm,
                  ctx_period=seq, lat_period=GRID_W, lat_seq=dec_seq)
    return (yc.reshape(batch, seq, D_MODEL), yl.reshape(dec_batch, dec_seq, D_MODEL),
            new_state[:, None])
```

```python
import functools
import math

import jax
import jax.numpy as jnp
from jax import lax
from jax.experimental import pallas as pl
from jax.experimental.pallas import tpu as pltpu

D_MODEL = 1024
LRU_HEADS = 8
LRU_WIDTH = 512
LRU_HEAD_DIM = 64
LRU_C = 8.0
SGU_GROUPS = 4
SGU_WIDTH = 512
CHUNK = 128
D_FF = 3072
N_MOD = 6
EPS = 1e-6
GRID_W = 64

LANES = 128
SUBLANES = 8
MOD_ROWS = 8
BF16 = jnp.bfloat16
F32 = jnp.float32

_GELU_C0 = math.sqrt(2.0 / math.pi)
_GELU_C1 = _GELU_C0 * 0.044715


def _rms(x, gain):
    return x * lax.rsqrt(jnp.mean(x * x, axis=-1, keepdims=True) + EPS) * gain


def _gelu(x):
    return (0.5 * x) * (1.0 + jnp.tanh(x * (_GELU_C0 + _GELU_C1 * (x * x))))


def _mod_row(mod_ref, k, row):
    return mod_ref[k, pl.ds(row, 1), :]


def _mod_kernel(c_ref, w_ref, b_ref, o_ref):
    s = jax.nn.silu(c_ref[...]).astype(BF16)
    o_ref[...] = jnp.dot(s, w_ref[...].astype(BF16), preferred_element_type=F32) + b_ref[...]


def _modulation(cc, w_ada, b_ada):
    return pl.pallas_call(
        _mod_kernel,
        grid=(N_MOD,),
        in_specs=[
            pl.BlockSpec((MOD_ROWS, D_MODEL), lambda j: (0, 0)),
            pl.BlockSpec((D_MODEL, D_MODEL), lambda j: (0, j)),
            pl.BlockSpec((1, D_MODEL), lambda j: (0, j)),
        ],
        out_specs=pl.BlockSpec((None, MOD_ROWS, D_MODEL), lambda j: (j, 0, 0)),
        out_shape=jax.ShapeDtypeStruct((N_MOD, MOD_ROWS, D_MODEL), F32),
        compiler_params=pltpu.CompilerParams(dimension_semantics=("arbitrary",)),
        name="modulation",
    )(cc, w_ada, b_ada)


def _token_groups(tm, n_ctx, n_lat, lat_seq):
    assert n_ctx % tm == 0 and n_lat % tm == 0 and lat_seq % tm == 0
    nc = n_ctx // tm
    ctx_map = lambda i: (jnp.minimum(i, nc - 1), 0)
    lat_map = lambda i: (jnp.maximum(i - nc, 0), 0)
    lat_mod_row = lambda i: 1 + ((i - nc) * tm) // lat_seq
    return nc, nc + n_lat // tm, ctx_map, lat_map, lat_mod_row


def _inproj_kernel(xc_ref, xl_ref, mod_ref, n1_ref, win_ref, ws_ref, bs_ref, w_ref,
                   xy_ref, osgu_ref, w_b_ref, *, tm, nc, lat_mod_row):
    i = pl.program_id(0)
    w_b_ref[...] = w_ref[...].astype(BF16)

    def body(x_ref, row):
        sh1 = _mod_row(mod_ref, 0, row)
        sc1 = _mod_row(mod_ref, 1, row)
        hb = (_rms(x_ref[...], n1_ref[...] * (1.0 + sc1)) + sh1).astype(BF16)
        uv = jnp.dot(hb, win_ref[:, 2 * LRU_WIDTH:], preferred_element_type=F32)
        xy_ref[...] = jnp.dot(hb, win_ref[:, 0:2 * LRU_WIDTH], preferred_element_type=F32)
        for g in range(SGU_GROUPS):
            lo = g * LANES
            gu = _gelu(uv[:, lo:lo + LANES])
            gv = _gelu(uv[:, SGU_WIDTH + lo:SGU_WIDTH + lo + LANES]).astype(BF16)
            bias = jnp.broadcast_to(bs_ref[:, g:g + 1], (CHUNK, LANES))
            for ck in range(tm // CHUNK):
                r0 = ck * CHUNK
                s = jnp.dot(ws_ref[g], gv[r0:r0 + CHUNK], preferred_element_type=F32) + bias
                osgu_ref[r0:r0 + CHUNK, lo:lo + LANES] = gu[r0:r0 + CHUNK] * s

    @pl.when(i < nc)
    def _():
        body(xc_ref, 0)

    @pl.when(i >= nc)
    def _():
        body(xl_ref, lat_mod_row(i))


BF16_SUBLANES = 16


def _cast_slice_spec(shape, grid):
    n_steps = math.prod(grid)
    rows = shape[0]
    r = BF16_SUBLANES
    while rows % r or rows // r > n_steps:
        r += BF16_SUBLANES
    last = rows // r - 1

    def index_map(*idx):
        step = 0
        for i, n in zip(idx, grid):
            step = step * n + i
        return (jnp.minimum(step, last), 0)

    return pl.BlockSpec((r, shape[1]), index_map)


def _inproj(xc, xl, mod, norm1, w_in, ws, bs_t, later_weight, *, tm, lat_seq):
    nc, n, ctx_map, lat_map, lat_mod_row = _token_groups(tm, xc.shape[0], xl.shape[0], lat_seq)
    m = n * tm
    const = lambda i: (0, 0)
    cast_specs = [_cast_slice_spec(later_weight.shape, (n,))]
    return pl.pallas_call(
        functools.partial(_inproj_kernel, tm=tm, nc=nc, lat_mod_row=lat_mod_row),
        grid=(n,),
        in_specs=[
            pl.BlockSpec((tm, D_MODEL), ctx_map),
            pl.BlockSpec((tm, D_MODEL), lat_map),
            pl.BlockSpec((N_MOD, MOD_ROWS, D_MODEL), lambda i: (0, 0, 0)),
            pl.BlockSpec((1, D_MODEL), const),
            pl.BlockSpec((D_MODEL, 4 * LRU_WIDTH), const),
            pl.BlockSpec((SGU_GROUPS, CHUNK, CHUNK), lambda i: (0, 0, 0)),
            pl.BlockSpec((CHUNK, SGU_GROUPS), const),
        ] + cast_specs,
        out_specs=[
            pl.BlockSpec((tm, 2 * LRU_WIDTH), lambda i: (i, 0)),
            pl.BlockSpec((tm, SGU_WIDTH), lambda i: (i, 0)),
        ] + cast_specs,
        out_shape=[
            jax.ShapeDtypeStruct((m, 2 * LRU_WIDTH), F32),
            jax.ShapeDtypeStruct((m, SGU_WIDTH), F32),
        ] + [jax.ShapeDtypeStruct(later_weight.shape, BF16)],
        compiler_params=pltpu.CompilerParams(
            dimension_semantics=("arbitrary",), vmem_limit_bytes=48 * 1024 * 1024),
        name="inproj_sgu",
    )(xc, xl, mod, norm1, w_in, ws, bs_t, later_weight)


def _scan_pitch(seq):
    pitch = seq // SUBLANES
    return pitch + (4 - pitch % 8) % 8


def _lru_kernel(xr_ref, yr_ref, cw_ref, cb_ref, wg_ref, bg_ref, lam_ref, h0_ref, w_ref,
                o_ref, st_ref, w_b_ref, a_sc, b_sc, p_sc, h_sc, *, seq, nsb, nk, pitch, rc, unroll):
    sp_rows = SUBLANES * pitch
    n_rc = (seq * nsb) // rc
    seg = min(seq, rc)
    segs = rc // seg
    chunks_per_seq = seq // seg
    w_b_ref[...] = w_ref[...].astype(BF16)
    c4_all = (-0.5 * LRU_C) * jax.nn.softplus(-lam_ref[...])
    cw_all = cw_ref[...]
    cb_all = cb_ref[...]

    for s in range(nsb):
        lo, hi = s * sp_rows + seq, (s + 1) * sp_rows
        for slab in range(2 * nk):
            a_sc[slab, lo:hi, :] = jnp.ones((hi - lo, LANES), F32)
            b_sc[slab, lo:hi, :] = jnp.zeros((hi - lo, LANES), F32)

    def scratch_row(ci, k):
        if chunks_per_seq > 1:
            return pl.multiple_of((ci // chunks_per_seq) * sp_rows + (ci % chunks_per_seq) * seg,
                                  SUBLANES)
        return pl.multiple_of((ci * segs + k) * sp_rows, SUBLANES)

    def conv_segment(xs, prev, nxt, cw, cb):
        ext = jnp.concatenate([prev, xs, nxt], axis=0)
        n_ext = seg + 2 * SUBLANES
        xc = cb + cw[2:3] * xs
        xc = xc + cw[0:1] * pltpu.roll(ext, 2, axis=0)[SUBLANES:SUBLANES + seg]
        xc = xc + cw[1:2] * pltpu.roll(ext, 1, axis=0)[SUBLANES:SUBLANES + seg]
        return xc + cw[3:4] * pltpu.roll(ext, n_ext - 1, axis=0)[SUBLANES:SUBLANES + seg]

    def gates(ci, _):
        base = pl.multiple_of(ci * rc, rc)
        for k in range(nk):
            lanes = slice(k * LANES, (k + 1) * LANES)
            cw, cb, c4 = cw_all[:, lanes], cb_all[:, lanes], c4_all[:, lanes]
            xm = xr_ref[pl.ds(base, rc), lanes]
            if chunks_per_seq > 1:
                cs = ci % chunks_per_seq
                p0 = pl.multiple_of(jnp.maximum(base - SUBLANES, 0), SUBLANES)
                n0 = pl.multiple_of(jnp.minimum(base + rc, seq * nsb - SUBLANES), SUBLANES)
                prev = jnp.where(cs == 0, 0.0, xr_ref[pl.ds(p0, SUBLANES), lanes])
                nxt = jnp.where(cs == chunks_per_seq - 1, 0.0, xr_ref[pl.ds(n0, SUBLANES), lanes])
                xc = conv_segment(xm, prev, nxt, cw, cb)
            else:
                zeros = jnp.zeros((SUBLANES, LANES), F32)
                xc = jnp.concatenate(
                    [conv_segment(xm[j * seg:(j + 1) * seg], zeros, zeros, cw, cb) for j in range(segs)],
                    axis=0)
            g = jnp.dot(xc.astype(BF16), wg_ref[k], preferred_element_type=F32) + bg_ref[k]
            hxc = 0.5 * xc
            for d in range(2):
                th_r = jnp.tanh(g[:, (2 * d) * LANES:(2 * d + 1) * LANES])
                th_i = jnp.tanh(g[:, (2 * d + 1) * LANES:(2 * d + 2) * LANES])
                log_a = c4[d:d + 1] * th_r + c4[d:d + 1]
                a = jnp.exp(log_a)
                om = jnp.tanh(log_a) * (-1.0 - a * a)
                root = jnp.where(om > 0.0, om * lax.rsqrt(om), 0.0)
                b = root * ((th_i + 1.0) * hxc)
                for j in range(segs):
                    dst = scratch_row(ci, j)
                    a_sc[d * nk + k, pl.ds(dst, seg), :] = a[j * seg:(j + 1) * seg]
                    b_sc[d * nk + k, pl.ds(dst, seg), :] = b[j * seg:(j + 1) * seg]
        return 0

    lax.fori_loop(0, n_rc, gates, 0)

    def scan_sequence(s, _):
        off = s * sp_rows

        def rows(t):
            return pl.ds(off + t, SUBLANES, stride=pitch)

        def local(t, carry):
            out = []
            for slab, (h, p) in enumerate(carry):
                tt = t if slab < nk else pitch - 1 - t
                a = a_sc[slab, rows(tt), :]
                h = a * h + b_sc[slab, rows(tt), :]
                p = a * p
                h_sc[slab, rows(tt), :] = h
                p_sc[slab, rows(tt), :] = p
                out.append((h, p))
            return tuple(out)

        zero = jnp.zeros((SUBLANES, LANES), F32)
        one = jnp.ones((SUBLANES, LANES), F32)
        ends = lax.fori_loop(0, pitch, local, ((zero, one),) * (2 * nk), unroll=unroll)

        h0 = h0_ref[s]
        starts = []
        finals = [[None] * nk, [None] * nk]
        for slab, (h, p) in enumerate(ends):
            d, k = divmod(slab, nk)
            c = h0[d:d + 1, k * LANES:(k + 1) * LANES]
            cs = [None] * SUBLANES
            for j in (range(SUBLANES) if d == 0 else reversed(range(SUBLANES))):
                cs[j] = c
                c = h[j:j + 1] + p[j:j + 1] * c
            finals[d][k] = c
            starts.append(jnp.concatenate(cs, axis=0))
        st_ref[s] = jnp.concatenate(
            [jnp.concatenate(finals[0], axis=1), jnp.concatenate(finals[1], axis=1)], axis=0)

        def fix(t, _):
            for k in range(nk):
                a_sc[k, rows(t), :] = (h_sc[k, rows(t), :] + p_sc[k, rows(t), :] * starts[k]
                                       + h_sc[nk + k, rows(t), :] + p_sc[nk + k, rows(t), :] * starts[nk + k])
            return 0

        lax.fori_loop(0, pitch, fix, 0, unroll=unroll)
        return 0

    if nsb == 1:
        scan_sequence(0, 0)
    else:
        lax.fori_loop(0, nsb, scan_sequence, 0)

    def gate_out(ci, _):
        base = pl.multiple_of(ci * rc, rc)
        for k in range(nk):
            lanes = slice(k * LANES, (k + 1) * LANES)
            gy = _gelu(yr_ref[pl.ds(base, rc), lanes])
            for j in range(segs):
                src = scratch_row(ci, j)
                o_ref[pl.ds(base + j * seg, seg), lanes] = (gy[j * seg:(j + 1) * seg]
                                                            * a_sc[k, pl.ds(src, seg), :])
        return 0

    lax.fori_loop(0, n_rc, gate_out, 0)


def _lru(xy, conv_w, conv_b, wg, bg, lam, h0, later_weight, *, row_start, m, seq, nsb):
    nblk = m // (seq * nsb)
    assert row_start % (seq * nsb) == 0
    blk0 = row_start // (seq * nsb)
    nk = 2
    width = nk * LANES
    ncb = LRU_WIDTH // width
    pitch = _scan_pitch(seq)
    sc_rows = nsb * SUBLANES * pitch
    rows = seq * nsb
    cast_spec = _cast_slice_spec(later_weight.shape, (ncb, nblk))
    return pl.pallas_call(
        functools.partial(_lru_kernel, seq=seq, nsb=nsb, nk=nk, pitch=pitch, rc=512, unroll=4),
        grid=(ncb, nblk),
        in_specs=[
            pl.BlockSpec((rows, width), lambda j, b: (blk0 + b, j)),
            pl.BlockSpec((rows, width), lambda j, b: (blk0 + b, ncb + j)),
            pl.BlockSpec((4, width), lambda j, b: (0, j)),
            pl.BlockSpec((1, width), lambda j, b: (0, j)),
            pl.BlockSpec((nk, LANES, 4 * LANES), lambda j, b: (j, 0, 0)),
            pl.BlockSpec((nk, 1, 4 * LANES), lambda j, b: (j, 0, 0)),
            pl.BlockSpec((2, width), lambda j, b: (0, j)),
            pl.BlockSpec((nsb, 2, width), lambda j, b: (b, 0, j)),
            cast_spec,
        ],
        out_specs=[
            pl.BlockSpec((rows, width), lambda j, b: (b, j)),
            pl.BlockSpec((nsb, 2, width), lambda j, b: (b, 0, j)),
            cast_spec,
        ],
        out_shape=[
            jax.ShapeDtypeStruct((m, LRU_WIDTH), F32),
            jax.ShapeDtypeStruct((m // seq, 2, LRU_WIDTH), F32),
            jax.ShapeDtypeStruct(later_weight.shape, BF16),
        ],
        scratch_shapes=[pltpu.VMEM((2 * nk, sc_rows, LANES), F32)] * 4,
        compiler_params=pltpu.CompilerParams(
            dimension_semantics=("arbitrary", "arbitrary"), vmem_limit_bytes=56 * 1024 * 1024),
        name="rglru",
    )(xy, xy, conv_w, conv_b, wg, bg, lam, h0, later_weight)


GROUP = SUBLANES * SUBLANES


def _swap_rows(slab_ref, val, r0):
    n = val.shape[0]
    for k in range(D_MODEL // LANES):
        slab_ref[k, r0:r0 + n, :] = val[:, k * LANES:(k + 1) * LANES]
    cols = []
    for k in range(D_MODEL // LANES):
        rows = [slab_ref[k, pl.ds(r0 + g * GROUP + t, SUBLANES, stride=SUBLANES), :]
                for g in range(n // GROUP) for t in range(SUBLANES)]
        cols.append(jnp.concatenate(rows, axis=0))
    return jnp.concatenate(cols, axis=1)


def _ffn_tile(x_ref, ol_ref, os_ref, mod_ref, gl_ref, gs_ref, wout_ref, n2_ref, up_ref,
              cw_ref, cb_ref, down_ref, fn_ref, y_ref, slab_sc, *, tm, fc, es, period, row):
    g1 = _mod_row(mod_ref, 2, row)
    sh2 = _mod_row(mod_ref, 3, row)
    sc2 = _mod_row(mod_ref, 4, row)
    g2 = _mod_row(mod_ref, 5, row)
    nl = _rms(ol_ref[...], gl_ref[...]).astype(BF16)
    ns = _rms(os_ref[...], gs_ref[...]).astype(BF16)
    o = (jnp.dot(nl, wout_ref[0:LRU_WIDTH, :], preferred_element_type=F32)
         + jnp.dot(ns, wout_ref[LRU_WIDTH:LRU_WIDTH + SGU_WIDTH, :], preferred_element_type=F32))
    x1 = x_ref[...] + g1 * o
    h2f = _rms(x1, n2_ref[...] * (1.0 + sc2)) + sh2
    h2 = _swap_rows(slab_sc, h2f, 0).astype(BF16)

    sub = lax.broadcasted_iota(jnp.int32, (SUBLANES, fc), 0)
    groups_per_period = period // GROUP
    n_groups = tm // GROUP

    def conv(z, c0):
        w = cw_ref[:, c0:c0 + fc]
        b = cb_ref[:, c0:c0 + fc]
        w0, w1, w2 = w[0:1], w[1:2], w[2:3]
        z8 = [[z[g * GROUP + t * SUBLANES:g * GROUP + (t + 1) * SUBLANES] for t in range(SUBLANES)]
              for g in range(n_groups)]
        outs = []
        for g in range(n_groups):
            cur = z8[g]
            lo = pltpu.roll(cur[SUBLANES - 1], 1, axis=0)
            if g % groups_per_period == 0:
                lo = jnp.where(sub == 0, 0.0, lo)
            else:
                lo = jnp.where(sub == 0, pltpu.roll(z8[g - 1][SUBLANES - 1], 1, axis=0), lo)
            hi = pltpu.roll(cur[0], SUBLANES - 1, axis=0)
            if g % groups_per_period == groups_per_period - 1:
                hi = jnp.where(sub == SUBLANES - 1, 0.0, hi)
            else:
                hi = jnp.where(sub == SUBLANES - 1, pltpu.roll(z8[g + 1][0], SUBLANES - 1, axis=0), hi)
            zm = [lo] + cur[:SUBLANES - 1]
            zp = cur[1:] + [hi]
            for t in range(SUBLANES):
                outs.append(b + w0 * zm[t] + w1 * cur[t] + w2 * zp[t])
        return jnp.concatenate(outs, axis=0)

    def up(f0):
        zg = jnp.dot(h2, up_ref[:, f0:f0 + fc], preferred_element_type=F32)
        zv = jnp.dot(h2, up_ref[:, D_FF + f0:D_FF + f0 + fc], preferred_element_type=F32)
        return zg, zv

    acc = jnp.zeros((tm, D_MODEL), F32)
    nxt = up(0)
    for f0 in range(0, D_FF, fc):
        zg, zv = nxt
        if f0 + fc < D_FF:
            nxt = up(f0 + fc)
        hg = 0.5 * conv(zg, f0)
        act = (hg * (jnp.tanh(hg) + 1.0) * conv(zv, D_FF + f0)).astype(BF16)
        if f0 + fc < D_FF:
            acc = acc + jnp.dot(act, down_ref[f0:f0 + fc, :], preferred_element_type=F32)
        else:
            for r0 in range(0, tm, es):
                a = acc[r0:r0 + es] + jnp.dot(act[r0:r0 + es], down_ref[f0:f0 + fc, :],
                                              preferred_element_type=F32)
                x2 = x1[r0:r0 + es] + g2 * _swap_rows(slab_sc, a, r0)
                y_ref[r0:r0 + es, :] = _rms(x2, fn_ref[...])


def _ffn_kernel(xc_ref, xl_ref, olc_ref, oll_ref, os_ref, mod_ref, gl_ref, gs_ref, wout_ref, n2_ref,
                up_ref, cw_ref, cb_ref, down_ref, fn_ref, yc_ref, yl_ref, slab_sc, *,
                nc, lat_mod_row, ctx_period, lat_period, **tile):
    i = pl.program_id(0)
    shared = (mod_ref, gl_ref, gs_ref, wout_ref, n2_ref, up_ref, cw_ref, cb_ref, down_ref, fn_ref)

    @pl.when(i < nc)
    def _():
        _ffn_tile(xc_ref, olc_ref, os_ref, *shared, yc_ref, slab_sc, period=ctx_period, row=0, **tile)

    @pl.when(i >= nc)
    def _():
        _ffn_tile(xl_ref, oll_ref, os_ref, *shared, yl_ref, slab_sc, period=lat_period,
                  row=lat_mod_row(i), **tile)


def _ffn(xc, xl, olru_c, olru_l, osgu, mod, g_lru, g_sgu, w_out, norm2, ffn_up, conv_w, conv_b,
         ffn_down, final_norm, *, tm, ctx_period, lat_period, lat_seq):
    nc, n, ctx_map, lat_map, lat_mod_row = _token_groups(tm, xc.shape[0], xl.shape[0], lat_seq)
    const = lambda i: (0, 0)
    resident = lambda shape: pl.BlockSpec(shape, const, pipeline_mode=pl.Buffered(1))
    return pl.pallas_call(
        functools.partial(_ffn_kernel, nc=nc, lat_mod_row=lat_mod_row, ctx_period=ctx_period,
                          lat_period=lat_period, tm=tm, fc=512, es=128),
        grid=(n,),
        in_specs=[
            pl.BlockSpec((tm, D_MODEL), ctx_map),
            pl.BlockSpec((tm, D_MODEL), lat_map),
            pl.BlockSpec((tm, LRU_WIDTH), ctx_map),
            pl.BlockSpec((tm, LRU_WIDTH), lat_map),
            pl.BlockSpec((tm, SGU_WIDTH), lambda i: (i, 0)),
            pl.BlockSpec((N_MOD, MOD_ROWS, D_MODEL), lambda i: (0, 0, 0)),
            pl.BlockSpec((1, LRU_WIDTH), const),
            pl.BlockSpec((1, SGU_WIDTH), const),
            resident((LRU_WIDTH + SGU_WIDTH, D_MODEL)),
            pl.BlockSpec((1, D_MODEL), const),
            resident((D_MODEL, 2 * D_FF)),
            pl.BlockSpec((3, 2 * D_FF), const),
            pl.BlockSpec((1, 2 * D_FF), const),
            resident((D_FF, D_MODEL)),
            pl.BlockSpec((1, D_MODEL), const),
        ],
        out_specs=[pl.BlockSpec((tm, D_MODEL), ctx_map), pl.BlockSpec((tm, D_MODEL), lat_map)],
        out_shape=[jax.ShapeDtypeStruct(xc.shape, F32), jax.ShapeDtypeStruct(xl.shape, F32)],
        scratch_shapes=[pltpu.VMEM((D_MODEL // LANES, tm, LANES), F32)],
        compiler_params=pltpu.CompilerParams(
            dimension_semantics=("arbitrary",), vmem_limit_bytes=60 * 1024 * 1024),
        name="outproj_ffn",
    )(xc, xl, olru_c, olru_l, osgu, mod, g_lru, g_sgu, w_out, norm2, ffn_up, conv_w, conv_b, ffn_down,
      final_norm)


def _block_diag_pairs(w):
    nd = w.shape[0]
    wp = w.reshape(nd, LRU_HEADS // 2, 2, LRU_HEAD_DIM, LRU_HEAD_DIM)
    z = jnp.zeros_like(wp[:, :, 0])
    top = jnp.concatenate([wp[:, :, 0], z], axis=-1)
    bot = jnp.concatenate([z, wp[:, :, 1]], axis=-1)
    return jnp.concatenate([top, bot], axis=-2)


def kernel(x_prompt, x_sample, state_lru, c, c_ctx, norm1, norm2, w_ada, b_ada, w_in, lru_conv_w,
           lru_conv_b, lru_wa, lru_ba, lru_wx, lru_bx, lru_lam, sgu_ws, sgu_bs, g_lru, g_sgu, w_out,
           ffn_up, ffn_conv_w, ffn_conv_b, ffn_down, final_norm):
    batch, seq, _ = x_prompt.shape
    dec_batch, dec_seq, _ = x_sample.shape
    depth = norm1.shape[0]
    assert depth == 1, "the final norm is fused into the (single) layer's last kernel"
    nlb = LRU_WIDTH // LANES
    l = 0
    tm = 512

    xc = x_prompt.reshape(batch * seq, D_MODEL)
    xl = x_sample.reshape(dec_batch * dec_seq, D_MODEL)
    n_ctx, n_lat = xc.shape[0], xl.shape[0]
    cc = jnp.concatenate(
        [c_ctx[None, :], c, jnp.zeros((MOD_ROWS - 1 - dec_batch, D_MODEL), F32)], axis=0)
    zeros_state = jnp.zeros((batch, 2, LRU_WIDTH), x_prompt.dtype)

    mod = _modulation(cc, w_ada[l], b_ada[l][None, :])
    bd_a = _block_diag_pairs(0.5 * lru_wa[l])
    bd_x = _block_diag_pairs(0.5 * lru_wx[l])
    wg = jnp.concatenate([bd_a[0], bd_x[0], bd_a[1], bd_x[1]], axis=-1).astype(BF16)
    ba = (0.5 * lru_ba[l]).reshape(2, nlb, 1, LANES)
    bx = (0.5 * lru_bx[l]).reshape(2, nlb, 1, LANES)
    bg = jnp.concatenate([ba[0], bx[0], ba[1], bx[1]], axis=-1)
    lru_params = (lru_conv_w[l], lru_conv_b[l][None, :], wg, bg, lru_lam[l])

    xy, osgu, w_out_b = _inproj(
        xc, xl, mod, norm1[l][None, :], w_in[l].astype(BF16), sgu_ws[l].astype(BF16), sgu_bs[l].T,
        w_out[l], tm=tm, lat_seq=dec_seq)
    olru_c, new_state, down_b = _lru(xy, *lru_params, zeros_state, ffn_down[l], row_start=0, m=n_ctx,
                                     seq=seq, nsb=8)
    olru_l, _, up_b = _lru(xy, *lru_params, state_lru[:, l], ffn_up[l], row_start=n_ctx, m=n_lat,
                           seq=dec_seq, nsb=1)
    yc, yl = _ffn(xc, xl, olru_c, olru_l, osgu, mod, g_lru[l][None, :], g_sgu[l][None, :],
                  w_out_b, norm2[l][None, :], up_b, ffn_conv_w[l],
                  ffn_conv_b[l][None, :], down_b, final_norm[None, :], tm=tm,
                  ctx_period=seq, lat_period=GRID_W, lat_seq=dec_seq)
    return (yc.reshape(batch, seq, D_MODEL), yl.reshape(dec_batch, dec_seq, D_MODEL),
            new_state[:, None])
```

```python
import functools
import math

import jax
import jax.numpy as jnp
from jax import lax
from jax.experimental import pallas as pl
from jax.experimental.pallas import tpu as pltpu

D_MODEL = 1024
LRU_HEADS = 8
LRU_WIDTH = 512
LRU_HEAD_DIM = 64
LRU_C = 8.0
SGU_GROUPS = 4
SGU_WIDTH = 512
CHUNK = 128
D_FF = 3072
N_MOD = 6
EPS = 1e-6
GRID_W = 64

LANES = 128
SUBLANES = 8
MOD_ROWS = 8
BF16 = jnp.bfloat16
F32 = jnp.float32

_GELU_C0 = math.sqrt(2.0 / math.pi)
_GELU_C1 = _GELU_C0 * 0.044715


def _rms(x, gain):
    return x * lax.rsqrt(jnp.mean(x * x, axis=-1, keepdims=True) + EPS) * gain


def _gelu(x):
    return (0.5 * x) * (1.0 + jnp.tanh(x * (_GELU_C0 + _GELU_C1 * (x * x))))


def _mod_row(mod_ref, k, row):
    return mod_ref[k, pl.ds(row, 1), :]


def _mod_kernel(c_ref, w_ref, b_ref, o_ref):
    s = jax.nn.silu(c_ref[...]).astype(BF16)
    o_ref[...] = jnp.dot(s, w_ref[...].astype(BF16), preferred_element_type=F32) + b_ref[...]


def _modulation(cc, w_ada, b_ada):
    tn = D_MODEL // 2
    per = D_MODEL // tn
    return pl.pallas_call(
        _mod_kernel,
        grid=(N_MOD * per,),
        in_specs=[
            pl.BlockSpec((MOD_ROWS, D_MODEL), lambda j: (0, 0)),
            pl.BlockSpec((D_MODEL, tn), lambda j: (0, j)),
            pl.BlockSpec((1, tn), lambda j: (0, j)),
        ],
        out_specs=pl.BlockSpec((None, MOD_ROWS, tn), lambda j: (j // per, 0, j % per)),
        out_shape=jax.ShapeDtypeStruct((N_MOD, MOD_ROWS, D_MODEL), F32),
        compiler_params=pltpu.CompilerParams(dimension_semantics=("arbitrary",)),
        name="modulation",
    )(cc, w_ada, b_ada)


def _token_groups(tm, n_ctx, n_lat, lat_seq):
    assert n_ctx % tm == 0 and n_lat % tm == 0 and lat_seq % tm == 0
    nc = n_ctx // tm
    ctx_map = lambda i: (jnp.minimum(i, nc - 1), 0)
    lat_map = lambda i: (jnp.maximum(i - nc, 0), 0)
    lat_mod_row = lambda i: 1 + ((i - nc) * tm) // lat_seq
    return nc, nc + n_lat // tm, ctx_map, lat_map, lat_mod_row


def _inproj_kernel(xc_ref, xl_ref, mod_ref, n1_ref, win_f_ref, ws_ref, bs_ref, w_ref,
                   xy_ref, osgu_ref, w_b_ref, win_ref, *, tm, nc, lat_mod_row):
    i = pl.program_id(0)
    w_b_ref[...] = w_ref[...].astype(BF16)

    @pl.when(i == 0)
    def _():
        win_ref[...] = win_f_ref[...].astype(BF16)

    def body(x_ref, row):
        sh1 = _mod_row(mod_ref, 0, row)
        sc1 = _mod_row(mod_ref, 1, row)
        hb = (_rms(x_ref[...], n1_ref[...] * (1.0 + sc1)) + sh1).astype(BF16)
        uv = jnp.dot(hb, win_ref[:, 2 * LRU_WIDTH:], preferred_element_type=F32)
        xy_ref[...] = jnp.dot(hb, win_ref[:, 0:2 * LRU_WIDTH], preferred_element_type=F32)
        for g in range(SGU_GROUPS):
            lo = g * LANES
            gu = _gelu(uv[:, lo:lo + LANES])
            gv = _gelu(uv[:, SGU_WIDTH + lo:SGU_WIDTH + lo + LANES]).astype(BF16)
            bias = jnp.broadcast_to(bs_ref[:, g:g + 1], (CHUNK, LANES))
            for ck in range(tm // CHUNK):
                r0 = ck * CHUNK
                s = jnp.dot(ws_ref[g], gv[r0:r0 + CHUNK], preferred_element_type=F32) + bias
                osgu_ref[r0:r0 + CHUNK, lo:lo + LANES] = gu[r0:r0 + CHUNK] * s

    @pl.when(i < nc)
    def _():
        body(xc_ref, 0)

    @pl.when(i >= nc)
    def _():
        body(xl_ref, lat_mod_row(i))


BF16_SUBLANES = 16


def _cast_slice_spec(shape, grid):
    n_steps = math.prod(grid)
    rows = shape[0]
    r = BF16_SUBLANES
    while rows % r or rows // r > n_steps:
        r += BF16_SUBLANES
    last = rows // r - 1

    def index_map(*idx):
        step = 0
        for i, n in zip(idx, grid):
            step = step * n + i
        return (jnp.minimum(step, last), 0)

    return pl.BlockSpec((r, shape[1]), index_map)


def _inproj(xc, xl, mod, norm1, w_in, ws, bs_t, later_weight, *, tm, lat_seq):
    nc, n, ctx_map, lat_map, lat_mod_row = _token_groups(tm, xc.shape[0], xl.shape[0], lat_seq)
    m = n * tm
    const = lambda i: (0, 0)
    cast_specs = [_cast_slice_spec(later_weight.shape, (n,))]
    return pl.pallas_call(
        functools.partial(_inproj_kernel, tm=tm, nc=nc, lat_mod_row=lat_mod_row),
        grid=(n,),
        in_specs=[
            pl.BlockSpec((tm, D_MODEL), ctx_map),
            pl.BlockSpec((tm, D_MODEL), lat_map),
            pl.BlockSpec((N_MOD, MOD_ROWS, D_MODEL), lambda i: (0, 0, 0)),
            pl.BlockSpec((1, D_MODEL), const),
            pl.BlockSpec((D_MODEL, 4 * LRU_WIDTH), const, pipeline_mode=pl.Buffered(1)),
            pl.BlockSpec((SGU_GROUPS, CHUNK, CHUNK), lambda i: (0, 0, 0)),
            pl.BlockSpec((CHUNK, SGU_GROUPS), const),
        ] + cast_specs,
        out_specs=[
            pl.BlockSpec((tm, 2 * LRU_WIDTH), lambda i: (i, 0)),
            pl.BlockSpec((tm, SGU_WIDTH), lambda i: (i, 0)),
        ] + cast_specs,
        out_shape=[
            jax.ShapeDtypeStruct((m, 2 * LRU_WIDTH), F32),
            jax.ShapeDtypeStruct((m, SGU_WIDTH), F32),
        ] + [jax.ShapeDtypeStruct(later_weight.shape, BF16)],
        scratch_shapes=[pltpu.VMEM(w_in.shape, BF16)],
        compiler_params=pltpu.CompilerParams(
            dimension_semantics=("arbitrary",), vmem_limit_bytes=48 * 1024 * 1024),
        name="inproj_sgu",
    )(xc, xl, mod, norm1, w_in, ws, bs_t, later_weight)


def _scan_pitch(seq):
    pitch = seq // SUBLANES
    return pitch + (4 - pitch % 8) % 8


def _lru_kernel(xr_ref, yr_ref, cw_ref, cb_ref, wg_ref, bg_ref, lam_ref, h0_ref, w_ref,
                o_ref, st_ref, w_b_ref, a_sc, b_sc, p_sc, h_sc, *, seq, nsb, nk, pitch, rc, unroll):
    sp_rows = SUBLANES * pitch
    n_rc = (seq * nsb) // rc
    seg = min(seq, rc)
    segs = rc // seg
    chunks_per_seq = seq // seg
    w_b_ref[...] = w_ref[...].astype(BF16)
    c4_all = (-0.5 * LRU_C) * jax.nn.softplus(-lam_ref[...])
    cw_all = cw_ref[...]
    cb_all = cb_ref[...]

    for s in range(nsb):
        lo, hi = s * sp_rows + seq, (s + 1) * sp_rows
        for slab in range(2 * nk):
            a_sc[slab, lo:hi, :] = jnp.ones((hi - lo, LANES), F32)
            b_sc[slab, lo:hi, :] = jnp.zeros((hi - lo, LANES), F32)

    def scratch_row(ci, k):
        if chunks_per_seq > 1:
            return pl.multiple_of((ci // chunks_per_seq) * sp_rows + (ci % chunks_per_seq) * seg,
                                  SUBLANES)
        return pl.multiple_of((ci * segs + k) * sp_rows, SUBLANES)

    def conv_segment(xs, prev, nxt, cw, cb):
        ext = jnp.concatenate([prev, xs, nxt], axis=0)
        n_ext = seg + 2 * SUBLANES
        xc = cb + cw[2:3] * xs
        xc = xc + cw[0:1] * pltpu.roll(ext, 2, axis=0)[SUBLANES:SUBLANES + seg]
        xc = xc + cw[1:2] * pltpu.roll(ext, 1, axis=0)[SUBLANES:SUBLANES + seg]
        return xc + cw[3:4] * pltpu.roll(ext, n_ext - 1, axis=0)[SUBLANES:SUBLANES + seg]

    def gates(ci, _):
        base = pl.multiple_of(ci * rc, rc)
        for k in range(nk):
            lanes = slice(k * LANES, (k + 1) * LANES)
            cw, cb, c4 = cw_all[:, lanes], cb_all[:, lanes], c4_all[:, lanes]
            xm = xr_ref[pl.ds(base, rc), lanes]
            if chunks_per_seq > 1:
                cs = ci % chunks_per_seq
                p0 = pl.multiple_of(jnp.maximum(base - SUBLANES, 0), SUBLANES)
                n0 = pl.multiple_of(jnp.minimum(base + rc, seq * nsb - SUBLANES), SUBLANES)
                prev = jnp.where(cs == 0, 0.0, xr_ref[pl.ds(p0, SUBLANES), lanes])
                nxt = jnp.where(cs == chunks_per_seq - 1, 0.0, xr_ref[pl.ds(n0, SUBLANES), lanes])
                xc = conv_segment(xm, prev, nxt, cw, cb)
            else:
                zeros = jnp.zeros((SUBLANES, LANES), F32)
                xc = jnp.concatenate(
                    [conv_segment(xm[j * seg:(j + 1) * seg], zeros, zeros, cw, cb) for j in range(segs)],
                    axis=0)
            g = jnp.dot(xc.astype(BF16), wg_ref[k], preferred_element_type=F32) + bg_ref[k]
            hxc = 0.5 * xc
            for d in range(2):
                th_r = jnp.tanh(g[:, (2 * d) * LANES:(2 * d + 1) * LANES])
                th_i = jnp.tanh(g[:, (2 * d + 1) * LANES:(2 * d + 2) * LANES])
                log_a = c4[d:d + 1] * th_r + c4[d:d + 1]
                a = jnp.exp(log_a)
                om = jnp.tanh(log_a) * (-1.0 - a * a)
                root = jnp.where(om > 0.0, om * lax.rsqrt(om), 0.0)
                b = root * ((th_i + 1.0) * hxc)
                for j in range(segs):
                    dst = scratch_row(ci, j)
                    a_sc[d * nk + k, pl.ds(dst, seg), :] = a[j * seg:(j + 1) * seg]
                    b_sc[d * nk + k, pl.ds(dst, seg), :] = b[j * seg:(j + 1) * seg]
        return 0

    lax.fori_loop(0, n_rc, gates, 0)

    def scan_sequence(s, _):
        off = s * sp_rows

        def rows(t):
            return pl.ds(off + t, SUBLANES, stride=pitch)

        def local(t, carry):
            out = []
            for slab, (h, p) in enumerate(carry):
                tt = t if slab < nk else pitch - 1 - t
                a = a_sc[slab, rows(tt), :]
                h = a * h + b_sc[slab, rows(tt), :]
                p = a * p
                h_sc[slab, rows(tt), :] = h
                p_sc[slab, rows(tt), :] = p
                out.append((h, p))
            return tuple(out)

        zero = jnp.zeros((SUBLANES, LANES), F32)
        one = jnp.ones((SUBLANES, LANES), F32)
        ends = lax.fori_loop(0, pitch, local, ((zero, one),) * (2 * nk), unroll=unroll)

        h0 = h0_ref[s]
        starts = []
        finals = [[None] * nk, [None] * nk]
        for slab, (h, p) in enumerate(ends):
            d, k = divmod(slab, nk)
            c = h0[d:d + 1, k * LANES:(k + 1) * LANES]
            cs = [None] * SUBLANES
            for j in (range(SUBLANES) if d == 0 else reversed(range(SUBLANES))):
                cs[j] = c
                c = h[j:j + 1] + p[j:j + 1] * c
            finals[d][k] = c
            starts.append(jnp.concatenate(cs, axis=0))
        st_ref[s] = jnp.concatenate(
            [jnp.concatenate(finals[0], axis=1), jnp.concatenate(finals[1], axis=1)], axis=0)

        def fix(t, _):
            for k in range(nk):
                a_sc[k, rows(t), :] = (h_sc[k, rows(t), :] + p_sc[k, rows(t), :] * starts[k]
                                       + h_sc[nk + k, rows(t), :] + p_sc[nk + k, rows(t), :] * starts[nk + k])
            return 0

        lax.fori_loop(0, pitch, fix, 0, unroll=unroll)
        return 0

    if nsb == 1:
        scan_sequence(0, 0)
    else:
        lax.fori_loop(0, nsb, scan_sequence, 0)

    def gate_out(ci, _):
        base = pl.multiple_of(ci * rc, rc)
        for k in range(nk):
            lanes = slice(k * LANES, (k + 1) * LANES)
            gy = _gelu(yr_ref[pl.ds(base, rc), lanes])
            for j in range(segs):
                src = scratch_row(ci, j)
                o_ref[pl.ds(base + j * seg, seg), lanes] = (gy[j * seg:(j + 1) * seg]
                                                            * a_sc[k, pl.ds(src, seg), :])
        return 0

    lax.fori_loop(0, n_rc, gate_out, 0)


def _lru(xy, conv_w, conv_b, wg, bg, lam, h0, later_weight, *, row_start, m, seq, nsb):
    nblk = m // (seq * nsb)
    assert row_start % (seq * nsb) == 0
    blk0 = row_start // (seq * nsb)
    nk = 2
    width = nk * LANES
    ncb = LRU_WIDTH // width
    pitch = _scan_pitch(seq)
    sc_rows = nsb * SUBLANES * pitch
    rows = seq * nsb
    cast_spec = _cast_slice_spec(later_weight.shape, (ncb, nblk))
    return pl.pallas_call(
        functools.partial(_lru_kernel, seq=seq, nsb=nsb, nk=nk, pitch=pitch, rc=512, unroll=4),
        grid=(ncb, nblk),
        in_specs=[
            pl.BlockSpec((rows, width), lambda j, b: (blk0 + b, j)),
            pl.BlockSpec((rows, width), lambda j, b: (blk0 + b, ncb + j)),
            pl.BlockSpec((4, width), lambda j, b: (0, j)),
            pl.BlockSpec((1, width), lambda j, b: (0, j)),
            pl.BlockSpec((nk, LANES, 4 * LANES), lambda j, b: (j, 0, 0)),
            pl.BlockSpec((nk, 1, 4 * LANES), lambda j, b: (j, 0, 0)),
            pl.BlockSpec((2, width), lambda j, b: (0, j)),
            pl.BlockSpec((nsb, 2, width), lambda j, b: (b, 0, j)),
            cast_spec,
        ],
        out_specs=[
            pl.BlockSpec((rows, width), lambda j, b: (b, j)),
            pl.BlockSpec((nsb, 2, width), lambda j, b: (b, 0, j)),
            cast_spec,
        ],
        out_shape=[
            jax.ShapeDtypeStruct((m, LRU_WIDTH), F32),
            jax.ShapeDtypeStruct((m // seq, 2, LRU_WIDTH), F32),
            jax.ShapeDtypeStruct(later_weight.shape, BF16),
        ],
        scratch_shapes=[pltpu.VMEM((2 * nk, sc_rows, LANES), F32)] * 4,
        compiler_params=pltpu.CompilerParams(
            dimension_semantics=("arbitrary", "arbitrary"), vmem_limit_bytes=56 * 1024 * 1024),
        name="rglru",
    )(xy, xy, conv_w, conv_b, wg, bg, lam, h0, later_weight)


GROUP = SUBLANES * SUBLANES


def _swap_rows(slab_ref, val, r0):
    n = val.shape[0]
    for k in range(D_MODEL // LANES):
        slab_ref[k, r0:r0 + n, :] = val[:, k * LANES:(k + 1) * LANES]
    cols = []
    for k in range(D_MODEL // LANES):
        rows = [slab_ref[k, pl.ds(r0 + g * GROUP + t, SUBLANES, stride=SUBLANES), :]
                for g in range(n // GROUP) for t in range(SUBLANES)]
        cols.append(jnp.concatenate(rows, axis=0))
    return jnp.concatenate(cols, axis=1)


def _ffn_tile(x_ref, ol_ref, os_ref, mod_ref, gl_ref, gs_ref, wout_ref, n2_ref, up_ref,
              cw_ref, cb_ref, down_ref, fn_ref, y_ref, slab_sc, *, tm, fc, es, period, row):
    g1 = _mod_row(mod_ref, 2, row)
    sh2 = _mod_row(mod_ref, 3, row)
    sc2 = _mod_row(mod_ref, 4, row)
    g2 = _mod_row(mod_ref, 5, row)
    nl = _rms(ol_ref[...], gl_ref[...]).astype(BF16)
    ns = _rms(os_ref[...], gs_ref[...]).astype(BF16)
    o = (jnp.dot(nl, wout_ref[0:LRU_WIDTH, :], preferred_element_type=F32)
         + jnp.dot(ns, wout_ref[LRU_WIDTH:LRU_WIDTH + SGU_WIDTH, :], preferred_element_type=F32))
    x1 = x_ref[...] + g1 * o
    h2f = _rms(x1, n2_ref[...] * (1.0 + sc2)) + sh2
    h2 = _swap_rows(slab_sc, h2f, 0).astype(BF16)

    sub = lax.broadcasted_iota(jnp.int32, (SUBLANES, fc), 0)
    groups_per_period = period // GROUP
    n_groups = tm // GROUP

    def conv(z, c0):
        w = cw_ref[:, c0:c0 + fc]
        b = cb_ref[:, c0:c0 + fc]
        w0, w1, w2 = w[0:1], w[1:2], w[2:3]
        z8 = [[z[g * GROUP + t * SUBLANES:g * GROUP + (t + 1) * SUBLANES] for t in range(SUBLANES)]
              for g in range(n_groups)]
        outs = []
        for g in range(n_groups):
            cur = z8[g]
            lo = pltpu.roll(cur[SUBLANES - 1], 1, axis=0)
            if g % groups_per_period == 0:
                lo = jnp.where(sub == 0, 0.0, lo)
            else:
                lo = jnp.where(sub == 0, pltpu.roll(z8[g - 1][SUBLANES - 1], 1, axis=0), lo)
            hi = pltpu.roll(cur[0], SUBLANES - 1, axis=0)
            if g % groups_per_period == groups_per_period - 1:
                hi = jnp.where(sub == SUBLANES - 1, 0.0, hi)
            else:
                hi = jnp.where(sub == SUBLANES - 1, pltpu.roll(z8[g + 1][0], SUBLANES - 1, axis=0), hi)
            zm = [lo] + cur[:SUBLANES - 1]
            zp = cur[1:] + [hi]
            for t in range(SUBLANES):
                outs.append(b + w0 * zm[t] + w1 * cur[t] + w2 * zp[t])
        return jnp.concatenate(outs, axis=0)

    def up(f0):
        zg = jnp.dot(h2, up_ref[:, f0:f0 + fc], preferred_element_type=F32)
        zv = jnp.dot(h2, up_ref[:, D_FF + f0:D_FF + f0 + fc], preferred_element_type=F32)
        return zg, zv

    acc = jnp.zeros((tm, D_MODEL), F32)
    nxt = up(0)
    for f0 in range(0, D_FF, fc):
        zg, zv = nxt
        if f0 + fc < D_FF:
            nxt = up(f0 + fc)
        hg = 0.5 * conv(zg, f0)
        act = (hg * (jnp.tanh(hg) + 1.0) * conv(zv, D_FF + f0)).astype(BF16)
        if f0 + fc < D_FF:
            acc = acc + jnp.dot(act, down_ref[f0:f0 + fc, :], preferred_element_type=F32)
        else:
            for r0 in range(0, tm, es):
                a = acc[r0:r0 + es] + jnp.dot(act[r0:r0 + es], down_ref[f0:f0 + fc, :],
                                              preferred_element_type=F32)
                x2 = x1[r0:r0 + es] + g2 * _swap_rows(slab_sc, a, r0)
                y_ref[r0:r0 + es, :] = _rms(x2, fn_ref[...])


def _ffn_kernel(xc_ref, xl_ref, olc_ref, oll_ref, os_ref, mod_ref, gl_ref, gs_ref, wout_ref, n2_ref,
                up_ref, cw_ref, cb_ref, down_ref, fn_ref, yc_ref, yl_ref, slab_sc, *,
                nc, lat_mod_row, ctx_period, lat_period, **tile):
    i = pl.program_id(0)
    shared = (mod_ref, gl_ref, gs_ref, wout_ref, n2_ref, up_ref, cw_ref, cb_ref, down_ref, fn_ref)

    @pl.when(i < nc)
    def _():
        _ffn_tile(xc_ref, olc_ref, os_ref, *shared, yc_ref, slab_sc, period=ctx_period, row=0, **tile)

    @pl.when(i >= nc)
    def _():
        _ffn_tile(xl_ref, oll_ref, os_ref, *shared, yl_ref, slab_sc, period=lat_period,
                  row=lat_mod_row(i), **tile)


def _ffn(xc, xl, olru_c, olru_l, osgu, mod, g_lru, g_sgu, w_out, norm2, ffn_up, conv_w, conv_b,
         ffn_down, final_norm, *, tm, ctx_period, lat_period, lat_seq):
    nc, n, ctx_map, lat_map, lat_mod_row = _token_groups(tm, xc.shape[0], xl.shape[0], lat_seq)
    const = lambda i: (0, 0)
    resident = lambda shape: pl.BlockSpec(shape, const, pipeline_mode=pl.Buffered(1))
    return pl.pallas_call(
        functools.partial(_ffn_kernel, nc=nc, lat_mod_row=lat_mod_row, ctx_period=ctx_period,
                          lat_period=lat_period, tm=tm, fc=512, es=128),
        grid=(n,),
        in_specs=[
            pl.BlockSpec((tm, D_MODEL), ctx_map),
            pl.BlockSpec((tm, D_MODEL), lat_map),
            pl.BlockSpec((tm, LRU_WIDTH), ctx_map),
            pl.BlockSpec((tm, LRU_WIDTH), lat_map),
            pl.BlockSpec((tm, SGU_WIDTH), lambda i: (i, 0)),
            pl.BlockSpec((N_MOD, MOD_ROWS, D_MODEL), lambda i: (0, 0, 0)),
            pl.BlockSpec((1, LRU_WIDTH), const),
            pl.BlockSpec((1, SGU_WIDTH), const),
            resident((LRU_WIDTH + SGU_WIDTH, D_MODEL)),
            pl.BlockSpec((1, D_MODEL), const),
            resident((D_MODEL, 2 * D_FF)),
            pl.BlockSpec((3, 2 * D_FF), const),
            pl.BlockSpec((1, 2 * D_FF), const),
            resident((D_FF, D_MODEL)),
            pl.BlockSpec((1, D_MODEL), const),
        ],
        out_specs=[pl.BlockSpec((tm, D_MODEL), ctx_map), pl.BlockSpec((tm, D_MODEL), lat_map)],
        out_shape=[jax.ShapeDtypeStruct(xc.shape, F32), jax.ShapeDtypeStruct(xl.shape, F32)],
        scratch_shapes=[pltpu.VMEM((D_MODEL // LANES, tm, LANES), F32)],
        compiler_params=pltpu.CompilerParams(
            dimension_semantics=("arbitrary",), vmem_limit_bytes=60 * 1024 * 1024),
        name="outproj_ffn",
    )(xc, xl, olru_c, olru_l, osgu, mod, g_lru, g_sgu, w_out, norm2, ffn_up, conv_w, conv_b, ffn_down,
      final_norm)


def _gate_params(wa, ba, wx, bx):
    nlb = LRU_HEADS // 2
    w = jnp.stack([wa[0], wx[0], wa[1], wx[1]], axis=0)
    wp = w.reshape(4, nlb, 2, LRU_HEAD_DIM, LRU_HEAD_DIM)
    pair = 0.5 * jnp.eye(2, dtype=F32)
    bd = wp[:, :, :, :, None, :] * pair[None, None, :, None, :, None]
    bd = bd.reshape(4, nlb, LANES, LANES)
    wg = jnp.transpose(bd, (1, 2, 0, 3)).reshape(nlb, LANES, 4 * LANES).astype(BF16)
    b = 0.5 * jnp.stack([ba[0], bx[0], ba[1], bx[1]], axis=0)
    bg = jnp.transpose(b.reshape(4, nlb, LANES), (1, 0, 2)).reshape(nlb, 1, 4 * LANES)
    return wg, bg


def kernel(x_prompt, x_sample, state_lru, c, c_ctx, norm1, norm2, w_ada, b_ada, w_in, lru_conv_w,
           lru_conv_b, lru_wa, lru_ba, lru_wx, lru_bx, lru_lam, sgu_ws, sgu_bs, g_lru, g_sgu, w_out,
           ffn_up, ffn_conv_w, ffn_conv_b, ffn_down, final_norm):
    batch, seq, _ = x_prompt.shape
    dec_batch, dec_seq, _ = x_sample.shape
    depth = norm1.shape[0]
    assert depth == 1, "the final norm is fused into the (single) layer's last kernel"
    nlb = LRU_WIDTH // LANES
    l = 0
    tm = 512

    xc = x_prompt.reshape(batch * seq, D_MODEL)
    xl = x_sample.reshape(dec_batch * dec_seq, D_MODEL)
    n_ctx, n_lat = xc.shape[0], xl.shape[0]
    cc = jnp.concatenate(
        [c_ctx[None, :], c, jnp.zeros((MOD_ROWS - 1 - dec_batch, D_MODEL), F32)], axis=0)
    zeros_state = jnp.zeros((batch, 2, LRU_WIDTH), x_prompt.dtype)

    mod = _modulation(cc, w_ada[l], b_ada[l][None, :])
    wg, bg = _gate_params(lru_wa[l], lru_ba[l], lru_wx[l], lru_bx[l])
    lru_params = (lru_conv_w[l], lru_conv_b[l][None, :], wg, bg, lru_lam[l])
    h0_lat = state_lru.reshape(dec_batch, 2, LRU_WIDTH)

    xy, osgu, w_out_b = _inproj(
        xc, xl, mod, norm1[l][None, :], w_in[l], sgu_ws[l].astype(BF16), sgu_bs[l].T,
        w_out[l], tm=tm, lat_seq=dec_seq)
    olru_c, new_state, down_b = _lru(xy, *lru_params, zeros_state, ffn_down[l], row_start=0, m=n_ctx,
                                     seq=seq, nsb=8)
    olru_l, _, up_b = _lru(xy, *lru_params, h0_lat, ffn_up[l], row_start=n_ctx, m=n_lat,
                           seq=dec_seq, nsb=1)
    yc, yl = _ffn(xc, xl, olru_c, olru_l, osgu, mod, g_lru[l][None, :], g_sgu[l][None, :],
                  w_out_b, norm2[l][None, :], up_b, ffn_conv_w[l],
                  ffn_conv_b[l][None, :], down_b, final_norm[None, :], tm=tm,
                  ctx_period=seq, lat_period=GRID_W, lat_seq=dec_seq)
    return (yc.reshape(batch, seq, D_MODEL), yl.reshape(dec_batch, dec_seq, D_MODEL),
            new_state[:, None])
```

```python
import functools
import math

import jax
import jax.numpy as jnp
from jax import lax
from jax.experimental import pallas as pl
from jax.experimental.pallas import tpu as pltpu

D_MODEL = 1024
LRU_HEADS = 8
LRU_WIDTH = 512
LRU_HEAD_DIM = 64
LRU_C = 8.0
SGU_GROUPS = 4
SGU_WIDTH = 512
CHUNK = 128
D_FF = 3072
N_MOD = 6
EPS = 1e-6
GRID_W = 64

LANES = 128
SUBLANES = 8
MOD_ROWS = 8
BF16 = jnp.bfloat16
F32 = jnp.float32

_GELU_C0 = math.sqrt(2.0 / math.pi)
_GELU_C1 = _GELU_C0 * 0.044715


def _rms(x, gain):
    return x * lax.rsqrt(jnp.mean(x * x, axis=-1, keepdims=True) + EPS) * gain


def _gelu(x):
    return (0.5 * x) * (1.0 + jnp.tanh(x * (_GELU_C0 + _GELU_C1 * (x * x))))


def _mod_row(mod_ref, k, row):
    return mod_ref[k, pl.ds(row, 1), :]


def _mod_kernel(c_ref, w_ref, b_ref, o_ref):
    s = jax.nn.silu(c_ref[...]).astype(BF16)
    o_ref[...] = jnp.dot(s, w_ref[...].astype(BF16), preferred_element_type=F32) + b_ref[...]


def _modulation(cc, w_ada, b_ada):
    tn = D_MODEL
    per = D_MODEL // tn
    return pl.pallas_call(
        _mod_kernel,
        grid=(N_MOD * per,),
        in_specs=[
            pl.BlockSpec((MOD_ROWS, D_MODEL), lambda j: (0, 0)),
            pl.BlockSpec((D_MODEL, tn), lambda j: (0, j)),
            pl.BlockSpec((1, tn), lambda j: (0, j)),
        ],
        out_specs=pl.BlockSpec((None, MOD_ROWS, tn), lambda j: (j // per, 0, j % per)),
        out_shape=jax.ShapeDtypeStruct((N_MOD, MOD_ROWS, D_MODEL), F32),
        compiler_params=pltpu.CompilerParams(dimension_semantics=("arbitrary",)),
        name="modulation",
    )(cc, w_ada, b_ada)


def _token_groups(tm, n_ctx, n_lat, lat_seq):
    assert n_ctx % tm == 0 and n_lat % tm == 0 and lat_seq % tm == 0
    nc = n_ctx // tm
    ctx_map = lambda i: (jnp.minimum(i, nc - 1), 0)
    lat_map = lambda i: (jnp.maximum(i - nc, 0), 0)
    lat_mod_row = lambda i: 1 + ((i - nc) * tm) // lat_seq
    return nc, nc + n_lat // tm, ctx_map, lat_map, lat_mod_row


def _inproj_kernel(xc_ref, xl_ref, mod_ref, n1_ref, win_f_ref, ws_ref, bs_ref, w_ref,
                   xy_ref, osgu_ref, w_b_ref, win_ref, *, tm, nc, lat_mod_row):
    i = pl.program_id(0)
    w_b_ref[...] = w_ref[...].astype(BF16)

    @pl.when(i == 0)
    def _():
        win_ref[...] = win_f_ref[...].astype(BF16)

    def body(x_ref, row):
        sh1 = _mod_row(mod_ref, 0, row)
        sc1 = _mod_row(mod_ref, 1, row)
        hb = (_rms(x_ref[...], n1_ref[...] * (1.0 + sc1)) + sh1).astype(BF16)
        uv = jnp.dot(hb, win_ref[:, 2 * LRU_WIDTH:], preferred_element_type=F32)
        xy_ref[...] = jnp.dot(hb, win_ref[:, 0:2 * LRU_WIDTH], preferred_element_type=F32)
        for g in range(SGU_GROUPS):
            lo = g * LANES
            gu = _gelu(uv[:, lo:lo + LANES])
            gv = _gelu(uv[:, SGU_WIDTH + lo:SGU_WIDTH + lo + LANES]).astype(BF16)
            bias = jnp.broadcast_to(bs_ref[:, g:g + 1], (CHUNK, LANES))
            for ck in range(tm // CHUNK):
                r0 = ck * CHUNK
                s = jnp.dot(ws_ref[g], gv[r0:r0 + CHUNK], preferred_element_type=F32) + bias
                osgu_ref[r0:r0 + CHUNK, lo:lo + LANES] = gu[r0:r0 + CHUNK] * s

    @pl.when(i < nc)
    def _():
        body(xc_ref, 0)

    @pl.when(i >= nc)
    def _():
        body(xl_ref, lat_mod_row(i))


BF16_SUBLANES = 16


def _cast_slice_spec(shape, grid):
    n_steps = math.prod(grid)
    rows = shape[0]
    r = BF16_SUBLANES
    while rows % r or rows // r > n_steps:
        r += BF16_SUBLANES
    last = rows // r - 1

    def index_map(*idx):
        step = 0
        for i, n in zip(idx, grid):
            step = step * n + i
        return (jnp.minimum(step, last), 0)

    return pl.BlockSpec((r, shape[1]), index_map)


def _inproj(xc, xl, mod, norm1, w_in, ws, bs_t, later_weight, *, tm, lat_seq):
    nc, n, ctx_map, lat_map, lat_mod_row = _token_groups(tm, xc.shape[0], xl.shape[0], lat_seq)
    m = n * tm
    const = lambda i: (0, 0)
    cast_specs = [_cast_slice_spec(later_weight.shape, (n,))]
    return pl.pallas_call(
        functools.partial(_inproj_kernel, tm=tm, nc=nc, lat_mod_row=lat_mod_row),
        grid=(n,),
        in_specs=[
            pl.BlockSpec((tm, D_MODEL), ctx_map),
            pl.BlockSpec((tm, D_MODEL), lat_map),
            pl.BlockSpec((N_MOD, MOD_ROWS, D_MODEL), lambda i: (0, 0, 0)),
            pl.BlockSpec((1, D_MODEL), const),
            pl.BlockSpec((D_MODEL, 4 * LRU_WIDTH), const, pipeline_mode=pl.Buffered(1)),
            pl.BlockSpec((SGU_GROUPS, CHUNK, CHUNK), lambda i: (0, 0, 0)),
            pl.BlockSpec((CHUNK, SGU_GROUPS), const),
        ] + cast_specs,
        out_specs=[
            pl.BlockSpec((tm, 2 * LRU_WIDTH), lambda i: (i, 0)),
            pl.BlockSpec((tm, SGU_WIDTH), lambda i: (i, 0)),
        ] + cast_specs,
        out_shape=[
            jax.ShapeDtypeStruct((m, 2 * LRU_WIDTH), F32),
            jax.ShapeDtypeStruct((m, SGU_WIDTH), F32),
        ] + [jax.ShapeDtypeStruct(later_weight.shape, BF16)],
        scratch_shapes=[pltpu.VMEM(w_in.shape, BF16)],
        compiler_params=pltpu.CompilerParams(
            dimension_semantics=("arbitrary",), vmem_limit_bytes=48 * 1024 * 1024),
        name="inproj_sgu",
    )(xc, xl, mod, norm1, w_in, ws, bs_t, later_weight)


def _scan_pitch(seq):
    pitch = seq // SUBLANES
    return pitch + (4 - pitch % 8) % 8


def _lru_kernel(xr_ref, yr_ref, cw_ref, cb_ref, wg_ref, bg_ref, lam_ref, h0_ref, w_ref,
                o_ref, st_ref, w_b_ref, a_sc, b_sc, p_sc, h_sc, *, seq, nsb, nk, pitch, rc, unroll):
    sp_rows = SUBLANES * pitch
    n_rc = (seq * nsb) // rc
    seg = min(seq, rc)
    segs = rc // seg
    chunks_per_seq = seq // seg
    w_b_ref[...] = w_ref[...].astype(BF16)
    c4_all = (-0.5 * LRU_C) * jax.nn.softplus(-lam_ref[...])
    cw_all = cw_ref[...]
    cb_all = cb_ref[...]

    for s in range(nsb):
        lo, hi = s * sp_rows + seq, (s + 1) * sp_rows
        for slab in range(2 * nk):
            a_sc[slab, lo:hi, :] = jnp.ones((hi - lo, LANES), F32)
            b_sc[slab, lo:hi, :] = jnp.zeros((hi - lo, LANES), F32)

    def scratch_row(ci, k):
        if chunks_per_seq > 1:
            return pl.multiple_of((ci // chunks_per_seq) * sp_rows + (ci % chunks_per_seq) * seg,
                                  SUBLANES)
        return pl.multiple_of((ci * segs + k) * sp_rows, SUBLANES)

    def conv_segment(xs, prev, nxt, cw, cb):
        ext = jnp.concatenate([prev, xs, nxt], axis=0)
        n_ext = seg + 2 * SUBLANES
        xc = cb + cw[2:3] * xs
        xc = xc + cw[0:1] * pltpu.roll(ext, 2, axis=0)[SUBLANES:SUBLANES + seg]
        xc = xc + cw[1:2] * pltpu.roll(ext, 1, axis=0)[SUBLANES:SUBLANES + seg]
        return xc + cw[3:4] * pltpu.roll(ext, n_ext - 1, axis=0)[SUBLANES:SUBLANES + seg]

    def gates(ci, _):
        base = pl.multiple_of(ci * rc, rc)
        for k in range(nk):
            lanes = slice(k * LANES, (k + 1) * LANES)
            cw, cb, c4 = cw_all[:, lanes], cb_all[:, lanes], c4_all[:, lanes]
            xm = xr_ref[pl.ds(base, rc), lanes]
            if chunks_per_seq > 1:
                cs = ci % chunks_per_seq
                p0 = pl.multiple_of(jnp.maximum(base - SUBLANES, 0), SUBLANES)
                n0 = pl.multiple_of(jnp.minimum(base + rc, seq * nsb - SUBLANES), SUBLANES)
                prev = jnp.where(cs == 0, 0.0, xr_ref[pl.ds(p0, SUBLANES), lanes])
                nxt = jnp.where(cs == chunks_per_seq - 1, 0.0, xr_ref[pl.ds(n0, SUBLANES), lanes])
                xc = conv_segment(xm, prev, nxt, cw, cb)
            else:
                zeros = jnp.zeros((SUBLANES, LANES), F32)
                xc = jnp.concatenate(
                    [conv_segment(xm[j * seg:(j + 1) * seg], zeros, zeros, cw, cb) for j in range(segs)],
                    axis=0)
            g = jnp.dot(xc.astype(BF16), wg_ref[k], preferred_element_type=F32) + bg_ref[k]
            hxc = 0.5 * xc
            for d in range(2):
                th_r = jnp.tanh(g[:, (2 * d) * LANES:(2 * d + 1) * LANES])
                th_i = jnp.tanh(g[:, (2 * d + 1) * LANES:(2 * d + 2) * LANES])
                log_a = c4[d:d + 1] * th_r + c4[d:d + 1]
                a = jnp.exp(log_a)
                om = jnp.tanh(log_a) * (-1.0 - a * a)
                root = jnp.where(om > 0.0, om * lax.rsqrt(om), 0.0)
                b = root * ((th_i + 1.0) * hxc)
                for j in range(segs):
                    dst = scratch_row(ci, j)
                    a_sc[d * nk + k, pl.ds(dst, seg), :] = a[j * seg:(j + 1) * seg]
                    b_sc[d * nk + k, pl.ds(dst, seg), :] = b[j * seg:(j + 1) * seg]
        return 0

    lax.fori_loop(0, n_rc, gates, 0)

    def scan_sequence(s, _):
        off = s * sp_rows

        def rows(t):
            return pl.ds(off + t, SUBLANES, stride=pitch)

        def local(t, carry):
            out = []
            for slab, (h, p) in enumerate(carry):
                tt = t if slab < nk else pitch - 1 - t
                a = a_sc[slab, rows(tt), :]
                h = a * h + b_sc[slab, rows(tt), :]
                p = a * p
                h_sc[slab, rows(tt), :] = h
                p_sc[slab, rows(tt), :] = p
                out.append((h, p))
            return tuple(out)

        zero = jnp.zeros((SUBLANES, LANES), F32)
        one = jnp.ones((SUBLANES, LANES), F32)
        ends = lax.fori_loop(0, pitch, local, ((zero, one),) * (2 * nk), unroll=unroll)

        h0 = h0_ref[s]
        starts = []
        finals = [[None] * nk, [None] * nk]
        for slab, (h, p) in enumerate(ends):
            d, k = divmod(slab, nk)
            c = h0[d:d + 1, k * LANES:(k + 1) * LANES]
            cs = [None] * SUBLANES
            for j in (range(SUBLANES) if d == 0 else reversed(range(SUBLANES))):
                cs[j] = c
                c = h[j:j + 1] + p[j:j + 1] * c
            finals[d][k] = c
            starts.append(jnp.concatenate(cs, axis=0))
        st_ref[s] = jnp.concatenate(
            [jnp.concatenate(finals[0], axis=1), jnp.concatenate(finals[1], axis=1)], axis=0)

        def fix(t, _):
            for k in range(nk):
                a_sc[k, rows(t), :] = (h_sc[k, rows(t), :] + p_sc[k, rows(t), :] * starts[k]
                                       + h_sc[nk + k, rows(t), :] + p_sc[nk + k, rows(t), :] * starts[nk + k])
            return 0

        lax.fori_loop(0, pitch, fix, 0, unroll=unroll)
        return 0

    if nsb == 1:
        scan_sequence(0, 0)
    else:
        lax.fori_loop(0, nsb, scan_sequence, 0)

    def gate_out(ci, _):
        base = pl.multiple_of(ci * rc, rc)
        for k in range(nk):
            lanes = slice(k * LANES, (k + 1) * LANES)
            gy = _gelu(yr_ref[pl.ds(base, rc), lanes])
            for j in range(segs):
                src = scratch_row(ci, j)
                o_ref[pl.ds(base + j * seg, seg), lanes] = (gy[j * seg:(j + 1) * seg]
                                                            * a_sc[k, pl.ds(src, seg), :])
        return 0

    lax.fori_loop(0, n_rc, gate_out, 0)


def _lru(xy, conv_w, conv_b, wg, bg, lam, h0, later_weight, *, row_start, m, seq, nsb):
    nblk = m // (seq * nsb)
    assert row_start % (seq * nsb) == 0
    blk0 = row_start // (seq * nsb)
    nk = 2
    width = nk * LANES
    ncb = LRU_WIDTH // width
    pitch = _scan_pitch(seq)
    sc_rows = nsb * SUBLANES * pitch
    rows = seq * nsb
    cast_spec = _cast_slice_spec(later_weight.shape, (ncb, nblk))
    return pl.pallas_call(
        functools.partial(_lru_kernel, seq=seq, nsb=nsb, nk=nk, pitch=pitch, rc=512, unroll=4),
        grid=(ncb, nblk),
        in_specs=[
            pl.BlockSpec((rows, width), lambda j, b: (blk0 + b, j)),
            pl.BlockSpec((rows, width), lambda j, b: (blk0 + b, ncb + j)),
            pl.BlockSpec((4, width), lambda j, b: (0, j)),
            pl.BlockSpec((1, width), lambda j, b: (0, j)),
            pl.BlockSpec((nk, LANES, 4 * LANES), lambda j, b: (j, 0, 0)),
            pl.BlockSpec((nk, 1, 4 * LANES), lambda j, b: (j, 0, 0)),
            pl.BlockSpec((2, width), lambda j, b: (0, j)),
            pl.BlockSpec((nsb, 2, width), lambda j, b: (b, 0, j)),
            cast_spec,
        ],
        out_specs=[
            pl.BlockSpec((rows, width), lambda j, b: (b, j)),
            pl.BlockSpec((nsb, 2, width), lambda j, b: (b, 0, j)),
            cast_spec,
        ],
        out_shape=[
            jax.ShapeDtypeStruct((m, LRU_WIDTH), F32),
            jax.ShapeDtypeStruct((m // seq, 2, LRU_WIDTH), F32),
            jax.ShapeDtypeStruct(later_weight.shape, BF16),
        ],
        scratch_shapes=[pltpu.VMEM((2 * nk, sc_rows, LANES), F32)] * 4,
        compiler_params=pltpu.CompilerParams(
            dimension_semantics=("arbitrary", "arbitrary"), vmem_limit_bytes=56 * 1024 * 1024),
        name="rglru",
    )(xy, xy, conv_w, conv_b, wg, bg, lam, h0, later_weight)


GROUP = SUBLANES * SUBLANES


def _swap_rows(slab_ref, val, r0):
    n = val.shape[0]
    for k in range(D_MODEL // LANES):
        slab_ref[k, r0:r0 + n, :] = val[:, k * LANES:(k + 1) * LANES]
    cols = []
    for k in range(D_MODEL // LANES):
        rows = [slab_ref[k, pl.ds(r0 + g * GROUP + t, SUBLANES, stride=SUBLANES), :]
                for g in range(n // GROUP) for t in range(SUBLANES)]
        cols.append(jnp.concatenate(rows, axis=0))
    return jnp.concatenate(cols, axis=1)


def _ffn_tile(x_ref, ol_ref, os_ref, mod_ref, gl_ref, gs_ref, wout_ref, n2_ref, up_ref,
              cw_ref, cb_ref, down_ref, fn_ref, y_ref, slab_sc, *, tm, fc, es, period, row):
    g1 = _mod_row(mod_ref, 2, row)
    sh2 = _mod_row(mod_ref, 3, row)
    sc2 = _mod_row(mod_ref, 4, row)
    g2 = _mod_row(mod_ref, 5, row)
    nl = _rms(ol_ref[...], gl_ref[...]).astype(BF16)
    ns = _rms(os_ref[...], gs_ref[...]).astype(BF16)
    o = (jnp.dot(nl, wout_ref[0:LRU_WIDTH, :], preferred_element_type=F32)
         + jnp.dot(ns, wout_ref[LRU_WIDTH:LRU_WIDTH + SGU_WIDTH, :], preferred_element_type=F32))
    x1 = x_ref[...] + g1 * o
    h2f = _rms(x1, n2_ref[...] * (1.0 + sc2)) + sh2
    h2 = _swap_rows(slab_sc, h2f, 0).astype(BF16)

    sub = lax.broadcasted_iota(jnp.int32, (SUBLANES, fc), 0)
    groups_per_period = period // GROUP
    n_groups = tm // GROUP

    def conv(z, c0):
        w = cw_ref[:, c0:c0 + fc]
        b = cb_ref[:, c0:c0 + fc]
        w0, w1, w2 = w[0:1], w[1:2], w[2:3]
        z8 = [[z[g * GROUP + t * SUBLANES:g * GROUP + (t + 1) * SUBLANES] for t in range(SUBLANES)]
              for g in range(n_groups)]
        outs = []
        for g in range(n_groups):
            cur = z8[g]
            lo = pltpu.roll(cur[SUBLANES - 1], 1, axis=0)
            if g % groups_per_period == 0:
                lo = jnp.where(sub == 0, 0.0, lo)
            else:
                lo = jnp.where(sub == 0, pltpu.roll(z8[g - 1][SUBLANES - 1], 1, axis=0), lo)
            hi = pltpu.roll(cur[0], SUBLANES - 1, axis=0)
            if g % groups_per_period == groups_per_period - 1:
                hi = jnp.where(sub == SUBLANES - 1, 0.0, hi)
            else:
                hi = jnp.where(sub == SUBLANES - 1, pltpu.roll(z8[g + 1][0], SUBLANES - 1, axis=0), hi)
            zm = [lo] + cur[:SUBLANES - 1]
            zp = cur[1:] + [hi]
            for t in range(SUBLANES):
                outs.append(b + w0 * zm[t] + w1 * cur[t] + w2 * zp[t])
        return jnp.concatenate(outs, axis=0)

    def up(f0):
        zg = jnp.dot(h2, up_ref[:, f0:f0 + fc], preferred_element_type=F32)
        zv = jnp.dot(h2, up_ref[:, D_FF + f0:D_FF + f0 + fc], preferred_element_type=F32)
        return zg, zv

    nxt = up(0)
    acts = []
    for f0 in range(0, D_FF, fc):
        zg, zv = nxt
        if f0 + fc < D_FF:
            nxt = up(f0 + fc)
        hg = 0.5 * conv(zg, f0)
        acts.append((hg * (jnp.tanh(hg) + 1.0) * conv(zv, D_FF + f0)).astype(BF16))
    act = jnp.concatenate(acts, axis=1)
    for r0 in range(0, tm, es):
        a = jnp.dot(act[r0:r0 + es], down_ref[...], preferred_element_type=F32)
        x2 = x1[r0:r0 + es] + g2 * _swap_rows(slab_sc, a, r0)
        y_ref[r0:r0 + es, :] = _rms(x2, fn_ref[...])


def _ffn_kernel(xc_ref, xl_ref, olc_ref, oll_ref, os_ref, mod_ref, gl_ref, gs_ref, wout_ref, n2_ref,
                up_ref, cw_ref, cb_ref, down_ref, fn_ref, yc_ref, yl_ref, slab_sc, *,
                nc, lat_mod_row, ctx_period, lat_period, **tile):
    i = pl.program_id(0)
    shared = (mod_ref, gl_ref, gs_ref, wout_ref, n2_ref, up_ref, cw_ref, cb_ref, down_ref, fn_ref)

    @pl.when(i < nc)
    def _():
        _ffn_tile(xc_ref, olc_ref, os_ref, *shared, yc_ref, slab_sc, period=ctx_period, row=0, **tile)

    @pl.when(i >= nc)
    def _():
        _ffn_tile(xl_ref, oll_ref, os_ref, *shared, yl_ref, slab_sc, period=lat_period,
                  row=lat_mod_row(i), **tile)


def _ffn(xc, xl, olru_c, olru_l, osgu, mod, g_lru, g_sgu, w_out, norm2, ffn_up, conv_w, conv_b,
         ffn_down, final_norm, *, tm, ctx_period, lat_period, lat_seq):
    nc, n, ctx_map, lat_map, lat_mod_row = _token_groups(tm, xc.shape[0], xl.shape[0], lat_seq)
    const = lambda i: (0, 0)
    resident = lambda shape: pl.BlockSpec(shape, const, pipeline_mode=pl.Buffered(1))
    return pl.pallas_call(
        functools.partial(_ffn_kernel, nc=nc, lat_mod_row=lat_mod_row, ctx_period=ctx_period,
                          lat_period=lat_period, tm=tm, fc=512, es=128),
        grid=(n,),
        in_specs=[
            pl.BlockSpec((tm, D_MODEL), ctx_map),
            pl.BlockSpec((tm, D_MODEL), lat_map),
            pl.BlockSpec((tm, LRU_WIDTH), ctx_map),
            pl.BlockSpec((tm, LRU_WIDTH), lat_map),
            pl.BlockSpec((tm, SGU_WIDTH), lambda i: (i, 0)),
            pl.BlockSpec((N_MOD, MOD_ROWS, D_MODEL), lambda i: (0, 0, 0)),
            pl.BlockSpec((1, LRU_WIDTH), const),
            pl.BlockSpec((1, SGU_WIDTH), const),
            resident((LRU_WIDTH + SGU_WIDTH, D_MODEL)),
            pl.BlockSpec((1, D_MODEL), const),
            resident((D_MODEL, 2 * D_FF)),
            pl.BlockSpec((3, 2 * D_FF), const),
            pl.BlockSpec((1, 2 * D_FF), const),
            resident((D_FF, D_MODEL)),
            pl.BlockSpec((1, D_MODEL), const),
        ],
        out_specs=[pl.BlockSpec((tm, D_MODEL), ctx_map), pl.BlockSpec((tm, D_MODEL), lat_map)],
        out_shape=[jax.ShapeDtypeStruct(xc.shape, F32), jax.ShapeDtypeStruct(xl.shape, F32)],
        scratch_shapes=[pltpu.VMEM((D_MODEL // LANES, tm, LANES), F32)],
        compiler_params=pltpu.CompilerParams(
            dimension_semantics=("arbitrary",), vmem_limit_bytes=60 * 1024 * 1024),
        name="outproj_ffn",
    )(xc, xl, olru_c, olru_l, osgu, mod, g_lru, g_sgu, w_out, norm2, ffn_up, conv_w, conv_b, ffn_down,
      final_norm)


def _gate_params(wa, ba, wx, bx):
    nlb = LRU_HEADS // 2
    w = jnp.stack([wa[0], wx[0], wa[1], wx[1]], axis=0)
    wp = w.reshape(4, nlb, 2, LRU_HEAD_DIM, LRU_HEAD_DIM)
    pair = 0.5 * jnp.eye(2, dtype=F32)
    bd = wp[:, :, :, :, None, :] * pair[None, None, :, None, :, None]
    bd = bd.reshape(4, nlb, LANES, LANES)
    wg = jnp.transpose(bd, (1, 2, 0, 3)).reshape(nlb, LANES, 4 * LANES).astype(BF16)
    b = 0.5 * jnp.stack([ba[0], bx[0], ba[1], bx[1]], axis=0)
    bg = jnp.transpose(b.reshape(4, nlb, LANES), (1, 0, 2)).reshape(nlb, 1, 4 * LANES)
    return wg, bg


def kernel(x_prompt, x_sample, state_lru, c, c_ctx, norm1, norm2, w_ada, b_ada, w_in, lru_conv_w,
           lru_conv_b, lru_wa, lru_ba, lru_wx, lru_bx, lru_lam, sgu_ws, sgu_bs, g_lru, g_sgu, w_out,
           ffn_up, ffn_conv_w, ffn_conv_b, ffn_down, final_norm):
    batch, seq, _ = x_prompt.shape
    dec_batch, dec_seq, _ = x_sample.shape
    depth = norm1.shape[0]
    assert depth == 1, "the final norm is fused into the (single) layer's last kernel"
    nlb = LRU_WIDTH // LANES
    l = 0
    tm = 512

    xc = x_prompt.reshape(batch * seq, D_MODEL)
    xl = x_sample.reshape(dec_batch * dec_seq, D_MODEL)
    n_ctx, n_lat = xc.shape[0], xl.shape[0]
    cc = jnp.concatenate(
        [c_ctx[None, :], c, jnp.zeros((MOD_ROWS - 1 - dec_batch, D_MODEL), F32)], axis=0)
    zeros_state = jnp.zeros((batch, 2, LRU_WIDTH), x_prompt.dtype)

    mod = _modulation(cc, w_ada[l], b_ada[l][None, :])
    wg, bg = _gate_params(lru_wa[l], lru_ba[l], lru_wx[l], lru_bx[l])
    lru_params = (lru_conv_w[l], lru_conv_b[l][None, :], wg, bg, lru_lam[l])
    h0_lat = state_lru.reshape(dec_batch, 2, LRU_WIDTH)

    xy, osgu, w_out_b = _inproj(
        xc, xl, mod, norm1[l][None, :], w_in[l], sgu_ws[l].astype(BF16), sgu_bs[l].T,
        w_out[l], tm=tm, lat_seq=dec_seq)
    olru_c, new_state, down_b = _lru(xy, *lru_params, zeros_state, ffn_down[l], row_start=0, m=n_ctx,
                                     seq=seq, nsb=8)
    olru_l, _, up_b = _lru(xy, *lru_params, h0_lat, ffn_up[l], row_start=n_ctx, m=n_lat,
                           seq=dec_seq, nsb=1)
    yc, yl = _ffn(xc, xl, olru_c, olru_l, osgu, mod, g_lru[l][None, :], g_sgu[l][None, :],
                  w_out_b, norm2[l][None, :], up_b, ffn_conv_w[l],
                  ffn_conv_b[l][None, :], down_b, final_norm[None, :], tm=tm,
                  ctx_period=seq, lat_period=GRID_W, lat_seq=dec_seq)
    return (yc.reshape(batch, seq, D_MODEL), yl.reshape(dec_batch, dec_seq, D_MODEL),
            new_state[:, None])
```

```python
import functools
import math

import jax
import jax.numpy as jnp
from jax import lax
from jax.experimental import pallas as pl
from jax.experimental.pallas import tpu as pltpu

D_MODEL = 1024
LRU_HEADS = 8
LRU_WIDTH = 512
LRU_HEAD_DIM = 64
LRU_C = 8.0
SGU_GROUPS = 4
SGU_WIDTH = 512
CHUNK = 128
D_FF = 3072
N_MOD = 6
EPS = 1e-6
GRID_W = 64

LANES = 128
SUBLANES = 8
MOD_ROWS = 8
BF16 = jnp.bfloat16
F32 = jnp.float32

_GELU_C0 = math.sqrt(2.0 / math.pi)
_GELU_C1 = _GELU_C0 * 0.044715


def _rms(x, gain):
    return x * lax.rsqrt(jnp.mean(x * x, axis=-1, keepdims=True) + EPS) * gain


def _gelu(x):
    return (0.5 * x) * (1.0 + jnp.tanh(x * (_GELU_C0 + _GELU_C1 * (x * x))))


def _mod_row(mod_ref, k, row):
    return mod_ref[k, pl.ds(row, 1), :]


def _mod_kernel(c_ref, w_ref, b_ref, o_ref):
    s = jax.nn.silu(c_ref[...]).astype(BF16)
    o_ref[...] = jnp.dot(s, w_ref[...].astype(BF16), preferred_element_type=F32) + b_ref[...]


def _modulation(cc, w_ada, b_ada):
    tn = D_MODEL
    per = D_MODEL // tn
    return pl.pallas_call(
        _mod_kernel,
        grid=(N_MOD * per,),
        in_specs=[
            pl.BlockSpec((MOD_ROWS, D_MODEL), lambda j: (0, 0)),
            pl.BlockSpec((D_MODEL, tn), lambda j: (0, j)),
            pl.BlockSpec((1, tn), lambda j: (0, j)),
        ],
        out_specs=pl.BlockSpec((None, MOD_ROWS, tn), lambda j: (j // per, 0, j % per)),
        out_shape=jax.ShapeDtypeStruct((N_MOD, MOD_ROWS, D_MODEL), F32),
        compiler_params=pltpu.CompilerParams(dimension_semantics=("arbitrary",)),
        name="modulation",
    )(cc, w_ada, b_ada)


def _token_groups(tm, n_ctx, n_lat, lat_seq):
    assert n_ctx % tm == 0 and n_lat % tm == 0 and lat_seq % tm == 0
    nc = n_ctx // tm
    ctx_map = lambda i: (jnp.minimum(i, nc - 1), 0)
    lat_map = lambda i: (jnp.maximum(i - nc, 0), 0)
    lat_mod_row = lambda i: 1 + ((i - nc) * tm) // lat_seq
    return nc, nc + n_lat // tm, ctx_map, lat_map, lat_mod_row


def _inproj_kernel(xc_ref, xl_ref, mod_ref, n1_ref, win_f_ref, ws_ref, bs_ref, w_ref,
                   xy_ref, osgu_ref, w_b_ref, win_ref, *, tm, nc, lat_mod_row):
    i = pl.program_id(0)
    w_b_ref[...] = w_ref[...].astype(BF16)

    @pl.when(i == 0)
    def _():
        win_ref[...] = win_f_ref[...].astype(BF16)

    def body(x_ref, row):
        sh1 = _mod_row(mod_ref, 0, row)
        sc1 = _mod_row(mod_ref, 1, row)
        hb = (_rms(x_ref[...], n1_ref[...] * (1.0 + sc1)) + sh1).astype(BF16)
        uv = jnp.dot(hb, win_ref[:, 2 * LRU_WIDTH:], preferred_element_type=F32)
        xy_ref[...] = jnp.dot(hb, win_ref[:, 0:2 * LRU_WIDTH], preferred_element_type=F32)
        for g in range(SGU_GROUPS):
            lo = g * LANES
            gu = _gelu(uv[:, lo:lo + LANES])
            gv = _gelu(uv[:, SGU_WIDTH + lo:SGU_WIDTH + lo + LANES]).astype(BF16)
            bias = jnp.broadcast_to(bs_ref[:, g:g + 1], (CHUNK, LANES))
            for ck in range(tm // CHUNK):
                r0 = ck * CHUNK
                s = jnp.dot(ws_ref[g], gv[r0:r0 + CHUNK], preferred_element_type=F32) + bias
                osgu_ref[r0:r0 + CHUNK, lo:lo + LANES] = gu[r0:r0 + CHUNK] * s

    @pl.when(i < nc)
    def _():
        body(xc_ref, 0)

    @pl.when(i >= nc)
    def _():
        body(xl_ref, lat_mod_row(i))


BF16_SUBLANES = 16


def _cast_slice_spec(shape, grid):
    n_steps = math.prod(grid)
    rows = shape[0]
    r = BF16_SUBLANES
    while rows % r or rows // r > n_steps:
        r += BF16_SUBLANES
    last = rows // r - 1

    def index_map(*idx):
        step = 0
        for i, n in zip(idx, grid):
            step = step * n + i
        return (jnp.minimum(step, last), 0)

    return pl.BlockSpec((r, shape[1]), index_map)


def _inproj(xc, xl, mod, norm1, w_in, ws, bs_t, later_weight, *, tm, lat_seq):
    nc, n, ctx_map, lat_map, lat_mod_row = _token_groups(tm, xc.shape[0], xl.shape[0], lat_seq)
    m = n * tm
    const = lambda i: (0, 0)
    cast_specs = [_cast_slice_spec(later_weight.shape, (n,))]
    return pl.pallas_call(
        functools.partial(_inproj_kernel, tm=tm, nc=nc, lat_mod_row=lat_mod_row),
        grid=(n,),
        in_specs=[
            pl.BlockSpec((tm, D_MODEL), ctx_map),
            pl.BlockSpec((tm, D_MODEL), lat_map),
            pl.BlockSpec((N_MOD, MOD_ROWS, D_MODEL), lambda i: (0, 0, 0)),
            pl.BlockSpec((1, D_MODEL), const),
            pl.BlockSpec((D_MODEL, 4 * LRU_WIDTH), const, pipeline_mode=pl.Buffered(1)),
            pl.BlockSpec((SGU_GROUPS, CHUNK, CHUNK), lambda i: (0, 0, 0)),
            pl.BlockSpec((CHUNK, SGU_GROUPS), const),
        ] + cast_specs,
        out_specs=[
            pl.BlockSpec((tm, 2 * LRU_WIDTH), lambda i: (i, 0)),
            pl.BlockSpec((tm, SGU_WIDTH), lambda i: (i, 0)),
        ] + cast_specs,
        out_shape=[
            jax.ShapeDtypeStruct((m, 2 * LRU_WIDTH), F32),
            jax.ShapeDtypeStruct((m, SGU_WIDTH), F32),
        ] + [jax.ShapeDtypeStruct(later_weight.shape, BF16)],
        scratch_shapes=[pltpu.VMEM(w_in.shape, BF16)],
        compiler_params=pltpu.CompilerParams(
            dimension_semantics=("arbitrary",), vmem_limit_bytes=56 * 1024 * 1024),
        name="inproj_sgu",
    )(xc, xl, mod, norm1, w_in, ws, bs_t, later_weight)


def _scan_pitch(seq):
    pitch = seq // SUBLANES
    return pitch + (4 - pitch % 8) % 8


def _lru_kernel(xr_ref, yr_ref, cw_ref, cb_ref, wg_ref, bg_ref, lam_ref, h0_ref, w_ref,
                o_ref, st_ref, w_b_ref, a_sc, b_sc, p_sc, h_sc, *, seq, nsb, nk, pitch, rc, unroll):
    sp_rows = SUBLANES * pitch
    n_rc = (seq * nsb) // rc
    seg = min(seq, rc)
    segs = rc // seg
    chunks_per_seq = seq // seg
    w_b_ref[...] = w_ref[...].astype(BF16)
    c4_all = (-0.5 * LRU_C) * jax.nn.softplus(-lam_ref[...])
    cw_all = cw_ref[...]
    cb_all = cb_ref[...]

    for s in range(nsb):
        lo, hi = s * sp_rows + seq, (s + 1) * sp_rows
        for slab in range(2 * nk):
            a_sc[slab, lo:hi, :] = jnp.ones((hi - lo, LANES), F32)
            b_sc[slab, lo:hi, :] = jnp.zeros((hi - lo, LANES), F32)

    def scratch_row(ci, k):
        if chunks_per_seq > 1:
            return pl.multiple_of((ci // chunks_per_seq) * sp_rows + (ci % chunks_per_seq) * seg,
                                  SUBLANES)
        return pl.multiple_of((ci * segs + k) * sp_rows, SUBLANES)

    def conv_segment(xs, prev, nxt, cw, cb):
        ext = jnp.concatenate([prev, xs, nxt], axis=0)
        n_ext = seg + 2 * SUBLANES
        xc = cb + cw[2:3] * xs
        xc = xc + cw[0:1] * pltpu.roll(ext, 2, axis=0)[SUBLANES:SUBLANES + seg]
        xc = xc + cw[1:2] * pltpu.roll(ext, 1, axis=0)[SUBLANES:SUBLANES + seg]
        return xc + cw[3:4] * pltpu.roll(ext, n_ext - 1, axis=0)[SUBLANES:SUBLANES + seg]

    def gates(ci, _):
        base = pl.multiple_of(ci * rc, rc)
        for k in range(nk):
            lanes = slice(k * LANES, (k + 1) * LANES)
            cw, cb, c4 = cw_all[:, lanes], cb_all[:, lanes], c4_all[:, lanes]
            xm = xr_ref[pl.ds(base, rc), lanes]
            if chunks_per_seq > 1:
                cs = ci % chunks_per_seq
                p0 = pl.multiple_of(jnp.maximum(base - SUBLANES, 0), SUBLANES)
                n0 = pl.multiple_of(jnp.minimum(base + rc, seq * nsb - SUBLANES), SUBLANES)
                prev = jnp.where(cs == 0, 0.0, xr_ref[pl.ds(p0, SUBLANES), lanes])
                nxt = jnp.where(cs == chunks_per_seq - 1, 0.0, xr_ref[pl.ds(n0, SUBLANES), lanes])
                xc = conv_segment(xm, prev, nxt, cw, cb)
            else:
                zeros = jnp.zeros((SUBLANES, LANES), F32)
                xc = jnp.concatenate(
                    [conv_segment(xm[j * seg:(j + 1) * seg], zeros, zeros, cw, cb) for j in range(segs)],
                    axis=0)
            g = jnp.dot(xc.astype(BF16), wg_ref[k], preferred_element_type=F32) + bg_ref[k]
            hxc = 0.5 * xc
            for d in range(2):
                th_r = jnp.tanh(g[:, (2 * d) * LANES:(2 * d + 1) * LANES])
                th_i = jnp.tanh(g[:, (2 * d + 1) * LANES:(2 * d + 2) * LANES])
                log_a = c4[d:d + 1] * th_r + c4[d:d + 1]
                a = jnp.exp(log_a)
                om = jnp.tanh(log_a) * (-1.0 - a * a)
                root = jnp.where(om > 0.0, om * lax.rsqrt(om), 0.0)
                b = root * ((th_i + 1.0) * hxc)
                for j in range(segs):
                    dst = scratch_row(ci, j)
                    a_sc[d * nk + k, pl.ds(dst, seg), :] = a[j * seg:(j + 1) * seg]
                    b_sc[d * nk + k, pl.ds(dst, seg), :] = b[j * seg:(j + 1) * seg]
        return 0

    lax.fori_loop(0, n_rc, gates, 0)

    def scan_sequence(s, _):
        off = s * sp_rows

        def rows(t):
            return pl.ds(off + t, SUBLANES, stride=pitch)

        def local(t, carry):
            out = []
            for slab, (h, p) in enumerate(carry):
                tt = t if slab < nk else pitch - 1 - t
                a = a_sc[slab, rows(tt), :]
                h = a * h + b_sc[slab, rows(tt), :]
                p = a * p
                h_sc[slab, rows(tt), :] = h
                p_sc[slab, rows(tt), :] = p
                out.append((h, p))
            return tuple(out)

        zero = jnp.zeros((SUBLANES, LANES), F32)
        one = jnp.ones((SUBLANES, LANES), F32)
        ends = lax.fori_loop(0, pitch, local, ((zero, one),) * (2 * nk), unroll=unroll)

        h0 = h0_ref[s]
        starts = []
        finals = [[None] * nk, [None] * nk]
        for slab, (h, p) in enumerate(ends):
            d, k = divmod(slab, nk)
            c = h0[d:d + 1, k * LANES:(k + 1) * LANES]
            cs = [None] * SUBLANES
            for j in (range(SUBLANES) if d == 0 else reversed(range(SUBLANES))):
                cs[j] = c
                c = h[j:j + 1] + p[j:j + 1] * c
            finals[d][k] = c
            starts.append(jnp.concatenate(cs, axis=0))
        st_ref[s] = jnp.concatenate(
            [jnp.concatenate(finals[0], axis=1), jnp.concatenate(finals[1], axis=1)], axis=0)

        def fix(t, _):
            for k in range(nk):
                a_sc[k, rows(t), :] = (h_sc[k, rows(t), :] + p_sc[k, rows(t), :] * starts[k]
                                       + h_sc[nk + k, rows(t), :] + p_sc[nk + k, rows(t), :] * starts[nk + k])
            return 0

        lax.fori_loop(0, pitch, fix, 0, unroll=unroll)
        return 0

    if nsb == 1:
        scan_sequence(0, 0)
    else:
        lax.fori_loop(0, nsb, scan_sequence, 0)

    def gate_out(ci, _):
        base = pl.multiple_of(ci * rc, rc)
        for k in range(nk):
            lanes = slice(k * LANES, (k + 1) * LANES)
            gy = _gelu(yr_ref[pl.ds(base, rc), lanes])
            for j in range(segs):
                src = scratch_row(ci, j)
                o_ref[pl.ds(base + j * seg, seg), lanes] = (gy[j * seg:(j + 1) * seg]
                                                            * a_sc[k, pl.ds(src, seg), :])
        return 0

    lax.fori_loop(0, n_rc, gate_out, 0)


def _lru(xy, conv_w, conv_b, wg, bg, lam, h0, later_weight, *, row_start, m, seq, nsb):
    nblk = m // (seq * nsb)
    assert row_start % (seq * nsb) == 0
    blk0 = row_start // (seq * nsb)
    nk = 2
    width = nk * LANES
    ncb = LRU_WIDTH // width
    pitch = _scan_pitch(seq)
    sc_rows = nsb * SUBLANES * pitch
    rows = seq * nsb
    cast_spec = _cast_slice_spec(later_weight.shape, (ncb, nblk))
    return pl.pallas_call(
        functools.partial(_lru_kernel, seq=seq, nsb=nsb, nk=nk, pitch=pitch, rc=512, unroll=8),
        grid=(ncb, nblk),
        in_specs=[
            pl.BlockSpec((rows, width), lambda j, b: (blk0 + b, j)),
            pl.BlockSpec((rows, width), lambda j, b: (blk0 + b, ncb + j)),
            pl.BlockSpec((4, width), lambda j, b: (0, j)),
            pl.BlockSpec((1, width), lambda j, b: (0, j)),
            pl.BlockSpec((nk, LANES, 4 * LANES), lambda j, b: (j, 0, 0)),
            pl.BlockSpec((nk, 1, 4 * LANES), lambda j, b: (j, 0, 0)),
            pl.BlockSpec((2, width), lambda j, b: (0, j)),
            pl.BlockSpec((nsb, 2, width), lambda j, b: (b, 0, j)),
            cast_spec,
        ],
        out_specs=[
            pl.BlockSpec((rows, width), lambda j, b: (b, j)),
            pl.BlockSpec((nsb, 2, width), lambda j, b: (b, 0, j)),
            cast_spec,
        ],
        out_shape=[
            jax.ShapeDtypeStruct((m, LRU_WIDTH), F32),
            jax.ShapeDtypeStruct((m // seq, 2, LRU_WIDTH), F32),
            jax.ShapeDtypeStruct(later_weight.shape, BF16),
        ],
        scratch_shapes=[pltpu.VMEM((2 * nk, sc_rows, LANES), F32)] * 4,
        compiler_params=pltpu.CompilerParams(
            dimension_semantics=("arbitrary", "arbitrary"), vmem_limit_bytes=56 * 1024 * 1024),
        name="rglru",
    )(xy, xy, conv_w, conv_b, wg, bg, lam, h0, later_weight)


GROUP = SUBLANES * SUBLANES


def _swap_rows(slab_ref, val, r0):
    n = val.shape[0]
    for k in range(D_MODEL // LANES):
        slab_ref[k, r0:r0 + n, :] = val[:, k * LANES:(k + 1) * LANES]
    cols = []
    for k in range(D_MODEL // LANES):
        rows = [slab_ref[k, pl.ds(r0 + g * GROUP + t, SUBLANES, stride=SUBLANES), :]
                for g in range(n // GROUP) for t in range(SUBLANES)]
        cols.append(jnp.concatenate(rows, axis=0))
    return jnp.concatenate(cols, axis=1)


def _ffn_tile(x_ref, ol_ref, os_ref, mod_ref, gl_ref, gs_ref, wout_ref, n2_ref, up_ref,
              cw_ref, cb_ref, down_ref, fn_ref, y_ref, slab_sc, *, tm, fc, es, period, row):
    g1 = _mod_row(mod_ref, 2, row)
    sh2 = _mod_row(mod_ref, 3, row)
    sc2 = _mod_row(mod_ref, 4, row)
    g2 = _mod_row(mod_ref, 5, row)
    nl = _rms(ol_ref[...], gl_ref[...]).astype(BF16)
    ns = _rms(os_ref[...], gs_ref[...]).astype(BF16)
    o = (jnp.dot(nl, wout_ref[0:LRU_WIDTH, :], preferred_element_type=F32)
         + jnp.dot(ns, wout_ref[LRU_WIDTH:LRU_WIDTH + SGU_WIDTH, :], preferred_element_type=F32))
    x1 = x_ref[...] + g1 * o
    h2f = _rms(x1, n2_ref[...] * (1.0 + sc2)) + sh2
    h2 = _swap_rows(slab_sc, h2f, 0).astype(BF16)

    sub = lax.broadcasted_iota(jnp.int32, (SUBLANES, fc), 0)
    groups_per_period = period // GROUP
    n_groups = tm // GROUP

    def conv(z, c0):
        w = cw_ref[:, c0:c0 + fc]
        b = cb_ref[:, c0:c0 + fc]
        w0, w1, w2 = w[0:1], w[1:2], w[2:3]
        z8 = [[z[g * GROUP + t * SUBLANES:g * GROUP + (t + 1) * SUBLANES] for t in range(SUBLANES)]
              for g in range(n_groups)]
        outs = []
        for g in range(n_groups):
            cur = z8[g]
            lo = pltpu.roll(cur[SUBLANES - 1], 1, axis=0)
            if g % groups_per_period == 0:
                lo = jnp.where(sub == 0, 0.0, lo)
            else:
                lo = jnp.where(sub == 0, pltpu.roll(z8[g - 1][SUBLANES - 1], 1, axis=0), lo)
            hi = pltpu.roll(cur[0], SUBLANES - 1, axis=0)
            if g % groups_per_period == groups_per_period - 1:
                hi = jnp.where(sub == SUBLANES - 1, 0.0, hi)
            else:
                hi = jnp.where(sub == SUBLANES - 1, pltpu.roll(z8[g + 1][0], SUBLANES - 1, axis=0), hi)
            zm = [lo] + cur[:SUBLANES - 1]
            zp = cur[1:] + [hi]
            for t in range(SUBLANES):
                outs.append(b + w0 * zm[t] + w1 * cur[t] + w2 * zp[t])
        return jnp.concatenate(outs, axis=0)

    def up(f0):
        zg = jnp.dot(h2, up_ref[:, f0:f0 + fc], preferred_element_type=F32)
        zv = jnp.dot(h2, up_ref[:, D_FF + f0:D_FF + f0 + fc], preferred_element_type=F32)
        return zg, zv

    acc = jnp.zeros((tm, D_MODEL), F32)
    nxt = up(0)
    for f0 in range(0, D_FF, fc):
        zg, zv = nxt
        if f0 + fc < D_FF:
            nxt = up(f0 + fc)
        hg = 0.5 * conv(zg, f0)
        act = (hg * (jnp.tanh(hg) + 1.0) * conv(zv, D_FF + f0)).astype(BF16)
        if f0 + fc < D_FF:
            acc = acc + jnp.dot(act, down_ref[f0:f0 + fc, :], preferred_element_type=F32)
        else:
            for r0 in range(0, tm, es):
                a = acc[r0:r0 + es] + jnp.dot(act[r0:r0 + es], down_ref[f0:f0 + fc, :],
                                              preferred_element_type=F32)
                x2 = x1[r0:r0 + es] + g2 * _swap_rows(slab_sc, a, r0)
                y_ref[r0:r0 + es, :] = _rms(x2, fn_ref[...])


def _ffn_kernel(xc_ref, xl_ref, olc_ref, oll_ref, os_ref, mod_ref, gl_ref, gs_ref, wout_ref, n2_ref,
                up_ref, cw_ref, cb_ref, down_ref, fn_ref, yc_ref, yl_ref, slab_sc, *,
                nc, lat_mod_row, ctx_period, lat_period, **tile):
    i = pl.program_id(0)
    shared = (mod_ref, gl_ref, gs_ref, wout_ref, n2_ref, up_ref, cw_ref, cb_ref, down_ref, fn_ref)

    @pl.when(i < nc)
    def _():
        _ffn_tile(xc_ref, olc_ref, os_ref, *shared, yc_ref, slab_sc, period=ctx_period, row=0, **tile)

    @pl.when(i >= nc)
    def _():
        _ffn_tile(xl_ref, oll_ref, os_ref, *shared, yl_ref, slab_sc, period=lat_period,
                  row=lat_mod_row(i), **tile)


def _ffn(xc, xl, olru_c, olru_l, osgu, mod, g_lru, g_sgu, w_out, norm2, ffn_up, conv_w, conv_b,
         ffn_down, final_norm, *, tm, ctx_period, lat_period, lat_seq):
    nc, n, ctx_map, lat_map, lat_mod_row = _token_groups(tm, xc.shape[0], xl.shape[0], lat_seq)
    const = lambda i: (0, 0)
    resident = lambda shape: pl.BlockSpec(shape, const, pipeline_mode=pl.Buffered(1))
    return pl.pallas_call(
        functools.partial(_ffn_kernel, nc=nc, lat_mod_row=lat_mod_row, ctx_period=ctx_period,
                          lat_period=lat_period, tm=tm, fc=512, es=128),
        grid=(n,),
        in_specs=[
            pl.BlockSpec((tm, D_MODEL), ctx_map),
            pl.BlockSpec((tm, D_MODEL), lat_map),
            pl.BlockSpec((tm, LRU_WIDTH), ctx_map),
            pl.BlockSpec((tm, LRU_WIDTH), lat_map),
            pl.BlockSpec((tm, SGU_WIDTH), lambda i: (i, 0)),
            pl.BlockSpec((N_MOD, MOD_ROWS, D_MODEL), lambda i: (0, 0, 0)),
            pl.BlockSpec((1, LRU_WIDTH), const),
            pl.BlockSpec((1, SGU_WIDTH), const),
            resident((LRU_WIDTH + SGU_WIDTH, D_MODEL)),
            pl.BlockSpec((1, D_MODEL), const),
            resident((D_MODEL, 2 * D_FF)),
            pl.BlockSpec((3, 2 * D_FF), const),
            pl.BlockSpec((1, 2 * D_FF), const),
            resident((D_FF, D_MODEL)),
            pl.BlockSpec((1, D_MODEL), const),
        ],
        out_specs=[pl.BlockSpec((tm, D_MODEL), ctx_map), pl.BlockSpec((tm, D_MODEL), lat_map)],
        out_shape=[jax.ShapeDtypeStruct(xc.shape, F32), jax.ShapeDtypeStruct(xl.shape, F32)],
        scratch_shapes=[pltpu.VMEM((D_MODEL // LANES, tm, LANES), F32)],
        compiler_params=pltpu.CompilerParams(
            dimension_semantics=("arbitrary",), vmem_limit_bytes=60 * 1024 * 1024),
        name="outproj_ffn",
    )(xc, xl, olru_c, olru_l, osgu, mod, g_lru, g_sgu, w_out, norm2, ffn_up, conv_w, conv_b, ffn_down,
      final_norm)


def _gate_params(wa, ba, wx, bx):
    nlb = LRU_HEADS // 2
    w = jnp.stack([wa[0], wx[0], wa[1], wx[1]], axis=0)
    wp = w.reshape(4, nlb, 2, LRU_HEAD_DIM, LRU_HEAD_DIM)
    pair = 0.5 * jnp.eye(2, dtype=F32)
    bd = wp[:, :, :, :, None, :] * pair[None, None, :, None, :, None]
    bd = bd.reshape(4, nlb, LANES, LANES)
    wg = jnp.transpose(bd, (1, 2, 0, 3)).reshape(nlb, LANES, 4 * LANES).astype(BF16)
    b = 0.5 * jnp.stack([ba[0], bx[0], ba[1], bx[1]], axis=0)
    bg = jnp.transpose(b.reshape(4, nlb, LANES), (1, 0, 2)).reshape(nlb, 1, 4 * LANES)
    return wg, bg


def kernel(x_prompt, x_sample, state_lru, c, c_ctx, norm1, norm2, w_ada, b_ada, w_in, lru_conv_w,
           lru_conv_b, lru_wa, lru_ba, lru_wx, lru_bx, lru_lam, sgu_ws, sgu_bs, g_lru, g_sgu, w_out,
           ffn_up, ffn_conv_w, ffn_conv_b, ffn_down, final_norm):
    batch, seq, _ = x_prompt.shape
    dec_batch, dec_seq, _ = x_sample.shape
    depth = norm1.shape[0]
    assert depth == 1, "the final norm is fused into the (single) layer's last kernel"
    nlb = LRU_WIDTH // LANES
    l = 0
    tm_in, tm = 1024, 512

    xc = x_prompt.reshape(batch * seq, D_MODEL)
    xl = x_sample.reshape(dec_batch * dec_seq, D_MODEL)
    n_ctx, n_lat = xc.shape[0], xl.shape[0]
    cc = jnp.concatenate(
        [c_ctx[None, :], c, jnp.zeros((MOD_ROWS - 1 - dec_batch, D_MODEL), F32)], axis=0)
    zeros_state = jnp.zeros((batch, 2, LRU_WIDTH), x_prompt.dtype)

    mod = _modulation(cc, w_ada[l], b_ada[l][None, :])
    wg, bg = _gate_params(lru_wa[l], lru_ba[l], lru_wx[l], lru_bx[l])
    lru_params = (lru_conv_w[l], lru_conv_b[l][None, :], wg, bg, lru_lam[l])
    h0_lat = state_lru.reshape(dec_batch, 2, LRU_WIDTH)

    xy, osgu, w_out_b = _inproj(
        xc, xl, mod, norm1[l][None, :], w_in[l], sgu_ws[l].astype(BF16), sgu_bs[l].T,
        w_out[l], tm=tm_in, lat_seq=dec_seq)
    olru_c, new_state, down_b = _lru(xy, *lru_params, zeros_state, ffn_down[l], row_start=0, m=n_ctx,
                                     seq=seq, nsb=8)
    olru_l, _, up_b = _lru(xy, *lru_params, h0_lat, ffn_up[l], row_start=n_ctx, m=n_lat,
                           seq=dec_seq, nsb=1)
    yc, yl = _ffn(xc, xl, olru_c, olru_l, osgu, mod, g_lru[l][None, :], g_sgu[l][None, :],
                  w_out_b, norm2[l][None, :], up_b, ffn_conv_w[l],
                  ffn_conv_b[l][None, :], down_b, final_norm[None, :], tm=tm,
                  ctx_period=seq, lat_period=GRID_W, lat_seq=dec_seq)
    return (yc.reshape(batch, seq, D_MODEL), yl.reshape(dec_batch, dec_seq, D_MODEL),
            new_state[:, None])
```

```python
import functools
import math

import jax
import jax.numpy as jnp
from jax import lax
from jax.experimental import pallas as pl
from jax.experimental.pallas import tpu as pltpu

D_MODEL = 1024
LRU_HEADS = 8
LRU_WIDTH = 512
LRU_HEAD_DIM = 64
LRU_C = 8.0
SGU_GROUPS = 4
SGU_WIDTH = 512
CHUNK = 128
D_FF = 3072
N_MOD = 6
EPS = 1e-6
GRID_W = 64

LANES = 128
SUBLANES = 8
BF16_SUBLANES = 16
MOD_ROWS = 8
BF16 = jnp.bfloat16
F32 = jnp.float32

MIB = 1024 * 1024
INPROJ_TM = 1024
INPROJ_VMEM = 56 * MIB
LRU_ROW_CHUNK = 512
LRU_LANE_BLOCKS = 2
LRU_CTX_SEQS = 8
LRU_SCAN_UNROLL = 8
LRU_VMEM = 56 * MIB
FFN_TM = 512
FFN_FC = 512
FFN_ES = 128
FFN_VMEM = 60 * MIB

_GELU_C0 = math.sqrt(2.0 / math.pi)
_GELU_C1 = _GELU_C0 * 0.044715


def _rms(x, gain):
    return x * lax.rsqrt(jnp.mean(x * x, axis=-1, keepdims=True) + EPS) * gain


def _gelu(x):
    return (0.5 * x) * (1.0 + jnp.tanh(x * (_GELU_C0 + _GELU_C1 * (x * x))))


def _mod_row(mod_ref, k, row):
    return mod_ref[k, pl.ds(row, 1), :]


def _mod_kernel(c_ref, w_ref, b_ref, o_ref):
    s = jax.nn.silu(c_ref[...]).astype(BF16)
    o_ref[...] = jnp.dot(s, w_ref[...].astype(BF16), preferred_element_type=F32) + b_ref[...]


def _modulation(cc, w_ada, b_ada):
    tn = D_MODEL
    per = D_MODEL // tn
    return pl.pallas_call(
        _mod_kernel,
        grid=(N_MOD * per,),
        in_specs=[
            pl.BlockSpec((MOD_ROWS, D_MODEL), lambda j: (0, 0)),
            pl.BlockSpec((D_MODEL, tn), lambda j: (0, j)),
            pl.BlockSpec((1, tn), lambda j: (0, j)),
        ],
        out_specs=pl.BlockSpec((None, MOD_ROWS, tn), lambda j: (j // per, 0, j % per)),
        out_shape=jax.ShapeDtypeStruct((N_MOD, MOD_ROWS, D_MODEL), F32),
        compiler_params=pltpu.CompilerParams(dimension_semantics=("arbitrary",)),
        name="modulation",
    )(cc, w_ada, b_ada)


def _token_groups(tm, n_ctx, n_lat, lat_seq):
    assert n_ctx % tm == 0 and n_lat % tm == 0 and lat_seq % tm == 0
    nc = n_ctx // tm
    ctx_map = lambda i: (jnp.minimum(i, nc - 1), 0)
    lat_map = lambda i: (jnp.maximum(i - nc, 0), 0)
    lat_mod_row = lambda i: 1 + ((i - nc) * tm) // lat_seq
    return nc, nc + n_lat // tm, ctx_map, lat_map, lat_mod_row


def _inproj_kernel(xc_ref, xl_ref, mod_ref, n1_ref, win_f_ref, ws_ref, bs_ref, w_ref,
                   xy_ref, osgu_ref, w_b_ref, win_ref, *, tm, nc, lat_mod_row):
    i = pl.program_id(0)
    w_b_ref[...] = w_ref[...].astype(BF16)

    @pl.when(i == 0)
    def _():
        win_ref[...] = win_f_ref[...].astype(BF16)

    def body(x_ref, row):
        sh1 = _mod_row(mod_ref, 0, row)
        sc1 = _mod_row(mod_ref, 1, row)
        hb = (_rms(x_ref[...], n1_ref[...] * (1.0 + sc1)) + sh1).astype(BF16)
        uv = jnp.dot(hb, win_ref[:, 2 * LRU_WIDTH:], preferred_element_type=F32)
        xy_ref[...] = jnp.dot(hb, win_ref[:, 0:2 * LRU_WIDTH], preferred_element_type=F32)
        for g in range(SGU_GROUPS):
            lo = g * LANES
            gu = _gelu(uv[:, lo:lo + LANES])
            gv = _gelu(uv[:, SGU_WIDTH + lo:SGU_WIDTH + lo + LANES]).astype(BF16)
            bias = jnp.broadcast_to(bs_ref[:, g:g + 1], (CHUNK, LANES))
            for ck in range(tm // CHUNK):
                r0 = ck * CHUNK
                s = jnp.dot(ws_ref[g], gv[r0:r0 + CHUNK], preferred_element_type=F32) + bias
                osgu_ref[r0:r0 + CHUNK, lo:lo + LANES] = gu[r0:r0 + CHUNK] * s

    @pl.when(i < nc)
    def _():
        body(xc_ref, 0)

    @pl.when(i >= nc)
    def _():
        body(xl_ref, lat_mod_row(i))


def _cast_slice_spec(shape, grid):
    n_steps = math.prod(grid)
    rows = shape[0]
    r = BF16_SUBLANES
    while rows % r or rows // r > n_steps:
        r += BF16_SUBLANES
    last = rows // r - 1

    def index_map(*idx):
        step = 0
        for i, n in zip(idx, grid):
            step = step * n + i
        return (jnp.minimum(step, last), 0)

    return pl.BlockSpec((r, shape[1]), index_map)


def _inproj(xc, xl, mod, norm1, w_in, ws, bs_t, later_weight, *, tm, lat_seq):
    nc, n, ctx_map, lat_map, lat_mod_row = _token_groups(tm, xc.shape[0], xl.shape[0], lat_seq)
    m = n * tm
    const = lambda i: (0, 0)
    cast_specs = [_cast_slice_spec(later_weight.shape, (n,))]
    return pl.pallas_call(
        functools.partial(_inproj_kernel, tm=tm, nc=nc, lat_mod_row=lat_mod_row),
        grid=(n,),
        in_specs=[
            pl.BlockSpec((tm, D_MODEL), ctx_map),
            pl.BlockSpec((tm, D_MODEL), lat_map),
            pl.BlockSpec((N_MOD, MOD_ROWS, D_MODEL), lambda i: (0, 0, 0)),
            pl.BlockSpec((1, D_MODEL), const),
            pl.BlockSpec((D_MODEL, 4 * LRU_WIDTH), const, pipeline_mode=pl.Buffered(1)),
            pl.BlockSpec((SGU_GROUPS, CHUNK, CHUNK), lambda i: (0, 0, 0)),
            pl.BlockSpec((CHUNK, SGU_GROUPS), const),
        ] + cast_specs,
        out_specs=[
            pl.BlockSpec((tm, 2 * LRU_WIDTH), lambda i: (i, 0)),
            pl.BlockSpec((tm, SGU_WIDTH), lambda i: (i, 0)),
        ] + cast_specs,
        out_shape=[
            jax.ShapeDtypeStruct((m, 2 * LRU_WIDTH), F32),
            jax.ShapeDtypeStruct((m, SGU_WIDTH), F32),
        ] + [jax.ShapeDtypeStruct(later_weight.shape, BF16)],
        scratch_shapes=[pltpu.VMEM(w_in.shape, BF16)],
        compiler_params=pltpu.CompilerParams(
            dimension_semantics=("arbitrary",), vmem_limit_bytes=INPROJ_VMEM),
        name="inproj_sgu",
    )(xc, xl, mod, norm1, w_in, ws, bs_t, later_weight)


def _scan_pitch(seq):
    pitch = seq // SUBLANES
    return pitch + (4 - pitch % 8) % 8


def _lru_kernel(xr_ref, yr_ref, cw_ref, cb_ref, wg_ref, bg_ref, lam_ref, h0_ref, w_ref,
                o_ref, st_ref, w_b_ref, a_sc, b_sc, p_sc, h_sc, *, seq, nsb, nk, pitch, rc, unroll):
    sp_rows = SUBLANES * pitch
    n_rc = (seq * nsb) // rc
    seg = min(seq, rc)
    segs = rc // seg
    chunks_per_seq = seq // seg
    w_b_ref[...] = w_ref[...].astype(BF16)
    c4_all = (-0.5 * LRU_C) * jax.nn.softplus(-lam_ref[...])
    cw_all = cw_ref[...]
    cb_all = cb_ref[...]

    for s in range(nsb):
        lo, hi = s * sp_rows + seq, (s + 1) * sp_rows
        for slab in range(2 * nk):
            a_sc[slab, lo:hi, :] = jnp.ones((hi - lo, LANES), F32)
            b_sc[slab, lo:hi, :] = jnp.zeros((hi - lo, LANES), F32)

    def scratch_row(ci, k):
        if chunks_per_seq > 1:
            return pl.multiple_of((ci // chunks_per_seq) * sp_rows + (ci % chunks_per_seq) * seg,
                                  SUBLANES)
        return pl.multiple_of((ci * segs + k) * sp_rows, SUBLANES)

    def conv_segment(xs, prev, nxt, cw, cb):
        ext = jnp.concatenate([prev, xs, nxt], axis=0)
        n_ext = seg + 2 * SUBLANES
        xc = cb + cw[2:3] * xs
        xc = xc + cw[0:1] * pltpu.roll(ext, 2, axis=0)[SUBLANES:SUBLANES + seg]
        xc = xc + cw[1:2] * pltpu.roll(ext, 1, axis=0)[SUBLANES:SUBLANES + seg]
        return xc + cw[3:4] * pltpu.roll(ext, n_ext - 1, axis=0)[SUBLANES:SUBLANES + seg]

    def gates(ci, _):
        base = pl.multiple_of(ci * rc, rc)
        for k in range(nk):
            lanes = slice(k * LANES, (k + 1) * LANES)
            cw, cb, c4 = cw_all[:, lanes], cb_all[:, lanes], c4_all[:, lanes]
            xm = xr_ref[pl.ds(base, rc), lanes]
            if chunks_per_seq > 1:
                cs = ci % chunks_per_seq
                p0 = pl.multiple_of(jnp.maximum(base - SUBLANES, 0), SUBLANES)
                n0 = pl.multiple_of(jnp.minimum(base + rc, seq * nsb - SUBLANES), SUBLANES)
                prev = jnp.where(cs == 0, 0.0, xr_ref[pl.ds(p0, SUBLANES), lanes])
                nxt = jnp.where(cs == chunks_per_seq - 1, 0.0, xr_ref[pl.ds(n0, SUBLANES), lanes])
                xc = conv_segment(xm, prev, nxt, cw, cb)
            else:
                zeros = jnp.zeros((SUBLANES, LANES), F32)
                xc = jnp.concatenate(
                    [conv_segment(xm[j * seg:(j + 1) * seg], zeros, zeros, cw, cb) for j in range(segs)],
                    axis=0)
            g = jnp.dot(xc.astype(BF16), wg_ref[k], preferred_element_type=F32) + bg_ref[k]
            hxc = 0.5 * xc
            for d in range(2):
                th_r = jnp.tanh(g[:, (2 * d) * LANES:(2 * d + 1) * LANES])
                th_i = jnp.tanh(g[:, (2 * d + 1) * LANES:(2 * d + 2) * LANES])
                log_a = c4[d:d + 1] * th_r + c4[d:d + 1]
                a = jnp.exp(log_a)
                om = jnp.tanh(log_a) * (-1.0 - a * a)
                root = jnp.where(om > 0.0, om * lax.rsqrt(om), 0.0)
                b = root * ((th_i + 1.0) * hxc)
                for j in range(segs):
                    dst = scratch_row(ci, j)
                    a_sc[d * nk + k, pl.ds(dst, seg), :] = a[j * seg:(j + 1) * seg]
                    b_sc[d * nk + k, pl.ds(dst, seg), :] = b[j * seg:(j + 1) * seg]
        return 0

    lax.fori_loop(0, n_rc, gates, 0)

    def scan_sequence(s, _):
        off = s * sp_rows

        def rows(t):
            return pl.ds(off + t, SUBLANES, stride=pitch)

        def local(t, carry):
            out = []
            for slab, (h, p) in enumerate(carry):
                tt = t if slab < nk else pitch - 1 - t
                a = a_sc[slab, rows(tt), :]
                h = a * h + b_sc[slab, rows(tt), :]
                p = a * p
                h_sc[slab, rows(tt), :] = h
                p_sc[slab, rows(tt), :] = p
                out.append((h, p))
            return tuple(out)

        zero = jnp.zeros((SUBLANES, LANES), F32)
        one = jnp.ones((SUBLANES, LANES), F32)
        ends = lax.fori_loop(0, pitch, local, ((zero, one),) * (2 * nk), unroll=unroll)

        h0 = h0_ref[s]
        starts = []
        finals = [[None] * nk, [None] * nk]
        for slab, (h, p) in enumerate(ends):
            d, k = divmod(slab, nk)
            c = h0[d:d + 1, k * LANES:(k + 1) * LANES]
            cs = [None] * SUBLANES
            for j in (range(SUBLANES) if d == 0 else reversed(range(SUBLANES))):
                cs[j] = c
                c = h[j:j + 1] + p[j:j + 1] * c
            finals[d][k] = c
            starts.append(jnp.concatenate(cs, axis=0))
        st_ref[s] = jnp.concatenate(
            [jnp.concatenate(finals[0], axis=1), jnp.concatenate(finals[1], axis=1)], axis=0)

        def fix(t, _):
            for k in range(nk):
                a_sc[k, rows(t), :] = (h_sc[k, rows(t), :] + p_sc[k, rows(t), :] * starts[k]
                                       + h_sc[nk + k, rows(t), :] + p_sc[nk + k, rows(t), :] * starts[nk + k])
            return 0

        lax.fori_loop(0, pitch, fix, 0, unroll=unroll)
        return 0

    if nsb == 1:
        scan_sequence(0, 0)
    else:
        lax.fori_loop(0, nsb, scan_sequence, 0)

    def gate_out(ci, _):
        base = pl.multiple_of(ci * rc, rc)
        for k in range(nk):
            lanes = slice(k * LANES, (k + 1) * LANES)
            gy = _gelu(yr_ref[pl.ds(base, rc), lanes])
            for j in range(segs):
                src = scratch_row(ci, j)
                o_ref[pl.ds(base + j * seg, seg), lanes] = (gy[j * seg:(j + 1) * seg]
                                                            * a_sc[k, pl.ds(src, seg), :])
        return 0

    lax.fori_loop(0, n_rc, gate_out, 0)


def _lru(xy, conv_w, conv_b, wg, bg, lam, h0, later_weight, *, row_start, m, seq, nsb):
    nblk = m // (seq * nsb)
    assert row_start % (seq * nsb) == 0
    blk0 = row_start // (seq * nsb)
    nk = LRU_LANE_BLOCKS
    width = nk * LANES
    ncb = LRU_WIDTH // width
    pitch = _scan_pitch(seq)
    sc_rows = nsb * SUBLANES * pitch
    rows = seq * nsb
    cast_spec = _cast_slice_spec(later_weight.shape, (ncb, nblk))
    return pl.pallas_call(
        functools.partial(_lru_kernel, seq=seq, nsb=nsb, nk=nk, pitch=pitch,
                          rc=LRU_ROW_CHUNK, unroll=LRU_SCAN_UNROLL),
        grid=(ncb, nblk),
        in_specs=[
            pl.BlockSpec((rows, width), lambda j, b: (blk0 + b, j)),
            pl.BlockSpec((rows, width), lambda j, b: (blk0 + b, ncb + j)),
            pl.BlockSpec((4, width), lambda j, b: (0, j)),
            pl.BlockSpec((1, width), lambda j, b: (0, j)),
            pl.BlockSpec((nk, LANES, 4 * LANES), lambda j, b: (j, 0, 0)),
            pl.BlockSpec((nk, 1, 4 * LANES), lambda j, b: (j, 0, 0)),
            pl.BlockSpec((2, width), lambda j, b: (0, j)),
            pl.BlockSpec((nsb, 2, width), lambda j, b: (b, 0, j)),
            cast_spec,
        ],
        out_specs=[
            pl.BlockSpec((rows, width), lambda j, b: (b, j)),
            pl.BlockSpec((nsb, 2, width), lambda j, b: (b, 0, j)),
            cast_spec,
        ],
        out_shape=[
            jax.ShapeDtypeStruct((m, LRU_WIDTH), F32),
            jax.ShapeDtypeStruct((m // seq, 2, LRU_WIDTH), F32),
            jax.ShapeDtypeStruct(later_weight.shape, BF16),
        ],
        scratch_shapes=[pltpu.VMEM((2 * nk, sc_rows, LANES), F32)] * 4,
        compiler_params=pltpu.CompilerParams(
            dimension_semantics=("arbitrary", "arbitrary"), vmem_limit_bytes=LRU_VMEM),
        name="rglru",
    )(xy, xy, conv_w, conv_b, wg, bg, lam, h0, later_weight)


GROUP = SUBLANES * SUBLANES


def _swap_rows(slab_ref, val, r0):
    n = val.shape[0]
    for k in range(D_MODEL // LANES):
        slab_ref[k, r0:r0 + n, :] = val[:, k * LANES:(k + 1) * LANES]
    cols = []
    for k in range(D_MODEL // LANES):
        rows = [slab_ref[k, pl.ds(r0 + g * GROUP + t, SUBLANES, stride=SUBLANES), :]
                for g in range(n // GROUP) for t in range(SUBLANES)]
        cols.append(jnp.concatenate(rows, axis=0))
    return jnp.concatenate(cols, axis=1)


def _ffn_tile(x_ref, ol_ref, os_ref, mod_ref, gl_ref, gs_ref, wout_ref, n2_ref, up_ref,
              cw_ref, cb_ref, down_ref, fn_ref, y_ref, slab_sc, *, tm, fc, es, period, row):
    g1 = _mod_row(mod_ref, 2, row)
    sh2 = _mod_row(mod_ref, 3, row)
    sc2 = _mod_row(mod_ref, 4, row)
    g2 = _mod_row(mod_ref, 5, row)
    nl = _rms(ol_ref[...], gl_ref[...]).astype(BF16)
    ns = _rms(os_ref[...], gs_ref[...]).astype(BF16)
    o = (jnp.dot(nl, wout_ref[0:LRU_WIDTH, :], preferred_element_type=F32)
         + jnp.dot(ns, wout_ref[LRU_WIDTH:LRU_WIDTH + SGU_WIDTH, :], preferred_element_type=F32))
    x1 = x_ref[...] + g1 * o
    h2f = _rms(x1, n2_ref[...] * (1.0 + sc2)) + sh2
    h2 = _swap_rows(slab_sc, h2f, 0).astype(BF16)

    sub = lax.broadcasted_iota(jnp.int32, (SUBLANES, fc), 0)
    groups_per_period = period // GROUP
    n_groups = tm // GROUP

    def conv(z, c0):
        w = cw_ref[:, c0:c0 + fc]
        b = cb_ref[:, c0:c0 + fc]
        w0, w1, w2 = w[0:1], w[1:2], w[2:3]
        z8 = [[z[g * GROUP + t * SUBLANES:g * GROUP + (t + 1) * SUBLANES] for t in range(SUBLANES)]
              for g in range(n_groups)]
        outs = []
        for g in range(n_groups):
            cur = z8[g]
            lo = pltpu.roll(cur[SUBLANES - 1], 1, axis=0)
            if g % groups_per_period == 0:
                lo = jnp.where(sub == 0, 0.0, lo)
            else:
                lo = jnp.where(sub == 0, pltpu.roll(z8[g - 1][SUBLANES - 1], 1, axis=0), lo)
            hi = pltpu.roll(cur[0], SUBLANES - 1, axis=0)
            if g % groups_per_period == groups_per_period - 1:
                hi = jnp.where(sub == SUBLANES - 1, 0.0, hi)
            else:
                hi = jnp.where(sub == SUBLANES - 1, pltpu.roll(z8[g + 1][0], SUBLANES - 1, axis=0), hi)
            zm = [lo] + cur[:SUBLANES - 1]
            zp = cur[1:] + [hi]
            for t in range(SUBLANES):
                outs.append(b + w0 * zm[t] + w1 * cur[t] + w2 * zp[t])
        return jnp.concatenate(outs, axis=0)

    def up(f0):
        zg = jnp.dot(h2, up_ref[:, f0:f0 + fc], preferred_element_type=F32)
        zv = jnp.dot(h2, up_ref[:, D_FF + f0:D_FF + f0 + fc], preferred_element_type=F32)
        return zg, zv

    acc = jnp.zeros((tm, D_MODEL), F32)
    nxt = up(0)
    for f0 in range(0, D_FF, fc):
        zg, zv = nxt
        if f0 + fc < D_FF:
            nxt = up(f0 + fc)
        hg = 0.5 * conv(zg, f0)
        act = (hg * (jnp.tanh(hg) + 1.0) * conv(zv, D_FF + f0)).astype(BF16)
        if f0 + fc < D_FF:
            acc = acc + jnp.dot(act, down_ref[f0:f0 + fc, :], preferred_element_type=F32)
        else:
            for r0 in range(0, tm, es):
                a = acc[r0:r0 + es] + jnp.dot(act[r0:r0 + es], down_ref[f0:f0 + fc, :],
                                              preferred_element_type=F32)
                x2 = x1[r0:r0 + es] + g2 * _swap_rows(slab_sc, a, r0)
                y_ref[r0:r0 + es, :] = _rms(x2, fn_ref[...])


def _ffn_kernel(xc_ref, xl_ref, olc_ref, oll_ref, os_ref, mod_ref, gl_ref, gs_ref, wout_ref, n2_ref,
                up_ref, cw_ref, cb_ref, down_ref, fn_ref, yc_ref, yl_ref, slab_sc, *,
                nc, lat_mod_row, ctx_period, lat_period, **tile):
    i = pl.program_id(0)
    shared = (mod_ref, gl_ref, gs_ref, wout_ref, n2_ref, up_ref, cw_ref, cb_ref, down_ref, fn_ref)

    @pl.when(i < nc)
    def _():
        _ffn_tile(xc_ref, olc_ref, os_ref, *shared, yc_ref, slab_sc, period=ctx_period, row=0, **tile)

    @pl.when(i >= nc)
    def _():
        _ffn_tile(xl_ref, oll_ref, os_ref, *shared, yl_ref, slab_sc, period=lat_period,
                  row=lat_mod_row(i), **tile)


def _ffn(xc, xl, olru_c, olru_l, osgu, mod, g_lru, g_sgu, w_out, norm2, ffn_up, conv_w, conv_b,
         ffn_down, final_norm, *, tm, ctx_period, lat_period, lat_seq):
    nc, n, ctx_map, lat_map, lat_mod_row = _token_groups(tm, xc.shape[0], xl.shape[0], lat_seq)
    const = lambda i: (0, 0)
    resident = lambda shape: pl.BlockSpec(shape, const, pipeline_mode=pl.Buffered(1))
    return pl.pallas_call(
        functools.partial(_ffn_kernel, nc=nc, lat_mod_row=lat_mod_row, ctx_period=ctx_period,
                          lat_period=lat_period, tm=tm, fc=FFN_FC, es=FFN_ES),
        grid=(n,),
        in_specs=[
            pl.BlockSpec((tm, D_MODEL), ctx_map),
            pl.BlockSpec((tm, D_MODEL), lat_map),
            pl.BlockSpec((tm, LRU_WIDTH), ctx_map),
            pl.BlockSpec((tm, LRU_WIDTH), lat_map),
            pl.BlockSpec((tm, SGU_WIDTH), lambda i: (i, 0)),
            pl.BlockSpec((N_MOD, MOD_ROWS, D_MODEL), lambda i: (0, 0, 0)),
            pl.BlockSpec((1, LRU_WIDTH), const),
            pl.BlockSpec((1, SGU_WIDTH), const),
            resident((LRU_WIDTH + SGU_WIDTH, D_MODEL)),
            pl.BlockSpec((1, D_MODEL), const),
            resident((D_MODEL, 2 * D_FF)),
            pl.BlockSpec((3, 2 * D_FF), const),
            pl.BlockSpec((1, 2 * D_FF), const),
            resident((D_FF, D_MODEL)),
            pl.BlockSpec((1, D_MODEL), const),
        ],
        out_specs=[pl.BlockSpec((tm, D_MODEL), ctx_map), pl.BlockSpec((tm, D_MODEL), lat_map)],
        out_shape=[jax.ShapeDtypeStruct(xc.shape, F32), jax.ShapeDtypeStruct(xl.shape, F32)],
        scratch_shapes=[pltpu.VMEM((D_MODEL // LANES, tm, LANES), F32)],
        compiler_params=pltpu.CompilerParams(
            dimension_semantics=("arbitrary",), vmem_limit_bytes=FFN_VMEM),
        name="outproj_ffn",
    )(xc, xl, olru_c, olru_l, osgu, mod, g_lru, g_sgu, w_out, norm2, ffn_up, conv_w, conv_b, ffn_down,
      final_norm)


def _gate_params(wa, ba, wx, bx):
    nlb = LRU_HEADS // 2
    w = jnp.stack([wa[0], wx[0], wa[1], wx[1]], axis=0)
    wp = w.reshape(4, nlb, 2, LRU_HEAD_DIM, LRU_HEAD_DIM)
    pair = 0.5 * jnp.eye(2, dtype=F32)
    bd = wp[:, :, :, :, None, :] * pair[None, None, :, None, :, None]
    bd = bd.reshape(4, nlb, LANES, LANES)
    wg = jnp.transpose(bd, (1, 2, 0, 3)).reshape(nlb, LANES, 4 * LANES).astype(BF16)
    b = 0.5 * jnp.stack([ba[0], bx[0], ba[1], bx[1]], axis=0)
    bg = jnp.transpose(b.reshape(4, nlb, LANES), (1, 0, 2)).reshape(nlb, 1, 4 * LANES)
    return wg, bg


def kernel(x_prompt, x_sample, state_lru, c, c_ctx, norm1, norm2, w_ada, b_ada, w_in, lru_conv_w,
           lru_conv_b, lru_wa, lru_ba, lru_wx, lru_bx, lru_lam, sgu_ws, sgu_bs, g_lru, g_sgu, w_out,
           ffn_up, ffn_conv_w, ffn_conv_b, ffn_down, final_norm):
    batch, seq, _ = x_prompt.shape
    dec_batch, dec_seq, _ = x_sample.shape
    depth = norm1.shape[0]
    assert depth == 1, "the final norm is fused into the (single) layer's last kernel"
    l = 0

    xc = x_prompt.reshape(batch * seq, D_MODEL)
    xl = x_sample.reshape(dec_batch * dec_seq, D_MODEL)
    n_ctx, n_lat = xc.shape[0], xl.shape[0]
    cc = jnp.concatenate(
        [c_ctx[None, :], c, jnp.zeros((MOD_ROWS - 1 - dec_batch, D_MODEL), F32)], axis=0)
    zeros_state = jnp.zeros((batch, 2, LRU_WIDTH), x_prompt.dtype)

    mod = _modulation(cc, w_ada[l], b_ada[l][None, :])
    wg, bg = _gate_params(lru_wa[l], lru_ba[l], lru_wx[l], lru_bx[l])
    lru_params = (lru_conv_w[l], lru_conv_b[l][None, :], wg, bg, lru_lam[l])
    h0_lat = state_lru.reshape(dec_batch, 2, LRU_WIDTH)

    xy, osgu, w_out_b = _inproj(
        xc, xl, mod, norm1[l][None, :], w_in[l], sgu_ws[l].astype(BF16), sgu_bs[l].T,
        w_out[l], tm=INPROJ_TM, lat_seq=dec_seq)
    olru_c, new_state, down_b = _lru(xy, *lru_params, zeros_state, ffn_down[l], row_start=0, m=n_ctx,
                                     seq=seq, nsb=LRU_CTX_SEQS)
    olru_l, _, up_b = _lru(xy, *lru_params, h0_lat, ffn_up[l], row_start=n_ctx, m=n_lat,
                           seq=dec_seq, nsb=1)
    yc, yl = _ffn(xc, xl, olru_c, olru_l, osgu, mod, g_lru[l][None, :], g_sgu[l][None, :],
                  w_out_b, norm2[l][None, :], up_b, ffn_conv_w[l],
                  ffn_conv_b[l][None, :], down_b, final_norm[None, :], tm=FFN_TM,
                  ctx_period=seq, lat_period=GRID_W, lat_seq=dec_seq)
    return (yc.reshape(batch, seq, D_MODEL), yl.reshape(dec_batch, dec_seq, D_MODEL),
            new_state[:, None])
```

```python
import functools
import math

import jax
import jax.numpy as jnp
from jax import lax
from jax.experimental import pallas as pl
from jax.experimental.pallas import tpu as pltpu

D_MODEL = 1024
LRU_HEADS = 8
LRU_WIDTH = 512
LRU_HEAD_DIM = 64
LRU_C = 8.0
SGU_GROUPS = 4
SGU_WIDTH = 512
CHUNK = 128
D_FF = 3072
N_MOD = 6
EPS = 1e-6
GRID_W = 64

LANES = 128
SUBLANES = 8
BF16_SUBLANES = 16
MOD_ROWS = 8
BF16 = jnp.bfloat16
F32 = jnp.float32

MIB = 1024 * 1024
INPROJ_TM = 1024
INPROJ_VMEM = 56 * MIB
LRU_ROW_CHUNK = 512
LRU_LANE_BLOCKS = 2
LRU_CTX_SEQS = 8
LRU_SCAN_UNROLL = 8
LRU_VMEM = 56 * MIB
FFN_TM = 512
FFN_FC = 1024
FFN_ES = 128
FFN_VMEM = 60 * MIB

_GELU_C0 = math.sqrt(2.0 / math.pi)
_GELU_C1 = _GELU_C0 * 0.044715


def _rms(x, gain):
    return x * lax.rsqrt(jnp.mean(x * x, axis=-1, keepdims=True) + EPS) * gain


def _gelu(x):
    return (0.5 * x) * (1.0 + jnp.tanh(x * (_GELU_C0 + _GELU_C1 * (x * x))))


def _mod_row(mod_ref, k, row):
    return mod_ref[k, pl.ds(row, 1), :]


def _mod_kernel(c_ref, w_ref, b_ref, o_ref):
    s = jax.nn.silu(c_ref[...]).astype(BF16)
    o_ref[...] = jnp.dot(s, w_ref[...].astype(BF16), preferred_element_type=F32) + b_ref[...]


def _modulation(cc, w_ada, b_ada):
    tn = D_MODEL
    per = D_MODEL // tn
    return pl.pallas_call(
        _mod_kernel,
        grid=(N_MOD * per,),
        in_specs=[
            pl.BlockSpec((MOD_ROWS, D_MODEL), lambda j: (0, 0)),
            pl.BlockSpec((D_MODEL, tn), lambda j: (0, j)),
            pl.BlockSpec((1, tn), lambda j: (0, j)),
        ],
        out_specs=pl.BlockSpec((None, MOD_ROWS, tn), lambda j: (j // per, 0, j % per)),
        out_shape=jax.ShapeDtypeStruct((N_MOD, MOD_ROWS, D_MODEL), F32),
        compiler_params=pltpu.CompilerParams(dimension_semantics=("arbitrary",)),
        name="modulation",
    )(cc, w_ada, b_ada)


def _token_groups(tm, n_ctx, n_lat, lat_seq):
    assert n_ctx % tm == 0 and n_lat % tm == 0 and lat_seq % tm == 0
    nc = n_ctx // tm
    ctx_map = lambda i: (jnp.minimum(i, nc - 1), 0)
    lat_map = lambda i: (jnp.maximum(i - nc, 0), 0)
    lat_mod_row = lambda i: 1 + ((i - nc) * tm) // lat_seq
    return nc, nc + n_lat // tm, ctx_map, lat_map, lat_mod_row


def _inproj_kernel(xc_ref, xl_ref, mod_ref, n1_ref, win_f_ref, ws_ref, bs_ref, w_ref,
                   xy_ref, osgu_ref, w_b_ref, win_ref, *, tm, nc, lat_mod_row):
    i = pl.program_id(0)
    w_b_ref[...] = w_ref[...].astype(BF16)

    @pl.when(i == 0)
    def _():
        win_ref[...] = win_f_ref[...].astype(BF16)

    def body(x_ref, row):
        sh1 = _mod_row(mod_ref, 0, row)
        sc1 = _mod_row(mod_ref, 1, row)
        hb = (_rms(x_ref[...], n1_ref[...] * (1.0 + sc1)) + sh1).astype(BF16)
        uv = jnp.dot(hb, win_ref[:, 2 * LRU_WIDTH:], preferred_element_type=F32)
        xy_ref[...] = jnp.dot(hb, win_ref[:, 0:2 * LRU_WIDTH], preferred_element_type=F32)
        for g in range(SGU_GROUPS):
            lo = g * LANES
            gu = _gelu(uv[:, lo:lo + LANES])
            gv = _gelu(uv[:, SGU_WIDTH + lo:SGU_WIDTH + lo + LANES]).astype(BF16)
            bias = jnp.broadcast_to(bs_ref[:, g:g + 1], (CHUNK, LANES))
            for ck in range(tm // CHUNK):
                r0 = ck * CHUNK
                s = jnp.dot(ws_ref[g], gv[r0:r0 + CHUNK], preferred_element_type=F32) + bias
                osgu_ref[r0:r0 + CHUNK, lo:lo + LANES] = gu[r0:r0 + CHUNK] * s

    @pl.when(i < nc)
    def _():
        body(xc_ref, 0)

    @pl.when(i >= nc)
    def _():
        body(xl_ref, lat_mod_row(i))


def _cast_slice_spec(shape, grid):
    n_steps = math.prod(grid)
    rows = shape[0]
    r = BF16_SUBLANES
    while rows % r or rows // r > n_steps:
        r += BF16_SUBLANES
    last = rows // r - 1

    def index_map(*idx):
        step = 0
        for i, n in zip(idx, grid):
            step = step * n + i
        return (jnp.minimum(step, last), 0)

    return pl.BlockSpec((r, shape[1]), index_map)


def _inproj(xc, xl, mod, norm1, w_in, ws, bs_t, later_weight, *, tm, lat_seq):
    nc, n, ctx_map, lat_map, lat_mod_row = _token_groups(tm, xc.shape[0], xl.shape[0], lat_seq)
    m = n * tm
    const = lambda i: (0, 0)
    cast_specs = [_cast_slice_spec(later_weight.shape, (n,))]
    return pl.pallas_call(
        functools.partial(_inproj_kernel, tm=tm, nc=nc, lat_mod_row=lat_mod_row),
        grid=(n,),
        in_specs=[
            pl.BlockSpec((tm, D_MODEL), ctx_map),
            pl.BlockSpec((tm, D_MODEL), lat_map),
            pl.BlockSpec((N_MOD, MOD_ROWS, D_MODEL), lambda i: (0, 0, 0)),
            pl.BlockSpec((1, D_MODEL), const),
            pl.BlockSpec((D_MODEL, 4 * LRU_WIDTH), const, pipeline_mode=pl.Buffered(1)),
            pl.BlockSpec((SGU_GROUPS, CHUNK, CHUNK), lambda i: (0, 0, 0)),
            pl.BlockSpec((CHUNK, SGU_GROUPS), const),
        ] + cast_specs,
        out_specs=[
            pl.BlockSpec((tm, 2 * LRU_WIDTH), lambda i: (i, 0)),
            pl.BlockSpec((tm, SGU_WIDTH), lambda i: (i, 0)),
        ] + cast_specs,
        out_shape=[
            jax.ShapeDtypeStruct((m, 2 * LRU_WIDTH), F32),
            jax.ShapeDtypeStruct((m, SGU_WIDTH), F32),
        ] + [jax.ShapeDtypeStruct(later_weight.shape, BF16)],
        scratch_shapes=[pltpu.VMEM(w_in.shape, BF16)],
        compiler_params=pltpu.CompilerParams(
            dimension_semantics=("arbitrary",), vmem_limit_bytes=INPROJ_VMEM),
        name="inproj_sgu",
    )(xc, xl, mod, norm1, w_in, ws, bs_t, later_weight)


def _scan_pitch(seq):
    pitch = seq // SUBLANES
    return pitch + (4 - pitch % 8) % 8


def _lru_kernel(xr_ref, yr_ref, cw_ref, cb_ref, wg_ref, bg_ref, lam_ref, h0_ref, w_ref,
                o_ref, st_ref, w_b_ref, a_sc, b_sc, p_sc, h_sc, *, seq, nsb, nk, pitch, rc, unroll):
    sp_rows = SUBLANES * pitch
    n_rc = (seq * nsb) // rc
    seg = min(seq, rc)
    segs = rc // seg
    chunks_per_seq = seq // seg
    w_b_ref[...] = w_ref[...].astype(BF16)
    c4_all = (-0.5 * LRU_C) * jax.nn.softplus(-lam_ref[...])
    cw_all = cw_ref[...]
    cb_all = cb_ref[...]

    for s in range(nsb):
        lo, hi = s * sp_rows + seq, (s + 1) * sp_rows
        for slab in range(2 * nk):
            a_sc[slab, lo:hi, :] = jnp.ones((hi - lo, LANES), F32)
            b_sc[slab, lo:hi, :] = jnp.zeros((hi - lo, LANES), F32)

    def scratch_row(ci, k):
        if chunks_per_seq > 1:
            return pl.multiple_of((ci // chunks_per_seq) * sp_rows + (ci % chunks_per_seq) * seg,
                                  SUBLANES)
        return pl.multiple_of((ci * segs + k) * sp_rows, SUBLANES)

    def conv_segment(xs, prev, nxt, cw, cb):
        ext = jnp.concatenate([prev, xs, nxt], axis=0)
        n_ext = seg + 2 * SUBLANES
        xc = cb + cw[2:3] * xs
        xc = xc + cw[0:1] * pltpu.roll(ext, 2, axis=0)[SUBLANES:SUBLANES + seg]
        xc = xc + cw[1:2] * pltpu.roll(ext, 1, axis=0)[SUBLANES:SUBLANES + seg]
        return xc + cw[3:4] * pltpu.roll(ext, n_ext - 1, axis=0)[SUBLANES:SUBLANES + seg]

    def gates(ci, _):
        base = pl.multiple_of(ci * rc, rc)
        for k in range(nk):
            lanes = slice(k * LANES, (k + 1) * LANES)
            cw, cb, c4 = cw_all[:, lanes], cb_all[:, lanes], c4_all[:, lanes]
            xm = xr_ref[pl.ds(base, rc), lanes]
            if chunks_per_seq > 1:
                cs = ci % chunks_per_seq
                p0 = pl.multiple_of(jnp.maximum(base - SUBLANES, 0), SUBLANES)
                n0 = pl.multiple_of(jnp.minimum(base + rc, seq * nsb - SUBLANES), SUBLANES)
                prev = jnp.where(cs == 0, 0.0, xr_ref[pl.ds(p0, SUBLANES), lanes])
                nxt = jnp.where(cs == chunks_per_seq - 1, 0.0, xr_ref[pl.ds(n0, SUBLANES), lanes])
                xc = conv_segment(xm, prev, nxt, cw, cb)
            else:
                zeros = jnp.zeros((SUBLANES, LANES), F32)
                xc = jnp.concatenate(
                    [conv_segment(xm[j * seg:(j + 1) * seg], zeros, zeros, cw, cb) for j in range(segs)],
                    axis=0)
            g = jnp.dot(xc.astype(BF16), wg_ref[k], preferred_element_type=F32) + bg_ref[k]
            hxc = 0.5 * xc
            for d in range(2):
                th_r = jnp.tanh(g[:, (2 * d) * LANES:(2 * d + 1) * LANES])
                th_i = jnp.tanh(g[:, (2 * d + 1) * LANES:(2 * d + 2) * LANES])
                log_a = c4[d:d + 1] * th_r + c4[d:d + 1]
                a = jnp.exp(log_a)
                om = jnp.tanh(log_a) * (-1.0 - a * a)
                root = jnp.where(om > 0.0, om * lax.rsqrt(om), 0.0)
                b = root * ((th_i + 1.0) * hxc)
                for j in range(segs):
                    dst = scratch_row(ci, j)
                    a_sc[d * nk + k, pl.ds(dst, seg), :] = a[j * seg:(j + 1) * seg]
                    b_sc[d * nk + k, pl.ds(dst, seg), :] = b[j * seg:(j + 1) * seg]
        return 0

    lax.fori_loop(0, n_rc, gates, 0)

    def scan_sequence(s, _):
        off = s * sp_rows

        def rows(t):
            return pl.ds(off + t, SUBLANES, stride=pitch)

        def local(t, carry):
            out = []
            for slab, (h, p) in enumerate(carry):
                tt = t if slab < nk else pitch - 1 - t
                a = a_sc[slab, rows(tt), :]
                h = a * h + b_sc[slab, rows(tt), :]
                p = a * p
                h_sc[slab, rows(tt), :] = h
                p_sc[slab, rows(tt), :] = p
                out.append((h, p))
            return tuple(out)

        zero = jnp.zeros((SUBLANES, LANES), F32)
        one = jnp.ones((SUBLANES, LANES), F32)
        ends = lax.fori_loop(0, pitch, local, ((zero, one),) * (2 * nk), unroll=unroll)

        h0 = h0_ref[s]
        starts = []
        finals = [[None] * nk, [None] * nk]
        for slab, (h, p) in enumerate(ends):
            d, k = divmod(slab, nk)
            c = h0[d:d + 1, k * LANES:(k + 1) * LANES]
            cs = [None] * SUBLANES
            for j in (range(SUBLANES) if d == 0 else reversed(range(SUBLANES))):
                cs[j] = c
                c = h[j:j + 1] + p[j:j + 1] * c
            finals[d][k] = c
            starts.append(jnp.concatenate(cs, axis=0))
        st_ref[s] = jnp.concatenate(
            [jnp.concatenate(finals[0], axis=1), jnp.concatenate(finals[1], axis=1)], axis=0)

        def fix(t, _):
            for k in range(nk):
                a_sc[k, rows(t), :] = (h_sc[k, rows(t), :] + p_sc[k, rows(t), :] * starts[k]
                                       + h_sc[nk + k, rows(t), :] + p_sc[nk + k, rows(t), :] * starts[nk + k])
            return 0

        lax.fori_loop(0, pitch, fix, 0, unroll=unroll)
        return 0

    if nsb == 1:
        scan_sequence(0, 0)
    else:
        lax.fori_loop(0, nsb, scan_sequence, 0)

    def gate_out(ci, _):
        base = pl.multiple_of(ci * rc, rc)
        for k in range(nk):
            lanes = slice(k * LANES, (k + 1) * LANES)
            gy = _gelu(yr_ref[pl.ds(base, rc), lanes])
            for j in range(segs):
                src = scratch_row(ci, j)
                o_ref[pl.ds(base + j * seg, seg), lanes] = (gy[j * seg:(j + 1) * seg]
                                                            * a_sc[k, pl.ds(src, seg), :])
        return 0

    lax.fori_loop(0, n_rc, gate_out, 0)


def _lru(xy, conv_w, conv_b, wg, bg, lam, h0, later_weight, *, row_start, m, seq, nsb):
    nblk = m // (seq * nsb)
    assert row_start % (seq * nsb) == 0
    blk0 = row_start // (seq * nsb)
    nk = LRU_LANE_BLOCKS
    width = nk * LANES
    ncb = LRU_WIDTH // width
    pitch = _scan_pitch(seq)
    sc_rows = nsb * SUBLANES * pitch
    rows = seq * nsb
    cast_spec = _cast_slice_spec(later_weight.shape, (ncb, nblk))
    return pl.pallas_call(
        functools.partial(_lru_kernel, seq=seq, nsb=nsb, nk=nk, pitch=pitch,
                          rc=LRU_ROW_CHUNK, unroll=LRU_SCAN_UNROLL),
        grid=(ncb, nblk),
        in_specs=[
            pl.BlockSpec((rows, width), lambda j, b: (blk0 + b, j)),
            pl.BlockSpec((rows, width), lambda j, b: (blk0 + b, ncb + j)),
            pl.BlockSpec((4, width), lambda j, b: (0, j)),
            pl.BlockSpec((1, width), lambda j, b: (0, j)),
            pl.BlockSpec((nk, LANES, 4 * LANES), lambda j, b: (j, 0, 0)),
            pl.BlockSpec((nk, 1, 4 * LANES), lambda j, b: (j, 0, 0)),
            pl.BlockSpec((2, width), lambda j, b: (0, j)),
            pl.BlockSpec((nsb, 2, width), lambda j, b: (b, 0, j)),
            cast_spec,
        ],
        out_specs=[
            pl.BlockSpec((rows, width), lambda j, b: (b, j)),
            pl.BlockSpec((nsb, 2, width), lambda j, b: (b, 0, j)),
            cast_spec,
        ],
        out_shape=[
            jax.ShapeDtypeStruct((m, LRU_WIDTH), F32),
            jax.ShapeDtypeStruct((m // seq, 2, LRU_WIDTH), F32),
            jax.ShapeDtypeStruct(later_weight.shape, BF16),
        ],
        scratch_shapes=[pltpu.VMEM((2 * nk, sc_rows, LANES), F32)] * 4,
        compiler_params=pltpu.CompilerParams(
            dimension_semantics=("arbitrary", "arbitrary"), vmem_limit_bytes=LRU_VMEM),
        name="rglru",
    )(xy, xy, conv_w, conv_b, wg, bg, lam, h0, later_weight)


GROUP = SUBLANES * SUBLANES


def _swap_rows(slab_ref, val, r0):
    n = val.shape[0]
    for k in range(D_MODEL // LANES):
        slab_ref[k, r0:r0 + n, :] = val[:, k * LANES:(k + 1) * LANES]
    cols = []
    for k in range(D_MODEL // LANES):
        rows = [slab_ref[k, pl.ds(r0 + g * GROUP + t, SUBLANES, stride=SUBLANES), :]
                for g in range(n // GROUP) for t in range(SUBLANES)]
        cols.append(jnp.concatenate(rows, axis=0))
    return jnp.concatenate(cols, axis=1)


def _ffn_tile(x_ref, ol_ref, os_ref, mod_ref, gl_ref, gs_ref, wout_ref, n2_ref, up_ref,
              cw_ref, cb_ref, down_ref, fn_ref, y_ref, slab_sc, *, tm, fc, es, period, row):
    g1 = _mod_row(mod_ref, 2, row)
    sh2 = _mod_row(mod_ref, 3, row)
    sc2 = _mod_row(mod_ref, 4, row)
    g2 = _mod_row(mod_ref, 5, row)
    nl = _rms(ol_ref[...], gl_ref[...]).astype(BF16)
    ns = _rms(os_ref[...], gs_ref[...]).astype(BF16)
    o = (jnp.dot(nl, wout_ref[0:LRU_WIDTH, :], preferred_element_type=F32)
         + jnp.dot(ns, wout_ref[LRU_WIDTH:LRU_WIDTH + SGU_WIDTH, :], preferred_element_type=F32))
    x1 = x_ref[...] + g1 * o
    h2f = _rms(x1, n2_ref[...] * (1.0 + sc2)) + sh2
    h2 = _swap_rows(slab_sc, h2f, 0).astype(BF16)

    sub = lax.broadcasted_iota(jnp.int32, (SUBLANES, fc), 0)
    groups_per_period = period // GROUP
    n_groups = tm // GROUP

    def conv(z, c0):
        w = cw_ref[:, c0:c0 + fc]
        b = cb_ref[:, c0:c0 + fc]
        w0, w1, w2 = w[0:1], w[1:2], w[2:3]
        z8 = [[z[g * GROUP + t * SUBLANES:g * GROUP + (t + 1) * SUBLANES] for t in range(SUBLANES)]
              for g in range(n_groups)]
        outs = []
        for g in range(n_groups):
            cur = z8[g]
            lo = pltpu.roll(cur[SUBLANES - 1], 1, axis=0)
            if g % groups_per_period == 0:
                lo = jnp.where(sub == 0, 0.0, lo)
            else:
                lo = jnp.where(sub == 0, pltpu.roll(z8[g - 1][SUBLANES - 1], 1, axis=0), lo)
            hi = pltpu.roll(cur[0], SUBLANES - 1, axis=0)
            if g % groups_per_period == groups_per_period - 1:
                hi = jnp.where(sub == SUBLANES - 1, 0.0, hi)
            else:
                hi = jnp.where(sub == SUBLANES - 1, pltpu.roll(z8[g + 1][0], SUBLANES - 1, axis=0), hi)
            zm = [lo] + cur[:SUBLANES - 1]
            zp = cur[1:] + [hi]
            for t in range(SUBLANES):
                outs.append(b + w0 * zm[t] + w1 * cur[t] + w2 * zp[t])
        return jnp.concatenate(outs, axis=0)

    def up(f0):
        zg = jnp.dot(h2, up_ref[:, f0:f0 + fc], preferred_element_type=F32)
        zv = jnp.dot(h2, up_ref[:, D_FF + f0:D_FF + f0 + fc], preferred_element_type=F32)
        return zg, zv

    acc = jnp.zeros((tm, D_MODEL), F32)
    nxt = up(0)
    for f0 in range(0, D_FF, fc):
        zg, zv = nxt
        if f0 + fc < D_FF:
            nxt = up(f0 + fc)
        hg = 0.5 * conv(zg, f0)
        act = (hg * (jnp.tanh(hg) + 1.0) * conv(zv, D_FF + f0)).astype(BF16)
        if f0 + fc < D_FF:
            acc = acc + jnp.dot(act, down_ref[f0:f0 + fc, :], preferred_element_type=F32)
        else:
            for r0 in range(0, tm, es):
                a = acc[r0:r0 + es] + jnp.dot(act[r0:r0 + es], down_ref[f0:f0 + fc, :],
                                              preferred_element_type=F32)
                x2 = x1[r0:r0 + es] + g2 * _swap_rows(slab_sc, a, r0)
                y_ref[r0:r0 + es, :] = _rms(x2, fn_ref[...])


def _ffn_kernel(xc_ref, xl_ref, olc_ref, oll_ref, os_ref, mod_ref, gl_ref, gs_ref, wout_ref, n2_ref,
                up_ref, cw_ref, cb_ref, down_ref, fn_ref, yc_ref, yl_ref, slab_sc, *,
                nc, lat_mod_row, ctx_period, lat_period, **tile):
    i = pl.program_id(0)
    shared = (mod_ref, gl_ref, gs_ref, wout_ref, n2_ref, up_ref, cw_ref, cb_ref, down_ref, fn_ref)

    @pl.when(i < nc)
    def _():
        _ffn_tile(xc_ref, olc_ref, os_ref, *shared, yc_ref, slab_sc, period=ctx_period, row=0, **tile)

    @pl.when(i >= nc)
    def _():
        _ffn_tile(xl_ref, oll_ref, os_ref, *shared, yl_ref, slab_sc, period=lat_period,
                  row=lat_mod_row(i), **tile)


def _ffn(xc, xl, olru_c, olru_l, osgu, mod, g_lru, g_sgu, w_out, norm2, ffn_up, conv_w, conv_b,
         ffn_down, final_norm, *, tm, ctx_period, lat_period, lat_seq):
    nc, n, ctx_map, lat_map, lat_mod_row = _token_groups(tm, xc.shape[0], xl.shape[0], lat_seq)
    const = lambda i: (0, 0)
    resident = lambda shape: pl.BlockSpec(shape, const, pipeline_mode=pl.Buffered(1))
    return pl.pallas_call(
        functools.partial(_ffn_kernel, nc=nc, lat_mod_row=lat_mod_row, ctx_period=ctx_period,
                          lat_period=lat_period, tm=tm, fc=FFN_FC, es=FFN_ES),
        grid=(n,),
        in_specs=[
            pl.BlockSpec((tm, D_MODEL), ctx_map),
            pl.BlockSpec((tm, D_MODEL), lat_map),
            pl.BlockSpec((tm, LRU_WIDTH), ctx_map),
            pl.BlockSpec((tm, LRU_WIDTH), lat_map),
            pl.BlockSpec((tm, SGU_WIDTH), lambda i: (i, 0)),
            pl.BlockSpec((N_MOD, MOD_ROWS, D_MODEL), lambda i: (0, 0, 0)),
            pl.BlockSpec((1, LRU_WIDTH), const),
            pl.BlockSpec((1, SGU_WIDTH), const),
            resident((LRU_WIDTH + SGU_WIDTH, D_MODEL)),
            pl.BlockSpec((1, D_MODEL), const),
            resident((D_MODEL, 2 * D_FF)),
            pl.BlockSpec((3, 2 * D_FF), const),
            pl.BlockSpec((1, 2 * D_FF), const),
            resident((D_FF, D_MODEL)),
            pl.BlockSpec((1, D_MODEL), const),
        ],
        out_specs=[pl.BlockSpec((tm, D_MODEL), ctx_map), pl.BlockSpec((tm, D_MODEL), lat_map)],
        out_shape=[jax.ShapeDtypeStruct(xc.shape, F32), jax.ShapeDtypeStruct(xl.shape, F32)],
        scratch_shapes=[pltpu.VMEM((D_MODEL // LANES, tm, LANES), F32)],
        compiler_params=pltpu.CompilerParams(
            dimension_semantics=("arbitrary",), vmem_limit_bytes=FFN_VMEM),
        name="outproj_ffn",
    )(xc, xl, olru_c, olru_l, osgu, mod, g_lru, g_sgu, w_out, norm2, ffn_up, conv_w, conv_b, ffn_down,
      final_norm)


def _gate_params(wa, ba, wx, bx):
    nlb = LRU_HEADS // 2
    w = jnp.stack([wa[0], wx[0], wa[1], wx[1]], axis=0)
    wp = w.reshape(4, nlb, 2, LRU_HEAD_DIM, LRU_HEAD_DIM)
    pair = 0.5 * jnp.eye(2, dtype=F32)
    bd = wp[:, :, :, :, None, :] * pair[None, None, :, None, :, None]
    bd = bd.reshape(4, nlb, LANES, LANES)
    wg = jnp.transpose(bd, (1, 2, 0, 3)).reshape(nlb, LANES, 4 * LANES).astype(BF16)
    b = 0.5 * jnp.stack([ba[0], bx[0], ba[1], bx[1]], axis=0)
    bg = jnp.transpose(b.reshape(4, nlb, LANES), (1, 0, 2)).reshape(nlb, 1, 4 * LANES)
    return wg, bg


def kernel(x_prompt, x_sample, state_lru, c, c_ctx, norm1, norm2, w_ada, b_ada, w_in, lru_conv_w,
           lru_conv_b, lru_wa, lru_ba, lru_wx, lru_bx, lru_lam, sgu_ws, sgu_bs, g_lru, g_sgu, w_out,
           ffn_up, ffn_conv_w, ffn_conv_b, ffn_down, final_norm):
    batch, seq, _ = x_prompt.shape
    dec_batch, dec_seq, _ = x_sample.shape
    depth = norm1.shape[0]
    assert depth == 1, "the final norm is fused into the (single) layer's last kernel"
    l = 0

    xc = x_prompt.reshape(batch * seq, D_MODEL)
    xl = x_sample.reshape(dec_batch * dec_seq, D_MODEL)
    n_ctx, n_lat = xc.shape[0], xl.shape[0]
    cc = jnp.concatenate(
        [c_ctx[None, :], c, jnp.zeros((MOD_ROWS - 1 - dec_batch, D_MODEL), F32)], axis=0)
    zeros_state = jnp.zeros((batch, 2, LRU_WIDTH), x_prompt.dtype)

    mod = _modulation(cc, w_ada[l], b_ada[l][None, :])
    wg, bg = _gate_params(lru_wa[l], lru_ba[l], lru_wx[l], lru_bx[l])
    lru_params = (lru_conv_w[l], lru_conv_b[l][None, :], wg, bg, lru_lam[l])
    h0_lat = state_lru.reshape(dec_batch, 2, LRU_WIDTH)

    xy, osgu, w_out_b = _inproj(
        xc, xl, mod, norm1[l][None, :], w_in[l], sgu_ws[l].astype(BF16), sgu_bs[l].T,
        w_out[l], tm=INPROJ_TM, lat_seq=dec_seq)
    olru_c, new_state, down_b = _lru(xy, *lru_params, zeros_state, ffn_down[l], row_start=0, m=n_ctx,
                                     seq=seq, nsb=LRU_CTX_SEQS)
    olru_l, _, up_b = _lru(xy, *lru_params, h0_lat, ffn_up[l], row_start=n_ctx, m=n_lat,
                           seq=dec_seq, nsb=1)
    yc, yl = _ffn(xc, xl, olru_c, olru_l, osgu, mod, g_lru[l][None, :], g_sgu[l][None, :],
                  w_out_b, norm2[l][None, :], up_b, ffn_conv_w[l],
                  ffn_conv_b[l][None, :], down_b, final_norm[None, :], tm=FFN_TM,
                  ctx_period=seq, lat_period=GRID_W, lat_seq=dec_seq)
    return (yc.reshape(batch, seq, D_MODEL), yl.reshape(dec_batch, dec_seq, D_MODEL),
            new_state[:, None])
```

```python
import functools
import math

import jax
import jax.numpy as jnp
from jax import lax
from jax.experimental import pallas as pl
from jax.experimental.pallas import tpu as pltpu

D_MODEL = 1024
LRU_HEADS = 8
LRU_WIDTH = 512
LRU_HEAD_DIM = 64
LRU_C = 8.0
SGU_GROUPS = 4
SGU_WIDTH = 512
CHUNK = 128
D_FF = 3072
N_MOD = 6
EPS = 1e-6
GRID_W = 64

LANES = 128
SUBLANES = 8
BF16_SUBLANES = 16
MOD_ROWS = 8
BF16 = jnp.bfloat16
F32 = jnp.float32

MIB = 1024 * 1024
INPROJ_TM = 1024
INPROJ_VMEM = 56 * MIB
LRU_ROW_CHUNK = 512
LRU_LANE_BLOCKS = 2
LRU_CTX_SEQS = 8
LRU_SCAN_UNROLL = 8
LRU_VMEM = 56 * MIB
FFN_TM = 512
FFN_FC = 1024
FFN_ES = 128
FFN_VMEM = 60 * MIB

_GELU_C0 = math.sqrt(2.0 / math.pi)
_GELU_C1 = _GELU_C0 * 0.044715


def _rms(x, gain):
    return x * lax.rsqrt(jnp.mean(x * x, axis=-1, keepdims=True) + EPS) * gain


def _gelu(x):
    return (0.5 * x) * (1.0 + jnp.tanh(x * (_GELU_C0 + _GELU_C1 * (x * x))))


def _mod_row(mod_ref, k, row):
    return mod_ref[k, pl.ds(row, 1), :]


def _mod_kernel(c_ref, w_ref, b_ref, o_ref):
    s = jax.nn.silu(c_ref[...]).astype(BF16)
    o_ref[...] = jnp.dot(s, w_ref[...].astype(BF16), preferred_element_type=F32) + b_ref[...]


def _modulation(cc, w_ada, b_ada):
    return pl.pallas_call(
        _mod_kernel,
        grid=(N_MOD,),
        in_specs=[
            pl.BlockSpec((MOD_ROWS, D_MODEL), lambda j: (0, 0)),
            pl.BlockSpec((D_MODEL, D_MODEL), lambda j: (0, j)),
            pl.BlockSpec((1, D_MODEL), lambda j: (0, j)),
        ],
        out_specs=pl.BlockSpec((None, MOD_ROWS, D_MODEL), lambda j: (j, 0, 0)),
        out_shape=jax.ShapeDtypeStruct((N_MOD, MOD_ROWS, D_MODEL), F32),
        compiler_params=pltpu.CompilerParams(dimension_semantics=("arbitrary",)),
        name="modulation",
    )(cc, w_ada, b_ada)


def _token_groups(tm, n_ctx, n_lat, lat_seq):
    assert n_ctx % tm == 0 and n_lat % tm == 0 and lat_seq % tm == 0
    nc = n_ctx // tm
    ctx_map = lambda i: (jnp.minimum(i, nc - 1), 0)
    lat_map = lambda i: (jnp.maximum(i - nc, 0), 0)
    lat_mod_row = lambda i: 1 + ((i - nc) * tm) // lat_seq
    return nc, nc + n_lat // tm, ctx_map, lat_map, lat_mod_row


def _inproj_kernel(xc_ref, xl_ref, mod_ref, n1_ref, win_f_ref, ws_ref, bs_ref, w_ref,
                   xy_ref, osgu_ref, w_b_ref, win_ref, *, tm, nc, lat_mod_row):
    i = pl.program_id(0)
    w_b_ref[...] = w_ref[...].astype(BF16)

    @pl.when(i == 0)
    def _():
        win_ref[...] = win_f_ref[...].astype(BF16)

    def body(x_ref, row):
        sh1 = _mod_row(mod_ref, 0, row)
        sc1 = _mod_row(mod_ref, 1, row)
        hb = (_rms(x_ref[...], n1_ref[...] * (1.0 + sc1)) + sh1).astype(BF16)
        uv = jnp.dot(hb, win_ref[:, 2 * LRU_WIDTH:], preferred_element_type=F32)
        xy_ref[...] = jnp.dot(hb, win_ref[:, 0:2 * LRU_WIDTH], preferred_element_type=F32)
        for g in range(SGU_GROUPS):
            lo = g * LANES
            gu = _gelu(uv[:, lo:lo + LANES])
            gv = _gelu(uv[:, SGU_WIDTH + lo:SGU_WIDTH + lo + LANES]).astype(BF16)
            bias = jnp.broadcast_to(bs_ref[:, g:g + 1], (CHUNK, LANES))
            for ck in range(tm // CHUNK):
                r0 = ck * CHUNK
                s = jnp.dot(ws_ref[g], gv[r0:r0 + CHUNK], preferred_element_type=F32) + bias
                osgu_ref[r0:r0 + CHUNK, lo:lo + LANES] = gu[r0:r0 + CHUNK] * s

    @pl.when(i < nc)
    def _():
        body(xc_ref, 0)

    @pl.when(i >= nc)
    def _():
        body(xl_ref, lat_mod_row(i))


def _cast_slice_spec(shape, grid):
    n_steps = math.prod(grid)
    rows = shape[0]
    r = BF16_SUBLANES
    while rows % r or rows // r > n_steps:
        r += BF16_SUBLANES
    last = rows // r - 1

    def index_map(*idx):
        step = 0
        for i, n in zip(idx, grid):
            step = step * n + i
        return (jnp.minimum(step, last), 0)

    return pl.BlockSpec((r, shape[1]), index_map)


def _inproj(xc, xl, mod, norm1, w_in, ws, bs_t, later_weight, *, tm, lat_seq):
    nc, n, ctx_map, lat_map, lat_mod_row = _token_groups(tm, xc.shape[0], xl.shape[0], lat_seq)
    m = n * tm
    const = lambda i: (0, 0)
    cast_specs = [_cast_slice_spec(later_weight.shape, (n,))]
    return pl.pallas_call(
        functools.partial(_inproj_kernel, tm=tm, nc=nc, lat_mod_row=lat_mod_row),
        grid=(n,),
        in_specs=[
            pl.BlockSpec((tm, D_MODEL), ctx_map),
            pl.BlockSpec((tm, D_MODEL), lat_map),
            pl.BlockSpec((N_MOD, MOD_ROWS, D_MODEL), lambda i: (0, 0, 0)),
            pl.BlockSpec((1, D_MODEL), const),
            pl.BlockSpec((D_MODEL, 4 * LRU_WIDTH), const, pipeline_mode=pl.Buffered(1)),
            pl.BlockSpec((SGU_GROUPS, CHUNK, CHUNK), lambda i: (0, 0, 0)),
            pl.BlockSpec((CHUNK, SGU_GROUPS), const),
        ] + cast_specs,
        out_specs=[
            pl.BlockSpec((tm, 2 * LRU_WIDTH), lambda i: (i, 0)),
            pl.BlockSpec((tm, SGU_WIDTH), lambda i: (i, 0)),
        ] + cast_specs,
        out_shape=[
            jax.ShapeDtypeStruct((m, 2 * LRU_WIDTH), F32),
            jax.ShapeDtypeStruct((m, SGU_WIDTH), F32),
        ] + [jax.ShapeDtypeStruct(later_weight.shape, BF16)],
        scratch_shapes=[pltpu.VMEM(w_in.shape, BF16)],
        compiler_params=pltpu.CompilerParams(
            dimension_semantics=("arbitrary",), vmem_limit_bytes=INPROJ_VMEM),
        name="inproj_sgu",
    )(xc, xl, mod, norm1, w_in, ws, bs_t, later_weight)


def _scan_pitch(seq):
    pitch = seq // SUBLANES
    return pitch + (4 - pitch % 8) % 8


def _lru_kernel(xr_ref, yr_ref, cw_ref, cb_ref, wg_ref, bg_ref, lam_ref, h0_ref, w_ref,
                o_ref, st_ref, w_b_ref, a_sc, b_sc, p_sc, h_sc, *, seq, nsb, nk, pitch, rc, unroll):
    sp_rows = SUBLANES * pitch
    n_rc = (seq * nsb) // rc
    seg = min(seq, rc)
    segs = rc // seg
    chunks_per_seq = seq // seg
    w_b_ref[...] = w_ref[...].astype(BF16)
    c4_all = (-0.5 * LRU_C) * jax.nn.softplus(-lam_ref[...])
    cw_all = cw_ref[...]
    cb_all = cb_ref[...]

    for s in range(nsb):
        lo, hi = s * sp_rows + seq, (s + 1) * sp_rows
        for slab in range(2 * nk):
            a_sc[slab, lo:hi, :] = jnp.ones((hi - lo, LANES), F32)
            b_sc[slab, lo:hi, :] = jnp.zeros((hi - lo, LANES), F32)

    def scratch_row(ci, k):
        if chunks_per_seq > 1:
            return pl.multiple_of((ci // chunks_per_seq) * sp_rows + (ci % chunks_per_seq) * seg,
                                  SUBLANES)
        return pl.multiple_of((ci * segs + k) * sp_rows, SUBLANES)

    def conv_segment(xs, prev, nxt, cw, cb):
        ext = jnp.concatenate([prev, xs, nxt], axis=0)
        n_ext = seg + 2 * SUBLANES
        xc = cb + cw[2:3] * xs
        xc = xc + cw[0:1] * pltpu.roll(ext, 2, axis=0)[SUBLANES:SUBLANES + seg]
        xc = xc + cw[1:2] * pltpu.roll(ext, 1, axis=0)[SUBLANES:SUBLANES + seg]
        return xc + cw[3:4] * pltpu.roll(ext, n_ext - 1, axis=0)[SUBLANES:SUBLANES + seg]

    def gates(ci, _):
        base = pl.multiple_of(ci * rc, rc)
        for k in range(nk):
            lanes = slice(k * LANES, (k + 1) * LANES)
            cw, cb, c4 = cw_all[:, lanes], cb_all[:, lanes], c4_all[:, lanes]
            xm = xr_ref[pl.ds(base, rc), lanes]
            if chunks_per_seq > 1:
                cs = ci % chunks_per_seq
                p0 = pl.multiple_of(jnp.maximum(base - SUBLANES, 0), SUBLANES)
                n0 = pl.multiple_of(jnp.minimum(base + rc, seq * nsb - SUBLANES), SUBLANES)
                prev = jnp.where(cs == 0, 0.0, xr_ref[pl.ds(p0, SUBLANES), lanes])
                nxt = jnp.where(cs == chunks_per_seq - 1, 0.0, xr_ref[pl.ds(n0, SUBLANES), lanes])
                xc = conv_segment(xm, prev, nxt, cw, cb)
            else:
                zeros = jnp.zeros((SUBLANES, LANES), F32)
                xc = jnp.concatenate(
                    [conv_segment(xm[j * seg:(j + 1) * seg], zeros, zeros, cw, cb) for j in range(segs)],
                    axis=0)
            g = jnp.dot(xc.astype(BF16), wg_ref[k], preferred_element_type=F32) + bg_ref[k]
            hxc = 0.5 * xc
            for d in range(2):
                th_r = jnp.tanh(g[:, (2 * d) * LANES:(2 * d + 1) * LANES])
                th_i = jnp.tanh(g[:, (2 * d + 1) * LANES:(2 * d + 2) * LANES])
                log_a = c4[d:d + 1] * th_r + c4[d:d + 1]
                a = jnp.exp(log_a)
                om = jnp.tanh(log_a) * (-1.0 - a * a)
                root = jnp.where(om > 0.0, om * lax.rsqrt(om), 0.0)
                b = root * ((th_i + 1.0) * hxc)
                for j in range(segs):
                    dst = scratch_row(ci, j)
                    a_sc[d * nk + k, pl.ds(dst, seg), :] = a[j * seg:(j + 1) * seg]
                    b_sc[d * nk + k, pl.ds(dst, seg), :] = b[j * seg:(j + 1) * seg]
        return 0

    lax.fori_loop(0, n_rc, gates, 0)

    def scan_sequence(s, _):
        off = s * sp_rows

        def rows(t):
            return pl.ds(off + t, SUBLANES, stride=pitch)

        def local(t, carry):
            out = []
            for slab, (h, p) in enumerate(carry):
                tt = t if slab < nk else pitch - 1 - t
                a = a_sc[slab, rows(tt), :]
                h = a * h + b_sc[slab, rows(tt), :]
                p = a * p
                h_sc[slab, rows(tt), :] = h
                p_sc[slab, rows(tt), :] = p
                out.append((h, p))
            return tuple(out)

        zero = jnp.zeros((SUBLANES, LANES), F32)
        one = jnp.ones((SUBLANES, LANES), F32)
        ends = lax.fori_loop(0, pitch, local, ((zero, one),) * (2 * nk), unroll=unroll)

        h0 = h0_ref[s]
        starts = []
        finals = [[None] * nk, [None] * nk]
        for slab, (h, p) in enumerate(ends):
            d, k = divmod(slab, nk)
            c = h0[d:d + 1, k * LANES:(k + 1) * LANES]
            cs = [None] * SUBLANES
            for j in (range(SUBLANES) if d == 0 else reversed(range(SUBLANES))):
                cs[j] = c
                c = h[j:j + 1] + p[j:j + 1] * c
            finals[d][k] = c
            starts.append(jnp.concatenate(cs, axis=0))
        st_ref[s] = jnp.concatenate(
            [jnp.concatenate(finals[0], axis=1), jnp.concatenate(finals[1], axis=1)], axis=0)

        def fix(t, _):
            for k in range(nk):
                a_sc[k, rows(t), :] = (h_sc[k, rows(t), :] + p_sc[k, rows(t), :] * starts[k]
                                       + h_sc[nk + k, rows(t), :] + p_sc[nk + k, rows(t), :] * starts[nk + k])
            return 0

        lax.fori_loop(0, pitch, fix, 0, unroll=unroll)
        return 0

    if nsb == 1:
        scan_sequence(0, 0)
    else:
        lax.fori_loop(0, nsb, scan_sequence, 0)

    def gate_out(ci, _):
        base = pl.multiple_of(ci * rc, rc)
        for k in range(nk):
            lanes = slice(k * LANES, (k + 1) * LANES)
            gy = _gelu(yr_ref[pl.ds(base, rc), lanes])
            for j in range(segs):
                src = scratch_row(ci, j)
                o_ref[pl.ds(base + j * seg, seg), lanes] = (gy[j * seg:(j + 1) * seg]
                                                            * a_sc[k, pl.ds(src, seg), :])
        return 0

    lax.fori_loop(0, n_rc, gate_out, 0)


def _lru(xy, conv_w, conv_b, wg, bg, lam, h0, later_weight, *, row_start, m, seq, nsb):
    nblk = m // (seq * nsb)
    assert row_start % (seq * nsb) == 0
    blk0 = row_start // (seq * nsb)
    nk = LRU_LANE_BLOCKS
    width = nk * LANES
    ncb = LRU_WIDTH // width
    pitch = _scan_pitch(seq)
    sc_rows = nsb * SUBLANES * pitch
    rows = seq * nsb
    cast_spec = _cast_slice_spec(later_weight.shape, (ncb, nblk))
    return pl.pallas_call(
        functools.partial(_lru_kernel, seq=seq, nsb=nsb, nk=nk, pitch=pitch,
                          rc=LRU_ROW_CHUNK, unroll=LRU_SCAN_UNROLL),
        grid=(ncb, nblk),
        in_specs=[
            pl.BlockSpec((rows, width), lambda j, b: (blk0 + b, j)),
            pl.BlockSpec((rows, width), lambda j, b: (blk0 + b, ncb + j)),
            pl.BlockSpec((4, width), lambda j, b: (0, j)),
            pl.BlockSpec((1, width), lambda j, b: (0, j)),
            pl.BlockSpec((nk, LANES, 4 * LANES), lambda j, b: (j, 0, 0)),
            pl.BlockSpec((nk, 1, 4 * LANES), lambda j, b: (j, 0, 0)),
            pl.BlockSpec((2, width), lambda j, b: (0, j)),
            pl.BlockSpec((nsb, 2, width), lambda j, b: (b, 0, j)),
            cast_spec,
        ],
        out_specs=[
            pl.BlockSpec((rows, width), lambda j, b: (b, j)),
            pl.BlockSpec((nsb, 2, width), lambda j, b: (b, 0, j)),
            cast_spec,
        ],
        out_shape=[
            jax.ShapeDtypeStruct((m, LRU_WIDTH), F32),
            jax.ShapeDtypeStruct((m // seq, 2, LRU_WIDTH), F32),
            jax.ShapeDtypeStruct(later_weight.shape, BF16),
        ],
        scratch_shapes=[pltpu.VMEM((2 * nk, sc_rows, LANES), F32)] * 4,
        compiler_params=pltpu.CompilerParams(
            dimension_semantics=("arbitrary", "arbitrary"), vmem_limit_bytes=LRU_VMEM),
        name="rglru",
    )(xy, xy, conv_w, conv_b, wg, bg, lam, h0, later_weight)


GROUP = SUBLANES * SUBLANES


def _swap_rows(slab_ref, val, r0):
    n = val.shape[0]
    for k in range(D_MODEL // LANES):
        slab_ref[k, r0:r0 + n, :] = val[:, k * LANES:(k + 1) * LANES]
    cols = []
    for k in range(D_MODEL // LANES):
        rows = [slab_ref[k, pl.ds(r0 + g * GROUP + t, SUBLANES, stride=SUBLANES), :]
                for g in range(n // GROUP) for t in range(SUBLANES)]
        cols.append(jnp.concatenate(rows, axis=0))
    return jnp.concatenate(cols, axis=1)


def _ffn_tile(x_ref, ol_ref, os_ref, mod_ref, gl_ref, gs_ref, wout_ref, n2_ref, up_ref,
              cw_ref, cb_ref, down_ref, fn_ref, y_ref, slab_sc, *, tm, fc, es, period, row):
    g1 = _mod_row(mod_ref, 2, row)
    sh2 = _mod_row(mod_ref, 3, row)
    sc2 = _mod_row(mod_ref, 4, row)
    g2 = _mod_row(mod_ref, 5, row)
    nl = _rms(ol_ref[...], gl_ref[...]).astype(BF16)
    ns = _rms(os_ref[...], gs_ref[...]).astype(BF16)
    o = (jnp.dot(nl, wout_ref[0:LRU_WIDTH, :], preferred_element_type=F32)
         + jnp.dot(ns, wout_ref[LRU_WIDTH:LRU_WIDTH + SGU_WIDTH, :], preferred_element_type=F32))
    x1 = x_ref[...] + g1 * o
    h2f = _rms(x1, n2_ref[...] * (1.0 + sc2)) + sh2
    h2 = _swap_rows(slab_sc, h2f, 0).astype(BF16)

    sub = lax.broadcasted_iota(jnp.int32, (SUBLANES, fc), 0)
    groups_per_period = period // GROUP
    n_groups = tm // GROUP

    def conv(z, c0):
        w = cw_ref[:, c0:c0 + fc]
        b = cb_ref[:, c0:c0 + fc]
        w0, w1, w2 = w[0:1], w[1:2], w[2:3]
        z8 = [[z[g * GROUP + t * SUBLANES:g * GROUP + (t + 1) * SUBLANES] for t in range(SUBLANES)]
              for g in range(n_groups)]
        outs = []
        for g in range(n_groups):
            cur = z8[g]
            lo = pltpu.roll(cur[SUBLANES - 1], 1, axis=0)
            if g % groups_per_period == 0:
                lo = jnp.where(sub == 0, 0.0, lo)
            else:
                lo = jnp.where(sub == 0, pltpu.roll(z8[g - 1][SUBLANES - 1], 1, axis=0), lo)
            hi = pltpu.roll(cur[0], SUBLANES - 1, axis=0)
            if g % groups_per_period == groups_per_period - 1:
                hi = jnp.where(sub == SUBLANES - 1, 0.0, hi)
            else:
                hi = jnp.where(sub == SUBLANES - 1, pltpu.roll(z8[g + 1][0], SUBLANES - 1, axis=0), hi)
            zm = [lo] + cur[:SUBLANES - 1]
            zp = cur[1:] + [hi]
            for t in range(SUBLANES):
                outs.append(b + w0 * zm[t] + w1 * cur[t] + w2 * zp[t])
        return jnp.concatenate(outs, axis=0)

    def up(f0):
        zg = jnp.dot(h2, up_ref[:, f0:f0 + fc], preferred_element_type=F32)
        zv = jnp.dot(h2, up_ref[:, D_FF + f0:D_FF + f0 + fc], preferred_element_type=F32)
        return zg, zv

    acc = jnp.zeros((tm, D_MODEL), F32)
    nxt = up(0)
    for f0 in range(0, D_FF, fc):
        zg, zv = nxt
        if f0 + fc < D_FF:
            nxt = up(f0 + fc)
        hg = 0.5 * conv(zg, f0)
        act = (hg * (jnp.tanh(hg) + 1.0) * conv(zv, D_FF + f0)).astype(BF16)
        if f0 + fc < D_FF:
            acc = acc + jnp.dot(act, down_ref[f0:f0 + fc, :], preferred_element_type=F32)
        else:
            for r0 in range(0, tm, es):
                a = acc[r0:r0 + es] + jnp.dot(act[r0:r0 + es], down_ref[f0:f0 + fc, :],
                                              preferred_element_type=F32)
                x2 = x1[r0:r0 + es] + g2 * _swap_rows(slab_sc, a, r0)
                y_ref[r0:r0 + es, :] = _rms(x2, fn_ref[...])


def _ffn_kernel(xc_ref, xl_ref, olc_ref, oll_ref, os_ref, mod_ref, gl_ref, gs_ref, wout_ref, n2_ref,
                up_ref, cw_ref, cb_ref, down_ref, fn_ref, yc_ref, yl_ref, slab_sc, *,
                nc, lat_mod_row, ctx_period, lat_period, **tile):
    i = pl.program_id(0)
    shared = (mod_ref, gl_ref, gs_ref, wout_ref, n2_ref, up_ref, cw_ref, cb_ref, down_ref, fn_ref)

    @pl.when(i < nc)
    def _():
        _ffn_tile(xc_ref, olc_ref, os_ref, *shared, yc_ref, slab_sc, period=ctx_period, row=0, **tile)

    @pl.when(i >= nc)
    def _():
        _ffn_tile(xl_ref, oll_ref, os_ref, *shared, yl_ref, slab_sc, period=lat_period,
                  row=lat_mod_row(i), **tile)


def _ffn(xc, xl, olru_c, olru_l, osgu, mod, g_lru, g_sgu, w_out, norm2, ffn_up, conv_w, conv_b,
         ffn_down, final_norm, *, tm, ctx_period, lat_period, lat_seq):
    nc, n, ctx_map, lat_map, lat_mod_row = _token_groups(tm, xc.shape[0], xl.shape[0], lat_seq)
    const = lambda i: (0, 0)
    resident = lambda shape: pl.BlockSpec(shape, const, pipeline_mode=pl.Buffered(1))
    return pl.pallas_call(
        functools.partial(_ffn_kernel, nc=nc, lat_mod_row=lat_mod_row, ctx_period=ctx_period,
                          lat_period=lat_period, tm=tm, fc=FFN_FC, es=FFN_ES),
        grid=(n,),
        in_specs=[
            pl.BlockSpec((tm, D_MODEL), ctx_map),
            pl.BlockSpec((tm, D_MODEL), lat_map),
            pl.BlockSpec((tm, LRU_WIDTH), ctx_map),
            pl.BlockSpec((tm, LRU_WIDTH), lat_map),
            pl.BlockSpec((tm, SGU_WIDTH), lambda i: (i, 0)),
            pl.BlockSpec((N_MOD, MOD_ROWS, D_MODEL), lambda i: (0, 0, 0)),
            pl.BlockSpec((1, LRU_WIDTH), const),
            pl.BlockSpec((1, SGU_WIDTH), const),
            resident((LRU_WIDTH + SGU_WIDTH, D_MODEL)),
            pl.BlockSpec((1, D_MODEL), const),
            resident((D_MODEL, 2 * D_FF)),
            pl.BlockSpec((3, 2 * D_FF), const),
            pl.BlockSpec((1, 2 * D_FF), const),
            resident((D_FF, D_MODEL)),
            pl.BlockSpec((1, D_MODEL), const),
        ],
        out_specs=[pl.BlockSpec((tm, D_MODEL), ctx_map), pl.BlockSpec((tm, D_MODEL), lat_map)],
        out_shape=[jax.ShapeDtypeStruct(xc.shape, F32), jax.ShapeDtypeStruct(xl.shape, F32)],
        scratch_shapes=[pltpu.VMEM((D_MODEL // LANES, tm, LANES), F32)],
        compiler_params=pltpu.CompilerParams(
            dimension_semantics=("arbitrary",), vmem_limit_bytes=FFN_VMEM),
        name="outproj_ffn",
    )(xc, xl, olru_c, olru_l, osgu, mod, g_lru, g_sgu, w_out, norm2, ffn_up, conv_w, conv_b, ffn_down,
      final_norm)


def _gate_params(wa, ba, wx, bx):
    nlb = LRU_HEADS // 2
    w = jnp.stack([wa[0], wx[0], wa[1], wx[1]], axis=0)
    wp = w.reshape(4, nlb, 2, LRU_HEAD_DIM, LRU_HEAD_DIM)
    pair = 0.5 * jnp.eye(2, dtype=F32)
    bd = wp[:, :, :, :, None, :] * pair[None, None, :, None, :, None]
    bd = bd.reshape(4, nlb, LANES, LANES)
    wg = jnp.transpose(bd, (1, 2, 0, 3)).reshape(nlb, LANES, 4 * LANES).astype(BF16)
    b = 0.5 * jnp.stack([ba[0], bx[0], ba[1], bx[1]], axis=0)
    bg = jnp.transpose(b.reshape(4, nlb, LANES), (1, 0, 2)).reshape(nlb, 1, 4 * LANES)
    return wg, bg


def kernel(x_prompt, x_sample, state_lru, c, c_ctx, norm1, norm2, w_ada, b_ada, w_in, lru_conv_w,
           lru_conv_b, lru_wa, lru_ba, lru_wx, lru_bx, lru_lam, sgu_ws, sgu_bs, g_lru, g_sgu, w_out,
           ffn_up, ffn_conv_w, ffn_conv_b, ffn_down, final_norm):
    batch, seq, _ = x_prompt.shape
    dec_batch, dec_seq, _ = x_sample.shape
    depth = norm1.shape[0]
    assert depth == 1, "the final norm is fused into the (single) layer's last kernel"
    l = 0

    xc = x_prompt.reshape(batch * seq, D_MODEL)
    xl = x_sample.reshape(dec_batch * dec_seq, D_MODEL)
    n_ctx, n_lat = xc.shape[0], xl.shape[0]
    cc = jnp.concatenate(
        [c_ctx[None, :], c, jnp.zeros((MOD_ROWS - 1 - dec_batch, D_MODEL), F32)], axis=0)
    zeros_state = jnp.zeros((batch, 2, LRU_WIDTH), x_prompt.dtype)

    mod = _modulation(cc, w_ada[l], b_ada[l][None, :])
    wg, bg = _gate_params(lru_wa[l], lru_ba[l], lru_wx[l], lru_bx[l])
    lru_params = (lru_conv_w[l], lru_conv_b[l][None, :], wg, bg, lru_lam[l])
    h0_lat = state_lru.reshape(dec_batch, 2, LRU_WIDTH)

    xy, osgu, w_out_b = _inproj(
        xc, xl, mod, norm1[l][None, :], w_in[l], sgu_ws[l].astype(BF16), sgu_bs[l].T,
        w_out[l], tm=INPROJ_TM, lat_seq=dec_seq)
    olru_c, new_state, down_b = _lru(xy, *lru_params, zeros_state, ffn_down[l], row_start=0, m=n_ctx,
                                     seq=seq, nsb=LRU_CTX_SEQS)
    olru_l, _, up_b = _lru(xy, *lru_params, h0_lat, ffn_up[l], row_start=n_ctx, m=n_lat,
                           seq=dec_seq, nsb=1)
    yc, yl = _ffn(xc, xl, olru_c, olru_l, osgu, mod, g_lru[l][None, :], g_sgu[l][None, :],
                  w_out_b, norm2[l][None, :], up_b, ffn_conv_w[l],
                  ffn_conv_b[l][None, :], down_b, final_norm[None, :], tm=FFN_TM,
                  ctx_period=seq, lat_period=GRID_W, lat_seq=dec_seq)
    return (yc.reshape(batch, seq, D_MODEL), yl.reshape(dec_batch, dec_seq, D_MODEL),
            new_state[:, None])
```

```python
import functools
import math

import jax
import jax.numpy as jnp
from jax import lax
from jax.experimental import pallas as pl
from jax.experimental.pallas import tpu as pltpu

D_MODEL = 1024
LRU_HEADS = 8
LRU_WIDTH = 512
LRU_HEAD_DIM = 64
LRU_C = 8.0
SGU_GROUPS = 4
SGU_WIDTH = 512
CHUNK = 128
D_FF = 3072
N_MOD = 6
EPS = 1e-6
GRID_W = 64

LANES = 128
SUBLANES = 8
BF16_SUBLANES = 16
MOD_ROWS = 8
BF16 = jnp.bfloat16
F32 = jnp.float32

MIB = 1024 * 1024
INPROJ_TM = 1024
INPROJ_VMEM = 56 * MIB
LRU_ROW_CHUNK = 512
LRU_LANE_BLOCKS = 2
LRU_CTX_SEQS = 8
LRU_SCAN_UNROLL = 8
LRU_VMEM = 56 * MIB
FFN_TM = 512
FFN_FC = 1024
FFN_ES = 256
FFN_VMEM = 60 * MIB

_GELU_C0 = math.sqrt(2.0 / math.pi)
_GELU_C1 = _GELU_C0 * 0.044715


def _rms(x, gain):
    return x * lax.rsqrt(jnp.mean(x * x, axis=-1, keepdims=True) + EPS) * gain


def _gelu(x):
    return (0.5 * x) * (1.0 + jnp.tanh(x * (_GELU_C0 + _GELU_C1 * (x * x))))


def _mod_row(mod_ref, k, row):
    return mod_ref[k, pl.ds(row, 1), :]


def _mod_kernel(c_ref, w_ref, b_ref, o_ref):
    s = jax.nn.silu(c_ref[...]).astype(BF16)
    o_ref[...] = jnp.dot(s, w_ref[...].astype(BF16), preferred_element_type=F32) + b_ref[...]


def _modulation(cc, w_ada, b_ada):
    return pl.pallas_call(
        _mod_kernel,
        grid=(N_MOD,),
        in_specs=[
            pl.BlockSpec((MOD_ROWS, D_MODEL), lambda j: (0, 0)),
            pl.BlockSpec((D_MODEL, D_MODEL), lambda j: (0, j)),
            pl.BlockSpec((1, D_MODEL), lambda j: (0, j)),
        ],
        out_specs=pl.BlockSpec((None, MOD_ROWS, D_MODEL), lambda j: (j, 0, 0)),
        out_shape=jax.ShapeDtypeStruct((N_MOD, MOD_ROWS, D_MODEL), F32),
        compiler_params=pltpu.CompilerParams(dimension_semantics=("arbitrary",)),
        name="modulation",
    )(cc, w_ada, b_ada)


def _token_groups(tm, n_ctx, n_lat, lat_seq):
    assert n_ctx % tm == 0 and n_lat % tm == 0 and lat_seq % tm == 0
    nc = n_ctx // tm
    ctx_map = lambda i: (jnp.minimum(i, nc - 1), 0)
    lat_map = lambda i: (jnp.maximum(i - nc, 0), 0)
    lat_mod_row = lambda i: 1 + ((i - nc) * tm) // lat_seq
    return nc, nc + n_lat // tm, ctx_map, lat_map, lat_mod_row


def _inproj_kernel(xc_ref, xl_ref, mod_ref, n1_ref, win_f_ref, ws_ref, bs_ref, w_ref,
                   xy_ref, osgu_ref, w_b_ref, win_ref, *, tm, nc, lat_mod_row):
    i = pl.program_id(0)
    w_b_ref[...] = w_ref[...].astype(BF16)

    @pl.when(i == 0)
    def _():
        win_ref[...] = win_f_ref[...].astype(BF16)

    def body(x_ref, row):
        sh1 = _mod_row(mod_ref, 0, row)
        sc1 = _mod_row(mod_ref, 1, row)
        hb = (_rms(x_ref[...], n1_ref[...] * (1.0 + sc1)) + sh1).astype(BF16)
        uv = jnp.dot(hb, win_ref[:, 2 * LRU_WIDTH:], preferred_element_type=F32)
        xy_ref[...] = jnp.dot(hb, win_ref[:, 0:2 * LRU_WIDTH], preferred_element_type=F32)
        for g in range(SGU_GROUPS):
            lo = g * LANES
            gu = _gelu(uv[:, lo:lo + LANES])
            gv = _gelu(uv[:, SGU_WIDTH + lo:SGU_WIDTH + lo + LANES]).astype(BF16)
            bias = jnp.broadcast_to(bs_ref[:, g:g + 1], (CHUNK, LANES))
            for ck in range(tm // CHUNK):
                r0 = ck * CHUNK
                s = jnp.dot(ws_ref[g], gv[r0:r0 + CHUNK], preferred_element_type=F32) + bias
                osgu_ref[r0:r0 + CHUNK, lo:lo + LANES] = gu[r0:r0 + CHUNK] * s

    @pl.when(i < nc)
    def _():
        body(xc_ref, 0)

    @pl.when(i >= nc)
    def _():
        body(xl_ref, lat_mod_row(i))


def _cast_slice_spec(shape, grid):
    n_steps = math.prod(grid)
    rows = shape[0]
    r = BF16_SUBLANES
    while rows % r or rows // r > n_steps:
        r += BF16_SUBLANES
    last = rows // r - 1

    def index_map(*idx):
        step = 0
        for i, n in zip(idx, grid):
            step = step * n + i
        return (jnp.minimum(step, last), 0)

    return pl.BlockSpec((r, shape[1]), index_map)


def _inproj(xc, xl, mod, norm1, w_in, ws, bs_t, later_weight, *, tm, lat_seq):
    nc, n, ctx_map, lat_map, lat_mod_row = _token_groups(tm, xc.shape[0], xl.shape[0], lat_seq)
    m = n * tm
    const = lambda i: (0, 0)
    cast_specs = [_cast_slice_spec(later_weight.shape, (n,))]
    return pl.pallas_call(
        functools.partial(_inproj_kernel, tm=tm, nc=nc, lat_mod_row=lat_mod_row),
        grid=(n,),
        in_specs=[
            pl.BlockSpec((tm, D_MODEL), ctx_map),
            pl.BlockSpec((tm, D_MODEL), lat_map),
            pl.BlockSpec((N_MOD, MOD_ROWS, D_MODEL), lambda i: (0, 0, 0)),
            pl.BlockSpec((1, D_MODEL), const),
            pl.BlockSpec((D_MODEL, 4 * LRU_WIDTH), const, pipeline_mode=pl.Buffered(1)),
            pl.BlockSpec((SGU_GROUPS, CHUNK, CHUNK), lambda i: (0, 0, 0)),
            pl.BlockSpec((CHUNK, SGU_GROUPS), const),
        ] + cast_specs,
        out_specs=[
            pl.BlockSpec((tm, 2 * LRU_WIDTH), lambda i: (i, 0)),
            pl.BlockSpec((tm, SGU_WIDTH), lambda i: (i, 0)),
        ] + cast_specs,
        out_shape=[
            jax.ShapeDtypeStruct((m, 2 * LRU_WIDTH), F32),
            jax.ShapeDtypeStruct((m, SGU_WIDTH), F32),
        ] + [jax.ShapeDtypeStruct(later_weight.shape, BF16)],
        scratch_shapes=[pltpu.VMEM(w_in.shape, BF16)],
        compiler_params=pltpu.CompilerParams(
            dimension_semantics=("arbitrary",), vmem_limit_bytes=INPROJ_VMEM),
        name="inproj_sgu",
    )(xc, xl, mod, norm1, w_in, ws, bs_t, later_weight)


def _scan_pitch(seq):
    pitch = seq // SUBLANES
    return pitch + (4 - pitch % 8) % 8


def _lru_kernel(xr_ref, yr_ref, cw_ref, cb_ref, wg_ref, bg_ref, lam_ref, h0_ref, w_ref,
                o_ref, st_ref, w_b_ref, a_sc, b_sc, p_sc, h_sc, *, seq, nsb, nk, pitch, rc, unroll):
    sp_rows = SUBLANES * pitch
    n_rc = (seq * nsb) // rc
    seg = min(seq, rc)
    segs = rc // seg
    chunks_per_seq = seq // seg
    w_b_ref[...] = w_ref[...].astype(BF16)
    c4_all = (-0.5 * LRU_C) * jax.nn.softplus(-lam_ref[...])
    cw_all = cw_ref[...]
    cb_all = cb_ref[...]

    for s in range(nsb):
        lo, hi = s * sp_rows + seq, (s + 1) * sp_rows
        for slab in range(2 * nk):
            a_sc[slab, lo:hi, :] = jnp.ones((hi - lo, LANES), F32)
            b_sc[slab, lo:hi, :] = jnp.zeros((hi - lo, LANES), F32)

    def scratch_row(ci, k):
        if chunks_per_seq > 1:
            return pl.multiple_of((ci // chunks_per_seq) * sp_rows + (ci % chunks_per_seq) * seg,
                                  SUBLANES)
        return pl.multiple_of((ci * segs + k) * sp_rows, SUBLANES)

    def conv_segment(xs, prev, nxt, cw, cb):
        ext = jnp.concatenate([prev, xs, nxt], axis=0)
        n_ext = seg + 2 * SUBLANES
        xc = cb + cw[2:3] * xs
        xc = xc + cw[0:1] * pltpu.roll(ext, 2, axis=0)[SUBLANES:SUBLANES + seg]
        xc = xc + cw[1:2] * pltpu.roll(ext, 1, axis=0)[SUBLANES:SUBLANES + seg]
        return xc + cw[3:4] * pltpu.roll(ext, n_ext - 1, axis=0)[SUBLANES:SUBLANES + seg]

    def gates(ci, _):
        base = pl.multiple_of(ci * rc, rc)
        for k in range(nk):
            lanes = slice(k * LANES, (k + 1) * LANES)
            cw, cb, c4 = cw_all[:, lanes], cb_all[:, lanes], c4_all[:, lanes]
            xm = xr_ref[pl.ds(base, rc), lanes]
            if chunks_per_seq > 1:
                cs = ci % chunks_per_seq
                p0 = pl.multiple_of(jnp.maximum(base - SUBLANES, 0), SUBLANES)
                n0 = pl.multiple_of(jnp.minimum(base + rc, seq * nsb - SUBLANES), SUBLANES)
                prev = jnp.where(cs == 0, 0.0, xr_ref[pl.ds(p0, SUBLANES), lanes])
                nxt = jnp.where(cs == chunks_per_seq - 1, 0.0, xr_ref[pl.ds(n0, SUBLANES), lanes])
                xc = conv_segment(xm, prev, nxt, cw, cb)
            else:
                zeros = jnp.zeros((SUBLANES, LANES), F32)
                xc = jnp.concatenate(
                    [conv_segment(xm[j * seg:(j + 1) * seg], zeros, zeros, cw, cb) for j in range(segs)],
                    axis=0)
            g = jnp.dot(xc.astype(BF16), wg_ref[k], preferred_element_type=F32) + bg_ref[k]
            hxc = 0.5 * xc
            for d in range(2):
                th_r = jnp.tanh(g[:, (2 * d) * LANES:(2 * d + 1) * LANES])
                th_i = jnp.tanh(g[:, (2 * d + 1) * LANES:(2 * d + 2) * LANES])
                log_a = c4[d:d + 1] * th_r + c4[d:d + 1]
                a = jnp.exp(log_a)
                om = jnp.tanh(log_a) * (-1.0 - a * a)
                root = jnp.where(om > 0.0, om * lax.rsqrt(om), 0.0)
                b = root * ((th_i + 1.0) * hxc)
                for j in range(segs):
                    dst = scratch_row(ci, j)
                    a_sc[d * nk + k, pl.ds(dst, seg), :] = a[j * seg:(j + 1) * seg]
                    b_sc[d * nk + k, pl.ds(dst, seg), :] = b[j * seg:(j + 1) * seg]
        return 0

    lax.fori_loop(0, n_rc, gates, 0)

    def scan_sequence(s, _):
        off = s * sp_rows

        def rows(t):
            return pl.ds(off + t, SUBLANES, stride=pitch)

        def local(t, carry):
            out = []
            for slab, (h, p) in enumerate(carry):
                tt = t if slab < nk else pitch - 1 - t
                a = a_sc[slab, rows(tt), :]
                h = a * h + b_sc[slab, rows(tt), :]
                p = a * p
                h_sc[slab, rows(tt), :] = h
                p_sc[slab, rows(tt), :] = p
                out.append((h, p))
            return tuple(out)

        zero = jnp.zeros((SUBLANES, LANES), F32)
        one = jnp.ones((SUBLANES, LANES), F32)
        ends = lax.fori_loop(0, pitch, local, ((zero, one),) * (2 * nk), unroll=unroll)

        h0 = h0_ref[s]
        starts = []
        finals = [[None] * nk, [None] * nk]
        for slab, (h, p) in enumerate(ends):
            d, k = divmod(slab, nk)
            c = h0[d:d + 1, k * LANES:(k + 1) * LANES]
            cs = [None] * SUBLANES
            for j in (range(SUBLANES) if d == 0 else reversed(range(SUBLANES))):
                cs[j] = c
                c = h[j:j + 1] + p[j:j + 1] * c
            finals[d][k] = c
            starts.append(jnp.concatenate(cs, axis=0))
        st_ref[s] = jnp.concatenate(
            [jnp.concatenate(finals[0], axis=1), jnp.concatenate(finals[1], axis=1)], axis=0)

        def fix(t, _):
            for k in range(nk):
                a_sc[k, rows(t), :] = (h_sc[k, rows(t), :] + p_sc[k, rows(t), :] * starts[k]
                                       + h_sc[nk + k, rows(t), :] + p_sc[nk + k, rows(t), :] * starts[nk + k])
            return 0

        lax.fori_loop(0, pitch, fix, 0, unroll=unroll)
        return 0

    if nsb == 1:
        scan_sequence(0, 0)
    else:
        lax.fori_loop(0, nsb, scan_sequence, 0)

    def gate_out(ci, _):
        base = pl.multiple_of(ci * rc, rc)
        for k in range(nk):
            lanes = slice(k * LANES, (k + 1) * LANES)
            gy = _gelu(yr_ref[pl.ds(base, rc), lanes])
            for j in range(segs):
                src = scratch_row(ci, j)
                o_ref[pl.ds(base + j * seg, seg), lanes] = (gy[j * seg:(j + 1) * seg]
                                                            * a_sc[k, pl.ds(src, seg), :])
        return 0

    lax.fori_loop(0, n_rc, gate_out, 0)


def _lru(xy, conv_w, conv_b, wg, bg, lam, h0, later_weight, *, row_start, m, seq, nsb):
    nblk = m // (seq * nsb)
    assert row_start % (seq * nsb) == 0
    blk0 = row_start // (seq * nsb)
    nk = LRU_LANE_BLOCKS
    width = nk * LANES
    ncb = LRU_WIDTH // width
    pitch = _scan_pitch(seq)
    sc_rows = nsb * SUBLANES * pitch
    rows = seq * nsb
    cast_spec = _cast_slice_spec(later_weight.shape, (ncb, nblk))
    return pl.pallas_call(
        functools.partial(_lru_kernel, seq=seq, nsb=nsb, nk=nk, pitch=pitch,
                          rc=LRU_ROW_CHUNK, unroll=LRU_SCAN_UNROLL),
        grid=(ncb, nblk),
        in_specs=[
            pl.BlockSpec((rows, width), lambda j, b: (blk0 + b, j)),
            pl.BlockSpec((rows, width), lambda j, b: (blk0 + b, ncb + j)),
            pl.BlockSpec((4, width), lambda j, b: (0, j)),
            pl.BlockSpec((1, width), lambda j, b: (0, j)),
            pl.BlockSpec((nk, LANES, 4 * LANES), lambda j, b: (j, 0, 0)),
            pl.BlockSpec((nk, 1, 4 * LANES), lambda j, b: (j, 0, 0)),
            pl.BlockSpec((2, width), lambda j, b: (0, j)),
            pl.BlockSpec((nsb, 2, width), lambda j, b: (b, 0, j)),
            cast_spec,
        ],
        out_specs=[
            pl.BlockSpec((rows, width), lambda j, b: (b, j)),
            pl.BlockSpec((nsb, 2, width), lambda j, b: (b, 0, j)),
            cast_spec,
        ],
        out_shape=[
            jax.ShapeDtypeStruct((m, LRU_WIDTH), F32),
            jax.ShapeDtypeStruct((m // seq, 2, LRU_WIDTH), F32),
            jax.ShapeDtypeStruct(later_weight.shape, BF16),
        ],
        scratch_shapes=[pltpu.VMEM((2 * nk, sc_rows, LANES), F32)] * 4,
        compiler_params=pltpu.CompilerParams(
            dimension_semantics=("arbitrary", "arbitrary"), vmem_limit_bytes=LRU_VMEM),
        name="rglru",
    )(xy, xy, conv_w, conv_b, wg, bg, lam, h0, later_weight)


GROUP = SUBLANES * SUBLANES


def _swap_rows(slab_ref, val, r0):
    n = val.shape[0]
    for k in range(D_MODEL // LANES):
        slab_ref[k, r0:r0 + n, :] = val[:, k * LANES:(k + 1) * LANES]
    cols = []
    for k in range(D_MODEL // LANES):
        rows = [slab_ref[k, pl.ds(r0 + g * GROUP + t, SUBLANES, stride=SUBLANES), :]
                for g in range(n // GROUP) for t in range(SUBLANES)]
        cols.append(jnp.concatenate(rows, axis=0))
    return jnp.concatenate(cols, axis=1)


def _ffn_tile(x_ref, ol_ref, os_ref, mod_ref, gl_ref, gs_ref, wout_ref, n2_ref, up_ref,
              cw_ref, cb_ref, down_ref, fn_ref, y_ref, slab_sc, *, tm, fc, es, period, row):
    g1 = _mod_row(mod_ref, 2, row)
    sh2 = _mod_row(mod_ref, 3, row)
    sc2 = _mod_row(mod_ref, 4, row)
    g2 = _mod_row(mod_ref, 5, row)
    nl = _rms(ol_ref[...], gl_ref[...]).astype(BF16)
    ns = _rms(os_ref[...], gs_ref[...]).astype(BF16)
    o = (jnp.dot(nl, wout_ref[0:LRU_WIDTH, :], preferred_element_type=F32)
         + jnp.dot(ns, wout_ref[LRU_WIDTH:LRU_WIDTH + SGU_WIDTH, :], preferred_element_type=F32))
    x1 = x_ref[...] + g1 * o
    h2f = _rms(x1, n2_ref[...] * (1.0 + sc2)) + sh2
    h2 = _swap_rows(slab_sc, h2f, 0).astype(BF16)

    sub = lax.broadcasted_iota(jnp.int32, (SUBLANES, fc), 0)
    groups_per_period = period // GROUP
    n_groups = tm // GROUP

    def conv(z, c0):
        w = cw_ref[:, c0:c0 + fc]
        b = cb_ref[:, c0:c0 + fc]
        w0, w1, w2 = w[0:1], w[1:2], w[2:3]
        z8 = [[z[g * GROUP + t * SUBLANES:g * GROUP + (t + 1) * SUBLANES] for t in range(SUBLANES)]
              for g in range(n_groups)]
        outs = []
        for g in range(n_groups):
            cur = z8[g]
            lo = pltpu.roll(cur[SUBLANES - 1], 1, axis=0)
            if g % groups_per_period == 0:
                lo = jnp.where(sub == 0, 0.0, lo)
            else:
                lo = jnp.where(sub == 0, pltpu.roll(z8[g - 1][SUBLANES - 1], 1, axis=0), lo)
            hi = pltpu.roll(cur[0], SUBLANES - 1, axis=0)
            if g % groups_per_period == groups_per_period - 1:
                hi = jnp.where(sub == SUBLANES - 1, 0.0, hi)
            else:
                hi = jnp.where(sub == SUBLANES - 1, pltpu.roll(z8[g + 1][0], SUBLANES - 1, axis=0), hi)
            zm = [lo] + cur[:SUBLANES - 1]
            zp = cur[1:] + [hi]
            for t in range(SUBLANES):
                outs.append(b + w0 * zm[t] + w1 * cur[t] + w2 * zp[t])
        return jnp.concatenate(outs, axis=0)

    def up(f0):
        zg = jnp.dot(h2, up_ref[:, f0:f0 + fc], preferred_element_type=F32)
        zv = jnp.dot(h2, up_ref[:, D_FF + f0:D_FF + f0 + fc], preferred_element_type=F32)
        return zg, zv

    acc = jnp.zeros((tm, D_MODEL), F32)
    nxt = up(0)
    for f0 in range(0, D_FF, fc):
        zg, zv = nxt
        if f0 + fc < D_FF:
            nxt = up(f0 + fc)
        hg = 0.5 * conv(zg, f0)
        act = (hg * (jnp.tanh(hg) + 1.0) * conv(zv, D_FF + f0)).astype(BF16)
        if f0 + fc < D_FF:
            acc = acc + jnp.dot(act, down_ref[f0:f0 + fc, :], preferred_element_type=F32)
        else:
            for r0 in range(0, tm, es):
                a = acc[r0:r0 + es] + jnp.dot(act[r0:r0 + es], down_ref[f0:f0 + fc, :],
                                              preferred_element_type=F32)
                x2 = x1[r0:r0 + es] + g2 * _swap_rows(slab_sc, a, r0)
                y_ref[r0:r0 + es, :] = _rms(x2, fn_ref[...])


def _ffn_kernel(xc_ref, xl_ref, olc_ref, oll_ref, os_ref, mod_ref, gl_ref, gs_ref, wout_ref, n2_ref,
                up_ref, cw_ref, cb_ref, down_ref, fn_ref, yc_ref, yl_ref, slab_sc, *,
                nc, lat_mod_row, ctx_period, lat_period, **tile):
    i = pl.program_id(0)
    shared = (mod_ref, gl_ref, gs_ref, wout_ref, n2_ref, up_ref, cw_ref, cb_ref, down_ref, fn_ref)

    @pl.when(i < nc)
    def _():
        _ffn_tile(xc_ref, olc_ref, os_ref, *shared, yc_ref, slab_sc, period=ctx_period, row=0, **tile)

    @pl.when(i >= nc)
    def _():
        _ffn_tile(xl_ref, oll_ref, os_ref, *shared, yl_ref, slab_sc, period=lat_period,
                  row=lat_mod_row(i), **tile)


def _ffn(xc, xl, olru_c, olru_l, osgu, mod, g_lru, g_sgu, w_out, norm2, ffn_up, conv_w, conv_b,
         ffn_down, final_norm, *, tm, ctx_period, lat_period, lat_seq):
    nc, n, ctx_map, lat_map, lat_mod_row = _token_groups(tm, xc.shape[0], xl.shape[0], lat_seq)
    const = lambda i: (0, 0)
    resident = lambda shape: pl.BlockSpec(shape, const, pipeline_mode=pl.Buffered(1))
    return pl.pallas_call(
        functools.partial(_ffn_kernel, nc=nc, lat_mod_row=lat_mod_row, ctx_period=ctx_period,
                          lat_period=lat_period, tm=tm, fc=FFN_FC, es=FFN_ES),
        grid=(n,),
        in_specs=[
            pl.BlockSpec((tm, D_MODEL), ctx_map),
            pl.BlockSpec((tm, D_MODEL), lat_map),
            pl.BlockSpec((tm, LRU_WIDTH), ctx_map),
            pl.BlockSpec((tm, LRU_WIDTH), lat_map),
            pl.BlockSpec((tm, SGU_WIDTH), lambda i: (i, 0)),
            pl.BlockSpec((N_MOD, MOD_ROWS, D_MODEL), lambda i: (0, 0, 0)),
            pl.BlockSpec((1, LRU_WIDTH), const),
            pl.BlockSpec((1, SGU_WIDTH), const),
            resident((LRU_WIDTH + SGU_WIDTH, D_MODEL)),
            pl.BlockSpec((1, D_MODEL), const),
            resident((D_MODEL, 2 * D_FF)),
            pl.BlockSpec((3, 2 * D_FF), const),
            pl.BlockSpec((1, 2 * D_FF), const),
            resident((D_FF, D_MODEL)),
            pl.BlockSpec((1, D_MODEL), const),
        ],
        out_specs=[pl.BlockSpec((tm, D_MODEL), ctx_map), pl.BlockSpec((tm, D_MODEL), lat_map)],
        out_shape=[jax.ShapeDtypeStruct(xc.shape, F32), jax.ShapeDtypeStruct(xl.shape, F32)],
        scratch_shapes=[pltpu.VMEM((D_MODEL // LANES, tm, LANES), F32)],
        compiler_params=pltpu.CompilerParams(
            dimension_semantics=("arbitrary",), vmem_limit_bytes=FFN_VMEM),
        name="outproj_ffn",
    )(xc, xl, olru_c, olru_l, osgu, mod, g_lru, g_sgu, w_out, norm2, ffn_up, conv_w, conv_b, ffn_down,
      final_norm)


def _gate_params(wa, ba, wx, bx):
    nlb = LRU_HEADS // 2
    w = jnp.stack([wa[0], wx[0], wa[1], wx[1]], axis=0)
    wp = w.reshape(4, nlb, 2, LRU_HEAD_DIM, LRU_HEAD_DIM)
    pair = 0.5 * jnp.eye(2, dtype=F32)
    bd = wp[:, :, :, :, None, :] * pair[None, None, :, None, :, None]
    bd = bd.reshape(4, nlb, LANES, LANES)
    wg = jnp.transpose(bd, (1, 2, 0, 3)).reshape(nlb, LANES, 4 * LANES).astype(BF16)
    b = 0.5 * jnp.stack([ba[0], bx[0], ba[1], bx[1]], axis=0)
    bg = jnp.transpose(b.reshape(4, nlb, LANES), (1, 0, 2)).reshape(nlb, 1, 4 * LANES)
    return wg, bg


def kernel(x_prompt, x_sample, state_lru, c, c_ctx, norm1, norm2, w_ada, b_ada, w_in, lru_conv_w,
           lru_conv_b, lru_wa, lru_ba, lru_wx, lru_bx, lru_lam, sgu_ws, sgu_bs, g_lru, g_sgu, w_out,
           ffn_up, ffn_conv_w, ffn_conv_b, ffn_down, final_norm):
    batch, seq, _ = x_prompt.shape
    dec_batch, dec_seq, _ = x_sample.shape
    depth = norm1.shape[0]
    assert depth == 1, "the final norm is fused into the (single) layer's last kernel"
    l = 0

    xc = x_prompt.reshape(batch * seq, D_MODEL)
    xl = x_sample.reshape(dec_batch * dec_seq, D_MODEL)
    n_ctx, n_lat = xc.shape[0], xl.shape[0]
    cc = jnp.concatenate(
        [c_ctx[None, :], c, jnp.zeros((MOD_ROWS - 1 - dec_batch, D_MODEL), F32)], axis=0)
    zeros_state = jnp.zeros((batch, 2, LRU_WIDTH), x_prompt.dtype)

    mod = _modulation(cc, w_ada[l], b_ada[l][None, :])
    wg, bg = _gate_params(lru_wa[l], lru_ba[l], lru_wx[l], lru_bx[l])
    lru_params = (lru_conv_w[l], lru_conv_b[l][None, :], wg, bg, lru_lam[l])
    h0_lat = state_lru.reshape(dec_batch, 2, LRU_WIDTH)

    xy, osgu, w_out_b = _inproj(
        xc, xl, mod, norm1[l][None, :], w_in[l], sgu_ws[l].astype(BF16), sgu_bs[l].T,
        w_out[l], tm=INPROJ_TM, lat_seq=dec_seq)
    olru_c, new_state, down_b = _lru(xy, *lru_params, zeros_state, ffn_down[l], row_start=0, m=n_ctx,
                                     seq=seq, nsb=LRU_CTX_SEQS)
    olru_l, _, up_b = _lru(xy, *lru_params, h0_lat, ffn_up[l], row_start=n_ctx, m=n_lat,
                           seq=dec_seq, nsb=1)
    yc, yl = _ffn(xc, xl, olru_c, olru_l, osgu, mod, g_lru[l][None, :], g_sgu[l][None, :],
                  w_out_b, norm2[l][None, :], up_b, ffn_conv_w[l],
                  ffn_conv_b[l][None, :], down_b, final_norm[None, :], tm=FFN_TM,
                  ctx_period=seq, lat_period=GRID_W, lat_seq=dec_seq)
    return (yc.reshape(batch, seq, D_MODEL), yl.reshape(dec_batch, dec_seq, D_MODEL),
            new_state[:, None])
```

```python
import functools
import math

import jax
import jax.numpy as jnp
from jax import lax
from jax.experimental import pallas as pl
from jax.experimental.pallas import tpu as pltpu

D_MODEL = 1024
LRU_HEADS = 8
LRU_WIDTH = 512
LRU_HEAD_DIM = 64
LRU_C = 8.0
SGU_GROUPS = 4
SGU_WIDTH = 512
CHUNK = 128
D_FF = 3072
N_MOD = 6
EPS = 1e-6
GRID_W = 64

LANES = 128
SUBLANES = 8
BF16_SUBLANES = 16
MOD_ROWS = 8
BF16 = jnp.bfloat16
F32 = jnp.float32

MIB = 1024 * 1024
INPROJ_TM = 1024
INPROJ_VMEM = 56 * MIB
LRU_ROW_CHUNK = 512
LRU_LANE_BLOCKS = 2
LRU_CTX_SEQS = 8
LRU_SCAN_UNROLL = 8
LRU_VMEM = 56 * MIB
FFN_TM = 512
FFN_FC = 1536
FFN_ES = 256
FFN_VMEM = 60 * MIB

_GELU_C0 = math.sqrt(2.0 / math.pi)
_GELU_C1 = _GELU_C0 * 0.044715


def _rms(x, gain):
    return x * lax.rsqrt(jnp.mean(x * x, axis=-1, keepdims=True) + EPS) * gain


def _gelu(x):
    return (0.5 * x) * (1.0 + jnp.tanh(x * (_GELU_C0 + _GELU_C1 * (x * x))))


def _mod_row(mod_ref, k, row):
    return mod_ref[k, pl.ds(row, 1), :]


def _mod_kernel(c_ref, w_ref, b_ref, o_ref):
    s = jax.nn.silu(c_ref[...]).astype(BF16)
    o_ref[...] = jnp.dot(s, w_ref[...].astype(BF16), preferred_element_type=F32) + b_ref[...]


def _modulation(cc, w_ada, b_ada):
    return pl.pallas_call(
        _mod_kernel,
        grid=(N_MOD,),
        in_specs=[
            pl.BlockSpec((MOD_ROWS, D_MODEL), lambda j: (0, 0)),
            pl.BlockSpec((D_MODEL, D_MODEL), lambda j: (0, j)),
            pl.BlockSpec((1, D_MODEL), lambda j: (0, j)),
        ],
        out_specs=pl.BlockSpec((None, MOD_ROWS, D_MODEL), lambda j: (j, 0, 0)),
        out_shape=jax.ShapeDtypeStruct((N_MOD, MOD_ROWS, D_MODEL), F32),
        compiler_params=pltpu.CompilerParams(dimension_semantics=("arbitrary",)),
        name="modulation",
    )(cc, w_ada, b_ada)


def _token_groups(tm, n_ctx, n_lat, lat_seq):
    assert n_ctx % tm == 0 and n_lat % tm == 0 and lat_seq % tm == 0
    nc = n_ctx // tm
    ctx_map = lambda i: (jnp.minimum(i, nc - 1), 0)
    lat_map = lambda i: (jnp.maximum(i - nc, 0), 0)
    lat_mod_row = lambda i: 1 + ((i - nc) * tm) // lat_seq
    return nc, nc + n_lat // tm, ctx_map, lat_map, lat_mod_row


def _inproj_kernel(xc_ref, xl_ref, mod_ref, n1_ref, win_f_ref, ws_ref, bs_ref, w_ref,
                   xy_ref, osgu_ref, w_b_ref, win_ref, *, tm, nc, lat_mod_row):
    i = pl.program_id(0)
    w_b_ref[...] = w_ref[...].astype(BF16)

    @pl.when(i == 0)
    def _():
        win_ref[...] = win_f_ref[...].astype(BF16)

    def body(x_ref, row):
        sh1 = _mod_row(mod_ref, 0, row)
        sc1 = _mod_row(mod_ref, 1, row)
        hb = (_rms(x_ref[...], n1_ref[...] * (1.0 + sc1)) + sh1).astype(BF16)
        uv = jnp.dot(hb, win_ref[:, 2 * LRU_WIDTH:], preferred_element_type=F32)
        xy_ref[...] = jnp.dot(hb, win_ref[:, 0:2 * LRU_WIDTH], preferred_element_type=F32)
        for g in range(SGU_GROUPS):
            lo = g * LANES
            gu = _gelu(uv[:, lo:lo + LANES])
            gv = _gelu(uv[:, SGU_WIDTH + lo:SGU_WIDTH + lo + LANES]).astype(BF16)
            bias = jnp.broadcast_to(bs_ref[:, g:g + 1], (CHUNK, LANES))
            for ck in range(tm // CHUNK):
                r0 = ck * CHUNK
                s = jnp.dot(ws_ref[g], gv[r0:r0 + CHUNK], preferred_element_type=F32) + bias
                osgu_ref[r0:r0 + CHUNK, lo:lo + LANES] = gu[r0:r0 + CHUNK] * s

    @pl.when(i < nc)
    def _():
        body(xc_ref, 0)

    @pl.when(i >= nc)
    def _():
        body(xl_ref, lat_mod_row(i))


def _cast_slice_spec(shape, grid):
    n_steps = math.prod(grid)
    rows = shape[0]
    r = BF16_SUBLANES
    while rows % r or rows // r > n_steps:
        r += BF16_SUBLANES
    last = rows // r - 1

    def index_map(*idx):
        step = 0
        for i, n in zip(idx, grid):
            step = step * n + i
        return (jnp.minimum(step, last), 0)

    return pl.BlockSpec((r, shape[1]), index_map)


def _inproj(xc, xl, mod, norm1, w_in, ws, bs_t, later_weight, *, tm, lat_seq):
    nc, n, ctx_map, lat_map, lat_mod_row = _token_groups(tm, xc.shape[0], xl.shape[0], lat_seq)
    m = n * tm
    const = lambda i: (0, 0)
    cast_specs = [_cast_slice_spec(later_weight.shape, (n,))]
    return pl.pallas_call(
        functools.partial(_inproj_kernel, tm=tm, nc=nc, lat_mod_row=lat_mod_row),
        grid=(n,),
        in_specs=[
            pl.BlockSpec((tm, D_MODEL), ctx_map),
            pl.BlockSpec((tm, D_MODEL), lat_map),
            pl.BlockSpec((N_MOD, MOD_ROWS, D_MODEL), lambda i: (0, 0, 0)),
            pl.BlockSpec((1, D_MODEL), const),
            pl.BlockSpec((D_MODEL, 4 * LRU_WIDTH), const, pipeline_mode=pl.Buffered(1)),
            pl.BlockSpec((SGU_GROUPS, CHUNK, CHUNK), lambda i: (0, 0, 0)),
            pl.BlockSpec((CHUNK, SGU_GROUPS), const),
        ] + cast_specs,
        out_specs=[
            pl.BlockSpec((tm, 2 * LRU_WIDTH), lambda i: (i, 0)),
            pl.BlockSpec((tm, SGU_WIDTH), lambda i: (i, 0)),
        ] + cast_specs,
        out_shape=[
            jax.ShapeDtypeStruct((m, 2 * LRU_WIDTH), F32),
            jax.ShapeDtypeStruct((m, SGU_WIDTH), F32),
        ] + [jax.ShapeDtypeStruct(later_weight.shape, BF16)],
        scratch_shapes=[pltpu.VMEM(w_in.shape, BF16)],
        compiler_params=pltpu.CompilerParams(
            dimension_semantics=("arbitrary",), vmem_limit_bytes=INPROJ_VMEM),
        name="inproj_sgu",
    )(xc, xl, mod, norm1, w_in, ws, bs_t, later_weight)


def _scan_pitch(seq):
    pitch = seq // SUBLANES
    return pitch + (4 - pitch % 8) % 8


def _lru_kernel(xr_ref, yr_ref, cw_ref, cb_ref, wg_ref, bg_ref, lam_ref, h0_ref, w_ref,
                o_ref, st_ref, w_b_ref, a_sc, b_sc, p_sc, h_sc, *, seq, nsb, nk, pitch, rc, unroll):
    sp_rows = SUBLANES * pitch
    n_rc = (seq * nsb) // rc
    seg = min(seq, rc)
    segs = rc // seg
    chunks_per_seq = seq // seg
    w_b_ref[...] = w_ref[...].astype(BF16)
    c4_all = (-0.5 * LRU_C) * jax.nn.softplus(-lam_ref[...])
    cw_all = cw_ref[...]
    cb_all = cb_ref[...]

    for s in range(nsb):
        lo, hi = s * sp_rows + seq, (s + 1) * sp_rows
        for slab in range(2 * nk):
            a_sc[slab, lo:hi, :] = jnp.ones((hi - lo, LANES), F32)
            b_sc[slab, lo:hi, :] = jnp.zeros((hi - lo, LANES), F32)

    def scratch_row(ci, k):
        if chunks_per_seq > 1:
            return pl.multiple_of((ci // chunks_per_seq) * sp_rows + (ci % chunks_per_seq) * seg,
                                  SUBLANES)
        return pl.multiple_of((ci * segs + k) * sp_rows, SUBLANES)

    def conv_segment(xs, prev, nxt, cw, cb):
        ext = jnp.concatenate([prev, xs, nxt], axis=0)
        n_ext = seg + 2 * SUBLANES
        xc = cb + cw[2:3] * xs
        xc = xc + cw[0:1] * pltpu.roll(ext, 2, axis=0)[SUBLANES:SUBLANES + seg]
        xc = xc + cw[1:2] * pltpu.roll(ext, 1, axis=0)[SUBLANES:SUBLANES + seg]
        return xc + cw[3:4] * pltpu.roll(ext, n_ext - 1, axis=0)[SUBLANES:SUBLANES + seg]

    def gates(ci, _):
        base = pl.multiple_of(ci * rc, rc)
        for k in range(nk):
            lanes = slice(k * LANES, (k + 1) * LANES)
            cw, cb, c4 = cw_all[:, lanes], cb_all[:, lanes], c4_all[:, lanes]
            xm = xr_ref[pl.ds(base, rc), lanes]
            if chunks_per_seq > 1:
                cs = ci % chunks_per_seq
                p0 = pl.multiple_of(jnp.maximum(base - SUBLANES, 0), SUBLANES)
                n0 = pl.multiple_of(jnp.minimum(base + rc, seq * nsb - SUBLANES), SUBLANES)
                prev = jnp.where(cs == 0, 0.0, xr_ref[pl.ds(p0, SUBLANES), lanes])
                nxt = jnp.where(cs == chunks_per_seq - 1, 0.0, xr_ref[pl.ds(n0, SUBLANES), lanes])
                xc = conv_segment(xm, prev, nxt, cw, cb)
            else:
                zeros = jnp.zeros((SUBLANES, LANES), F32)
                xc = jnp.concatenate(
                    [conv_segment(xm[j * seg:(j + 1) * seg], zeros, zeros, cw, cb) for j in range(segs)],
                    axis=0)
            g = jnp.dot(xc.astype(BF16), wg_ref[k], preferred_element_type=F32) + bg_ref[k]
            hxc = 0.5 * xc
            for d in range(2):
                th_r = jnp.tanh(g[:, (2 * d) * LANES:(2 * d + 1) * LANES])
                th_i = jnp.tanh(g[:, (2 * d + 1) * LANES:(2 * d + 2) * LANES])
                log_a = c4[d:d + 1] * th_r + c4[d:d + 1]
                a = jnp.exp(log_a)
                om = jnp.tanh(log_a) * (-1.0 - a * a)
                root = jnp.where(om > 0.0, om * lax.rsqrt(om), 0.0)
                b = root * ((th_i + 1.0) * hxc)
                for j in range(segs):
                    dst = scratch_row(ci, j)
                    a_sc[d * nk + k, pl.ds(dst, seg), :] = a[j * seg:(j + 1) * seg]
                    b_sc[d * nk + k, pl.ds(dst, seg), :] = b[j * seg:(j + 1) * seg]
        return 0

    lax.fori_loop(0, n_rc, gates, 0)

    def scan_sequence(s, _):
        off = s * sp_rows

        def rows(t):
            return pl.ds(off + t, SUBLANES, stride=pitch)

        def local(t, carry):
            out = []
            for slab, (h, p) in enumerate(carry):
                tt = t if slab < nk else pitch - 1 - t
                a = a_sc[slab, rows(tt), :]
                h = a * h + b_sc[slab, rows(tt), :]
                p = a * p
                h_sc[slab, rows(tt), :] = h
                p_sc[slab, rows(tt), :] = p
                out.append((h, p))
            return tuple(out)

        zero = jnp.zeros((SUBLANES, LANES), F32)
        one = jnp.ones((SUBLANES, LANES), F32)
        ends = lax.fori_loop(0, pitch, local, ((zero, one),) * (2 * nk), unroll=unroll)

        h0 = h0_ref[s]
        starts = []
        finals = [[None] * nk, [None] * nk]
        for slab, (h, p) in enumerate(ends):
            d, k = divmod(slab, nk)
            c = h0[d:d + 1, k * LANES:(k + 1) * LANES]
            cs = [None] * SUBLANES
            for j in (range(SUBLANES) if d == 0 else reversed(range(SUBLANES))):
                cs[j] = c
                c = h[j:j + 1] + p[j:j + 1] * c
            finals[d][k] = c
            starts.append(jnp.concatenate(cs, axis=0))
        st_ref[s] = jnp.concatenate(
            [jnp.concatenate(finals[0], axis=1), jnp.concatenate(finals[1], axis=1)], axis=0)

        def fix(t, _):
            for k in range(nk):
                a_sc[k, rows(t), :] = (h_sc[k, rows(t), :] + p_sc[k, rows(t), :] * starts[k]
                                       + h_sc[nk + k, rows(t), :] + p_sc[nk + k, rows(t), :] * starts[nk + k])
            return 0

        lax.fori_loop(0, pitch, fix, 0, unroll=unroll)
        return 0

    if nsb == 1:
        scan_sequence(0, 0)
    else:
        lax.fori_loop(0, nsb, scan_sequence, 0)

    def gate_out(ci, _):
        base = pl.multiple_of(ci * rc, rc)
        for k in range(nk):
            lanes = slice(k * LANES, (k + 1) * LANES)
            gy = _gelu(yr_ref[pl.ds(base, rc), lanes])
            for j in range(segs):
                src = scratch_row(ci, j)
                o_ref[pl.ds(base + j * seg, seg), lanes] = (gy[j * seg:(j + 1) * seg]
                                                            * a_sc[k, pl.ds(src, seg), :])
        return 0

    lax.fori_loop(0, n_rc, gate_out, 0)


def _lru(xy, conv_w, conv_b, wg, bg, lam, h0, later_weight, *, row_start, m, seq, nsb):
    nblk = m // (seq * nsb)
    assert row_start % (seq * nsb) == 0
    blk0 = row_start // (seq * nsb)
    nk = LRU_LANE_BLOCKS
    width = nk * LANES
    ncb = LRU_WIDTH // width
    pitch = _scan_pitch(seq)
    sc_rows = nsb * SUBLANES * pitch
    rows = seq * nsb
    cast_spec = _cast_slice_spec(later_weight.shape, (ncb, nblk))
    return pl.pallas_call(
        functools.partial(_lru_kernel, seq=seq, nsb=nsb, nk=nk, pitch=pitch,
                          rc=LRU_ROW_CHUNK, unroll=LRU_SCAN_UNROLL),
        grid=(ncb, nblk),
        in_specs=[
            pl.BlockSpec((rows, width), lambda j, b: (blk0 + b, j)),
            pl.BlockSpec((rows, width), lambda j, b: (blk0 + b, ncb + j)),
            pl.BlockSpec((4, width), lambda j, b: (0, j)),
            pl.BlockSpec((1, width), lambda j, b: (0, j)),
            pl.BlockSpec((nk, LANES, 4 * LANES), lambda j, b: (j, 0, 0)),
            pl.BlockSpec((nk, 1, 4 * LANES), lambda j, b: (j, 0, 0)),
            pl.BlockSpec((2, width), lambda j, b: (0, j)),
            pl.BlockSpec((nsb, 2, width), lambda j, b: (b, 0, j)),
            cast_spec,
        ],
        out_specs=[
            pl.BlockSpec((rows, width), lambda j, b: (b, j)),
            pl.BlockSpec((nsb, 2, width), lambda j, b: (b, 0, j)),
            cast_spec,
        ],
        out_shape=[
            jax.ShapeDtypeStruct((m, LRU_WIDTH), F32),
            jax.ShapeDtypeStruct((m // seq, 2, LRU_WIDTH), F32),
            jax.ShapeDtypeStruct(later_weight.shape, BF16),
        ],
        scratch_shapes=[pltpu.VMEM((2 * nk, sc_rows, LANES), F32)] * 4,
        compiler_params=pltpu.CompilerParams(
            dimension_semantics=("arbitrary", "arbitrary"), vmem_limit_bytes=LRU_VMEM),
        name="rglru",
    )(xy, xy, conv_w, conv_b, wg, bg, lam, h0, later_weight)


GROUP = SUBLANES * SUBLANES


def _swap_rows(slab_ref, val, r0):
    n = val.shape[0]
    for k in range(D_MODEL // LANES):
        slab_ref[k, r0:r0 + n, :] = val[:, k * LANES:(k + 1) * LANES]
    cols = []
    for k in range(D_MODEL // LANES):
        rows = [slab_ref[k, pl.ds(r0 + g * GROUP + t, SUBLANES, stride=SUBLANES), :]
                for g in range(n // GROUP) for t in range(SUBLANES)]
        cols.append(jnp.concatenate(rows, axis=0))
    return jnp.concatenate(cols, axis=1)


def _ffn_tile(x_ref, ol_ref, os_ref, mod_ref, gl_ref, gs_ref, wout_ref, n2_ref, up_ref,
              cw_ref, cb_ref, down_ref, fn_ref, y_ref, slab_sc, *, tm, fc, es, period, row):
    g1 = _mod_row(mod_ref, 2, row)
    sh2 = _mod_row(mod_ref, 3, row)
    sc2 = _mod_row(mod_ref, 4, row)
    g2 = _mod_row(mod_ref, 5, row)
    nl = _rms(ol_ref[...], gl_ref[...]).astype(BF16)
    ns = _rms(os_ref[...], gs_ref[...]).astype(BF16)
    o = (jnp.dot(nl, wout_ref[0:LRU_WIDTH, :], preferred_element_type=F32)
         + jnp.dot(ns, wout_ref[LRU_WIDTH:LRU_WIDTH + SGU_WIDTH, :], preferred_element_type=F32))
    x1 = x_ref[...] + g1 * o
    h2f = _rms(x1, n2_ref[...] * (1.0 + sc2)) + sh2
    h2 = _swap_rows(slab_sc, h2f, 0).astype(BF16)

    sub = lax.broadcasted_iota(jnp.int32, (SUBLANES, fc), 0)
    groups_per_period = period // GROUP
    n_groups = tm // GROUP

    def conv(z, c0):
        w = cw_ref[:, c0:c0 + fc]
        b = cb_ref[:, c0:c0 + fc]
        w0, w1, w2 = w[0:1], w[1:2], w[2:3]
        z8 = [[z[g * GROUP + t * SUBLANES:g * GROUP + (t + 1) * SUBLANES] for t in range(SUBLANES)]
              for g in range(n_groups)]
        outs = []
        for g in range(n_groups):
            cur = z8[g]
            lo = pltpu.roll(cur[SUBLANES - 1], 1, axis=0)
            if g % groups_per_period == 0:
                lo = jnp.where(sub == 0, 0.0, lo)
            else:
                lo = jnp.where(sub == 0, pltpu.roll(z8[g - 1][SUBLANES - 1], 1, axis=0), lo)
            hi = pltpu.roll(cur[0], SUBLANES - 1, axis=0)
            if g % groups_per_period == groups_per_period - 1:
                hi = jnp.where(sub == SUBLANES - 1, 0.0, hi)
            else:
                hi = jnp.where(sub == SUBLANES - 1, pltpu.roll(z8[g + 1][0], SUBLANES - 1, axis=0), hi)
            zm = [lo] + cur[:SUBLANES - 1]
            zp = cur[1:] + [hi]
            for t in range(SUBLANES):
                outs.append(b + w0 * zm[t] + w1 * cur[t] + w2 * zp[t])
        return jnp.concatenate(outs, axis=0)

    def up(f0):
        zg = jnp.dot(h2, up_ref[:, f0:f0 + fc], preferred_element_type=F32)
        zv = jnp.dot(h2, up_ref[:, D_FF + f0:D_FF + f0 + fc], preferred_element_type=F32)
        return zg, zv

    acc = jnp.zeros((tm, D_MODEL), F32)
    nxt = up(0)
    for f0 in range(0, D_FF, fc):
        zg, zv = nxt
        if f0 + fc < D_FF:
            nxt = up(f0 + fc)
        hg = 0.5 * conv(zg, f0)
        act = (hg * (jnp.tanh(hg) + 1.0) * conv(zv, D_FF + f0)).astype(BF16)
        if f0 + fc < D_FF:
            acc = acc + jnp.dot(act, down_ref[f0:f0 + fc, :], preferred_element_type=F32)
        else:
            for r0 in range(0, tm, es):
                a = acc[r0:r0 + es] + jnp.dot(act[r0:r0 + es], down_ref[f0:f0 + fc, :],
                                              preferred_element_type=F32)
                x2 = x1[r0:r0 + es] + g2 * _swap_rows(slab_sc, a, r0)
                y_ref[r0:r0 + es, :] = _rms(x2, fn_ref[...])


def _ffn_kernel(xc_ref, xl_ref, olc_ref, oll_ref, os_ref, mod_ref, gl_ref, gs_ref, wout_ref, n2_ref,
                up_ref, cw_ref, cb_ref, down_ref, fn_ref, yc_ref, yl_ref, slab_sc, *,
                nc, lat_mod_row, ctx_period, lat_period, **tile):
    i = pl.program_id(0)
    shared = (mod_ref, gl_ref, gs_ref, wout_ref, n2_ref, up_ref, cw_ref, cb_ref, down_ref, fn_ref)

    @pl.when(i < nc)
    def _():
        _ffn_tile(xc_ref, olc_ref, os_ref, *shared, yc_ref, slab_sc, period=ctx_period, row=0, **tile)

    @pl.when(i >= nc)
    def _():
        _ffn_tile(xl_ref, oll_ref, os_ref, *shared, yl_ref, slab_sc, period=lat_period,
                  row=lat_mod_row(i), **tile)


def _ffn(xc, xl, olru_c, olru_l, osgu, mod, g_lru, g_sgu, w_out, norm2, ffn_up, conv_w, conv_b,
         ffn_down, final_norm, *, tm, ctx_period, lat_period, lat_seq):
    nc, n, ctx_map, lat_map, lat_mod_row = _token_groups(tm, xc.shape[0], xl.shape[0], lat_seq)
    const = lambda i: (0, 0)
    resident = lambda shape: pl.BlockSpec(shape, const, pipeline_mode=pl.Buffered(1))
    return pl.pallas_call(
        functools.partial(_ffn_kernel, nc=nc, lat_mod_row=lat_mod_row, ctx_period=ctx_period,
                          lat_period=lat_period, tm=tm, fc=FFN_FC, es=FFN_ES),
        grid=(n,),
        in_specs=[
            pl.BlockSpec((tm, D_MODEL), ctx_map),
            pl.BlockSpec((tm, D_MODEL), lat_map),
            pl.BlockSpec((tm, LRU_WIDTH), ctx_map),
            pl.BlockSpec((tm, LRU_WIDTH), lat_map),
            pl.BlockSpec((tm, SGU_WIDTH), lambda i: (i, 0)),
            pl.BlockSpec((N_MOD, MOD_ROWS, D_MODEL), lambda i: (0, 0, 0)),
            pl.BlockSpec((1, LRU_WIDTH), const),
            pl.BlockSpec((1, SGU_WIDTH), const),
            resident((LRU_WIDTH + SGU_WIDTH, D_MODEL)),
            pl.BlockSpec((1, D_MODEL), const),
            resident((D_MODEL, 2 * D_FF)),
            pl.BlockSpec((3, 2 * D_FF), const),
            pl.BlockSpec((1, 2 * D_FF), const),
            resident((D_FF, D_MODEL)),
            pl.BlockSpec((1, D_MODEL), const),
        ],
        out_specs=[pl.BlockSpec((tm, D_MODEL), ctx_map), pl.BlockSpec((tm, D_MODEL), lat_map)],
        out_shape=[jax.ShapeDtypeStruct(xc.shape, F32), jax.ShapeDtypeStruct(xl.shape, F32)],
        scratch_shapes=[pltpu.VMEM((D_MODEL // LANES, tm, LANES), F32)],
        compiler_params=pltpu.CompilerParams(
            dimension_semantics=("arbitrary",), vmem_limit_bytes=FFN_VMEM),
        name="outproj_ffn",
    )(xc, xl, olru_c, olru_l, osgu, mod, g_lru, g_sgu, w_out, norm2, ffn_up, conv_w, conv_b, ffn_down,
      final_norm)


def _gate_params(wa, ba, wx, bx):
    nlb = LRU_HEADS // 2
    w = jnp.stack([wa[0], wx[0], wa[1], wx[1]], axis=0)
    wp = w.reshape(4, nlb, 2, LRU_HEAD_DIM, LRU_HEAD_DIM)
    pair = 0.5 * jnp.eye(2, dtype=F32)
    bd = wp[:, :, :, :, None, :] * pair[None, None, :, None, :, None]
    bd = bd.reshape(4, nlb, LANES, LANES)
    wg = jnp.transpose(bd, (1, 2, 0, 3)).reshape(nlb, LANES, 4 * LANES).astype(BF16)
    b = 0.5 * jnp.stack([ba[0], bx[0], ba[1], bx[1]], axis=0)
    bg = jnp.transpose(b.reshape(4, nlb, LANES), (1, 0, 2)).reshape(nlb, 1, 4 * LANES)
    return wg, bg


def kernel(x_prompt, x_sample, state_lru, c, c_ctx, norm1, norm2, w_ada, b_ada, w_in, lru_conv_w,
           lru_conv_b, lru_wa, lru_ba, lru_wx, lru_bx, lru_lam, sgu_ws, sgu_bs, g_lru, g_sgu, w_out,
           ffn_up, ffn_conv_w, ffn_conv_b, ffn_down, final_norm):
    batch, seq, _ = x_prompt.shape
    dec_batch, dec_seq, _ = x_sample.shape
    depth = norm1.shape[0]
    assert depth == 1, "the final norm is fused into the (single) layer's last kernel"
    l = 0

    xc = x_prompt.reshape(batch * seq, D_MODEL)
    xl = x_sample.reshape(dec_batch * dec_seq, D_MODEL)
    n_ctx, n_lat = xc.shape[0], xl.shape[0]
    cc = jnp.concatenate(
        [c_ctx[None, :], c, jnp.zeros((MOD_ROWS - 1 - dec_batch, D_MODEL), F32)], axis=0)
    zeros_state = jnp.zeros((batch, 2, LRU_WIDTH), x_prompt.dtype)

    mod = _modulation(cc, w_ada[l], b_ada[l][None, :])
    wg, bg = _gate_params(lru_wa[l], lru_ba[l], lru_wx[l], lru_bx[l])
    lru_params = (lru_conv_w[l], lru_conv_b[l][None, :], wg, bg, lru_lam[l])
    h0_lat = state_lru.reshape(dec_batch, 2, LRU_WIDTH)

    xy, osgu, w_out_b = _inproj(
        xc, xl, mod, norm1[l][None, :], w_in[l], sgu_ws[l].astype(BF16), sgu_bs[l].T,
        w_out[l], tm=INPROJ_TM, lat_seq=dec_seq)
    olru_c, new_state, down_b = _lru(xy, *lru_params, zeros_state, ffn_down[l], row_start=0, m=n_ctx,
                                     seq=seq, nsb=LRU_CTX_SEQS)
    olru_l, _, up_b = _lru(xy, *lru_params, h0_lat, ffn_up[l], row_start=n_ctx, m=n_lat,
                           seq=dec_seq, nsb=1)
    yc, yl = _ffn(xc, xl, olru_c, olru_l, osgu, mod, g_lru[l][None, :], g_sgu[l][None, :],
                  w_out_b, norm2[l][None, :], up_b, ffn_conv_w[l],
                  ffn_conv_b[l][None, :], down_b, final_norm[None, :], tm=FFN_TM,
                  ctx_period=seq, lat_period=GRID_W, lat_seq=dec_seq)
    return (yc.reshape(batch, seq, D_MODEL), yl.reshape(dec_batch, dec_seq, D_MODEL),
            new_state[:, None])
```

```python
import functools
import math

import jax
import jax.numpy as jnp
from jax import lax
from jax.experimental import pallas as pl
from jax.experimental.pallas import tpu as pltpu

D_MODEL = 1024
LRU_HEADS = 8
LRU_WIDTH = 512
LRU_HEAD_DIM = 64
LRU_C = 8.0
SGU_GROUPS = 4
SGU_WIDTH = 512
CHUNK = 128
D_FF = 3072
N_MOD = 6
EPS = 1e-6
GRID_W = 64

LANES = 128
SUBLANES = 8
BF16_SUBLANES = 16
MOD_ROWS = 8
BF16 = jnp.bfloat16
F32 = jnp.float32

MIB = 1024 * 1024
INPROJ_TM = 1024
INPROJ_VMEM = 56 * MIB
LRU_ROW_CHUNK = 512
LRU_LANE_BLOCKS = 2
LRU_CTX_SEQS = 8
LRU_SCAN_UNROLL = 8
LRU_VMEM = 56 * MIB
FFN_TM = 512
FFN_FC = 3072
FFN_ES = 256
FFN_VMEM = 60 * MIB

_GELU_C0 = math.sqrt(2.0 / math.pi)
_GELU_C1 = _GELU_C0 * 0.044715


def _rms(x, gain):
    return x * lax.rsqrt(jnp.mean(x * x, axis=-1, keepdims=True) + EPS) * gain


def _gelu(x):
    return (0.5 * x) * (1.0 + jnp.tanh(x * (_GELU_C0 + _GELU_C1 * (x * x))))


def _mod_row(mod_ref, k, row):
    return mod_ref[k, pl.ds(row, 1), :]


def _mod_kernel(c_ref, w_ref, b_ref, o_ref):
    s = jax.nn.silu(c_ref[...]).astype(BF16)
    o_ref[...] = jnp.dot(s, w_ref[...].astype(BF16), preferred_element_type=F32) + b_ref[...]


def _modulation(cc, w_ada, b_ada):
    return pl.pallas_call(
        _mod_kernel,
        grid=(N_MOD,),
        in_specs=[
            pl.BlockSpec((MOD_ROWS, D_MODEL), lambda j: (0, 0)),
            pl.BlockSpec((D_MODEL, D_MODEL), lambda j: (0, j)),
            pl.BlockSpec((1, D_MODEL), lambda j: (0, j)),
        ],
        out_specs=pl.BlockSpec((None, MOD_ROWS, D_MODEL), lambda j: (j, 0, 0)),
        out_shape=jax.ShapeDtypeStruct((N_MOD, MOD_ROWS, D_MODEL), F32),
        compiler_params=pltpu.CompilerParams(dimension_semantics=("arbitrary",)),
        name="modulation",
    )(cc, w_ada, b_ada)


def _token_groups(tm, n_ctx, n_lat, lat_seq):
    assert n_ctx % tm == 0 and n_lat % tm == 0 and lat_seq % tm == 0
    nc = n_ctx // tm
    ctx_map = lambda i: (jnp.minimum(i, nc - 1), 0)
    lat_map = lambda i: (jnp.maximum(i - nc, 0), 0)
    lat_mod_row = lambda i: 1 + ((i - nc) * tm) // lat_seq
    return nc, nc + n_lat // tm, ctx_map, lat_map, lat_mod_row


def _inproj_kernel(xc_ref, xl_ref, mod_ref, n1_ref, win_f_ref, ws_ref, bs_ref, w_ref,
                   xy_ref, osgu_ref, w_b_ref, win_ref, *, tm, nc, lat_mod_row):
    i = pl.program_id(0)
    w_b_ref[...] = w_ref[...].astype(BF16)

    @pl.when(i == 0)
    def _():
        win_ref[...] = win_f_ref[...].astype(BF16)

    def body(x_ref, row):
        sh1 = _mod_row(mod_ref, 0, row)
        sc1 = _mod_row(mod_ref, 1, row)
        hb = (_rms(x_ref[...], n1_ref[...] * (1.0 + sc1)) + sh1).astype(BF16)
        uv = jnp.dot(hb, win_ref[:, 2 * LRU_WIDTH:], preferred_element_type=F32)
        xy_ref[...] = jnp.dot(hb, win_ref[:, 0:2 * LRU_WIDTH], preferred_element_type=F32)
        for g in range(SGU_GROUPS):
            lo = g * LANES
            gu = _gelu(uv[:, lo:lo + LANES])
            gv = _gelu(uv[:, SGU_WIDTH + lo:SGU_WIDTH + lo + LANES]).astype(BF16)
            bias = jnp.broadcast_to(bs_ref[:, g:g + 1], (CHUNK, LANES))
            for ck in range(tm // CHUNK):
                r0 = ck * CHUNK
                s = jnp.dot(ws_ref[g], gv[r0:r0 + CHUNK], preferred_element_type=F32) + bias
                osgu_ref[r0:r0 + CHUNK, lo:lo + LANES] = gu[r0:r0 + CHUNK] * s

    @pl.when(i < nc)
    def _():
        body(xc_ref, 0)

    @pl.when(i >= nc)
    def _():
        body(xl_ref, lat_mod_row(i))


def _cast_slice_spec(shape, grid):
    n_steps = math.prod(grid)
    rows = shape[0]
    r = BF16_SUBLANES
    while rows % r or rows // r > n_steps:
        r += BF16_SUBLANES
    last = rows // r - 1

    def index_map(*idx):
        step = 0
        for i, n in zip(idx, grid):
            step = step * n + i
        return (jnp.minimum(step, last), 0)

    return pl.BlockSpec((r, shape[1]), index_map)


def _inproj(xc, xl, mod, norm1, w_in, ws, bs_t, later_weight, *, tm, lat_seq):
    nc, n, ctx_map, lat_map, lat_mod_row = _token_groups(tm, xc.shape[0], xl.shape[0], lat_seq)
    m = n * tm
    const = lambda i: (0, 0)
    cast_specs = [_cast_slice_spec(later_weight.shape, (n,))]
    return pl.pallas_call(
        functools.partial(_inproj_kernel, tm=tm, nc=nc, lat_mod_row=lat_mod_row),
        grid=(n,),
        in_specs=[
            pl.BlockSpec((tm, D_MODEL), ctx_map),
            pl.BlockSpec((tm, D_MODEL), lat_map),
            pl.BlockSpec((N_MOD, MOD_ROWS, D_MODEL), lambda i: (0, 0, 0)),
            pl.BlockSpec((1, D_MODEL), const),
            pl.BlockSpec((D_MODEL, 4 * LRU_WIDTH), const, pipeline_mode=pl.Buffered(1)),
            pl.BlockSpec((SGU_GROUPS, CHUNK, CHUNK), lambda i: (0, 0, 0)),
            pl.BlockSpec((CHUNK, SGU_GROUPS), const),
        ] + cast_specs,
        out_specs=[
            pl.BlockSpec((tm, 2 * LRU_WIDTH), lambda i: (i, 0)),
            pl.BlockSpec((tm, SGU_WIDTH), lambda i: (i, 0)),
        ] + cast_specs,
        out_shape=[
            jax.ShapeDtypeStruct((m, 2 * LRU_WIDTH), F32),
            jax.ShapeDtypeStruct((m, SGU_WIDTH), F32),
        ] + [jax.ShapeDtypeStruct(later_weight.shape, BF16)],
        scratch_shapes=[pltpu.VMEM(w_in.shape, BF16)],
        compiler_params=pltpu.CompilerParams(
            dimension_semantics=("arbitrary",), vmem_limit_bytes=INPROJ_VMEM),
        name="inproj_sgu",
    )(xc, xl, mod, norm1, w_in, ws, bs_t, later_weight)


def _scan_pitch(seq):
    pitch = seq // SUBLANES
    return pitch + (4 - pitch % 8) % 8


def _lru_kernel(xr_ref, yr_ref, cw_ref, cb_ref, wg_ref, bg_ref, lam_ref, h0_ref, w_ref,
                o_ref, st_ref, w_b_ref, a_sc, b_sc, p_sc, h_sc, *, seq, nsb, nk, pitch, rc, unroll):
    sp_rows = SUBLANES * pitch
    n_rc = (seq * nsb) // rc
    seg = min(seq, rc)
    segs = rc // seg
    chunks_per_seq = seq // seg
    w_b_ref[...] = w_ref[...].astype(BF16)
    c4_all = (-0.5 * LRU_C) * jax.nn.softplus(-lam_ref[...])
    cw_all = cw_ref[...]
    cb_all = cb_ref[...]

    for s in range(nsb):
        lo, hi = s * sp_rows + seq, (s + 1) * sp_rows
        for slab in range(2 * nk):
            a_sc[slab, lo:hi, :] = jnp.ones((hi - lo, LANES), F32)
            b_sc[slab, lo:hi, :] = jnp.zeros((hi - lo, LANES), F32)

    def scratch_row(ci, k):
        if chunks_per_seq > 1:
            return pl.multiple_of((ci // chunks_per_seq) * sp_rows + (ci % chunks_per_seq) * seg,
                                  SUBLANES)
        return pl.multiple_of((ci * segs + k) * sp_rows, SUBLANES)

    def conv_segment(xs, prev, nxt, cw, cb):
        ext = jnp.concatenate([prev, xs, nxt], axis=0)
        n_ext = seg + 2 * SUBLANES
        xc = cb + cw[2:3] * xs
        xc = xc + cw[0:1] * pltpu.roll(ext, 2, axis=0)[SUBLANES:SUBLANES + seg]
        xc = xc + cw[1:2] * pltpu.roll(ext, 1, axis=0)[SUBLANES:SUBLANES + seg]
        return xc + cw[3:4] * pltpu.roll(ext, n_ext - 1, axis=0)[SUBLANES:SUBLANES + seg]

    def gates(ci, _):
        base = pl.multiple_of(ci * rc, rc)
        for k in range(nk):
            lanes = slice(k * LANES, (k + 1) * LANES)
            cw, cb, c4 = cw_all[:, lanes], cb_all[:, lanes], c4_all[:, lanes]
            xm = xr_ref[pl.ds(base, rc), lanes]
            if chunks_per_seq > 1:
                cs = ci % chunks_per_seq
                p0 = pl.multiple_of(jnp.maximum(base - SUBLANES, 0), SUBLANES)
                n0 = pl.multiple_of(jnp.minimum(base + rc, seq * nsb - SUBLANES), SUBLANES)
                prev = jnp.where(cs == 0, 0.0, xr_ref[pl.ds(p0, SUBLANES), lanes])
                nxt = jnp.where(cs == chunks_per_seq - 1, 0.0, xr_ref[pl.ds(n0, SUBLANES), lanes])
                xc = conv_segment(xm, prev, nxt, cw, cb)
            else:
                zeros = jnp.zeros((SUBLANES, LANES), F32)
                xc = jnp.concatenate(
                    [conv_segment(xm[j * seg:(j + 1) * seg], zeros, zeros, cw, cb) for j in range(segs)],
                    axis=0)
            g = jnp.dot(xc.astype(BF16), wg_ref[k], preferred_element_type=F32) + bg_ref[k]
            hxc = 0.5 * xc
            for d in range(2):
                th_r = jnp.tanh(g[:, (2 * d) * LANES:(2 * d + 1) * LANES])
                th_i = jnp.tanh(g[:, (2 * d + 1) * LANES:(2 * d + 2) * LANES])
                log_a = c4[d:d + 1] * th_r + c4[d:d + 1]
                a = jnp.exp(log_a)
                om = jnp.tanh(log_a) * (-1.0 - a * a)
                root = jnp.where(om > 0.0, om * lax.rsqrt(om), 0.0)
                b = root * ((th_i + 1.0) * hxc)
                for j in range(segs):
                    dst = scratch_row(ci, j)
                    a_sc[d * nk + k, pl.ds(dst, seg), :] = a[j * seg:(j + 1) * seg]
                    b_sc[d * nk + k, pl.ds(dst, seg), :] = b[j * seg:(j + 1) * seg]
        return 0

    lax.fori_loop(0, n_rc, gates, 0)

    def scan_sequence(s, _):
        off = s * sp_rows

        def rows(t):
            return pl.ds(off + t, SUBLANES, stride=pitch)

        def local(t, carry):
            out = []
            for slab, (h, p) in enumerate(carry):
                tt = t if slab < nk else pitch - 1 - t
                a = a_sc[slab, rows(tt), :]
                h = a * h + b_sc[slab, rows(tt), :]
                p = a * p
                h_sc[slab, rows(tt), :] = h
                p_sc[slab, rows(tt), :] = p
                out.append((h, p))
            return tuple(out)

        zero = jnp.zeros((SUBLANES, LANES), F32)
        one = jnp.ones((SUBLANES, LANES), F32)
        ends = lax.fori_loop(0, pitch, local, ((zero, one),) * (2 * nk), unroll=unroll)

        h0 = h0_ref[s]
        starts = []
        finals = [[None] * nk, [None] * nk]
        for slab, (h, p) in enumerate(ends):
            d, k = divmod(slab, nk)
            c = h0[d:d + 1, k * LANES:(k + 1) * LANES]
            cs = [None] * SUBLANES
            for j in (range(SUBLANES) if d == 0 else reversed(range(SUBLANES))):
                cs[j] = c
                c = h[j:j + 1] + p[j:j + 1] * c
            finals[d][k] = c
            starts.append(jnp.concatenate(cs, axis=0))
        st_ref[s] = jnp.concatenate(
            [jnp.concatenate(finals[0], axis=1), jnp.concatenate(finals[1], axis=1)], axis=0)

        def fix(t, _):
            for k in range(nk):
                a_sc[k, rows(t), :] = (h_sc[k, rows(t), :] + p_sc[k, rows(t), :] * starts[k]
                                       + h_sc[nk + k, rows(t), :] + p_sc[nk + k, rows(t), :] * starts[nk + k])
            return 0

        lax.fori_loop(0, pitch, fix, 0, unroll=unroll)
        return 0

    if nsb == 1:
        scan_sequence(0, 0)
    else:
        lax.fori_loop(0, nsb, scan_sequence, 0)

    def gate_out(ci, _):
        base = pl.multiple_of(ci * rc, rc)
        for k in range(nk):
            lanes = slice(k * LANES, (k + 1) * LANES)
            gy = _gelu(yr_ref[pl.ds(base, rc), lanes])
            for j in range(segs):
                src = scratch_row(ci, j)
                o_ref[pl.ds(base + j * seg, seg), lanes] = (gy[j * seg:(j + 1) * seg]
                                                            * a_sc[k, pl.ds(src, seg), :])
        return 0

    lax.fori_loop(0, n_rc, gate_out, 0)


def _lru(xy, conv_w, conv_b, wg, bg, lam, h0, later_weight, *, row_start, m, seq, nsb):
    nblk = m // (seq * nsb)
    assert row_start % (seq * nsb) == 0
    blk0 = row_start // (seq * nsb)
    nk = LRU_LANE_BLOCKS
    width = nk * LANES
    ncb = LRU_WIDTH // width
    pitch = _scan_pitch(seq)
    sc_rows = nsb * SUBLANES * pitch
    rows = seq * nsb
    cast_spec = _cast_slice_spec(later_weight.shape, (ncb, nblk))
    return pl.pallas_call(
        functools.partial(_lru_kernel, seq=seq, nsb=nsb, nk=nk, pitch=pitch,
                          rc=LRU_ROW_CHUNK, unroll=LRU_SCAN_UNROLL),
        grid=(ncb, nblk),
        in_specs=[
            pl.BlockSpec((rows, width), lambda j, b: (blk0 + b, j)),
            pl.BlockSpec((rows, width), lambda j, b: (blk0 + b, ncb + j)),
            pl.BlockSpec((4, width), lambda j, b: (0, j)),
            pl.BlockSpec((1, width), lambda j, b: (0, j)),
            pl.BlockSpec((nk, LANES, 4 * LANES), lambda j, b: (j, 0, 0)),
            pl.BlockSpec((nk, 1, 4 * LANES), lambda j, b: (j, 0, 0)),
            pl.BlockSpec((2, width), lambda j, b: (0, j)),
            pl.BlockSpec((nsb, 2, width), lambda j, b: (b, 0, j)),
            cast_spec,
        ],
        out_specs=[
            pl.BlockSpec((rows, width), lambda j, b: (b, j)),
            pl.BlockSpec((nsb, 2, width), lambda j, b: (b, 0, j)),
            cast_spec,
        ],
        out_shape=[
            jax.ShapeDtypeStruct((m, LRU_WIDTH), F32),
            jax.ShapeDtypeStruct((m // seq, 2, LRU_WIDTH), F32),
            jax.ShapeDtypeStruct(later_weight.shape, BF16),
        ],
        scratch_shapes=[pltpu.VMEM((2 * nk, sc_rows, LANES), F32)] * 4,
        compiler_params=pltpu.CompilerParams(
            dimension_semantics=("arbitrary", "arbitrary"), vmem_limit_bytes=LRU_VMEM),
        name="rglru",
    )(xy, xy, conv_w, conv_b, wg, bg, lam, h0, later_weight)


GROUP = SUBLANES * SUBLANES


def _swap_rows(slab_ref, val, r0):
    n = val.shape[0]
    for k in range(D_MODEL // LANES):
        slab_ref[k, r0:r0 + n, :] = val[:, k * LANES:(k + 1) * LANES]
    cols = []
    for k in range(D_MODEL // LANES):
        rows = [slab_ref[k, pl.ds(r0 + g * GROUP + t, SUBLANES, stride=SUBLANES), :]
                for g in range(n // GROUP) for t in range(SUBLANES)]
        cols.append(jnp.concatenate(rows, axis=0))
    return jnp.concatenate(cols, axis=1)


def _ffn_tile(x_ref, ol_ref, os_ref, mod_ref, gl_ref, gs_ref, wout_ref, n2_ref, up_ref,
              cw_ref, cb_ref, down_ref, fn_ref, y_ref, slab_sc, *, tm, fc, es, period, row):
    g1 = _mod_row(mod_ref, 2, row)
    sh2 = _mod_row(mod_ref, 3, row)
    sc2 = _mod_row(mod_ref, 4, row)
    g2 = _mod_row(mod_ref, 5, row)
    nl = _rms(ol_ref[...], gl_ref[...]).astype(BF16)
    ns = _rms(os_ref[...], gs_ref[...]).astype(BF16)
    o = (jnp.dot(nl, wout_ref[0:LRU_WIDTH, :], preferred_element_type=F32)
         + jnp.dot(ns, wout_ref[LRU_WIDTH:LRU_WIDTH + SGU_WIDTH, :], preferred_element_type=F32))
    x1 = x_ref[...] + g1 * o
    h2f = _rms(x1, n2_ref[...] * (1.0 + sc2)) + sh2
    h2 = _swap_rows(slab_sc, h2f, 0).astype(BF16)

    sub = lax.broadcasted_iota(jnp.int32, (SUBLANES, fc), 0)
    groups_per_period = period // GROUP
    n_groups = tm // GROUP

    def conv(z, c0):
        w = cw_ref[:, c0:c0 + fc]
        b = cb_ref[:, c0:c0 + fc]
        w0, w1, w2 = w[0:1], w[1:2], w[2:3]
        z8 = [[z[g * GROUP + t * SUBLANES:g * GROUP + (t + 1) * SUBLANES] for t in range(SUBLANES)]
              for g in range(n_groups)]
        outs = []
        for g in range(n_groups):
            cur = z8[g]
            lo = pltpu.roll(cur[SUBLANES - 1], 1, axis=0)
            if g % groups_per_period == 0:
                lo = jnp.where(sub == 0, 0.0, lo)
            else:
                lo = jnp.where(sub == 0, pltpu.roll(z8[g - 1][SUBLANES - 1], 1, axis=0), lo)
            hi = pltpu.roll(cur[0], SUBLANES - 1, axis=0)
            if g % groups_per_period == groups_per_period - 1:
                hi = jnp.where(sub == SUBLANES - 1, 0.0, hi)
            else:
                hi = jnp.where(sub == SUBLANES - 1, pltpu.roll(z8[g + 1][0], SUBLANES - 1, axis=0), hi)
            zm = [lo] + cur[:SUBLANES - 1]
            zp = cur[1:] + [hi]
            for t in range(SUBLANES):
                outs.append(b + w0 * zm[t] + w1 * cur[t] + w2 * zp[t])
        return jnp.concatenate(outs, axis=0)

    def up(f0):
        zg = jnp.dot(h2, up_ref[:, f0:f0 + fc], preferred_element_type=F32)
        zv = jnp.dot(h2, up_ref[:, D_FF + f0:D_FF + f0 + fc], preferred_element_type=F32)
        return zg, zv

    acc = jnp.zeros((tm, D_MODEL), F32)
    nxt = up(0)
    for f0 in range(0, D_FF, fc):
        zg, zv = nxt
        if f0 + fc < D_FF:
            nxt = up(f0 + fc)
        hg = 0.5 * conv(zg, f0)
        act = (hg * (jnp.tanh(hg) + 1.0) * conv(zv, D_FF + f0)).astype(BF16)
        if f0 + fc < D_FF:
            acc = acc + jnp.dot(act, down_ref[f0:f0 + fc, :], preferred_element_type=F32)
        else:
            for r0 in range(0, tm, es):
                a = acc[r0:r0 + es] + jnp.dot(act[r0:r0 + es], down_ref[f0:f0 + fc, :],
                                              preferred_element_type=F32)
                x2 = x1[r0:r0 + es] + g2 * _swap_rows(slab_sc, a, r0)
                y_ref[r0:r0 + es, :] = _rms(x2, fn_ref[...])


def _ffn_kernel(xc_ref, xl_ref, olc_ref, oll_ref, os_ref, mod_ref, gl_ref, gs_ref, wout_ref, n2_ref,
                up_ref, cw_ref, cb_ref, down_ref, fn_ref, yc_ref, yl_ref, slab_sc, *,
                nc, lat_mod_row, ctx_period, lat_period, **tile):
    i = pl.program_id(0)
    shared = (mod_ref, gl_ref, gs_ref, wout_ref, n2_ref, up_ref, cw_ref, cb_ref, down_ref, fn_ref)

    @pl.when(i < nc)
    def _():
        _ffn_tile(xc_ref, olc_ref, os_ref, *shared, yc_ref, slab_sc, period=ctx_period, row=0, **tile)

    @pl.when(i >= nc)
    def _():
        _ffn_tile(xl_ref, oll_ref, os_ref, *shared, yl_ref, slab_sc, period=lat_period,
                  row=lat_mod_row(i), **tile)


def _ffn(xc, xl, olru_c, olru_l, osgu, mod, g_lru, g_sgu, w_out, norm2, ffn_up, conv_w, conv_b,
         ffn_down, final_norm, *, tm, ctx_period, lat_period, lat_seq):
    nc, n, ctx_map, lat_map, lat_mod_row = _token_groups(tm, xc.shape[0], xl.shape[0], lat_seq)
    const = lambda i: (0, 0)
    resident = lambda shape: pl.BlockSpec(shape, const, pipeline_mode=pl.Buffered(1))
    return pl.pallas_call(
        functools.partial(_ffn_kernel, nc=nc, lat_mod_row=lat_mod_row, ctx_period=ctx_period,
                          lat_period=lat_period, tm=tm, fc=FFN_FC, es=FFN_ES),
        grid=(n,),
        in_specs=[
            pl.BlockSpec((tm, D_MODEL), ctx_map),
            pl.BlockSpec((tm, D_MODEL), lat_map),
            pl.BlockSpec((tm, LRU_WIDTH), ctx_map),
            pl.BlockSpec((tm, LRU_WIDTH), lat_map),
            pl.BlockSpec((tm, SGU_WIDTH), lambda i: (i, 0)),
            pl.BlockSpec((N_MOD, MOD_ROWS, D_MODEL), lambda i: (0, 0, 0)),
            pl.BlockSpec((1, LRU_WIDTH), const),
            pl.BlockSpec((1, SGU_WIDTH), const),
            resident((LRU_WIDTH + SGU_WIDTH, D_MODEL)),
            pl.BlockSpec((1, D_MODEL), const),
            resident((D_MODEL, 2 * D_FF)),
            pl.BlockSpec((3, 2 * D_FF), const),
            pl.BlockSpec((1, 2 * D_FF), const),
            resident((D_FF, D_MODEL)),
            pl.BlockSpec((1, D_MODEL), const),
        ],
        out_specs=[pl.BlockSpec((tm, D_MODEL), ctx_map), pl.BlockSpec((tm, D_MODEL), lat_map)],
        out_shape=[jax.ShapeDtypeStruct(xc.shape, F32), jax.ShapeDtypeStruct(xl.shape, F32)],
        scratch_shapes=[pltpu.VMEM((D_MODEL // LANES, tm, LANES), F32)],
        compiler_params=pltpu.CompilerParams(
            dimension_semantics=("arbitrary",), vmem_limit_bytes=FFN_VMEM),
        name="outproj_ffn",
    )(xc, xl, olru_c, olru_l, osgu, mod, g_lru, g_sgu, w_out, norm2, ffn_up, conv_w, conv_b, ffn_down,
      final_norm)


def _gate_params(wa, ba, wx, bx):
    nlb = LRU_HEADS // 2
    w = jnp.stack([wa[0], wx[0], wa[1], wx[1]], axis=0)
    wp = w.reshape(4, nlb, 2, LRU_HEAD_DIM, LRU_HEAD_DIM)
    pair = 0.5 * jnp.eye(2, dtype=F32)
    bd = wp[:, :, :, :, None, :] * pair[None, None, :, None, :, None]
    bd = bd.reshape(4, nlb, LANES, LANES)
    wg = jnp.transpose(bd, (1, 2, 0, 3)).reshape(nlb, LANES, 4 * LANES).astype(BF16)
    b = 0.5 * jnp.stack([ba[0], bx[0], ba[1], bx[1]], axis=0)
    bg = jnp.transpose(b.reshape(4, nlb, LANES), (1, 0, 2)).reshape(nlb, 1, 4 * LANES)
    return wg, bg


def kernel(x_prompt, x_sample, state_lru, c, c_ctx, norm1, norm2, w_ada, b_ada, w_in, lru_conv_w,
           lru_conv_b, lru_wa, lru_ba, lru_wx, lru_bx, lru_lam, sgu_ws, sgu_bs, g_lru, g_sgu, w_out,
           ffn_up, ffn_conv_w, ffn_conv_b, ffn_down, final_norm):
    batch, seq, _ = x_prompt.shape
    dec_batch, dec_seq, _ = x_sample.shape
    depth = norm1.shape[0]
    assert depth == 1, "the final norm is fused into the (single) layer's last kernel"
    l = 0

    xc = x_prompt.reshape(batch * seq, D_MODEL)
    xl = x_sample.reshape(dec_batch * dec_seq, D_MODEL)
    n_ctx, n_lat = xc.shape[0], xl.shape[0]
    cc = jnp.concatenate(
        [c_ctx[None, :], c, jnp.zeros((MOD_ROWS - 1 - dec_batch, D_MODEL), F32)], axis=0)
    zeros_state = jnp.zeros((batch, 2, LRU_WIDTH), x_prompt.dtype)

    mod = _modulation(cc, w_ada[l], b_ada[l][None, :])
    wg, bg = _gate_params(lru_wa[l], lru_ba[l], lru_wx[l], lru_bx[l])
    lru_params = (lru_conv_w[l], lru_conv_b[l][None, :], wg, bg, lru_lam[l])
    h0_lat = state_lru.reshape(dec_batch, 2, LRU_WIDTH)

    xy, osgu, w_out_b = _inproj(
        xc, xl, mod, norm1[l][None, :], w_in[l], sgu_ws[l].astype(BF16), sgu_bs[l].T,
        w_out[l], tm=INPROJ_TM, lat_seq=dec_seq)
    olru_c, new_state, down_b = _lru(xy, *lru_params, zeros_state, ffn_down[l], row_start=0, m=n_ctx,
                                     seq=seq, nsb=LRU_CTX_SEQS)
    olru_l, _, up_b = _lru(xy, *lru_params, h0_lat, ffn_up[l], row_start=n_ctx, m=n_lat,
                           seq=dec_seq, nsb=1)
    yc, yl = _ffn(xc, xl, olru_c, olru_l, osgu, mod, g_lru[l][None, :], g_sgu[l][None, :],
                  w_out_b, norm2[l][None, :], up_b, ffn_conv_w[l],
                  ffn_conv_b[l][None, :], down_b, final_norm[None, :], tm=FFN_TM,
                  ctx_period=seq, lat_period=GRID_W, lat_seq=dec_seq)
    return (yc.reshape(batch, seq, D_MODEL), yl.reshape(dec_batch, dec_seq, D_MODEL),
            new_state[:, None])
```

```python
import functools
import math

import jax
import jax.numpy as jnp
from jax import lax
from jax.experimental import pallas as pl
from jax.experimental.pallas import tpu as pltpu

D_MODEL = 1024
LRU_HEADS = 8
LRU_WIDTH = 512
LRU_HEAD_DIM = 64
LRU_C = 8.0
SGU_GROUPS = 4
SGU_WIDTH = 512
CHUNK = 128
D_FF = 3072
N_MOD = 6
EPS = 1e-6
GRID_W = 64

LANES = 128
SUBLANES = 8
BF16_SUBLANES = 16
MOD_ROWS = 8
BF16 = jnp.bfloat16
F32 = jnp.float32

MIB = 1024 * 1024
INPROJ_TM = 1024
INPROJ_VMEM = 56 * MIB
LRU_ROW_CHUNK = 512
LRU_LANE_BLOCKS = 2
LRU_CTX_SEQS = 8
LRU_SCAN_UNROLL = True
LRU_VMEM = 56 * MIB
FFN_TM = 512
FFN_FC = 1536
FFN_ES = 256
FFN_VMEM = 60 * MIB

_GELU_C0 = math.sqrt(2.0 / math.pi)
_GELU_C1 = _GELU_C0 * 0.044715


def _rms(x, gain):
    return x * lax.rsqrt(jnp.mean(x * x, axis=-1, keepdims=True) + EPS) * gain


def _gelu(x):
    return (0.5 * x) * (1.0 + jnp.tanh(x * (_GELU_C0 + _GELU_C1 * (x * x))))


def _mod_row(mod_ref, k, row):
    return mod_ref[k, pl.ds(row, 1), :]


def _mod_kernel(c_ref, w_ref, b_ref, o_ref):
    s = jax.nn.silu(c_ref[...]).astype(BF16)
    o_ref[...] = jnp.dot(s, w_ref[...].astype(BF16), preferred_element_type=F32) + b_ref[...]


def _modulation(cc, w_ada, b_ada):
    return pl.pallas_call(
        _mod_kernel,
        grid=(N_MOD,),
        in_specs=[
            pl.BlockSpec((MOD_ROWS, D_MODEL), lambda j: (0, 0)),
            pl.BlockSpec((D_MODEL, D_MODEL), lambda j: (0, j)),
            pl.BlockSpec((1, D_MODEL), lambda j: (0, j)),
        ],
        out_specs=pl.BlockSpec((None, MOD_ROWS, D_MODEL), lambda j: (j, 0, 0)),
        out_shape=jax.ShapeDtypeStruct((N_MOD, MOD_ROWS, D_MODEL), F32),
        compiler_params=pltpu.CompilerParams(dimension_semantics=("arbitrary",)),
        name="modulation",
    )(cc, w_ada, b_ada)


def _token_groups(tm, n_ctx, n_lat, lat_seq):
    assert n_ctx % tm == 0 and n_lat % tm == 0 and lat_seq % tm == 0
    nc = n_ctx // tm
    ctx_map = lambda i: (jnp.minimum(i, nc - 1), 0)
    lat_map = lambda i: (jnp.maximum(i - nc, 0), 0)
    lat_mod_row = lambda i: 1 + ((i - nc) * tm) // lat_seq
    return nc, nc + n_lat // tm, ctx_map, lat_map, lat_mod_row


def _inproj_kernel(xc_ref, xl_ref, mod_ref, n1_ref, win_f_ref, ws_ref, bs_ref, w_ref,
                   xy_ref, osgu_ref, w_b_ref, win_ref, *, tm, nc, lat_mod_row):
    i = pl.program_id(0)
    w_b_ref[...] = w_ref[...].astype(BF16)

    @pl.when(i == 0)
    def _():
        win_ref[...] = win_f_ref[...].astype(BF16)

    def body(x_ref, row):
        sh1 = _mod_row(mod_ref, 0, row)
        sc1 = _mod_row(mod_ref, 1, row)
        hb = (_rms(x_ref[...], n1_ref[...] * (1.0 + sc1)) + sh1).astype(BF16)
        uv = jnp.dot(hb, win_ref[:, 2 * LRU_WIDTH:], preferred_element_type=F32)
        xy_ref[...] = jnp.dot(hb, win_ref[:, 0:2 * LRU_WIDTH], preferred_element_type=F32)
        for g in range(SGU_GROUPS):
            lo = g * LANES
            gu = _gelu(uv[:, lo:lo + LANES])
            gv = _gelu(uv[:, SGU_WIDTH + lo:SGU_WIDTH + lo + LANES]).astype(BF16)
            bias = jnp.broadcast_to(bs_ref[:, g:g + 1], (CHUNK, LANES))
            for ck in range(tm // CHUNK):
                r0 = ck * CHUNK
                s = jnp.dot(ws_ref[g], gv[r0:r0 + CHUNK], preferred_element_type=F32) + bias
                osgu_ref[r0:r0 + CHUNK, lo:lo + LANES] = gu[r0:r0 + CHUNK] * s

    @pl.when(i < nc)
    def _():
        body(xc_ref, 0)

    @pl.when(i >= nc)
    def _():
        body(xl_ref, lat_mod_row(i))


def _cast_slice_spec(shape, grid):
    n_steps = math.prod(grid)
    rows = shape[0]
    r = BF16_SUBLANES
    while rows % r or rows // r > n_steps:
        r += BF16_SUBLANES
    last = rows // r - 1

    def index_map(*idx):
        step = 0
        for i, n in zip(idx, grid):
            step = step * n + i
        return (jnp.minimum(step, last), 0)

    return pl.BlockSpec((r, shape[1]), index_map)


def _inproj(xc, xl, mod, norm1, w_in, ws, bs_t, later_weight, *, tm, lat_seq):
    nc, n, ctx_map, lat_map, lat_mod_row = _token_groups(tm, xc.shape[0], xl.shape[0], lat_seq)
    m = n * tm
    const = lambda i: (0, 0)
    cast_specs = [_cast_slice_spec(later_weight.shape, (n,))]
    return pl.pallas_call(
        functools.partial(_inproj_kernel, tm=tm, nc=nc, lat_mod_row=lat_mod_row),
        grid=(n,),
        in_specs=[
            pl.BlockSpec((tm, D_MODEL), ctx_map),
            pl.BlockSpec((tm, D_MODEL), lat_map),
            pl.BlockSpec((N_MOD, MOD_ROWS, D_MODEL), lambda i: (0, 0, 0)),
            pl.BlockSpec((1, D_MODEL), const),
            pl.BlockSpec((D_MODEL, 4 * LRU_WIDTH), const, pipeline_mode=pl.Buffered(1)),
            pl.BlockSpec((SGU_GROUPS, CHUNK, CHUNK), lambda i: (0, 0, 0)),
            pl.BlockSpec((CHUNK, SGU_GROUPS), const),
        ] + cast_specs,
        out_specs=[
            pl.BlockSpec((tm, 2 * LRU_WIDTH), lambda i: (i, 0)),
            pl.BlockSpec((tm, SGU_WIDTH), lambda i: (i, 0)),
        ] + cast_specs,
        out_shape=[
            jax.ShapeDtypeStruct((m, 2 * LRU_WIDTH), F32),
            jax.ShapeDtypeStruct((m, SGU_WIDTH), F32),
        ] + [jax.ShapeDtypeStruct(later_weight.shape, BF16)],
        scratch_shapes=[pltpu.VMEM(w_in.shape, BF16)],
        compiler_params=pltpu.CompilerParams(
            dimension_semantics=("arbitrary",), vmem_limit_bytes=INPROJ_VMEM),
        name="inproj_sgu",
    )(xc, xl, mod, norm1, w_in, ws, bs_t, later_weight)


def _scan_pitch(seq):
    pitch = seq // SUBLANES
    return pitch + (4 - pitch % 8) % 8


def _lru_kernel(xr_ref, yr_ref, cw_ref, cb_ref, wg_ref, bg_ref, lam_ref, h0_ref, w_ref,
                o_ref, st_ref, w_b_ref, a_sc, b_sc, p_sc, h_sc, *, seq, nsb, nk, pitch, rc, unroll):
    sp_rows = SUBLANES * pitch
    n_rc = (seq * nsb) // rc
    seg = min(seq, rc)
    segs = rc // seg
    chunks_per_seq = seq // seg
    w_b_ref[...] = w_ref[...].astype(BF16)
    c4_all = (-0.5 * LRU_C) * jax.nn.softplus(-lam_ref[...])
    cw_all = cw_ref[...]
    cb_all = cb_ref[...]

    for s in range(nsb):
        lo, hi = s * sp_rows + seq, (s + 1) * sp_rows
        for slab in range(2 * nk):
            a_sc[slab, lo:hi, :] = jnp.ones((hi - lo, LANES), F32)
            b_sc[slab, lo:hi, :] = jnp.zeros((hi - lo, LANES), F32)

    def scratch_row(ci, k):
        if chunks_per_seq > 1:
            return pl.multiple_of((ci // chunks_per_seq) * sp_rows + (ci % chunks_per_seq) * seg,
                                  SUBLANES)
        return pl.multiple_of((ci * segs + k) * sp_rows, SUBLANES)

    def conv_segment(xs, prev, nxt, cw, cb):
        ext = jnp.concatenate([prev, xs, nxt], axis=0)
        n_ext = seg + 2 * SUBLANES
        xc = cb + cw[2:3] * xs
        xc = xc + cw[0:1] * pltpu.roll(ext, 2, axis=0)[SUBLANES:SUBLANES + seg]
        xc = xc + cw[1:2] * pltpu.roll(ext, 1, axis=0)[SUBLANES:SUBLANES + seg]
        return xc + cw[3:4] * pltpu.roll(ext, n_ext - 1, axis=0)[SUBLANES:SUBLANES + seg]

    def gates(ci, _):
        base = pl.multiple_of(ci * rc, rc)
        for k in range(nk):
            lanes = slice(k * LANES, (k + 1) * LANES)
            cw, cb, c4 = cw_all[:, lanes], cb_all[:, lanes], c4_all[:, lanes]
            xm = xr_ref[pl.ds(base, rc), lanes]
            if chunks_per_seq > 1:
                cs = ci % chunks_per_seq
                p0 = pl.multiple_of(jnp.maximum(base - SUBLANES, 0), SUBLANES)
                n0 = pl.multiple_of(jnp.minimum(base + rc, seq * nsb - SUBLANES), SUBLANES)
                prev = jnp.where(cs == 0, 0.0, xr_ref[pl.ds(p0, SUBLANES), lanes])
                nxt = jnp.where(cs == chunks_per_seq - 1, 0.0, xr_ref[pl.ds(n0, SUBLANES), lanes])
                xc = conv_segment(xm, prev, nxt, cw, cb)
            else:
                zeros = jnp.zeros((SUBLANES, LANES), F32)
                xc = jnp.concatenate(
                    [conv_segment(xm[j * seg:(j + 1) * seg], zeros, zeros, cw, cb) for j in range(segs)],
                    axis=0)
            g = jnp.dot(xc.astype(BF16), wg_ref[k], preferred_element_type=F32) + bg_ref[k]
            hxc = 0.5 * xc
            for d in range(2):
                th_r = jnp.tanh(g[:, (2 * d) * LANES:(2 * d + 1) * LANES])
                th_i = jnp.tanh(g[:, (2 * d + 1) * LANES:(2 * d + 2) * LANES])
                log_a = c4[d:d + 1] * th_r + c4[d:d + 1]
                a = jnp.exp(log_a)
                om = jnp.tanh(log_a) * (-1.0 - a * a)
                root = jnp.where(om > 0.0, om * lax.rsqrt(om), 0.0)
                b = root * ((th_i + 1.0) * hxc)
                for j in range(segs):
                    dst = scratch_row(ci, j)
                    a_sc[d * nk + k, pl.ds(dst, seg), :] = a[j * seg:(j + 1) * seg]
                    b_sc[d * nk + k, pl.ds(dst, seg), :] = b[j * seg:(j + 1) * seg]
        return 0

    lax.fori_loop(0, n_rc, gates, 0)

    def scan_sequence(s, _):
        off = s * sp_rows

        def rows(t):
            return pl.ds(off + t, SUBLANES, stride=pitch)

        def local(t, carry):
            out = []
            for slab, (h, p) in enumerate(carry):
                tt = t if slab < nk else pitch - 1 - t
                a = a_sc[slab, rows(tt), :]
                h = a * h + b_sc[slab, rows(tt), :]
                p = a * p
                h_sc[slab, rows(tt), :] = h
                p_sc[slab, rows(tt), :] = p
                out.append((h, p))
            return tuple(out)

        zero = jnp.zeros((SUBLANES, LANES), F32)
        one = jnp.ones((SUBLANES, LANES), F32)
        ends = lax.fori_loop(0, pitch, local, ((zero, one),) * (2 * nk), unroll=unroll)

        h0 = h0_ref[s]
        starts = []
        finals = [[None] * nk, [None] * nk]
        for slab, (h, p) in enumerate(ends):
            d, k = divmod(slab, nk)
            c = h0[d:d + 1, k * LANES:(k + 1) * LANES]
            cs = [None] * SUBLANES
            for j in (range(SUBLANES) if d == 0 else reversed(range(SUBLANES))):
                cs[j] = c
                c = h[j:j + 1] + p[j:j + 1] * c
            finals[d][k] = c
            starts.append(jnp.concatenate(cs, axis=0))
        st_ref[s] = jnp.concatenate(
            [jnp.concatenate(finals[0], axis=1), jnp.concatenate(finals[1], axis=1)], axis=0)

        def fix(t, _):
            for k in range(nk):
                a_sc[k, rows(t), :] = (h_sc[k, rows(t), :] + p_sc[k, rows(t), :] * starts[k]
                                       + h_sc[nk + k, rows(t), :] + p_sc[nk + k, rows(t), :] * starts[nk + k])
            return 0

        lax.fori_loop(0, pitch, fix, 0, unroll=unroll)
        return 0

    if nsb == 1:
        scan_sequence(0, 0)
    else:
        lax.fori_loop(0, nsb, scan_sequence, 0)

    def gate_out(ci, _):
        base = pl.multiple_of(ci * rc, rc)
        for k in range(nk):
            lanes = slice(k * LANES, (k + 1) * LANES)
            gy = _gelu(yr_ref[pl.ds(base, rc), lanes])
            for j in range(segs):
                src = scratch_row(ci, j)
                o_ref[pl.ds(base + j * seg, seg), lanes] = (gy[j * seg:(j + 1) * seg]
                                                            * a_sc[k, pl.ds(src, seg), :])
        return 0

    lax.fori_loop(0, n_rc, gate_out, 0)


def _lru(xy, conv_w, conv_b, wg, bg, lam, h0, later_weight, *, row_start, m, seq, nsb):
    nblk = m // (seq * nsb)
    assert row_start % (seq * nsb) == 0
    blk0 = row_start // (seq * nsb)
    nk = LRU_LANE_BLOCKS
    width = nk * LANES
    ncb = LRU_WIDTH // width
    pitch = _scan_pitch(seq)
    sc_rows = nsb * SUBLANES * pitch
    rows = seq * nsb
    cast_spec = _cast_slice_spec(later_weight.shape, (ncb, nblk))
    return pl.pallas_call(
        functools.partial(_lru_kernel, seq=seq, nsb=nsb, nk=nk, pitch=pitch,
                          rc=LRU_ROW_CHUNK, unroll=LRU_SCAN_UNROLL),
        grid=(ncb, nblk),
        in_specs=[
            pl.BlockSpec((rows, width), lambda j, b: (blk0 + b, j)),
            pl.BlockSpec((rows, width), lambda j, b: (blk0 + b, ncb + j)),
            pl.BlockSpec((4, width), lambda j, b: (0, j)),
            pl.BlockSpec((1, width), lambda j, b: (0, j)),
            pl.BlockSpec((nk, LANES, 4 * LANES), lambda j, b: (j, 0, 0)),
            pl.BlockSpec((nk, 1, 4 * LANES), lambda j, b: (j, 0, 0)),
            pl.BlockSpec((2, width), lambda j, b: (0, j)),
            pl.BlockSpec((nsb, 2, width), lambda j, b: (b, 0, j)),
            cast_spec,
        ],
        out_specs=[
            pl.BlockSpec((rows, width), lambda j, b: (b, j)),
            pl.BlockSpec((nsb, 2, width), lambda j, b: (b, 0, j)),
            cast_spec,
        ],
        out_shape=[
            jax.ShapeDtypeStruct((m, LRU_WIDTH), F32),
            jax.ShapeDtypeStruct((m // seq, 2, LRU_WIDTH), F32),
            jax.ShapeDtypeStruct(later_weight.shape, BF16),
        ],
        scratch_shapes=[pltpu.VMEM((2 * nk, sc_rows, LANES), F32)] * 4,
        compiler_params=pltpu.CompilerParams(
            dimension_semantics=("arbitrary", "arbitrary"), vmem_limit_bytes=LRU_VMEM),
        name="rglru",
    )(xy, xy, conv_w, conv_b, wg, bg, lam, h0, later_weight)


GROUP = SUBLANES * SUBLANES


def _swap_rows(slab_ref, val, r0):
    n = val.shape[0]
    for k in range(D_MODEL // LANES):
        slab_ref[k, r0:r0 + n, :] = val[:, k * LANES:(k + 1) * LANES]
    cols = []
    for k in range(D_MODEL // LANES):
        rows = [slab_ref[k, pl.ds(r0 + g * GROUP + t, SUBLANES, stride=SUBLANES), :]
                for g in range(n // GROUP) for t in range(SUBLANES)]
        cols.append(jnp.concatenate(rows, axis=0))
    return jnp.concatenate(cols, axis=1)


def _ffn_tile(x_ref, ol_ref, os_ref, mod_ref, gl_ref, gs_ref, wout_ref, n2_ref, up_ref,
              cw_ref, cb_ref, down_ref, fn_ref, y_ref, slab_sc, *, tm, fc, es, period, row):
    g1 = _mod_row(mod_ref, 2, row)
    sh2 = _mod_row(mod_ref, 3, row)
    sc2 = _mod_row(mod_ref, 4, row)
    g2 = _mod_row(mod_ref, 5, row)
    nl = _rms(ol_ref[...], gl_ref[...]).astype(BF16)
    ns = _rms(os_ref[...], gs_ref[...]).astype(BF16)
    o = (jnp.dot(nl, wout_ref[0:LRU_WIDTH, :], preferred_element_type=F32)
         + jnp.dot(ns, wout_ref[LRU_WIDTH:LRU_WIDTH + SGU_WIDTH, :], preferred_element_type=F32))
    x1 = x_ref[...] + g1 * o
    h2f = _rms(x1, n2_ref[...] * (1.0 + sc2)) + sh2
    h2 = _swap_rows(slab_sc, h2f, 0).astype(BF16)

    sub = lax.broadcasted_iota(jnp.int32, (SUBLANES, fc), 0)
    groups_per_period = period // GROUP
    n_groups = tm // GROUP

    def conv(z, c0):
        w = cw_ref[:, c0:c0 + fc]
        b = cb_ref[:, c0:c0 + fc]
        w0, w1, w2 = w[0:1], w[1:2], w[2:3]
        z8 = [[z[g * GROUP + t * SUBLANES:g * GROUP + (t + 1) * SUBLANES] for t in range(SUBLANES)]
              for g in range(n_groups)]
        outs = []
        for g in range(n_groups):
            cur = z8[g]
            lo = pltpu.roll(cur[SUBLANES - 1], 1, axis=0)
            if g % groups_per_period == 0:
                lo = jnp.where(sub == 0, 0.0, lo)
            else:
                lo = jnp.where(sub == 0, pltpu.roll(z8[g - 1][SUBLANES - 1], 1, axis=0), lo)
            hi = pltpu.roll(cur[0], SUBLANES - 1, axis=0)
            if g % groups_per_period == groups_per_period - 1:
                hi = jnp.where(sub == SUBLANES - 1, 0.0, hi)
            else:
                hi = jnp.where(sub == SUBLANES - 1, pltpu.roll(z8[g + 1][0], SUBLANES - 1, axis=0), hi)
            zm = [lo] + cur[:SUBLANES - 1]
            zp = cur[1:] + [hi]
            for t in range(SUBLANES):
                outs.append(b + w0 * zm[t] + w1 * cur[t] + w2 * zp[t])
        return jnp.concatenate(outs, axis=0)

    def up(f0):
        zg = jnp.dot(h2, up_ref[:, f0:f0 + fc], preferred_element_type=F32)
        zv = jnp.dot(h2, up_ref[:, D_FF + f0:D_FF + f0 + fc], preferred_element_type=F32)
        return zg, zv

    acc = jnp.zeros((tm, D_MODEL), F32)
    nxt = up(0)
    for f0 in range(0, D_FF, fc):
        zg, zv = nxt
        if f0 + fc < D_FF:
            nxt = up(f0 + fc)
        hg = 0.5 * conv(zg, f0)
        act = (hg * (jnp.tanh(hg) + 1.0) * conv(zv, D_FF + f0)).astype(BF16)
        if f0 + fc < D_FF:
            acc = acc + jnp.dot(act, down_ref[f0:f0 + fc, :], preferred_element_type=F32)
        else:
            for r0 in range(0, tm, es):
                a = acc[r0:r0 + es] + jnp.dot(act[r0:r0 + es], down_ref[f0:f0 + fc, :],
                                              preferred_element_type=F32)
                x2 = x1[r0:r0 + es] + g2 * _swap_rows(slab_sc, a, r0)
                y_ref[r0:r0 + es, :] = _rms(x2, fn_ref[...])


def _ffn_kernel(xc_ref, xl_ref, olc_ref, oll_ref, os_ref, mod_ref, gl_ref, gs_ref, wout_ref, n2_ref,
                up_ref, cw_ref, cb_ref, down_ref, fn_ref, yc_ref, yl_ref, slab_sc, *,
                nc, lat_mod_row, ctx_period, lat_period, **tile):
    i = pl.program_id(0)
    shared = (mod_ref, gl_ref, gs_ref, wout_ref, n2_ref, up_ref, cw_ref, cb_ref, down_ref, fn_ref)

    @pl.when(i < nc)
    def _():
        _ffn_tile(xc_ref, olc_ref, os_ref, *shared, yc_ref, slab_sc, period=ctx_period, row=0, **tile)

    @pl.when(i >= nc)
    def _():
        _ffn_tile(xl_ref, oll_ref, os_ref, *shared, yl_ref, slab_sc, period=lat_period,
                  row=lat_mod_row(i), **tile)


def _ffn(xc, xl, olru_c, olru_l, osgu, mod, g_lru, g_sgu, w_out, norm2, ffn_up, conv_w, conv_b,
         ffn_down, final_norm, *, tm, ctx_period, lat_period, lat_seq):
    nc, n, ctx_map, lat_map, lat_mod_row = _token_groups(tm, xc.shape[0], xl.shape[0], lat_seq)
    const = lambda i: (0, 0)
    resident = lambda shape: pl.BlockSpec(shape, const, pipeline_mode=pl.Buffered(1))
    return pl.pallas_call(
        functools.partial(_ffn_kernel, nc=nc, lat_mod_row=lat_mod_row, ctx_period=ctx_period,
                          lat_period=lat_period, tm=tm, fc=FFN_FC, es=FFN_ES),
        grid=(n,),
        in_specs=[
            pl.BlockSpec((tm, D_MODEL), ctx_map),
            pl.BlockSpec((tm, D_MODEL), lat_map),
            pl.BlockSpec((tm, LRU_WIDTH), ctx_map),
            pl.BlockSpec((tm, LRU_WIDTH), lat_map),
            pl.BlockSpec((tm, SGU_WIDTH), lambda i: (i, 0)),
            pl.BlockSpec((N_MOD, MOD_ROWS, D_MODEL), lambda i: (0, 0, 0)),
            pl.BlockSpec((1, LRU_WIDTH), const),
            pl.BlockSpec((1, SGU_WIDTH), const),
            resident((LRU_WIDTH + SGU_WIDTH, D_MODEL)),
            pl.BlockSpec((1, D_MODEL), const),
            resident((D_MODEL, 2 * D_FF)),
            pl.BlockSpec((3, 2 * D_FF), const),
            pl.BlockSpec((1, 2 * D_FF), const),
            resident((D_FF, D_MODEL)),
            pl.BlockSpec((1, D_MODEL), const),
        ],
        out_specs=[pl.BlockSpec((tm, D_MODEL), ctx_map), pl.BlockSpec((tm, D_MODEL), lat_map)],
        out_shape=[jax.ShapeDtypeStruct(xc.shape, F32), jax.ShapeDtypeStruct(xl.shape, F32)],
        scratch_shapes=[pltpu.VMEM((D_MODEL // LANES, tm, LANES), F32)],
        compiler_params=pltpu.CompilerParams(
            dimension_semantics=("arbitrary",), vmem_limit_bytes=FFN_VMEM),
        name="outproj_ffn",
    )(xc, xl, olru_c, olru_l, osgu, mod, g_lru, g_sgu, w_out, norm2, ffn_up, conv_w, conv_b, ffn_down,
      final_norm)


def _gate_params(wa, ba, wx, bx):
    nlb = LRU_HEADS // 2
    w = jnp.stack([wa[0], wx[0], wa[1], wx[1]], axis=0)
    wp = w.reshape(4, nlb, 2, LRU_HEAD_DIM, LRU_HEAD_DIM)
    pair = 0.5 * jnp.eye(2, dtype=F32)
    bd = wp[:, :, :, :, None, :] * pair[None, None, :, None, :, None]
    bd = bd.reshape(4, nlb, LANES, LANES)
    wg = jnp.transpose(bd, (1, 2, 0, 3)).reshape(nlb, LANES, 4 * LANES).astype(BF16)
    b = 0.5 * jnp.stack([ba[0], bx[0], ba[1], bx[1]], axis=0)
    bg = jnp.transpose(b.reshape(4, nlb, LANES), (1, 0, 2)).reshape(nlb, 1, 4 * LANES)
    return wg, bg


def kernel(x_prompt, x_sample, state_lru, c, c_ctx, norm1, norm2, w_ada, b_ada, w_in, lru_conv_w,
           lru_conv_b, lru_wa, lru_ba, lru_wx, lru_bx, lru_lam, sgu_ws, sgu_bs, g_lru, g_sgu, w_out,
           ffn_up, ffn_conv_w, ffn_conv_b, ffn_down, final_norm):
    batch, seq, _ = x_prompt.shape
    dec_batch, dec_seq, _ = x_sample.shape
    depth = norm1.shape[0]
    assert depth == 1, "the final norm is fused into the (single) layer's last kernel"
    l = 0

    xc = x_prompt.reshape(batch * seq, D_MODEL)
    xl = x_sample.reshape(dec_batch * dec_seq, D_MODEL)
    n_ctx, n_lat = xc.shape[0], xl.shape[0]
    cc = jnp.concatenate(
        [c_ctx[None, :], c, jnp.zeros((MOD_ROWS - 1 - dec_batch, D_MODEL), F32)], axis=0)
    zeros_state = jnp.zeros((batch, 2, LRU_WIDTH), x_prompt.dtype)

    mod = _modulation(cc, w_ada[l], b_ada[l][None, :])
    wg, bg = _gate_params(lru_wa[l], lru_ba[l], lru_wx[l], lru_bx[l])
    lru_params = (lru_conv_w[l], lru_conv_b[l][None, :], wg, bg, lru_lam[l])
    h0_lat = state_lru.reshape(dec_batch, 2, LRU_WIDTH)

    xy, osgu, w_out_b = _inproj(
        xc, xl, mod, norm1[l][None, :], w_in[l], sgu_ws[l].astype(BF16), sgu_bs[l].T,
        w_out[l], tm=INPROJ_TM, lat_seq=dec_seq)
    olru_c, new_state, down_b = _lru(xy, *lru_params, zeros_state, ffn_down[l], row_start=0, m=n_ctx,
                                     seq=seq, nsb=LRU_CTX_SEQS)
    olru_l, _, up_b = _lru(xy, *lru_params, h0_lat, ffn_up[l], row_start=n_ctx, m=n_lat,
                           seq=dec_seq, nsb=1)
    yc, yl = _ffn(xc, xl, olru_c, olru_l, osgu, mod, g_lru[l][None, :], g_sgu[l][None, :],
                  w_out_b, norm2[l][None, :], up_b, ffn_conv_w[l],
                  ffn_conv_b[l][None, :], down_b, final_norm[None, :], tm=FFN_TM,
                  ctx_period=seq, lat_period=GRID_W, lat_seq=dec_seq)
    return (yc.reshape(batch, seq, D_MODEL), yl.reshape(dec_batch, dec_seq, D_MODEL),
            new_state[:, None])
```

```python
import functools
import math

import jax
import jax.numpy as jnp
from jax import lax
from jax.experimental import pallas as pl
from jax.experimental.pallas import tpu as pltpu

D_MODEL = 1024
LRU_HEADS = 8
LRU_WIDTH = 512
LRU_HEAD_DIM = 64
LRU_C = 8.0
SGU_GROUPS = 4
SGU_WIDTH = 512
CHUNK = 128
D_FF = 3072
N_MOD = 6
EPS = 1e-6
GRID_W = 64

LANES = 128
SUBLANES = 8
BF16_SUBLANES = 16
MOD_ROWS = 8
BF16 = jnp.bfloat16
F32 = jnp.float32

MIB = 1024 * 1024
INPROJ_TM = 1024
INPROJ_VMEM = 56 * MIB
LRU_ROW_CHUNK = 512
LRU_LANE_BLOCKS = 2
LRU_CTX_SEQS = 8
LRU_SCAN_UNROLL = True
LRU_VMEM = 56 * MIB
FFN_TM = 512
FFN_FC = 1536
FFN_ES = 256
FFN_VMEM = 60 * MIB

_GELU_C0 = math.sqrt(2.0 / math.pi)
_GELU_C1 = _GELU_C0 * 0.044715


def _rms(x, gain):
    return x * lax.rsqrt(jnp.mean(x * x, axis=-1, keepdims=True) + EPS) * gain


def _gelu(x):
    return (0.5 * x) * (1.0 + jnp.tanh(x * (_GELU_C0 + _GELU_C1 * (x * x))))


def _mod_row(mod_ref, k, row):
    return mod_ref[k, pl.ds(row, 1), :]


def _mod_kernel(cctx_ref, c_ref, w_ref, b_ref, o_ref):
    pad = jnp.zeros((MOD_ROWS - 1 - c_ref.shape[0], D_MODEL), F32)
    cc = jnp.concatenate([cctx_ref[...], c_ref[...], pad], axis=0)
    s = jax.nn.silu(cc).astype(BF16)
    o_ref[...] = jnp.dot(s, w_ref[...].astype(BF16), preferred_element_type=F32) + b_ref[...]


def _modulation(c_ctx, c, w_ada, b_ada):
    assert 1 + c.shape[0] <= MOD_ROWS
    return pl.pallas_call(
        _mod_kernel,
        grid=(N_MOD,),
        in_specs=[
            pl.BlockSpec((1, D_MODEL), lambda j: (0, 0)),
            pl.BlockSpec(c.shape, lambda j: (0, 0)),
            pl.BlockSpec((D_MODEL, D_MODEL), lambda j: (0, j)),
            pl.BlockSpec((1, D_MODEL), lambda j: (0, j)),
        ],
        out_specs=pl.BlockSpec((None, MOD_ROWS, D_MODEL), lambda j: (j, 0, 0)),
        out_shape=jax.ShapeDtypeStruct((N_MOD, MOD_ROWS, D_MODEL), F32),
        compiler_params=pltpu.CompilerParams(dimension_semantics=("arbitrary",)),
        name="modulation",
    )(c_ctx, c, w_ada, b_ada)


def _token_groups(tm, n_ctx, n_lat, lat_seq):
    assert n_ctx % tm == 0 and n_lat % tm == 0 and lat_seq % tm == 0
    nc = n_ctx // tm
    ctx_map = lambda i: (jnp.minimum(i, nc - 1), 0)
    lat_map = lambda i: (jnp.maximum(i - nc, 0), 0)
    lat_mod_row = lambda i: 1 + ((i - nc) * tm) // lat_seq
    return nc, nc + n_lat // tm, ctx_map, lat_map, lat_mod_row


def _inproj_kernel(xc_ref, xl_ref, mod_ref, n1_ref, win_f_ref, ws_ref, bs_ref, w_ref,
                   xy_ref, osgu_ref, w_b_ref, win_ref, *, tm, nc, lat_mod_row):
    i = pl.program_id(0)
    w_b_ref[...] = w_ref[...].astype(BF16)

    @pl.when(i == 0)
    def _():
        win_ref[...] = win_f_ref[...].astype(BF16)

    def body(x_ref, row):
        sh1 = _mod_row(mod_ref, 0, row)
        sc1 = _mod_row(mod_ref, 1, row)
        hb = (_rms(x_ref[...], n1_ref[...] * (1.0 + sc1)) + sh1).astype(BF16)
        uv = jnp.dot(hb, win_ref[:, 2 * LRU_WIDTH:], preferred_element_type=F32)
        xy_ref[...] = jnp.dot(hb, win_ref[:, 0:2 * LRU_WIDTH], preferred_element_type=F32)
        for g in range(SGU_GROUPS):
            lo = g * LANES
            gu = _gelu(uv[:, lo:lo + LANES])
            gv = _gelu(uv[:, SGU_WIDTH + lo:SGU_WIDTH + lo + LANES]).astype(BF16)
            bias = jnp.broadcast_to(bs_ref[:, g:g + 1], (CHUNK, LANES))
            for ck in range(tm // CHUNK):
                r0 = ck * CHUNK
                s = jnp.dot(ws_ref[g], gv[r0:r0 + CHUNK], preferred_element_type=F32) + bias
                osgu_ref[r0:r0 + CHUNK, lo:lo + LANES] = gu[r0:r0 + CHUNK] * s

    @pl.when(i < nc)
    def _():
        body(xc_ref, 0)

    @pl.when(i >= nc)
    def _():
        body(xl_ref, lat_mod_row(i))


def _cast_slice_spec(shape, grid):
    n_steps = math.prod(grid)
    rows = shape[0]
    r = BF16_SUBLANES
    while rows % r or rows // r > n_steps:
        r += BF16_SUBLANES
    last = rows // r - 1

    def index_map(*idx):
        step = 0
        for i, n in zip(idx, grid):
            step = step * n + i
        return (jnp.minimum(step, last), 0)

    return pl.BlockSpec((r, shape[1]), index_map)


def _inproj(xc, xl, mod, norm1, w_in, ws, bs_t, later_weight, *, tm, lat_seq):
    nc, n, ctx_map, lat_map, lat_mod_row = _token_groups(tm, xc.shape[0], xl.shape[0], lat_seq)
    m = n * tm
    const = lambda i: (0, 0)
    cast_specs = [_cast_slice_spec(later_weight.shape, (n,))]
    return pl.pallas_call(
        functools.partial(_inproj_kernel, tm=tm, nc=nc, lat_mod_row=lat_mod_row),
        grid=(n,),
        in_specs=[
            pl.BlockSpec((tm, D_MODEL), ctx_map),
            pl.BlockSpec((tm, D_MODEL), lat_map),
            pl.BlockSpec((N_MOD, MOD_ROWS, D_MODEL), lambda i: (0, 0, 0)),
            pl.BlockSpec((1, D_MODEL), const),
            pl.BlockSpec((D_MODEL, 4 * LRU_WIDTH), const, pipeline_mode=pl.Buffered(1)),
            pl.BlockSpec((SGU_GROUPS, CHUNK, CHUNK), lambda i: (0, 0, 0)),
            pl.BlockSpec((CHUNK, SGU_GROUPS), const),
        ] + cast_specs,
        out_specs=[
            pl.BlockSpec((tm, 2 * LRU_WIDTH), lambda i: (i, 0)),
            pl.BlockSpec((tm, SGU_WIDTH), lambda i: (i, 0)),
        ] + cast_specs,
        out_shape=[
            jax.ShapeDtypeStruct((m, 2 * LRU_WIDTH), F32),
            jax.ShapeDtypeStruct((m, SGU_WIDTH), F32),
        ] + [jax.ShapeDtypeStruct(later_weight.shape, BF16)],
        scratch_shapes=[pltpu.VMEM(w_in.shape, BF16)],
        compiler_params=pltpu.CompilerParams(
            dimension_semantics=("arbitrary",), vmem_limit_bytes=INPROJ_VMEM),
        name="inproj_sgu",
    )(xc, xl, mod, norm1, w_in, ws, bs_t, later_weight)


def _scan_pitch(seq):
    pitch = seq // SUBLANES
    return pitch + (4 - pitch % 8) % 8


def _lru_kernel(xr_ref, yr_ref, cw_ref, cb_ref, wg_ref, bg_ref, lam_ref, h0_ref, w_ref,
                o_ref, st_ref, w_b_ref, a_sc, b_sc, p_sc, h_sc, *, seq, nsb, nk, pitch, rc, unroll):
    sp_rows = SUBLANES * pitch
    n_rc = (seq * nsb) // rc
    seg = min(seq, rc)
    segs = rc // seg
    chunks_per_seq = seq // seg
    w_b_ref[...] = w_ref[...].astype(BF16)
    c4_all = (-0.5 * LRU_C) * jax.nn.softplus(-lam_ref[...])
    cw_all = cw_ref[...]
    cb_all = cb_ref[...]

    for s in range(nsb):
        lo, hi = s * sp_rows + seq, (s + 1) * sp_rows
        for slab in range(2 * nk):
            a_sc[slab, lo:hi, :] = jnp.ones((hi - lo, LANES), F32)
            b_sc[slab, lo:hi, :] = jnp.zeros((hi - lo, LANES), F32)

    def scratch_row(ci, k):
        if chunks_per_seq > 1:
            return pl.multiple_of((ci // chunks_per_seq) * sp_rows + (ci % chunks_per_seq) * seg,
                                  SUBLANES)
        return pl.multiple_of((ci * segs + k) * sp_rows, SUBLANES)

    def conv_segment(xs, prev, nxt, cw, cb):
        ext = jnp.concatenate([prev, xs, nxt], axis=0)
        n_ext = seg + 2 * SUBLANES
        xc = cb + cw[2:3] * xs
        xc = xc + cw[0:1] * pltpu.roll(ext, 2, axis=0)[SUBLANES:SUBLANES + seg]
        xc = xc + cw[1:2] * pltpu.roll(ext, 1, axis=0)[SUBLANES:SUBLANES + seg]
        return xc + cw[3:4] * pltpu.roll(ext, n_ext - 1, axis=0)[SUBLANES:SUBLANES + seg]

    def gates(ci, _):
        base = pl.multiple_of(ci * rc, rc)
        for k in range(nk):
            lanes = slice(k * LANES, (k + 1) * LANES)
            cw, cb, c4 = cw_all[:, lanes], cb_all[:, lanes], c4_all[:, lanes]
            xm = xr_ref[pl.ds(base, rc), lanes]
            if chunks_per_seq > 1:
                cs = ci % chunks_per_seq
                p0 = pl.multiple_of(jnp.maximum(base - SUBLANES, 0), SUBLANES)
                n0 = pl.multiple_of(jnp.minimum(base + rc, seq * nsb - SUBLANES), SUBLANES)
                prev = jnp.where(cs == 0, 0.0, xr_ref[pl.ds(p0, SUBLANES), lanes])
                nxt = jnp.where(cs == chunks_per_seq - 1, 0.0, xr_ref[pl.ds(n0, SUBLANES), lanes])
                xc = conv_segment(xm, prev, nxt, cw, cb)
            else:
                zeros = jnp.zeros((SUBLANES, LANES), F32)
                xc = jnp.concatenate(
                    [conv_segment(xm[j * seg:(j + 1) * seg], zeros, zeros, cw, cb) for j in range(segs)],
                    axis=0)
            g = jnp.dot(xc.astype(BF16), wg_ref[k], preferred_element_type=F32) + bg_ref[k]
            hxc = 0.5 * xc
            for d in range(2):
                th_r = jnp.tanh(g[:, (2 * d) * LANES:(2 * d + 1) * LANES])
                th_i = jnp.tanh(g[:, (2 * d + 1) * LANES:(2 * d + 2) * LANES])
                log_a = c4[d:d + 1] * th_r + c4[d:d + 1]
                a = jnp.exp(log_a)
                om = jnp.tanh(log_a) * (-1.0 - a * a)
                root = jnp.where(om > 0.0, om * lax.rsqrt(om), 0.0)
                b = root * ((th_i + 1.0) * hxc)
                for j in range(segs):
                    dst = scratch_row(ci, j)
                    a_sc[d * nk + k, pl.ds(dst, seg), :] = a[j * seg:(j + 1) * seg]
                    b_sc[d * nk + k, pl.ds(dst, seg), :] = b[j * seg:(j + 1) * seg]
        return 0

    lax.fori_loop(0, n_rc, gates, 0)

    def scan_sequence(s, _):
        off = s * sp_rows

        def rows(t):
            return pl.ds(off + t, SUBLANES, stride=pitch)

        def local(t, carry):
            out = []
            for slab, (h, p) in enumerate(carry):
                tt = t if slab < nk else pitch - 1 - t
                a = a_sc[slab, rows(tt), :]
                h = a * h + b_sc[slab, rows(tt), :]
                p = a * p
                h_sc[slab, rows(tt), :] = h
                p_sc[slab, rows(tt), :] = p
                out.append((h, p))
            return tuple(out)

        zero = jnp.zeros((SUBLANES, LANES), F32)
        one = jnp.ones((SUBLANES, LANES), F32)
        ends = lax.fori_loop(0, pitch, local, ((zero, one),) * (2 * nk), unroll=unroll)

        h0 = h0_ref[s]
        starts = []
        finals = [[None] * nk, [None] * nk]
        for slab, (h, p) in enumerate(ends):
            d, k = divmod(slab, nk)
            c = h0[d:d + 1, k * LANES:(k + 1) * LANES]
            cs = [None] * SUBLANES
            for j in (range(SUBLANES) if d == 0 else reversed(range(SUBLANES))):
                cs[j] = c
                c = h[j:j + 1] + p[j:j + 1] * c
            finals[d][k] = c
            starts.append(jnp.concatenate(cs, axis=0))
        st_ref[s] = jnp.concatenate(
            [jnp.concatenate(finals[0], axis=1), jnp.concatenate(finals[1], axis=1)], axis=0)

        def fix(t, _):
            for k in range(nk):
                a_sc[k, rows(t), :] = (h_sc[k, rows(t), :] + p_sc[k, rows(t), :] * starts[k]
                                       + h_sc[nk + k, rows(t), :] + p_sc[nk + k, rows(t), :] * starts[nk + k])
            return 0

        lax.fori_loop(0, pitch, fix, 0, unroll=unroll)
        return 0

    if nsb == 1:
        scan_sequence(0, 0)
    else:
        lax.fori_loop(0, nsb, scan_sequence, 0)

    def gate_out(ci, _):
        base = pl.multiple_of(ci * rc, rc)
        for k in range(nk):
            lanes = slice(k * LANES, (k + 1) * LANES)
            gy = _gelu(yr_ref[pl.ds(base, rc), lanes])
            for j in range(segs):
                src = scratch_row(ci, j)
                o_ref[pl.ds(base + j * seg, seg), lanes] = (gy[j * seg:(j + 1) * seg]
                                                            * a_sc[k, pl.ds(src, seg), :])
        return 0

    lax.fori_loop(0, n_rc, gate_out, 0)


def _lru(xy, conv_w, conv_b, wg, bg, lam, h0, later_weight, *, row_start, m, seq, nsb):
    nblk = m // (seq * nsb)
    assert row_start % (seq * nsb) == 0
    blk0 = row_start // (seq * nsb)
    nk = LRU_LANE_BLOCKS
    width = nk * LANES
    ncb = LRU_WIDTH // width
    pitch = _scan_pitch(seq)
    sc_rows = nsb * SUBLANES * pitch
    rows = seq * nsb
    cast_spec = _cast_slice_spec(later_weight.shape, (ncb, nblk))
    return pl.pallas_call(
        functools.partial(_lru_kernel, seq=seq, nsb=nsb, nk=nk, pitch=pitch,
                          rc=LRU_ROW_CHUNK, unroll=LRU_SCAN_UNROLL),
        grid=(ncb, nblk),
        in_specs=[
            pl.BlockSpec((rows, width), lambda j, b: (blk0 + b, j)),
            pl.BlockSpec((rows, width), lambda j, b: (blk0 + b, ncb + j)),
            pl.BlockSpec((4, width), lambda j, b: (0, j)),
            pl.BlockSpec((1, width), lambda j, b: (0, j)),
            pl.BlockSpec((nk, LANES, 4 * LANES), lambda j, b: (j, 0, 0)),
            pl.BlockSpec((nk, 1, 4 * LANES), lambda j, b: (j, 0, 0)),
            pl.BlockSpec((2, width), lambda j, b: (0, j)),
            pl.BlockSpec((nsb, 2, width), lambda j, b: (b, 0, j)),
            cast_spec,
        ],
        out_specs=[
            pl.BlockSpec((rows, width), lambda j, b: (b, j)),
            pl.BlockSpec((nsb, 2, width), lambda j, b: (b, 0, j)),
            cast_spec,
        ],
        out_shape=[
            jax.ShapeDtypeStruct((m, LRU_WIDTH), F32),
            jax.ShapeDtypeStruct((m // seq, 2, LRU_WIDTH), F32),
            jax.ShapeDtypeStruct(later_weight.shape, BF16),
        ],
        scratch_shapes=[pltpu.VMEM((2 * nk, sc_rows, LANES), F32)] * 4,
        compiler_params=pltpu.CompilerParams(
            dimension_semantics=("arbitrary", "arbitrary"), vmem_limit_bytes=LRU_VMEM),
        name="rglru",
    )(xy, xy, conv_w, conv_b, wg, bg, lam, h0, later_weight)


GROUP = SUBLANES * SUBLANES


def _swap_rows(slab_ref, val, r0):
    n = val.shape[0]
    for k in range(D_MODEL // LANES):
        slab_ref[k, r0:r0 + n, :] = val[:, k * LANES:(k + 1) * LANES]
    cols = []
    for k in range(D_MODEL // LANES):
        rows = [slab_ref[k, pl.ds(r0 + g * GROUP + t, SUBLANES, stride=SUBLANES), :]
                for g in range(n // GROUP) for t in range(SUBLANES)]
        cols.append(jnp.concatenate(rows, axis=0))
    return jnp.concatenate(cols, axis=1)


def _ffn_tile(x_ref, ol_ref, os_ref, mod_ref, gl_ref, gs_ref, wout_ref, n2_ref, up_ref,
              cw_ref, cb_ref, down_ref, fn_ref, y_ref, slab_sc, *, tm, fc, es, period, row):
    g1 = _mod_row(mod_ref, 2, row)
    sh2 = _mod_row(mod_ref, 3, row)
    sc2 = _mod_row(mod_ref, 4, row)
    g2 = _mod_row(mod_ref, 5, row)
    nl = _rms(ol_ref[...], gl_ref[...]).astype(BF16)
    ns = _rms(os_ref[...], gs_ref[...]).astype(BF16)
    o = (jnp.dot(nl, wout_ref[0:LRU_WIDTH, :], preferred_element_type=F32)
         + jnp.dot(ns, wout_ref[LRU_WIDTH:LRU_WIDTH + SGU_WIDTH, :], preferred_element_type=F32))
    x1 = x_ref[...] + g1 * o
    h2f = _rms(x1, n2_ref[...] * (1.0 + sc2)) + sh2
    h2 = _swap_rows(slab_sc, h2f, 0).astype(BF16)

    sub = lax.broadcasted_iota(jnp.int32, (SUBLANES, fc), 0)
    groups_per_period = period // GROUP
    n_groups = tm // GROUP

    def conv(z, c0):
        w = cw_ref[:, c0:c0 + fc]
        b = cb_ref[:, c0:c0 + fc]
        w0, w1, w2 = w[0:1], w[1:2], w[2:3]
        z8 = [[z[g * GROUP + t * SUBLANES:g * GROUP + (t + 1) * SUBLANES] for t in range(SUBLANES)]
              for g in range(n_groups)]
        outs = []
        for g in range(n_groups):
            cur = z8[g]
            lo = pltpu.roll(cur[SUBLANES - 1], 1, axis=0)
            if g % groups_per_period == 0:
                lo = jnp.where(sub == 0, 0.0, lo)
            else:
                lo = jnp.where(sub == 0, pltpu.roll(z8[g - 1][SUBLANES - 1], 1, axis=0), lo)
            hi = pltpu.roll(cur[0], SUBLANES - 1, axis=0)
            if g % groups_per_period == groups_per_period - 1:
                hi = jnp.where(sub == SUBLANES - 1, 0.0, hi)
            else:
                hi = jnp.where(sub == SUBLANES - 1, pltpu.roll(z8[g + 1][0], SUBLANES - 1, axis=0), hi)
            zm = [lo] + cur[:SUBLANES - 1]
            zp = cur[1:] + [hi]
            for t in range(SUBLANES):
                outs.append(b + w0 * zm[t] + w1 * cur[t] + w2 * zp[t])
        return jnp.concatenate(outs, axis=0)

    def up(f0):
        zg = jnp.dot(h2, up_ref[:, f0:f0 + fc], preferred_element_type=F32)
        zv = jnp.dot(h2, up_ref[:, D_FF + f0:D_FF + f0 + fc], preferred_element_type=F32)
        return zg, zv

    acc = jnp.zeros((tm, D_MODEL), F32)
    nxt = up(0)
    for f0 in range(0, D_FF, fc):
        zg, zv = nxt
        if f0 + fc < D_FF:
            nxt = up(f0 + fc)
        hg = 0.5 * conv(zg, f0)
        act = (hg * (jnp.tanh(hg) + 1.0) * conv(zv, D_FF + f0)).astype(BF16)
        if f0 + fc < D_FF:
            acc = acc + jnp.dot(act, down_ref[f0:f0 + fc, :], preferred_element_type=F32)
        else:
            for r0 in range(0, tm, es):
                a = acc[r0:r0 + es] + jnp.dot(act[r0:r0 + es], down_ref[f0:f0 + fc, :],
                                              preferred_element_type=F32)
                x2 = x1[r0:r0 + es] + g2 * _swap_rows(slab_sc, a, r0)
                y_ref[r0:r0 + es, :] = _rms(x2, fn_ref[...])


def _ffn_kernel(xc_ref, xl_ref, olc_ref, oll_ref, os_ref, mod_ref, gl_ref, gs_ref, wout_ref, n2_ref,
                up_ref, cw_ref, cb_ref, down_ref, fn_ref, yc_ref, yl_ref, slab_sc, *,
                nc, lat_mod_row, ctx_period, lat_period, **tile):
    i = pl.program_id(0)
    shared = (mod_ref, gl_ref, gs_ref, wout_ref, n2_ref, up_ref, cw_ref, cb_ref, down_ref, fn_ref)

    @pl.when(i < nc)
    def _():
        _ffn_tile(xc_ref, olc_ref, os_ref, *shared, yc_ref, slab_sc, period=ctx_period, row=0, **tile)

    @pl.when(i >= nc)
    def _():
        _ffn_tile(xl_ref, oll_ref, os_ref, *shared, yl_ref, slab_sc, period=lat_period,
                  row=lat_mod_row(i), **tile)


def _ffn(xc, xl, olru_c, olru_l, osgu, mod, g_lru, g_sgu, w_out, norm2, ffn_up, conv_w, conv_b,
         ffn_down, final_norm, *, tm, ctx_period, lat_period, lat_seq):
    nc, n, ctx_map, lat_map, lat_mod_row = _token_groups(tm, xc.shape[0], xl.shape[0], lat_seq)
    const = lambda i: (0, 0)
    resident = lambda shape: pl.BlockSpec(shape, const, pipeline_mode=pl.Buffered(1))
    return pl.pallas_call(
        functools.partial(_ffn_kernel, nc=nc, lat_mod_row=lat_mod_row, ctx_period=ctx_period,
                          lat_period=lat_period, tm=tm, fc=FFN_FC, es=FFN_ES),
        grid=(n,),
        in_specs=[
            pl.BlockSpec((tm, D_MODEL), ctx_map),
            pl.BlockSpec((tm, D_MODEL), lat_map),
            pl.BlockSpec((tm, LRU_WIDTH), ctx_map),
            pl.BlockSpec((tm, LRU_WIDTH), lat_map),
            pl.BlockSpec((tm, SGU_WIDTH), lambda i: (i, 0)),
            pl.BlockSpec((N_MOD, MOD_ROWS, D_MODEL), lambda i: (0, 0, 0)),
            pl.BlockSpec((1, LRU_WIDTH), const),
            pl.BlockSpec((1, SGU_WIDTH), const),
            resident((LRU_WIDTH + SGU_WIDTH, D_MODEL)),
            pl.BlockSpec((1, D_MODEL), const),
            resident((D_MODEL, 2 * D_FF)),
            pl.BlockSpec((3, 2 * D_FF), const),
            pl.BlockSpec((1, 2 * D_FF), const),
            resident((D_FF, D_MODEL)),
            pl.BlockSpec((1, D_MODEL), const),
        ],
        out_specs=[pl.BlockSpec((tm, D_MODEL), ctx_map), pl.BlockSpec((tm, D_MODEL), lat_map)],
        out_shape=[jax.ShapeDtypeStruct(xc.shape, F32), jax.ShapeDtypeStruct(xl.shape, F32)],
        scratch_shapes=[pltpu.VMEM((D_MODEL // LANES, tm, LANES), F32)],
        compiler_params=pltpu.CompilerParams(
            dimension_semantics=("arbitrary",), vmem_limit_bytes=FFN_VMEM),
        name="outproj_ffn",
    )(xc, xl, olru_c, olru_l, osgu, mod, g_lru, g_sgu, w_out, norm2, ffn_up, conv_w, conv_b, ffn_down,
      final_norm)


def _gate_params(wa, ba, wx, bx):
    nlb = LRU_HEADS // 2
    w = jnp.stack([wa[0], wx[0], wa[1], wx[1]], axis=2)
    wp = w.reshape(nlb, 2, LRU_HEAD_DIM, 4, LRU_HEAD_DIM)
    pair = 0.5 * jnp.eye(2, dtype=F32)
    bd = wp[:, :, :, :, None, :] * pair[None, :, None, None, :, None]
    wg = bd.reshape(nlb, LANES, 4 * LANES).astype(BF16)
    b = 0.5 * jnp.stack([ba[0], bx[0], ba[1], bx[1]], axis=0)
    bg = jnp.transpose(b.reshape(4, nlb, LANES), (1, 0, 2)).reshape(nlb, 1, 4 * LANES)
    return wg, bg


def kernel(x_prompt, x_sample, state_lru, c, c_ctx, norm1, norm2, w_ada, b_ada, w_in, lru_conv_w,
           lru_conv_b, lru_wa, lru_ba, lru_wx, lru_bx, lru_lam, sgu_ws, sgu_bs, g_lru, g_sgu, w_out,
           ffn_up, ffn_conv_w, ffn_conv_b, ffn_down, final_norm):
    batch, seq, _ = x_prompt.shape
    dec_batch, dec_seq, _ = x_sample.shape
    depth = norm1.shape[0]
    assert depth == 1, "the final norm is fused into the (single) layer's last kernel"
    l = 0

    xc = x_prompt.reshape(batch * seq, D_MODEL)
    xl = x_sample.reshape(dec_batch * dec_seq, D_MODEL)
    n_ctx, n_lat = xc.shape[0], xl.shape[0]
    zeros_state = jnp.zeros((batch, 2, LRU_WIDTH), x_prompt.dtype)

    mod = _modulation(c_ctx[None, :], c, w_ada[l], b_ada[l][None, :])
    wg, bg = _gate_params(lru_wa[l], lru_ba[l], lru_wx[l], lru_bx[l])
    lru_params = (lru_conv_w[l], lru_conv_b[l][None, :], wg, bg, lru_lam[l])
    h0_lat = state_lru.reshape(dec_batch, 2, LRU_WIDTH)

    xy, osgu, w_out_b = _inproj(
        xc, xl, mod, norm1[l][None, :], w_in[l], sgu_ws[l].astype(BF16), sgu_bs[l].T,
        w_out[l], tm=INPROJ_TM, lat_seq=dec_seq)
    olru_c, new_state, down_b = _lru(xy, *lru_params, zeros_state, ffn_down[l], row_start=0, m=n_ctx,
                                     seq=seq, nsb=LRU_CTX_SEQS)
    olru_l, _, up_b = _lru(xy, *lru_params, h0_lat, ffn_up[l], row_start=n_ctx, m=n_lat,
                           seq=dec_seq, nsb=1)
    yc, yl = _ffn(xc, xl, olru_c, olru_l, osgu, mod, g_lru[l][None, :], g_sgu[l][None, :],
                  w_out_b, norm2[l][None, :], up_b, ffn_conv_w[l],
                  ffn_conv_b[l][None, :], down_b, final_norm[None, :], tm=FFN_TM,
                  ctx_period=seq, lat_period=GRID_W, lat_seq=dec_seq)
    return (yc.reshape(batch, seq, D_MODEL), yl.reshape(dec_batch, dec_seq, D_MODEL),
            new_state[:, None])
```

```python
import functools
import math

import jax
import jax.numpy as jnp
from jax import lax
from jax.experimental import pallas as pl
from jax.experimental.pallas import tpu as pltpu

D_MODEL = 1024
LRU_HEADS = 8
LRU_WIDTH = 512
LRU_HEAD_DIM = 64
LRU_C = 8.0
SGU_GROUPS = 4
SGU_WIDTH = 512
CHUNK = 128
D_FF = 3072
N_MOD = 6
EPS = 1e-6
GRID_W = 64

LANES = 128
SUBLANES = 8
BF16_SUBLANES = 16
MOD_ROWS = 8
BF16 = jnp.bfloat16
F32 = jnp.float32

MIB = 1024 * 1024
INPROJ_TM = 1024
INPROJ_VMEM = 56 * MIB
LRU_ROW_CHUNK = 512
LRU_LANE_BLOCKS = 2
LRU_CTX_SEQS = 8
LRU_SCAN_UNROLL = True
LRU_VMEM = 56 * MIB
FFN_TM = 512
FFN_FC = 1536
FFN_ES = 256
FFN_VMEM = 60 * MIB

_GELU_C0 = math.sqrt(2.0 / math.pi)
_GELU_C1 = _GELU_C0 * 0.044715


def _rms(x, gain):
    return x * lax.rsqrt(jnp.mean(x * x, axis=-1, keepdims=True) + EPS) * gain


def _gelu(x):
    return (0.5 * x) * (1.0 + jnp.tanh(x * (_GELU_C0 + _GELU_C1 * (x * x))))


def _mod_row(mod_ref, k, row):
    return mod_ref[k, pl.ds(row, 1), :]


def _mod_kernel(cctx_ref, c_ref, w_ref, b_ref, o_ref):
    pad = jnp.zeros((MOD_ROWS - 1 - c_ref.shape[0], D_MODEL), F32)
    cc = jnp.concatenate([cctx_ref[...], c_ref[...], pad], axis=0)
    s = jax.nn.silu(cc).astype(BF16)
    o_ref[...] = jnp.dot(s, w_ref[...].astype(BF16), preferred_element_type=F32) + b_ref[...]


def _modulation(c_ctx, c, w_ada, b_ada):
    assert 1 + c.shape[0] <= MOD_ROWS
    return pl.pallas_call(
        _mod_kernel,
        grid=(N_MOD,),
        in_specs=[
            pl.BlockSpec((1, D_MODEL), lambda j: (0, 0)),
            pl.BlockSpec(c.shape, lambda j: (0, 0)),
            pl.BlockSpec((D_MODEL, D_MODEL), lambda j: (0, j)),
            pl.BlockSpec((1, D_MODEL), lambda j: (0, j)),
        ],
        out_specs=pl.BlockSpec((None, MOD_ROWS, D_MODEL), lambda j: (j, 0, 0)),
        out_shape=jax.ShapeDtypeStruct((N_MOD, MOD_ROWS, D_MODEL), F32),
        compiler_params=pltpu.CompilerParams(dimension_semantics=("arbitrary",)),
        name="modulation",
    )(c_ctx, c, w_ada, b_ada)


def _token_groups(tm, n_ctx, n_lat, lat_seq):
    assert n_ctx % tm == 0 and n_lat % tm == 0 and lat_seq % tm == 0
    nc = n_ctx // tm
    ctx_map = lambda i: (jnp.minimum(i, nc - 1), 0)
    lat_map = lambda i: (jnp.maximum(i - nc, 0), 0)
    lat_mod_row = lambda i: 1 + ((i - nc) * tm) // lat_seq
    return nc, nc + n_lat // tm, ctx_map, lat_map, lat_mod_row


def _inproj_kernel(xc_ref, xl_ref, mod_ref, n1_ref, win_f_ref, ws_ref, bs_ref, w_ref,
                   xy_ref, osgu_ref, w_b_ref, win_ref, *, tm, nc, lat_mod_row):
    i = pl.program_id(0)
    w_b_ref[...] = w_ref[...].astype(BF16)

    @pl.when(i == 0)
    def _():
        win_ref[...] = win_f_ref[...].astype(BF16)

    def body(x_ref, row):
        sh1 = _mod_row(mod_ref, 0, row)
        sc1 = _mod_row(mod_ref, 1, row)
        hb = (_rms(x_ref[...], n1_ref[...] * (1.0 + sc1)) + sh1).astype(BF16)
        uv = jnp.dot(hb, win_ref[:, 2 * LRU_WIDTH:], preferred_element_type=F32)
        xy_ref[...] = jnp.dot(hb, win_ref[:, 0:2 * LRU_WIDTH], preferred_element_type=F32)
        for g in range(SGU_GROUPS):
            lo = g * LANES
            gu = _gelu(uv[:, lo:lo + LANES])
            gv = _gelu(uv[:, SGU_WIDTH + lo:SGU_WIDTH + lo + LANES]).astype(BF16)
            bias = jnp.broadcast_to(bs_ref[:, g:g + 1], (CHUNK, LANES))
            ws = ws_ref[g].astype(BF16)
            for ck in range(tm // CHUNK):
                r0 = ck * CHUNK
                s = jnp.dot(ws, gv[r0:r0 + CHUNK], preferred_element_type=F32) + bias
                osgu_ref[r0:r0 + CHUNK, lo:lo + LANES] = gu[r0:r0 + CHUNK] * s

    @pl.when(i < nc)
    def _():
        body(xc_ref, 0)

    @pl.when(i >= nc)
    def _():
        body(xl_ref, lat_mod_row(i))


def _cast_slice_spec(shape, grid):
    n_steps = math.prod(grid)
    rows = shape[0]
    r = BF16_SUBLANES
    while rows % r or rows // r > n_steps:
        r += BF16_SUBLANES
    last = rows // r - 1

    def index_map(*idx):
        step = 0
        for i, n in zip(idx, grid):
            step = step * n + i
        return (jnp.minimum(step, last), 0)

    return pl.BlockSpec((r, shape[1]), index_map)


def _inproj(xc, xl, mod, norm1, w_in, ws, bs_t, later_weight, *, tm, lat_seq):
    nc, n, ctx_map, lat_map, lat_mod_row = _token_groups(tm, xc.shape[0], xl.shape[0], lat_seq)
    m = n * tm
    const = lambda i: (0, 0)
    cast_specs = [_cast_slice_spec(later_weight.shape, (n,))]
    return pl.pallas_call(
        functools.partial(_inproj_kernel, tm=tm, nc=nc, lat_mod_row=lat_mod_row),
        grid=(n,),
        in_specs=[
            pl.BlockSpec((tm, D_MODEL), ctx_map),
            pl.BlockSpec((tm, D_MODEL), lat_map),
            pl.BlockSpec((N_MOD, MOD_ROWS, D_MODEL), lambda i: (0, 0, 0)),
            pl.BlockSpec((1, D_MODEL), const),
            pl.BlockSpec((D_MODEL, 4 * LRU_WIDTH), const, pipeline_mode=pl.Buffered(1)),
            pl.BlockSpec((SGU_GROUPS, CHUNK, CHUNK), lambda i: (0, 0, 0)),
            pl.BlockSpec((CHUNK, SGU_GROUPS), const),
        ] + cast_specs,
        out_specs=[
            pl.BlockSpec((tm, 2 * LRU_WIDTH), lambda i: (i, 0)),
            pl.BlockSpec((tm, SGU_WIDTH), lambda i: (i, 0)),
        ] + cast_specs,
        out_shape=[
            jax.ShapeDtypeStruct((m, 2 * LRU_WIDTH), F32),
            jax.ShapeDtypeStruct((m, SGU_WIDTH), F32),
        ] + [jax.ShapeDtypeStruct(later_weight.shape, BF16)],
        scratch_shapes=[pltpu.VMEM(w_in.shape, BF16)],
        compiler_params=pltpu.CompilerParams(
            dimension_semantics=("arbitrary",), vmem_limit_bytes=INPROJ_VMEM),
        name="inproj_sgu",
    )(xc, xl, mod, norm1, w_in, ws, bs_t, later_weight)


def _scan_pitch(seq):
    pitch = seq // SUBLANES
    return pitch + (4 - pitch % 8) % 8


def _lru_kernel(xr_ref, yr_ref, cw_ref, cb_ref, wa_ref, ba_ref, wx_ref, bx_ref, lam_ref, h0_ref, w_ref,
                o_ref, st_ref, w_b_ref, a_sc, b_sc, p_sc, h_sc, wg_ref, bg_ref, *,
                seq, nsb, nk, pitch, rc, unroll):
    sp_rows = SUBLANES * pitch
    n_rc = (seq * nsb) // rc
    seg = min(seq, rc)
    segs = rc // seg
    chunks_per_seq = seq // seg
    w_b_ref[...] = w_ref[...].astype(BF16)

    @pl.when(pl.program_id(1) == 0)
    def _():
        zeros = jnp.zeros((LRU_HEAD_DIM, LRU_HEAD_DIM), F32)
        for k in range(nk):
            blocks = []
            for d in range(2):
                for src in (wa_ref, wx_ref):
                    top = jnp.concatenate([src[d, 2 * k], zeros], axis=1)
                    bot = jnp.concatenate([zeros, src[d, 2 * k + 1]], axis=1)
                    blocks.append(jnp.concatenate([top, bot], axis=0))
            wg_ref[k] = (0.5 * jnp.concatenate(blocks, axis=1)).astype(BF16)
            lanes = slice(k * LANES, (k + 1) * LANES)
            bg_ref[k] = 0.5 * jnp.concatenate(
                [ba_ref[0:1, lanes], bx_ref[0:1, lanes], ba_ref[1:2, lanes], bx_ref[1:2, lanes]], axis=1)

    c4_all = (-0.5 * LRU_C) * jax.nn.softplus(-lam_ref[...])
    cw_all = cw_ref[...]
    cb_all = cb_ref[...]

    for s in range(nsb):
        lo, hi = s * sp_rows + seq, (s + 1) * sp_rows
        for slab in range(2 * nk):
            a_sc[slab, lo:hi, :] = jnp.ones((hi - lo, LANES), F32)
            b_sc[slab, lo:hi, :] = jnp.zeros((hi - lo, LANES), F32)

    def scratch_row(ci, k):
        if chunks_per_seq > 1:
            return pl.multiple_of((ci // chunks_per_seq) * sp_rows + (ci % chunks_per_seq) * seg,
                                  SUBLANES)
        return pl.multiple_of((ci * segs + k) * sp_rows, SUBLANES)

    def conv_segment(xs, prev, nxt, cw, cb):
        ext = jnp.concatenate([prev, xs, nxt], axis=0)
        n_ext = seg + 2 * SUBLANES
        xc = cb + cw[2:3] * xs
        xc = xc + cw[0:1] * pltpu.roll(ext, 2, axis=0)[SUBLANES:SUBLANES + seg]
        xc = xc + cw[1:2] * pltpu.roll(ext, 1, axis=0)[SUBLANES:SUBLANES + seg]
        return xc + cw[3:4] * pltpu.roll(ext, n_ext - 1, axis=0)[SUBLANES:SUBLANES + seg]

    def gates(ci, _):
        base = pl.multiple_of(ci * rc, rc)
        for k in range(nk):
            lanes = slice(k * LANES, (k + 1) * LANES)
            cw, cb, c4 = cw_all[:, lanes], cb_all[:, lanes], c4_all[:, lanes]
            xm = xr_ref[pl.ds(base, rc), lanes]
            if chunks_per_seq > 1:
                cs = ci % chunks_per_seq
                p0 = pl.multiple_of(jnp.maximum(base - SUBLANES, 0), SUBLANES)
                n0 = pl.multiple_of(jnp.minimum(base + rc, seq * nsb - SUBLANES), SUBLANES)
                prev = jnp.where(cs == 0, 0.0, xr_ref[pl.ds(p0, SUBLANES), lanes])
                nxt = jnp.where(cs == chunks_per_seq - 1, 0.0, xr_ref[pl.ds(n0, SUBLANES), lanes])
                xc = conv_segment(xm, prev, nxt, cw, cb)
            else:
                zeros = jnp.zeros((SUBLANES, LANES), F32)
                xc = jnp.concatenate(
                    [conv_segment(xm[j * seg:(j + 1) * seg], zeros, zeros, cw, cb) for j in range(segs)],
                    axis=0)
            g = jnp.dot(xc.astype(BF16), wg_ref[k], preferred_element_type=F32) + bg_ref[k]
            hxc = 0.5 * xc
            for d in range(2):
                th_r = jnp.tanh(g[:, (2 * d) * LANES:(2 * d + 1) * LANES])
                th_i = jnp.tanh(g[:, (2 * d + 1) * LANES:(2 * d + 2) * LANES])
                log_a = c4[d:d + 1] * th_r + c4[d:d + 1]
                a = jnp.exp(log_a)
                om = jnp.tanh(log_a) * (-1.0 - a * a)
                root = jnp.where(om > 0.0, om * lax.rsqrt(om), 0.0)
                b = root * ((th_i + 1.0) * hxc)
                for j in range(segs):
                    dst = scratch_row(ci, j)
                    a_sc[d * nk + k, pl.ds(dst, seg), :] = a[j * seg:(j + 1) * seg]
                    b_sc[d * nk + k, pl.ds(dst, seg), :] = b[j * seg:(j + 1) * seg]
        return 0

    lax.fori_loop(0, n_rc, gates, 0)

    def scan_sequence(s, _):
        off = s * sp_rows

        def rows(t):
            return pl.ds(off + t, SUBLANES, stride=pitch)

        def local(t, carry):
            out = []
            for slab, (h, p) in enumerate(carry):
                tt = t if slab < nk else pitch - 1 - t
                a = a_sc[slab, rows(tt), :]
                h = a * h + b_sc[slab, rows(tt), :]
                p = a * p
                h_sc[slab, rows(tt), :] = h
                p_sc[slab, rows(tt), :] = p
                out.append((h, p))
            return tuple(out)

        zero = jnp.zeros((SUBLANES, LANES), F32)
        one = jnp.ones((SUBLANES, LANES), F32)
        ends = lax.fori_loop(0, pitch, local, ((zero, one),) * (2 * nk), unroll=unroll)

        h0 = h0_ref[s]
        starts = []
        finals = [[None] * nk, [None] * nk]
        for slab, (h, p) in enumerate(ends):
            d, k = divmod(slab, nk)
            c = h0[d:d + 1, k * LANES:(k + 1) * LANES]
            cs = [None] * SUBLANES
            for j in (range(SUBLANES) if d == 0 else reversed(range(SUBLANES))):
                cs[j] = c
                c = h[j:j + 1] + p[j:j + 1] * c
            finals[d][k] = c
            starts.append(jnp.concatenate(cs, axis=0))
        st_ref[s] = jnp.concatenate(
            [jnp.concatenate(finals[0], axis=1), jnp.concatenate(finals[1], axis=1)], axis=0)

        def fix(t, _):
            for k in range(nk):
                a_sc[k, rows(t), :] = (h_sc[k, rows(t), :] + p_sc[k, rows(t), :] * starts[k]
                                       + h_sc[nk + k, rows(t), :] + p_sc[nk + k, rows(t), :] * starts[nk + k])
            return 0

        lax.fori_loop(0, pitch, fix, 0, unroll=unroll)
        return 0

    if nsb == 1:
        scan_sequence(0, 0)
    else:
        lax.fori_loop(0, nsb, scan_sequence, 0)

    def gate_out(ci, _):
        base = pl.multiple_of(ci * rc, rc)
        for k in range(nk):
            lanes = slice(k * LANES, (k + 1) * LANES)
            gy = _gelu(yr_ref[pl.ds(base, rc), lanes])
            for j in range(segs):
                src = scratch_row(ci, j)
                o_ref[pl.ds(base + j * seg, seg), lanes] = (gy[j * seg:(j + 1) * seg]
                                                            * a_sc[k, pl.ds(src, seg), :])
        return 0

    lax.fori_loop(0, n_rc, gate_out, 0)


def _lru(xy, conv_w, conv_b, wa, ba, wx, bx, lam, h0, later_weight, *, row_start, m, seq, nsb):
    nblk = m // (seq * nsb)
    assert row_start % (seq * nsb) == 0
    blk0 = row_start // (seq * nsb)
    nk = LRU_LANE_BLOCKS
    width = nk * LANES
    ncb = LRU_WIDTH // width
    pitch = _scan_pitch(seq)
    sc_rows = nsb * SUBLANES * pitch
    rows = seq * nsb
    cast_spec = _cast_slice_spec(later_weight.shape, (ncb, nblk))
    return pl.pallas_call(
        functools.partial(_lru_kernel, seq=seq, nsb=nsb, nk=nk, pitch=pitch,
                          rc=LRU_ROW_CHUNK, unroll=LRU_SCAN_UNROLL),
        grid=(ncb, nblk),
        in_specs=[
            pl.BlockSpec((rows, width), lambda j, b: (blk0 + b, j)),
            pl.BlockSpec((rows, width), lambda j, b: (blk0 + b, ncb + j)),
            pl.BlockSpec((4, width), lambda j, b: (0, j)),
            pl.BlockSpec((1, width), lambda j, b: (0, j)),
            pl.BlockSpec((2, 2 * nk, LRU_HEAD_DIM, LRU_HEAD_DIM), lambda j, b: (0, j, 0, 0)),
            pl.BlockSpec((2, width), lambda j, b: (0, j)),
            pl.BlockSpec((2, 2 * nk, LRU_HEAD_DIM, LRU_HEAD_DIM), lambda j, b: (0, j, 0, 0)),
            pl.BlockSpec((2, width), lambda j, b: (0, j)),
            pl.BlockSpec((2, width), lambda j, b: (0, j)),
            pl.BlockSpec((nsb, 2, width), lambda j, b: (b, 0, j)),
            cast_spec,
        ],
        out_specs=[
            pl.BlockSpec((rows, width), lambda j, b: (b, j)),
            pl.BlockSpec((nsb, 2, width), lambda j, b: (b, 0, j)),
            cast_spec,
        ],
        out_shape=[
            jax.ShapeDtypeStruct((m, LRU_WIDTH), F32),
            jax.ShapeDtypeStruct((m // seq, 2, LRU_WIDTH), F32),
            jax.ShapeDtypeStruct(later_weight.shape, BF16),
        ],
        scratch_shapes=[pltpu.VMEM((2 * nk, sc_rows, LANES), F32)] * 4 + [
            pltpu.VMEM((nk, LANES, 4 * LANES), BF16), pltpu.VMEM((nk, 1, 4 * LANES), F32)],
        compiler_params=pltpu.CompilerParams(
            dimension_semantics=("arbitrary", "arbitrary"), vmem_limit_bytes=LRU_VMEM),
        name="rglru",
    )(xy, xy, conv_w, conv_b, wa, ba, wx, bx, lam, h0, later_weight)


GROUP = SUBLANES * SUBLANES


def _swap_rows(slab_ref, val, r0):
    n = val.shape[0]
    for k in range(D_MODEL // LANES):
        slab_ref[k, r0:r0 + n, :] = val[:, k * LANES:(k + 1) * LANES]
    cols = []
    for k in range(D_MODEL // LANES):
        rows = [slab_ref[k, pl.ds(r0 + g * GROUP + t, SUBLANES, stride=SUBLANES), :]
                for g in range(n // GROUP) for t in range(SUBLANES)]
        cols.append(jnp.concatenate(rows, axis=0))
    return jnp.concatenate(cols, axis=1)


def _ffn_tile(x_ref, ol_ref, os_ref, mod_ref, gl_ref, gs_ref, wout_ref, n2_ref, up_ref,
              cw_ref, cb_ref, down_ref, fn_ref, y_ref, slab_sc, *, tm, fc, es, period, row):
    g1 = _mod_row(mod_ref, 2, row)
    sh2 = _mod_row(mod_ref, 3, row)
    sc2 = _mod_row(mod_ref, 4, row)
    g2 = _mod_row(mod_ref, 5, row)
    nl = _rms(ol_ref[...], gl_ref[...]).astype(BF16)
    ns = _rms(os_ref[...], gs_ref[...]).astype(BF16)
    o = (jnp.dot(nl, wout_ref[0:LRU_WIDTH, :], preferred_element_type=F32)
         + jnp.dot(ns, wout_ref[LRU_WIDTH:LRU_WIDTH + SGU_WIDTH, :], preferred_element_type=F32))
    x1 = x_ref[...] + g1 * o
    h2f = _rms(x1, n2_ref[...] * (1.0 + sc2)) + sh2
    h2 = _swap_rows(slab_sc, h2f, 0).astype(BF16)

    sub = lax.broadcasted_iota(jnp.int32, (SUBLANES, fc), 0)
    groups_per_period = period // GROUP
    n_groups = tm // GROUP

    def conv(z, c0):
        w = cw_ref[:, c0:c0 + fc]
        b = cb_ref[:, c0:c0 + fc]
        w0, w1, w2 = w[0:1], w[1:2], w[2:3]
        z8 = [[z[g * GROUP + t * SUBLANES:g * GROUP + (t + 1) * SUBLANES] for t in range(SUBLANES)]
              for g in range(n_groups)]
        outs = []
        for g in range(n_groups):
            cur = z8[g]
            lo = pltpu.roll(cur[SUBLANES - 1], 1, axis=0)
            if g % groups_per_period == 0:
                lo = jnp.where(sub == 0, 0.0, lo)
            else:
                lo = jnp.where(sub == 0, pltpu.roll(z8[g - 1][SUBLANES - 1], 1, axis=0), lo)
            hi = pltpu.roll(cur[0], SUBLANES - 1, axis=0)
            if g % groups_per_period == groups_per_period - 1:
                hi = jnp.where(sub == SUBLANES - 1, 0.0, hi)
            else:
                hi = jnp.where(sub == SUBLANES - 1, pltpu.roll(z8[g + 1][0], SUBLANES - 1, axis=0), hi)
            zm = [lo] + cur[:SUBLANES - 1]
            zp = cur[1:] + [hi]
            for t in range(SUBLANES):
                outs.append(b + w0 * zm[t] + w1 * cur[t] + w2 * zp[t])
        return jnp.concatenate(outs, axis=0)

    def up(f0):
        zg = jnp.dot(h2, up_ref[:, f0:f0 + fc], preferred_element_type=F32)
        zv = jnp.dot(h2, up_ref[:, D_FF + f0:D_FF + f0 + fc], preferred_element_type=F32)
        return zg, zv

    acc = jnp.zeros((tm, D_MODEL), F32)
    nxt = up(0)
    for f0 in range(0, D_FF, fc):
        zg, zv = nxt
        if f0 + fc < D_FF:
            nxt = up(f0 + fc)
        hg = 0.5 * conv(zg, f0)
        act = (hg * (jnp.tanh(hg) + 1.0) * conv(zv, D_FF + f0)).astype(BF16)
        if f0 + fc < D_FF:
            acc = acc + jnp.dot(act, down_ref[f0:f0 + fc, :], preferred_element_type=F32)
        else:
            for r0 in range(0, tm, es):
                a = acc[r0:r0 + es] + jnp.dot(act[r0:r0 + es], down_ref[f0:f0 + fc, :],
                                              preferred_element_type=F32)
                x2 = x1[r0:r0 + es] + g2 * _swap_rows(slab_sc, a, r0)
                y_ref[r0:r0 + es, :] = _rms(x2, fn_ref[...])


def _ffn_kernel(xc_ref, xl_ref, olc_ref, oll_ref, os_ref, mod_ref, gl_ref, gs_ref, wout_ref, n2_ref,
                up_ref, cw_ref, cb_ref, down_ref, fn_ref, yc_ref, yl_ref, slab_sc, *,
                nc, lat_mod_row, ctx_period, lat_period, **tile):
    i = pl.program_id(0)
    shared = (mod_ref, gl_ref, gs_ref, wout_ref, n2_ref, up_ref, cw_ref, cb_ref, down_ref, fn_ref)

    @pl.when(i < nc)
    def _():
        _ffn_tile(xc_ref, olc_ref, os_ref, *shared, yc_ref, slab_sc, period=ctx_period, row=0, **tile)

    @pl.when(i >= nc)
    def _():
        _ffn_tile(xl_ref, oll_ref, os_ref, *shared, yl_ref, slab_sc, period=lat_period,
                  row=lat_mod_row(i), **tile)


def _ffn(xc, xl, olru_c, olru_l, osgu, mod, g_lru, g_sgu, w_out, norm2, ffn_up, conv_w, conv_b,
         ffn_down, final_norm, *, tm, ctx_period, lat_period, lat_seq):
    nc, n, ctx_map, lat_map, lat_mod_row = _token_groups(tm, xc.shape[0], xl.shape[0], lat_seq)
    const = lambda i: (0, 0)
    resident = lambda shape: pl.BlockSpec(shape, const, pipeline_mode=pl.Buffered(1))
    return pl.pallas_call(
        functools.partial(_ffn_kernel, nc=nc, lat_mod_row=lat_mod_row, ctx_period=ctx_period,
                          lat_period=lat_period, tm=tm, fc=FFN_FC, es=FFN_ES),
        grid=(n,),
        in_specs=[
            pl.BlockSpec((tm, D_MODEL), ctx_map),
            pl.BlockSpec((tm, D_MODEL), lat_map),
            pl.BlockSpec((tm, LRU_WIDTH), ctx_map),
            pl.BlockSpec((tm, LRU_WIDTH), lat_map),
            pl.BlockSpec((tm, SGU_WIDTH), lambda i: (i, 0)),
            pl.BlockSpec((N_MOD, MOD_ROWS, D_MODEL), lambda i: (0, 0, 0)),
            pl.BlockSpec((1, LRU_WIDTH), const),
            pl.BlockSpec((1, SGU_WIDTH), const),
            resident((LRU_WIDTH + SGU_WIDTH, D_MODEL)),
            pl.BlockSpec((1, D_MODEL), const),
            resident((D_MODEL, 2 * D_FF)),
            pl.BlockSpec((3, 2 * D_FF), const),
            pl.BlockSpec((1, 2 * D_FF), const),
            resident((D_FF, D_MODEL)),
            pl.BlockSpec((1, D_MODEL), const),
        ],
        out_specs=[pl.BlockSpec((tm, D_MODEL), ctx_map), pl.BlockSpec((tm, D_MODEL), lat_map)],
        out_shape=[jax.ShapeDtypeStruct(xc.shape, F32), jax.ShapeDtypeStruct(xl.shape, F32)],
        scratch_shapes=[pltpu.VMEM((D_MODEL // LANES, tm, LANES), F32)],
        compiler_params=pltpu.CompilerParams(
            dimension_semantics=("arbitrary",), vmem_limit_bytes=FFN_VMEM),
        name="outproj_ffn",
    )(xc, xl, olru_c, olru_l, osgu, mod, g_lru, g_sgu, w_out, norm2, ffn_up, conv_w, conv_b, ffn_down,
      final_norm)


def kernel(x_prompt, x_sample, state_lru, c, c_ctx, norm1, norm2, w_ada, b_ada, w_in, lru_conv_w,
           lru_conv_b, lru_wa, lru_ba, lru_wx, lru_bx, lru_lam, sgu_ws, sgu_bs, g_lru, g_sgu, w_out,
           ffn_up, ffn_conv_w, ffn_conv_b, ffn_down, final_norm):
    batch, seq, _ = x_prompt.shape
    dec_batch, dec_seq, _ = x_sample.shape
    depth = norm1.shape[0]
    assert depth == 1, "the final norm is fused into the (single) layer's last kernel"
    l = 0

    xc = x_prompt.reshape(batch * seq, D_MODEL)
    xl = x_sample.reshape(dec_batch * dec_seq, D_MODEL)
    n_ctx, n_lat = xc.shape[0], xl.shape[0]
    zeros_state = jnp.zeros((batch, 2, LRU_WIDTH), x_prompt.dtype)

    mod = _modulation(c_ctx[None, :], c, w_ada[l], b_ada[l][None, :])
    lru_params = (lru_conv_w[l], lru_conv_b[l][None, :], lru_wa[l], lru_ba[l], lru_wx[l], lru_bx[l],
                  lru_lam[l])
    h0_lat = state_lru.reshape(dec_batch, 2, LRU_WIDTH)

    xy, osgu, w_out_b = _inproj(
        xc, xl, mod, norm1[l][None, :], w_in[l], sgu_ws[l], sgu_bs[l].T,
        w_out[l], tm=INPROJ_TM, lat_seq=dec_seq)
    olru_c, new_state, down_b = _lru(xy, *lru_params, zeros_state, ffn_down[l], row_start=0, m=n_ctx,
                                     seq=seq, nsb=LRU_CTX_SEQS)
    olru_l, _, up_b = _lru(xy, *lru_params, h0_lat, ffn_up[l], row_start=n_ctx, m=n_lat,
                           seq=dec_seq, nsb=1)
    yc, yl = _ffn(xc, xl, olru_c, olru_l, osgu, mod, g_lru[l][None, :], g_sgu[l][None, :],
                  w_out_b, norm2[l][None, :], up_b, ffn_conv_w[l],
                  ffn_conv_b[l][None, :], down_b, final_norm[None, :], tm=FFN_TM,
                  ctx_period=seq, lat_period=GRID_W, lat_seq=dec_seq)
    return (yc.reshape(batch, seq, D_MODEL), yl.reshape(dec_batch, dec_seq, D_MODEL),
            new_state[:, None])
```

```python
import functools
import math

import jax
import jax.numpy as jnp
from jax import lax
from jax.experimental import pallas as pl
from jax.experimental.pallas import tpu as pltpu

D_MODEL = 1024
LRU_HEADS = 8
LRU_WIDTH = 512
LRU_HEAD_DIM = 64
LRU_C = 8.0
SGU_GROUPS = 4
SGU_WIDTH = 512
CHUNK = 128
D_FF = 3072
N_MOD = 6
EPS = 1e-6
GRID_W = 64

LANES = 128
SUBLANES = 8
BF16_SUBLANES = 16
MOD_ROWS = 8
BF16 = jnp.bfloat16
F32 = jnp.float32

MIB = 1024 * 1024
INPROJ_TM = 1024
INPROJ_VMEM = 56 * MIB
LRU_ROW_CHUNK = 512
LRU_LANE_BLOCKS = 2
LRU_CTX_SEQS = 8
LRU_SCAN_UNROLL = True
LRU_VMEM = 56 * MIB
FFN_TM = 512
FFN_FC = 1536
FFN_ES = 256
FFN_VMEM = 60 * MIB

_GELU_C0 = math.sqrt(2.0 / math.pi)
_GELU_C1 = _GELU_C0 * 0.044715


def _rms(x, gain):
    return x * lax.rsqrt(jnp.mean(x * x, axis=-1, keepdims=True) + EPS) * gain


def _gelu(x):
    return (0.5 * x) * (1.0 + jnp.tanh(x * (_GELU_C0 + _GELU_C1 * (x * x))))


def _mod_row(mod_ref, k, row):
    return mod_ref[k, pl.ds(row, 1), :]


def _mod_kernel(cctx_ref, c_ref, w_ref, b_ref, o_ref):
    pad = jnp.zeros((MOD_ROWS - 1 - c_ref.shape[0], D_MODEL), F32)
    cc = jnp.concatenate([cctx_ref[...], c_ref[...], pad], axis=0)
    s = jax.nn.silu(cc).astype(BF16)
    o_ref[...] = jnp.dot(s, w_ref[...].astype(BF16), preferred_element_type=F32) + b_ref[...]


def _modulation(c_ctx, c, w_ada, b_ada):
    assert 1 + c.shape[0] <= MOD_ROWS
    return pl.pallas_call(
        _mod_kernel,
        grid=(N_MOD,),
        in_specs=[
            pl.BlockSpec((1, D_MODEL), lambda j: (0, 0)),
            pl.BlockSpec(c.shape, lambda j: (0, 0)),
            pl.BlockSpec((D_MODEL, D_MODEL), lambda j: (0, j)),
            pl.BlockSpec((1, D_MODEL), lambda j: (0, j)),
        ],
        out_specs=pl.BlockSpec((None, MOD_ROWS, D_MODEL), lambda j: (j, 0, 0)),
        out_shape=jax.ShapeDtypeStruct((N_MOD, MOD_ROWS, D_MODEL), F32),
        compiler_params=pltpu.CompilerParams(dimension_semantics=("arbitrary",)),
        name="modulation",
    )(c_ctx, c, w_ada, b_ada)


def _token_groups(tm, n_ctx, n_lat, lat_seq):
    assert n_ctx % tm == 0 and n_lat % tm == 0 and lat_seq % tm == 0
    nc = n_ctx // tm
    ctx_map = lambda i: (jnp.minimum(i, nc - 1), 0)
    lat_map = lambda i: (jnp.maximum(i - nc, 0), 0)
    lat_mod_row = lambda i: 1 + ((i - nc) * tm) // lat_seq
    return nc, nc + n_lat // tm, ctx_map, lat_map, lat_mod_row


def _inproj_kernel(xc_ref, xl_ref, mod_ref, n1_ref, win_f_ref, ws_ref, bs_ref, w_ref,
                   xy_ref, osgu_ref, w_b_ref, win_ref, *, tm, nc, lat_mod_row):
    i = pl.program_id(0)
    w_b_ref[...] = w_ref[...].astype(BF16)

    @pl.when(i == 0)
    def _():
        win_ref[...] = win_f_ref[...].astype(BF16)

    def body(x_ref, row):
        sh1 = _mod_row(mod_ref, 0, row)
        sc1 = _mod_row(mod_ref, 1, row)
        hb = (_rms(x_ref[...], n1_ref[...] * (1.0 + sc1)) + sh1).astype(BF16)
        uv = jnp.dot(hb, win_ref[:, 2 * LRU_WIDTH:], preferred_element_type=F32)
        xy_ref[...] = jnp.dot(hb, win_ref[:, 0:2 * LRU_WIDTH], preferred_element_type=F32)
        for g in range(SGU_GROUPS):
            lo = g * LANES
            gu = _gelu(uv[:, lo:lo + LANES])
            gv = _gelu(uv[:, SGU_WIDTH + lo:SGU_WIDTH + lo + LANES]).astype(BF16)
            bias = jnp.broadcast_to(bs_ref[g:g + 1, :], (LANES, CHUNK)).T
            ws = ws_ref[g].astype(BF16)
            for ck in range(tm // CHUNK):
                r0 = ck * CHUNK
                s = jnp.dot(ws, gv[r0:r0 + CHUNK], preferred_element_type=F32) + bias
                osgu_ref[r0:r0 + CHUNK, lo:lo + LANES] = gu[r0:r0 + CHUNK] * s

    @pl.when(i < nc)
    def _():
        body(xc_ref, 0)

    @pl.when(i >= nc)
    def _():
        body(xl_ref, lat_mod_row(i))


def _cast_slice_spec(shape, grid):
    n_steps = math.prod(grid)
    rows = shape[0]
    r = BF16_SUBLANES
    while rows % r or rows // r > n_steps:
        r += BF16_SUBLANES
    last = rows // r - 1

    def index_map(*idx):
        step = 0
        for i, n in zip(idx, grid):
            step = step * n + i
        return (jnp.minimum(step, last), 0)

    return pl.BlockSpec((r, shape[1]), index_map)


def _inproj(xc, xl, mod, norm1, w_in, ws, bs, later_weight, *, tm, lat_seq):
    nc, n, ctx_map, lat_map, lat_mod_row = _token_groups(tm, xc.shape[0], xl.shape[0], lat_seq)
    m = n * tm
    const = lambda i: (0, 0)
    cast_specs = [_cast_slice_spec(later_weight.shape, (n,))]
    return pl.pallas_call(
        functools.partial(_inproj_kernel, tm=tm, nc=nc, lat_mod_row=lat_mod_row),
        grid=(n,),
        in_specs=[
            pl.BlockSpec((tm, D_MODEL), ctx_map),
            pl.BlockSpec((tm, D_MODEL), lat_map),
            pl.BlockSpec((N_MOD, MOD_ROWS, D_MODEL), lambda i: (0, 0, 0)),
            pl.BlockSpec((1, D_MODEL), const),
            pl.BlockSpec((D_MODEL, 4 * LRU_WIDTH), const, pipeline_mode=pl.Buffered(1)),
            pl.BlockSpec((SGU_GROUPS, CHUNK, CHUNK), lambda i: (0, 0, 0)),
            pl.BlockSpec((SGU_GROUPS, CHUNK), const),
        ] + cast_specs,
        out_specs=[
            pl.BlockSpec((tm, 2 * LRU_WIDTH), lambda i: (i, 0)),
            pl.BlockSpec((tm, SGU_WIDTH), lambda i: (i, 0)),
        ] + cast_specs,
        out_shape=[
            jax.ShapeDtypeStruct((m, 2 * LRU_WIDTH), F32),
            jax.ShapeDtypeStruct((m, SGU_WIDTH), F32),
        ] + [jax.ShapeDtypeStruct(later_weight.shape, BF16)],
        scratch_shapes=[pltpu.VMEM(w_in.shape, BF16)],
        compiler_params=pltpu.CompilerParams(
            dimension_semantics=("arbitrary",), vmem_limit_bytes=INPROJ_VMEM),
        name="inproj_sgu",
    )(xc, xl, mod, norm1, w_in, ws, bs, later_weight)


def _scan_pitch(seq):
    pitch = seq // SUBLANES
    return pitch + (4 - pitch % 8) % 8


def _lru_kernel(xr_ref, yr_ref, cw_ref, cb_ref, wa_ref, ba_ref, wx_ref, bx_ref, lam_ref, h0_ref, w_ref,
                o_ref, st_ref, w_b_ref, a_sc, b_sc, p_sc, h_sc, wg_ref, bg_ref, *,
                seq, nsb, nk, pitch, rc, unroll):
    sp_rows = SUBLANES * pitch
    n_rc = (seq * nsb) // rc
    seg = min(seq, rc)
    segs = rc // seg
    chunks_per_seq = seq // seg
    w_b_ref[...] = w_ref[...].astype(BF16)

    @pl.when(pl.program_id(1) == 0)
    def _():
        zeros = jnp.zeros((LRU_HEAD_DIM, LRU_HEAD_DIM), F32)
        for k in range(nk):
            blocks = []
            for d in range(2):
                for src in (wa_ref, wx_ref):
                    top = jnp.concatenate([src[d, 2 * k], zeros], axis=1)
                    bot = jnp.concatenate([zeros, src[d, 2 * k + 1]], axis=1)
                    blocks.append(jnp.concatenate([top, bot], axis=0))
            wg_ref[k] = (0.5 * jnp.concatenate(blocks, axis=1)).astype(BF16)
            lanes = slice(k * LANES, (k + 1) * LANES)
            bg_ref[k] = 0.5 * jnp.concatenate(
                [ba_ref[0:1, lanes], bx_ref[0:1, lanes], ba_ref[1:2, lanes], bx_ref[1:2, lanes]], axis=1)

    c4_all = (-0.5 * LRU_C) * jax.nn.softplus(-lam_ref[...])
    cw_all = cw_ref[...]
    cb_all = cb_ref[...]

    for s in range(nsb):
        lo, hi = s * sp_rows + seq, (s + 1) * sp_rows
        for slab in range(2 * nk):
            a_sc[slab, lo:hi, :] = jnp.ones((hi - lo, LANES), F32)
            b_sc[slab, lo:hi, :] = jnp.zeros((hi - lo, LANES), F32)

    def scratch_row(ci, k):
        if chunks_per_seq > 1:
            return pl.multiple_of((ci // chunks_per_seq) * sp_rows + (ci % chunks_per_seq) * seg,
                                  SUBLANES)
        return pl.multiple_of((ci * segs + k) * sp_rows, SUBLANES)

    def conv_segment(xs, prev, nxt, cw, cb):
        ext = jnp.concatenate([prev, xs, nxt], axis=0)
        n_ext = seg + 2 * SUBLANES
        xc = cb + cw[2:3] * xs
        xc = xc + cw[0:1] * pltpu.roll(ext, 2, axis=0)[SUBLANES:SUBLANES + seg]
        xc = xc + cw[1:2] * pltpu.roll(ext, 1, axis=0)[SUBLANES:SUBLANES + seg]
        return xc + cw[3:4] * pltpu.roll(ext, n_ext - 1, axis=0)[SUBLANES:SUBLANES + seg]

    def gates(ci, _):
        base = pl.multiple_of(ci * rc, rc)
        for k in range(nk):
            lanes = slice(k * LANES, (k + 1) * LANES)
            cw, cb, c4 = cw_all[:, lanes], cb_all[:, lanes], c4_all[:, lanes]
            xm = xr_ref[pl.ds(base, rc), lanes]
            if chunks_per_seq > 1:
                cs = ci % chunks_per_seq
                p0 = pl.multiple_of(jnp.maximum(base - SUBLANES, 0), SUBLANES)
                n0 = pl.multiple_of(jnp.minimum(base + rc, seq * nsb - SUBLANES), SUBLANES)
                prev = jnp.where(cs == 0, 0.0, xr_ref[pl.ds(p0, SUBLANES), lanes])
                nxt = jnp.where(cs == chunks_per_seq - 1, 0.0, xr_ref[pl.ds(n0, SUBLANES), lanes])
                xc = conv_segment(xm, prev, nxt, cw, cb)
            else:
                zeros = jnp.zeros((SUBLANES, LANES), F32)
                xc = jnp.concatenate(
                    [conv_segment(xm[j * seg:(j + 1) * seg], zeros, zeros, cw, cb) for j in range(segs)],
                    axis=0)
            g = jnp.dot(xc.astype(BF16), wg_ref[k], preferred_element_type=F32) + bg_ref[k]
            hxc = 0.5 * xc
            for d in range(2):
                th_r = jnp.tanh(g[:, (2 * d) * LANES:(2 * d + 1) * LANES])
                th_i = jnp.tanh(g[:, (2 * d + 1) * LANES:(2 * d + 2) * LANES])
                log_a = c4[d:d + 1] * th_r + c4[d:d + 1]
                a = jnp.exp(log_a)
                om = jnp.tanh(log_a) * (-1.0 - a * a)
                root = jnp.where(om > 0.0, om * lax.rsqrt(om), 0.0)
                b = root * ((th_i + 1.0) * hxc)
                for j in range(segs):
                    dst = scratch_row(ci, j)
                    a_sc[d * nk + k, pl.ds(dst, seg), :] = a[j * seg:(j + 1) * seg]
                    b_sc[d * nk + k, pl.ds(dst, seg), :] = b[j * seg:(j + 1) * seg]
        return 0

    lax.fori_loop(0, n_rc, gates, 0)

    def scan_sequence(s, _):
        off = s * sp_rows

        def rows(t):
            return pl.ds(off + t, SUBLANES, stride=pitch)

        def local(t, carry):
            out = []
            for slab, (h, p) in enumerate(carry):
                tt = t if slab < nk else pitch - 1 - t
                a = a_sc[slab, rows(tt), :]
                h = a * h + b_sc[slab, rows(tt), :]
                p = a * p
                h_sc[slab, rows(tt), :] = h
                p_sc[slab, rows(tt), :] = p
                out.append((h, p))
            return tuple(out)

        zero = jnp.zeros((SUBLANES, LANES), F32)
        one = jnp.ones((SUBLANES, LANES), F32)
        ends = lax.fori_loop(0, pitch, local, ((zero, one),) * (2 * nk), unroll=unroll)

        h0 = h0_ref[s]
        starts = []
        finals = [[None] * nk, [None] * nk]
        for slab, (h, p) in enumerate(ends):
            d, k = divmod(slab, nk)
            c = h0[d:d + 1, k * LANES:(k + 1) * LANES]
            cs = [None] * SUBLANES
            for j in (range(SUBLANES) if d == 0 else reversed(range(SUBLANES))):
                cs[j] = c
                c = h[j:j + 1] + p[j:j + 1] * c
            finals[d][k] = c
            starts.append(jnp.concatenate(cs, axis=0))
        st_ref[s] = jnp.concatenate(
            [jnp.concatenate(finals[0], axis=1), jnp.concatenate(finals[1], axis=1)], axis=0)

        def fix(t, _):
            for k in range(nk):
                a_sc[k, rows(t), :] = (h_sc[k, rows(t), :] + p_sc[k, rows(t), :] * starts[k]
                                       + h_sc[nk + k, rows(t), :] + p_sc[nk + k, rows(t), :] * starts[nk + k])
            return 0

        lax.fori_loop(0, pitch, fix, 0, unroll=unroll)
        return 0

    if nsb == 1:
        scan_sequence(0, 0)
    else:
        lax.fori_loop(0, nsb, scan_sequence, 0)

    def gate_out(ci, _):
        base = pl.multiple_of(ci * rc, rc)
        for k in range(nk):
            lanes = slice(k * LANES, (k + 1) * LANES)
            gy = _gelu(yr_ref[pl.ds(base, rc), lanes])
            for j in range(segs):
                src = scratch_row(ci, j)
                o_ref[pl.ds(base + j * seg, seg), lanes] = (gy[j * seg:(j + 1) * seg]
                                                            * a_sc[k, pl.ds(src, seg), :])
        return 0

    lax.fori_loop(0, n_rc, gate_out, 0)


def _lru(xy, conv_w, conv_b, wa, ba, wx, bx, lam, h0, later_weight, *, row_start, m, seq, nsb):
    nblk = m // (seq * nsb)
    assert row_start % (seq * nsb) == 0
    blk0 = row_start // (seq * nsb)
    nk = LRU_LANE_BLOCKS
    width = nk * LANES
    ncb = LRU_WIDTH // width
    pitch = _scan_pitch(seq)
    sc_rows = nsb * SUBLANES * pitch
    rows = seq * nsb
    cast_spec = _cast_slice_spec(later_weight.shape, (ncb, nblk))
    return pl.pallas_call(
        functools.partial(_lru_kernel, seq=seq, nsb=nsb, nk=nk, pitch=pitch,
                          rc=LRU_ROW_CHUNK, unroll=LRU_SCAN_UNROLL),
        grid=(ncb, nblk),
        in_specs=[
            pl.BlockSpec((rows, width), lambda j, b: (blk0 + b, j)),
            pl.BlockSpec((rows, width), lambda j, b: (blk0 + b, ncb + j)),
            pl.BlockSpec((4, width), lambda j, b: (0, j)),
            pl.BlockSpec((1, width), lambda j, b: (0, j)),
            pl.BlockSpec((2, 2 * nk, LRU_HEAD_DIM, LRU_HEAD_DIM), lambda j, b: (0, j, 0, 0)),
            pl.BlockSpec((2, width), lambda j, b: (0, j)),
            pl.BlockSpec((2, 2 * nk, LRU_HEAD_DIM, LRU_HEAD_DIM), lambda j, b: (0, j, 0, 0)),
            pl.BlockSpec((2, width), lambda j, b: (0, j)),
            pl.BlockSpec((2, width), lambda j, b: (0, j)),
            pl.BlockSpec((nsb, 2, width), lambda j, b: (b, 0, j)),
            cast_spec,
        ],
        out_specs=[
            pl.BlockSpec((rows, width), lambda j, b: (b, j)),
            pl.BlockSpec((nsb, 2, width), lambda j, b: (b, 0, j)),
            cast_spec,
        ],
        out_shape=[
            jax.ShapeDtypeStruct((m, LRU_WIDTH), F32),
            jax.ShapeDtypeStruct((m // seq, 2, LRU_WIDTH), F32),
            jax.ShapeDtypeStruct(later_weight.shape, BF16),
        ],
        scratch_shapes=[pltpu.VMEM((2 * nk, sc_rows, LANES), F32)] * 4 + [
            pltpu.VMEM((nk, LANES, 4 * LANES), BF16), pltpu.VMEM((nk, 1, 4 * LANES), F32)],
        compiler_params=pltpu.CompilerParams(
            dimension_semantics=("arbitrary", "arbitrary"), vmem_limit_bytes=LRU_VMEM),
        name="rglru",
    )(xy, xy, conv_w, conv_b, wa, ba, wx, bx, lam, h0, later_weight)


GROUP = SUBLANES * SUBLANES


def _swap_rows(slab_ref, val, r0):
    n = val.shape[0]
    for k in range(D_MODEL // LANES):
        slab_ref[k, r0:r0 + n, :] = val[:, k * LANES:(k + 1) * LANES]
    cols = []
    for k in range(D_MODEL // LANES):
        rows = [slab_ref[k, pl.ds(r0 + g * GROUP + t, SUBLANES, stride=SUBLANES), :]
                for g in range(n // GROUP) for t in range(SUBLANES)]
        cols.append(jnp.concatenate(rows, axis=0))
    return jnp.concatenate(cols, axis=1)


def _ffn_tile(x_ref, ol_ref, os_ref, mod_ref, gl_ref, gs_ref, wout_ref, n2_ref, up_ref,
              cw_ref, cb_ref, down_ref, fn_ref, y_ref, slab_sc, *, tm, fc, es, period, row):
    g1 = _mod_row(mod_ref, 2, row)
    sh2 = _mod_row(mod_ref, 3, row)
    sc2 = _mod_row(mod_ref, 4, row)
    g2 = _mod_row(mod_ref, 5, row)
    nl = _rms(ol_ref[...], gl_ref[...]).astype(BF16)
    ns = _rms(os_ref[...], gs_ref[...]).astype(BF16)
    o = (jnp.dot(nl, wout_ref[0:LRU_WIDTH, :], preferred_element_type=F32)
         + jnp.dot(ns, wout_ref[LRU_WIDTH:LRU_WIDTH + SGU_WIDTH, :], preferred_element_type=F32))
    x1 = x_ref[...] + g1 * o
    h2f = _rms(x1, n2_ref[...] * (1.0 + sc2)) + sh2
    h2 = _swap_rows(slab_sc, h2f, 0).astype(BF16)

    sub = lax.broadcasted_iota(jnp.int32, (SUBLANES, fc), 0)
    groups_per_period = period // GROUP
    n_groups = tm // GROUP

    def conv(z, c0):
        w = cw_ref[:, c0:c0 + fc]
        b = cb_ref[:, c0:c0 + fc]
        w0, w1, w2 = w[0:1], w[1:2], w[2:3]
        z8 = [[z[g * GROUP + t * SUBLANES:g * GROUP + (t + 1) * SUBLANES] for t in range(SUBLANES)]
              for g in range(n_groups)]
        outs = []
        for g in range(n_groups):
            cur = z8[g]
            lo = pltpu.roll(cur[SUBLANES - 1], 1, axis=0)
            if g % groups_per_period == 0:
                lo = jnp.where(sub == 0, 0.0, lo)
            else:
                lo = jnp.where(sub == 0, pltpu.roll(z8[g - 1][SUBLANES - 1], 1, axis=0), lo)
            hi = pltpu.roll(cur[0], SUBLANES - 1, axis=0)
            if g % groups_per_period == groups_per_period - 1:
                hi = jnp.where(sub == SUBLANES - 1, 0.0, hi)
            else:
                hi = jnp.where(sub == SUBLANES - 1, pltpu.roll(z8[g + 1][0], SUBLANES - 1, axis=0), hi)
            zm = [lo] + cur[:SUBLANES - 1]
            zp = cur[1:] + [hi]
            for t in range(SUBLANES):
                outs.append(b + w0 * zm[t] + w1 * cur[t] + w2 * zp[t])
        return jnp.concatenate(outs, axis=0)

    def up(f0):
        zg = jnp.dot(h2, up_ref[:, f0:f0 + fc], preferred_element_type=F32)
        zv = jnp.dot(h2, up_ref[:, D_FF + f0:D_FF + f0 + fc], preferred_element_type=F32)
        return zg, zv

    acc = jnp.zeros((tm, D_MODEL), F32)
    nxt = up(0)
    for f0 in range(0, D_FF, fc):
        zg, zv = nxt
        if f0 + fc < D_FF:
            nxt = up(f0 + fc)
        hg = 0.5 * conv(zg, f0)
        act = (hg * (jnp.tanh(hg) + 1.0) * conv(zv, D_FF + f0)).astype(BF16)
        if f0 + fc < D_FF:
            acc = acc + jnp.dot(act, down_ref[f0:f0 + fc, :], preferred_element_type=F32)
        else:
            for r0 in range(0, tm, es):
                a = acc[r0:r0 + es] + jnp.dot(act[r0:r0 + es], down_ref[f0:f0 + fc, :],
                                              preferred_element_type=F32)
                x2 = x1[r0:r0 + es] + g2 * _swap_rows(slab_sc, a, r0)
                y_ref[r0:r0 + es, :] = _rms(x2, fn_ref[...])


def _ffn_kernel(xc_ref, xl_ref, olc_ref, oll_ref, os_ref, mod_ref, gl_ref, gs_ref, wout_ref, n2_ref,
                up_ref, cw_ref, cb_ref, down_ref, fn_ref, yc_ref, yl_ref, slab_sc, *,
                nc, lat_mod_row, ctx_period, lat_period, **tile):
    i = pl.program_id(0)
    shared = (mod_ref, gl_ref, gs_ref, wout_ref, n2_ref, up_ref, cw_ref, cb_ref, down_ref, fn_ref)

    @pl.when(i < nc)
    def _():
        _ffn_tile(xc_ref, olc_ref, os_ref, *shared, yc_ref, slab_sc, period=ctx_period, row=0, **tile)

    @pl.when(i >= nc)
    def _():
        _ffn_tile(xl_ref, oll_ref, os_ref, *shared, yl_ref, slab_sc, period=lat_period,
                  row=lat_mod_row(i), **tile)


def _ffn(xc, xl, olru_c, olru_l, osgu, mod, g_lru, g_sgu, w_out, norm2, ffn_up, conv_w, conv_b,
         ffn_down, final_norm, *, tm, ctx_period, lat_period, lat_seq):
    nc, n, ctx_map, lat_map, lat_mod_row = _token_groups(tm, xc.shape[0], xl.shape[0], lat_seq)
    const = lambda i: (0, 0)
    resident = lambda shape: pl.BlockSpec(shape, const, pipeline_mode=pl.Buffered(1))
    return pl.pallas_call(
        functools.partial(_ffn_kernel, nc=nc, lat_mod_row=lat_mod_row, ctx_period=ctx_period,
                          lat_period=lat_period, tm=tm, fc=FFN_FC, es=FFN_ES),
        grid=(n,),
        in_specs=[
            pl.BlockSpec((tm, D_MODEL), ctx_map),
            pl.BlockSpec((tm, D_MODEL), lat_map),
            pl.BlockSpec((tm, LRU_WIDTH), ctx_map),
            pl.BlockSpec((tm, LRU_WIDTH), lat_map),
            pl.BlockSpec((tm, SGU_WIDTH), lambda i: (i, 0)),
            pl.BlockSpec((N_MOD, MOD_ROWS, D_MODEL), lambda i: (0, 0, 0)),
            pl.BlockSpec((1, LRU_WIDTH), const),
            pl.BlockSpec((1, SGU_WIDTH), const),
            resident((LRU_WIDTH + SGU_WIDTH, D_MODEL)),
            pl.BlockSpec((1, D_MODEL), const),
            resident((D_MODEL, 2 * D_FF)),
            pl.BlockSpec((3, 2 * D_FF), const),
            pl.BlockSpec((1, 2 * D_FF), const),
            resident((D_FF, D_MODEL)),
            pl.BlockSpec((1, D_MODEL), const),
        ],
        out_specs=[pl.BlockSpec((tm, D_MODEL), ctx_map), pl.BlockSpec((tm, D_MODEL), lat_map)],
        out_shape=[jax.ShapeDtypeStruct(xc.shape, F32), jax.ShapeDtypeStruct(xl.shape, F32)],
        scratch_shapes=[pltpu.VMEM((D_MODEL // LANES, tm, LANES), F32)],
        compiler_params=pltpu.CompilerParams(
            dimension_semantics=("arbitrary",), vmem_limit_bytes=FFN_VMEM),
        name="outproj_ffn",
    )(xc, xl, olru_c, olru_l, osgu, mod, g_lru, g_sgu, w_out, norm2, ffn_up, conv_w, conv_b, ffn_down,
      final_norm)


def kernel(x_prompt, x_sample, state_lru, c, c_ctx, norm1, norm2, w_ada, b_ada, w_in, lru_conv_w,
           lru_conv_b, lru_wa, lru_ba, lru_wx, lru_bx, lru_lam, sgu_ws, sgu_bs, g_lru, g_sgu, w_out,
           ffn_up, ffn_conv_w, ffn_conv_b, ffn_down, final_norm):
    batch, seq, _ = x_prompt.shape
    dec_batch, dec_seq, _ = x_sample.shape
    depth = norm1.shape[0]
    assert depth == 1, "the final norm is fused into the (single) layer's last kernel"
    l = 0

    xc = x_prompt.reshape(batch * seq, D_MODEL)
    xl = x_sample.reshape(dec_batch * dec_seq, D_MODEL)
    n_ctx, n_lat = xc.shape[0], xl.shape[0]
    zeros_state = jnp.zeros((batch, 2, LRU_WIDTH), x_prompt.dtype)

    mod = _modulation(c_ctx[None, :], c, w_ada[l], b_ada[l][None, :])
    lru_params = (lru_conv_w[l], lru_conv_b[l][None, :], lru_wa[l], lru_ba[l], lru_wx[l], lru_bx[l],
                  lru_lam[l])
    h0_lat = state_lru.reshape(dec_batch, 2, LRU_WIDTH)

    xy, osgu, w_out_b = _inproj(
        xc, xl, mod, norm1[l][None, :], w_in[l], sgu_ws[l], sgu_bs[l],
        w_out[l], tm=INPROJ_TM, lat_seq=dec_seq)
    olru_c, new_state, down_b = _lru(xy, *lru_params, zeros_state, ffn_down[l], row_start=0, m=n_ctx,
                                     seq=seq, nsb=LRU_CTX_SEQS)
    olru_l, _, up_b = _lru(xy, *lru_params, h0_lat, ffn_up[l], row_start=n_ctx, m=n_lat,
                           seq=dec_seq, nsb=1)
    yc, yl = _ffn(xc, xl, olru_c, olru_l, osgu, mod, g_lru[l][None, :], g_sgu[l][None, :],
                  w_out_b, norm2[l][None, :], up_b, ffn_conv_w[l],
                  ffn_conv_b[l][None, :], down_b, final_norm[None, :], tm=FFN_TM,
                  ctx_period=seq, lat_period=GRID_W, lat_seq=dec_seq)
    return (yc.reshape(batch, seq, D_MODEL), yl.reshape(dec_batch, dec_seq, D_MODEL),
            new_state[:, None])
```

```python
import functools
import math

import jax
import jax.numpy as jnp
from jax import lax
from jax.experimental import pallas as pl
from jax.experimental.pallas import tpu as pltpu

D_MODEL = 1024
LRU_HEADS = 8
LRU_WIDTH = 512
LRU_HEAD_DIM = 64
LRU_C = 8.0
SGU_GROUPS = 4
SGU_WIDTH = 512
CHUNK = 128
D_FF = 3072
N_MOD = 6
EPS = 1e-6
GRID_W = 64

LANES = 128
SUBLANES = 8
BF16_SUBLANES = 16
MOD_ROWS = 8
BF16 = jnp.bfloat16
F32 = jnp.float32

MIB = 1024 * 1024
INPROJ_TM = 1024
INPROJ_VMEM = 56 * MIB
LRU_ROW_CHUNK = 512
LRU_LANE_BLOCKS = 2
LRU_CTX_SEQS = 8
LRU_SCAN_UNROLL = True
LRU_VMEM = 56 * MIB
FFN_TM = 512
FFN_FC = 1536
FFN_ES = 256
FFN_VMEM = 60 * MIB

_GELU_C0 = math.sqrt(2.0 / math.pi)
_GELU_C1 = _GELU_C0 * 0.044715


def _rms(x, gain):
    return x * lax.rsqrt(jnp.mean(x * x, axis=-1, keepdims=True) + EPS) * gain


def _gelu(x):
    return (0.5 * x) * (1.0 + jnp.tanh(x * (_GELU_C0 + _GELU_C1 * (x * x))))


def _mod_row(mod_ref, k, row):
    return mod_ref[k, pl.ds(row, 1), :]


def _mod_kernel(cctx_ref, c_ref, w_ref, b_ref, o_ref):
    pad = jnp.zeros((MOD_ROWS - 1 - c_ref.shape[0], D_MODEL), F32)
    cc = jnp.concatenate([cctx_ref[...], c_ref[...], pad], axis=0)
    s = jax.nn.silu(cc).astype(BF16)
    o_ref[...] = jnp.dot(s, w_ref[...].astype(BF16), preferred_element_type=F32) + b_ref[...]


def _modulation(c_ctx, c, w_ada, b_ada):
    assert 1 + c.shape[0] <= MOD_ROWS
    return pl.pallas_call(
        _mod_kernel,
        grid=(N_MOD,),
        in_specs=[
            pl.BlockSpec((1, D_MODEL), lambda j: (0, 0)),
            pl.BlockSpec(c.shape, lambda j: (0, 0)),
            pl.BlockSpec((D_MODEL, D_MODEL), lambda j: (0, j)),
            pl.BlockSpec((1, D_MODEL), lambda j: (0, j)),
        ],
        out_specs=pl.BlockSpec((None, MOD_ROWS, D_MODEL), lambda j: (j, 0, 0)),
        out_shape=jax.ShapeDtypeStruct((N_MOD, MOD_ROWS, D_MODEL), F32),
        compiler_params=pltpu.CompilerParams(dimension_semantics=("arbitrary",)),
        name="modulation",
    )(c_ctx, c, w_ada, b_ada)


def _token_groups(tm, n_ctx, n_lat, lat_seq):
    assert n_ctx % tm == 0 and n_lat % tm == 0 and lat_seq % tm == 0
    nc = n_ctx // tm
    ctx_map = lambda i: (jnp.minimum(i, nc - 1), 0)
    lat_map = lambda i: (jnp.maximum(i - nc, 0), 0)
    lat_mod_row = lambda i: 1 + ((i - nc) * tm) // lat_seq
    return nc, nc + n_lat // tm, ctx_map, lat_map, lat_mod_row


def _inproj_kernel(xc_ref, xl_ref, mod_ref, n1_ref, win_f_ref, ws_ref, bs_ref, w_ref,
                   xy_ref, osgu_ref, w_b_ref, win_ref, *, tm, nc, lat_mod_row):
    i = pl.program_id(0)
    w_b_ref[...] = w_ref[...].astype(BF16)

    @pl.when(i == 0)
    def _():
        win_ref[...] = win_f_ref[...].astype(BF16)

    def body(x_ref, row):
        sh1 = _mod_row(mod_ref, 0, row)
        sc1 = _mod_row(mod_ref, 1, row)
        hb = (_rms(x_ref[...], n1_ref[...] * (1.0 + sc1)) + sh1).astype(BF16)
        uv = jnp.dot(hb, win_ref[:, 2 * LRU_WIDTH:], preferred_element_type=F32)
        xy_ref[...] = jnp.dot(hb, win_ref[:, 0:2 * LRU_WIDTH], preferred_element_type=F32)
        for g in range(SGU_GROUPS):
            lo = g * LANES
            gu = _gelu(uv[:, lo:lo + LANES])
            gv = _gelu(uv[:, SGU_WIDTH + lo:SGU_WIDTH + lo + LANES]).astype(BF16)
            bias = jnp.broadcast_to(bs_ref[g:g + 1, :], (LANES, CHUNK)).T
            ws = ws_ref[g].astype(BF16)
            for ck in range(tm // CHUNK):
                r0 = ck * CHUNK
                s = jnp.dot(ws, gv[r0:r0 + CHUNK], preferred_element_type=F32) + bias
                osgu_ref[r0:r0 + CHUNK, lo:lo + LANES] = gu[r0:r0 + CHUNK] * s

    @pl.when(i < nc)
    def _():
        body(xc_ref, 0)

    @pl.when(i >= nc)
    def _():
        body(xl_ref, lat_mod_row(i))


def _cast_slice_spec(shape, grid):
    n_steps = math.prod(grid)
    rows = shape[0]
    r = BF16_SUBLANES
    while rows % r or rows // r > n_steps:
        r += BF16_SUBLANES
    last = rows // r - 1

    def index_map(*idx):
        step = 0
        for i, n in zip(idx, grid):
            step = step * n + i
        return (jnp.minimum(step, last), 0)

    return pl.BlockSpec((r, shape[1]), index_map)


def _inproj(xc, xl, mod, norm1, w_in, ws, bs, later_weight, *, tm, lat_seq):
    nc, n, ctx_map, lat_map, lat_mod_row = _token_groups(tm, xc.shape[0], xl.shape[0], lat_seq)
    m = n * tm
    const = lambda i: (0, 0)
    cast_specs = [_cast_slice_spec(later_weight.shape, (n,))]
    return pl.pallas_call(
        functools.partial(_inproj_kernel, tm=tm, nc=nc, lat_mod_row=lat_mod_row),
        grid=(n,),
        in_specs=[
            pl.BlockSpec((tm, D_MODEL), ctx_map),
            pl.BlockSpec((tm, D_MODEL), lat_map),
            pl.BlockSpec((N_MOD, MOD_ROWS, D_MODEL), lambda i: (0, 0, 0)),
            pl.BlockSpec((1, D_MODEL), const),
            pl.BlockSpec((D_MODEL, 4 * LRU_WIDTH), const, pipeline_mode=pl.Buffered(1)),
            pl.BlockSpec((SGU_GROUPS, CHUNK, CHUNK), lambda i: (0, 0, 0)),
            pl.BlockSpec((SGU_GROUPS, CHUNK), const),
        ] + cast_specs,
        out_specs=[
            pl.BlockSpec((tm, 2 * LRU_WIDTH), lambda i: (i, 0)),
            pl.BlockSpec((tm, SGU_WIDTH), lambda i: (i, 0)),
        ] + cast_specs,
        out_shape=[
            jax.ShapeDtypeStruct((m, 2 * LRU_WIDTH), F32),
            jax.ShapeDtypeStruct((m, SGU_WIDTH), F32),
        ] + [jax.ShapeDtypeStruct(later_weight.shape, BF16)],
        scratch_shapes=[pltpu.VMEM(w_in.shape, BF16)],
        compiler_params=pltpu.CompilerParams(
            dimension_semantics=("arbitrary",), vmem_limit_bytes=INPROJ_VMEM),
        name="inproj_sgu",
    )(xc, xl, mod, norm1, w_in, ws, bs, later_weight)


def _scan_pitch(seq):
    pitch = seq // SUBLANES
    return pitch + (4 - pitch % 8) % 8


def _lru_kernel(xr_ref, yr_ref, cw_ref, cb_ref, wa_ref, ba_ref, wx_ref, bx_ref, lam_ref, h0_ref, w_ref,
                o_ref, st_ref, w_b_ref, a_sc, b_sc, p_sc, h_sc, wg_ref, *,
                seq, nsb, nk, pitch, rc, unroll):
    sp_rows = SUBLANES * pitch
    n_rc = (seq * nsb) // rc
    seg = min(seq, rc)
    segs = rc // seg
    chunks_per_seq = seq // seg
    w_b_ref[...] = w_ref[...].astype(BF16)

    @pl.when(pl.program_id(1) == 0)
    def _():
        zeros = jnp.zeros((LRU_HEAD_DIM, LRU_HEAD_DIM), F32)
        for k in range(nk):
            blocks = []
            for d in range(2):
                for src in (wa_ref, wx_ref):
                    top = jnp.concatenate([src[d, 2 * k], zeros], axis=1)
                    bot = jnp.concatenate([zeros, src[d, 2 * k + 1]], axis=1)
                    blocks.append(jnp.concatenate([top, bot], axis=0))
            wg_ref[k, 0:LANES, :] = (0.5 * jnp.concatenate(blocks, axis=1)).astype(BF16)
            lanes = slice(k * LANES, (k + 1) * LANES)
            bias = 0.5 * jnp.concatenate(
                [ba_ref[0:1, lanes], bx_ref[0:1, lanes], ba_ref[1:2, lanes], bx_ref[1:2, lanes]], axis=1)
            hi = bias.astype(BF16).astype(F32)
            row = lax.broadcasted_iota(jnp.int32, (LANES, 4 * LANES), 0)
            extra = jnp.where(row == 0, hi, jnp.where(row == 1, bias - hi, 0.0))
            wg_ref[k, LANES:2 * LANES, :] = extra.astype(BF16)

    c4_all = (-0.5 * LRU_C) * jax.nn.softplus(-lam_ref[...])
    cw_all = cw_ref[...]
    cb_all = cb_ref[...]

    for s in range(nsb):
        lo, hi = s * sp_rows + seq, (s + 1) * sp_rows
        for slab in range(2 * nk):
            a_sc[slab, lo:hi, :] = jnp.ones((hi - lo, LANES), F32)
            b_sc[slab, lo:hi, :] = jnp.zeros((hi - lo, LANES), F32)

    def scratch_row(ci, k):
        if chunks_per_seq > 1:
            return pl.multiple_of((ci // chunks_per_seq) * sp_rows + (ci % chunks_per_seq) * seg,
                                  SUBLANES)
        return pl.multiple_of((ci * segs + k) * sp_rows, SUBLANES)

    def conv_segment(xs, prev, nxt, cw, cb):
        ext = jnp.concatenate([prev, xs, nxt], axis=0)
        n_ext = seg + 2 * SUBLANES
        xc = cb + cw[2:3] * xs
        xc = xc + cw[0:1] * pltpu.roll(ext, 2, axis=0)[SUBLANES:SUBLANES + seg]
        xc = xc + cw[1:2] * pltpu.roll(ext, 1, axis=0)[SUBLANES:SUBLANES + seg]
        return xc + cw[3:4] * pltpu.roll(ext, n_ext - 1, axis=0)[SUBLANES:SUBLANES + seg]

    lane = lax.broadcasted_iota(jnp.int32, (rc, LANES), 1)
    ones_cols = jnp.where(lane < 2, 1.0, 0.0).astype(BF16)

    def gates(ci, _):
        base = pl.multiple_of(ci * rc, rc)
        for k in range(nk):
            lanes = slice(k * LANES, (k + 1) * LANES)
            cw, cb, c4 = cw_all[:, lanes], cb_all[:, lanes], c4_all[:, lanes]
            xm = xr_ref[pl.ds(base, rc), lanes]
            if chunks_per_seq > 1:
                cs = ci % chunks_per_seq
                p0 = pl.multiple_of(jnp.maximum(base - SUBLANES, 0), SUBLANES)
                n0 = pl.multiple_of(jnp.minimum(base + rc, seq * nsb - SUBLANES), SUBLANES)
                prev = jnp.where(cs == 0, 0.0, xr_ref[pl.ds(p0, SUBLANES), lanes])
                nxt = jnp.where(cs == chunks_per_seq - 1, 0.0, xr_ref[pl.ds(n0, SUBLANES), lanes])
                xc = conv_segment(xm, prev, nxt, cw, cb)
            else:
                zeros = jnp.zeros((SUBLANES, LANES), F32)
                xc = jnp.concatenate(
                    [conv_segment(xm[j * seg:(j + 1) * seg], zeros, zeros, cw, cb) for j in range(segs)],
                    axis=0)
            lhs = jnp.concatenate([xc.astype(BF16), ones_cols], axis=1)
            g = jnp.dot(lhs, wg_ref[k], preferred_element_type=F32)
            hxc = 0.5 * xc
            for d in range(2):
                th_r = jnp.tanh(g[:, (2 * d) * LANES:(2 * d + 1) * LANES])
                th_i = jnp.tanh(g[:, (2 * d + 1) * LANES:(2 * d + 2) * LANES])
                log_a = c4[d:d + 1] * th_r + c4[d:d + 1]
                a = jnp.exp(log_a)
                om = jnp.tanh(log_a) * (-1.0 - a * a)
                root = jnp.where(om > 0.0, om * lax.rsqrt(om), 0.0)
                b = root * ((th_i + 1.0) * hxc)
                for j in range(segs):
                    dst = scratch_row(ci, j)
                    a_sc[d * nk + k, pl.ds(dst, seg), :] = a[j * seg:(j + 1) * seg]
                    b_sc[d * nk + k, pl.ds(dst, seg), :] = b[j * seg:(j + 1) * seg]
        return 0

    lax.fori_loop(0, n_rc, gates, 0)

    def scan_sequence(s, _):
        off = s * sp_rows

        def rows(t):
            return pl.ds(off + t, SUBLANES, stride=pitch)

        def local(t, carry):
            out = []
            for slab, (h, p) in enumerate(carry):
                tt = t if slab < nk else pitch - 1 - t
                a = a_sc[slab, rows(tt), :]
                h = a * h + b_sc[slab, rows(tt), :]
                p = a * p
                h_sc[slab, rows(tt), :] = h
                p_sc[slab, rows(tt), :] = p
                out.append((h, p))
            return tuple(out)

        zero = jnp.zeros((SUBLANES, LANES), F32)
        one = jnp.ones((SUBLANES, LANES), F32)
        ends = lax.fori_loop(0, pitch, local, ((zero, one),) * (2 * nk), unroll=unroll)

        h0 = h0_ref[s]
        starts = []
        finals = [[None] * nk, [None] * nk]
        for slab, (h, p) in enumerate(ends):
            d, k = divmod(slab, nk)
            c = h0[d:d + 1, k * LANES:(k + 1) * LANES]
            cs = [None] * SUBLANES
            for j in (range(SUBLANES) if d == 0 else reversed(range(SUBLANES))):
                cs[j] = c
                c = h[j:j + 1] + p[j:j + 1] * c
            finals[d][k] = c
            starts.append(jnp.concatenate(cs, axis=0))
        st_ref[s] = jnp.concatenate(
            [jnp.concatenate(finals[0], axis=1), jnp.concatenate(finals[1], axis=1)], axis=0)

        def fix(t, _):
            for k in range(nk):
                a_sc[k, rows(t), :] = (h_sc[k, rows(t), :] + p_sc[k, rows(t), :] * starts[k]
                                       + h_sc[nk + k, rows(t), :] + p_sc[nk + k, rows(t), :] * starts[nk + k])
            return 0

        lax.fori_loop(0, pitch, fix, 0, unroll=unroll)
        return 0

    if nsb == 1:
        scan_sequence(0, 0)
    else:
        lax.fori_loop(0, nsb, scan_sequence, 0)

    def gate_out(ci, _):
        base = pl.multiple_of(ci * rc, rc)
        for k in range(nk):
            lanes = slice(k * LANES, (k + 1) * LANES)
            gy = _gelu(yr_ref[pl.ds(base, rc), lanes])
            for j in range(segs):
                src = scratch_row(ci, j)
                o_ref[pl.ds(base + j * seg, seg), lanes] = (gy[j * seg:(j + 1) * seg]
                                                            * a_sc[k, pl.ds(src, seg), :])
        return 0

    lax.fori_loop(0, n_rc, gate_out, 0)


def _lru(xy, conv_w, conv_b, wa, ba, wx, bx, lam, h0, later_weight, *, row_start, m, seq, nsb):
    nblk = m // (seq * nsb)
    assert row_start % (seq * nsb) == 0
    blk0 = row_start // (seq * nsb)
    nk = LRU_LANE_BLOCKS
    width = nk * LANES
    ncb = LRU_WIDTH // width
    pitch = _scan_pitch(seq)
    sc_rows = nsb * SUBLANES * pitch
    rows = seq * nsb
    cast_spec = _cast_slice_spec(later_weight.shape, (ncb, nblk))
    return pl.pallas_call(
        functools.partial(_lru_kernel, seq=seq, nsb=nsb, nk=nk, pitch=pitch,
                          rc=LRU_ROW_CHUNK, unroll=LRU_SCAN_UNROLL),
        grid=(ncb, nblk),
        in_specs=[
            pl.BlockSpec((rows, width), lambda j, b: (blk0 + b, j)),
            pl.BlockSpec((rows, width), lambda j, b: (blk0 + b, ncb + j)),
            pl.BlockSpec((4, width), lambda j, b: (0, j)),
            pl.BlockSpec((1, width), lambda j, b: (0, j)),
            pl.BlockSpec((2, 2 * nk, LRU_HEAD_DIM, LRU_HEAD_DIM), lambda j, b: (0, j, 0, 0)),
            pl.BlockSpec((2, width), lambda j, b: (0, j)),
            pl.BlockSpec((2, 2 * nk, LRU_HEAD_DIM, LRU_HEAD_DIM), lambda j, b: (0, j, 0, 0)),
            pl.BlockSpec((2, width), lambda j, b: (0, j)),
            pl.BlockSpec((2, width), lambda j, b: (0, j)),
            pl.BlockSpec((nsb, 2, width), lambda j, b: (b, 0, j)),
            cast_spec,
        ],
        out_specs=[
            pl.BlockSpec((rows, width), lambda j, b: (b, j)),
            pl.BlockSpec((nsb, 2, width), lambda j, b: (b, 0, j)),
            cast_spec,
        ],
        out_shape=[
            jax.ShapeDtypeStruct((m, LRU_WIDTH), F32),
            jax.ShapeDtypeStruct((m // seq, 2, LRU_WIDTH), F32),
            jax.ShapeDtypeStruct(later_weight.shape, BF16),
        ],
        scratch_shapes=[pltpu.VMEM((2 * nk, sc_rows, LANES), F32)] * 4 + [
            pltpu.VMEM((nk, 2 * LANES, 4 * LANES), BF16)],
        compiler_params=pltpu.CompilerParams(
            dimension_semantics=("arbitrary", "arbitrary"), vmem_limit_bytes=LRU_VMEM),
        name="rglru",
    )(xy, xy, conv_w, conv_b, wa, ba, wx, bx, lam, h0, later_weight)


GROUP = SUBLANES * SUBLANES


def _swap_rows(slab_ref, val, r0):
    n = val.shape[0]
    for k in range(D_MODEL // LANES):
        slab_ref[k, r0:r0 + n, :] = val[:, k * LANES:(k + 1) * LANES]
    cols = []
    for k in range(D_MODEL // LANES):
        rows = [slab_ref[k, pl.ds(r0 + g * GROUP + t, SUBLANES, stride=SUBLANES), :]
                for g in range(n // GROUP) for t in range(SUBLANES)]
        cols.append(jnp.concatenate(rows, axis=0))
    return jnp.concatenate(cols, axis=1)


def _ffn_tile(x_ref, ol_ref, os_ref, mod_ref, gl_ref, gs_ref, wout_ref, n2_ref, up_ref,
              cw_ref, cb_ref, down_ref, fn_ref, y_ref, slab_sc, *, tm, fc, es, period, row):
    g1 = _mod_row(mod_ref, 2, row)
    sh2 = _mod_row(mod_ref, 3, row)
    sc2 = _mod_row(mod_ref, 4, row)
    g2 = _mod_row(mod_ref, 5, row)
    nl = _rms(ol_ref[...], gl_ref[...]).astype(BF16)
    ns = _rms(os_ref[...], gs_ref[...]).astype(BF16)
    o = (jnp.dot(nl, wout_ref[0:LRU_WIDTH, :], preferred_element_type=F32)
         + jnp.dot(ns, wout_ref[LRU_WIDTH:LRU_WIDTH + SGU_WIDTH, :], preferred_element_type=F32))
    x1 = x_ref[...] + g1 * o
    h2f = _rms(x1, n2_ref[...] * (1.0 + sc2)) + sh2
    h2 = _swap_rows(slab_sc, h2f, 0).astype(BF16)

    sub = lax.broadcasted_iota(jnp.int32, (SUBLANES, fc), 0)
    groups_per_period = period // GROUP
    n_groups = tm // GROUP

    def conv(z, c0):
        w = cw_ref[:, c0:c0 + fc]
        b = cb_ref[:, c0:c0 + fc]
        w0, w1, w2 = w[0:1], w[1:2], w[2:3]
        z8 = [[z[g * GROUP + t * SUBLANES:g * GROUP + (t + 1) * SUBLANES] for t in range(SUBLANES)]
              for g in range(n_groups)]
        outs = []
        for g in range(n_groups):
            cur = z8[g]
            lo = pltpu.roll(cur[SUBLANES - 1], 1, axis=0)
            if g % groups_per_period == 0:
                lo = jnp.where(sub == 0, 0.0, lo)
            else:
                lo = jnp.where(sub == 0, pltpu.roll(z8[g - 1][SUBLANES - 1], 1, axis=0), lo)
            hi = pltpu.roll(cur[0], SUBLANES - 1, axis=0)
            if g % groups_per_period == groups_per_period - 1:
                hi = jnp.where(sub == SUBLANES - 1, 0.0, hi)
            else:
                hi = jnp.where(sub == SUBLANES - 1, pltpu.roll(z8[g + 1][0], SUBLANES - 1, axis=0), hi)
            zm = [lo] + cur[:SUBLANES - 1]
            zp = cur[1:] + [hi]
            for t in range(SUBLANES):
                outs.append(b + w0 * zm[t] + w1 * cur[t] + w2 * zp[t])
        return jnp.concatenate(outs, axis=0)

    def up(f0):
        zg = jnp.dot(h2, up_ref[:, f0:f0 + fc], preferred_element_type=F32)
        zv = jnp.dot(h2, up_ref[:, D_FF + f0:D_FF + f0 + fc], preferred_element_type=F32)
        return zg, zv

    acc = jnp.zeros((tm, D_MODEL), F32)
    nxt = up(0)
    for f0 in range(0, D_FF, fc):
        zg, zv = nxt
        if f0 + fc < D_FF:
            nxt = up(f0 + fc)
        hg = 0.5 * conv(zg, f0)
        act = (hg * (jnp.tanh(hg) + 1.0) * conv(zv, D_FF + f0)).astype(BF16)
        if f0 + fc < D_FF:
            acc = acc + jnp.dot(act, down_ref[f0:f0 + fc, :], preferred_element_type=F32)
        else:
            for r0 in range(0, tm, es):
                a = acc[r0:r0 + es] + jnp.dot(act[r0:r0 + es], down_ref[f0:f0 + fc, :],
                                              preferred_element_type=F32)
                x2 = x1[r0:r0 + es] + g2 * _swap_rows(slab_sc, a, r0)
                y_ref[r0:r0 + es, :] = _rms(x2, fn_ref[...])


def _ffn_kernel(xc_ref, xl_ref, olc_ref, oll_ref, os_ref, mod_ref, gl_ref, gs_ref, wout_ref, n2_ref,
                up_ref, cw_ref, cb_ref, down_ref, fn_ref, yc_ref, yl_ref, slab_sc, *,
                nc, lat_mod_row, ctx_period, lat_period, **tile):
    i = pl.program_id(0)
    shared = (mod_ref, gl_ref, gs_ref, wout_ref, n2_ref, up_ref, cw_ref, cb_ref, down_ref, fn_ref)

    @pl.when(i < nc)
    def _():
        _ffn_tile(xc_ref, olc_ref, os_ref, *shared, yc_ref, slab_sc, period=ctx_period, row=0, **tile)

    @pl.when(i >= nc)
    def _():
        _ffn_tile(xl_ref, oll_ref, os_ref, *shared, yl_ref, slab_sc, period=lat_period,
                  row=lat_mod_row(i), **tile)


def _ffn(xc, xl, olru_c, olru_l, osgu, mod, g_lru, g_sgu, w_out, norm2, ffn_up, conv_w, conv_b,
         ffn_down, final_norm, *, tm, ctx_period, lat_period, lat_seq):
    nc, n, ctx_map, lat_map, lat_mod_row = _token_groups(tm, xc.shape[0], xl.shape[0], lat_seq)
    const = lambda i: (0, 0)
    resident = lambda shape: pl.BlockSpec(shape, const, pipeline_mode=pl.Buffered(1))
    return pl.pallas_call(
        functools.partial(_ffn_kernel, nc=nc, lat_mod_row=lat_mod_row, ctx_period=ctx_period,
                          lat_period=lat_period, tm=tm, fc=FFN_FC, es=FFN_ES),
        grid=(n,),
        in_specs=[
            pl.BlockSpec((tm, D_MODEL), ctx_map),
            pl.BlockSpec((tm, D_MODEL), lat_map),
            pl.BlockSpec((tm, LRU_WIDTH), ctx_map),
            pl.BlockSpec((tm, LRU_WIDTH), lat_map),
            pl.BlockSpec((tm, SGU_WIDTH), lambda i: (i, 0)),
            pl.BlockSpec((N_MOD, MOD_ROWS, D_MODEL), lambda i: (0, 0, 0)),
            pl.BlockSpec((1, LRU_WIDTH), const),
            pl.BlockSpec((1, SGU_WIDTH), const),
            resident((LRU_WIDTH + SGU_WIDTH, D_MODEL)),
            pl.BlockSpec((1, D_MODEL), const),
            resident((D_MODEL, 2 * D_FF)),
            pl.BlockSpec((3, 2 * D_FF), const),
            pl.BlockSpec((1, 2 * D_FF), const),
            resident((D_FF, D_MODEL)),
            pl.BlockSpec((1, D_MODEL), const),
        ],
        out_specs=[pl.BlockSpec((tm, D_MODEL), ctx_map), pl.BlockSpec((tm, D_MODEL), lat_map)],
        out_shape=[jax.ShapeDtypeStruct(xc.shape, F32), jax.ShapeDtypeStruct(xl.shape, F32)],
        scratch_shapes=[pltpu.VMEM((D_MODEL // LANES, tm, LANES), F32)],
        compiler_params=pltpu.CompilerParams(
            dimension_semantics=("arbitrary",), vmem_limit_bytes=FFN_VMEM),
        name="outproj_ffn",
    )(xc, xl, olru_c, olru_l, osgu, mod, g_lru, g_sgu, w_out, norm2, ffn_up, conv_w, conv_b, ffn_down,
      final_norm)


def kernel(x_prompt, x_sample, state_lru, c, c_ctx, norm1, norm2, w_ada, b_ada, w_in, lru_conv_w,
           lru_conv_b, lru_wa, lru_ba, lru_wx, lru_bx, lru_lam, sgu_ws, sgu_bs, g_lru, g_sgu, w_out,
           ffn_up, ffn_conv_w, ffn_conv_b, ffn_down, final_norm):
    batch, seq, _ = x_prompt.shape
    dec_batch, dec_seq, _ = x_sample.shape
    depth = norm1.shape[0]
    assert depth == 1, "the final norm is fused into the (single) layer's last kernel"
    l = 0

    xc = x_prompt.reshape(batch * seq, D_MODEL)
    xl = x_sample.reshape(dec_batch * dec_seq, D_MODEL)
    n_ctx, n_lat = xc.shape[0], xl.shape[0]
    zeros_state = jnp.zeros((batch, 2, LRU_WIDTH), x_prompt.dtype)

    mod = _modulation(c_ctx[None, :], c, w_ada[l], b_ada[l][None, :])
    lru_params = (lru_conv_w[l], lru_conv_b[l][None, :], lru_wa[l], lru_ba[l], lru_wx[l], lru_bx[l],
                  lru_lam[l])
    h0_lat = state_lru.reshape(dec_batch, 2, LRU_WIDTH)

    xy, osgu, w_out_b = _inproj(
        xc, xl, mod, norm1[l][None, :], w_in[l], sgu_ws[l], sgu_bs[l],
        w_out[l], tm=INPROJ_TM, lat_seq=dec_seq)
    olru_c, new_state, down_b = _lru(xy, *lru_params, zeros_state, ffn_down[l], row_start=0, m=n_ctx,
                                     seq=seq, nsb=LRU_CTX_SEQS)
    olru_l, _, up_b = _lru(xy, *lru_params, h0_lat, ffn_up[l], row_start=n_ctx, m=n_lat,
                           seq=dec_seq, nsb=1)
    yc, yl = _ffn(xc, xl, olru_c, olru_l, osgu, mod, g_lru[l][None, :], g_sgu[l][None, :],
                  w_out_b, norm2[l][None, :], up_b, ffn_conv_w[l],
                  ffn_conv_b[l][None, :], down_b, final_norm[None, :], tm=FFN_TM,
                  ctx_period=seq, lat_period=GRID_W, lat_seq=dec_seq)
    return (yc.reshape(batch, seq, D_MODEL), yl.reshape(dec_batch, dec_seq, D_MODEL),
            new_state[:, None])
```

```python
import functools
import math

import jax
import jax.numpy as jnp
from jax import lax
from jax.experimental import pallas as pl
from jax.experimental.pallas import tpu as pltpu

D_MODEL = 1024
LRU_HEADS = 8
LRU_WIDTH = 512
LRU_HEAD_DIM = 64
LRU_C = 8.0
SGU_GROUPS = 4
SGU_WIDTH = 512
CHUNK = 128
D_FF = 3072
N_MOD = 6
EPS = 1e-6
GRID_W = 64

LANES = 128
SUBLANES = 8
BF16_SUBLANES = 16
MOD_ROWS = 8
BF16 = jnp.bfloat16
F32 = jnp.float32

MIB = 1024 * 1024
INPROJ_TM = 1024
INPROJ_VMEM = 56 * MIB
LRU_ROW_CHUNK = 512
LRU_LANE_BLOCKS = 2
LRU_CTX_SEQS = 8
LRU_SCAN_UNROLL = True
LRU_VMEM = 56 * MIB
FFN_TM = 512
FFN_FC = 1536
FFN_ES = 256
FFN_VMEM = 60 * MIB

_GELU_C0 = math.sqrt(2.0 / math.pi)
_GELU_C1 = _GELU_C0 * 0.044715


def _rms(x, gain):
    return x * lax.rsqrt(jnp.mean(x * x, axis=-1, keepdims=True) + EPS) * gain


def _gelu(x):
    return (0.5 * x) * (1.0 + jnp.tanh(x * (_GELU_C0 + _GELU_C1 * (x * x))))


def _mod_row(mod_ref, k, row):
    return mod_ref[k, pl.ds(row, 1), :]


def _mod_kernel(cctx_ref, c_ref, w_ref, b_ref, o_ref):
    pad = jnp.zeros((MOD_ROWS - 1 - c_ref.shape[0], D_MODEL), F32)
    cc = jnp.concatenate([cctx_ref[...], c_ref[...], pad], axis=0)
    s = jax.nn.silu(cc).astype(BF16)
    o_ref[...] = jnp.dot(s, w_ref[...].astype(BF16), preferred_element_type=F32) + b_ref[...]


def _modulation(c_ctx, c, w_ada, b_ada):
    assert 1 + c.shape[0] <= MOD_ROWS
    return pl.pallas_call(
        _mod_kernel,
        grid=(N_MOD,),
        in_specs=[
            pl.BlockSpec((1, D_MODEL), lambda j: (0, 0)),
            pl.BlockSpec(c.shape, lambda j: (0, 0)),
            pl.BlockSpec((D_MODEL, D_MODEL), lambda j: (0, j)),
            pl.BlockSpec((1, D_MODEL), lambda j: (0, j)),
        ],
        out_specs=pl.BlockSpec((None, MOD_ROWS, D_MODEL), lambda j: (j, 0, 0)),
        out_shape=jax.ShapeDtypeStruct((N_MOD, MOD_ROWS, D_MODEL), F32),
        compiler_params=pltpu.CompilerParams(dimension_semantics=("arbitrary",)),
        name="modulation",
    )(c_ctx, c, w_ada, b_ada)


def _token_groups(tm, n_ctx, n_lat, lat_seq):
    assert n_ctx % tm == 0 and n_lat % tm == 0 and lat_seq % tm == 0
    nc = n_ctx // tm
    ctx_map = lambda i: (jnp.minimum(i, nc - 1), 0)
    lat_map = lambda i: (jnp.maximum(i - nc, 0), 0)
    lat_mod_row = lambda i: 1 + ((i - nc) * tm) // lat_seq
    return nc, nc + n_lat // tm, ctx_map, lat_map, lat_mod_row


def _inproj_kernel(xc_ref, xl_ref, mod_ref, n1_ref, win_f_ref, ws_ref, bs_ref, w_ref,
                   xy_ref, osgu_ref, w_b_ref, win_ref, *, tm, nc, lat_mod_row):
    i = pl.program_id(0)
    w_b_ref[...] = w_ref[...].astype(BF16)

    @pl.when(i == 0)
    def _():
        win_ref[...] = win_f_ref[...].astype(BF16)

    def body(x_ref, row):
        sh1 = _mod_row(mod_ref, 0, row)
        sc1 = _mod_row(mod_ref, 1, row)
        hb = (_rms(x_ref[...], n1_ref[...] * (1.0 + sc1)) + sh1).astype(BF16)
        uv = jnp.dot(hb, win_ref[:, 2 * LRU_WIDTH:], preferred_element_type=F32)
        xy_ref[...] = jnp.dot(hb, win_ref[:, 0:2 * LRU_WIDTH], preferred_element_type=F32)
        for g in range(SGU_GROUPS):
            lo = g * LANES
            gu = _gelu(uv[:, lo:lo + LANES])
            gv = _gelu(uv[:, SGU_WIDTH + lo:SGU_WIDTH + lo + LANES]).astype(BF16)
            bias = jnp.broadcast_to(bs_ref[g:g + 1, :], (LANES, CHUNK)).T
            ws = ws_ref[g].astype(BF16)
            for ck in range(tm // CHUNK):
                r0 = ck * CHUNK
                s = jnp.dot(ws, gv[r0:r0 + CHUNK], preferred_element_type=F32) + bias
                osgu_ref[r0:r0 + CHUNK, lo:lo + LANES] = gu[r0:r0 + CHUNK] * s

    @pl.when(i < nc)
    def _():
        body(xc_ref, 0)

    @pl.when(i >= nc)
    def _():
        body(xl_ref, lat_mod_row(i))


def _cast_slice_spec(shape, grid):
    n_steps = math.prod(grid)
    rows = shape[0]
    r = BF16_SUBLANES
    while rows % r or rows // r > n_steps:
        r += BF16_SUBLANES
    last = rows // r - 1

    def index_map(*idx):
        step = 0
        for i, n in zip(idx, grid):
            step = step * n + i
        return (jnp.minimum(step, last), 0)

    return pl.BlockSpec((r, shape[1]), index_map)


def _inproj(xc, xl, mod, norm1, w_in, ws, bs, later_weight, *, tm, lat_seq):
    nc, n, ctx_map, lat_map, lat_mod_row = _token_groups(tm, xc.shape[0], xl.shape[0], lat_seq)
    m = n * tm
    const = lambda i: (0, 0)
    cast_specs = [_cast_slice_spec(later_weight.shape, (n,))]
    return pl.pallas_call(
        functools.partial(_inproj_kernel, tm=tm, nc=nc, lat_mod_row=lat_mod_row),
        grid=(n,),
        in_specs=[
            pl.BlockSpec((tm, D_MODEL), ctx_map),
            pl.BlockSpec((tm, D_MODEL), lat_map),
            pl.BlockSpec((N_MOD, MOD_ROWS, D_MODEL), lambda i: (0, 0, 0)),
            pl.BlockSpec((1, D_MODEL), const),
            pl.BlockSpec((D_MODEL, 4 * LRU_WIDTH), const, pipeline_mode=pl.Buffered(1)),
            pl.BlockSpec((SGU_GROUPS, CHUNK, CHUNK), lambda i: (0, 0, 0)),
            pl.BlockSpec((SGU_GROUPS, CHUNK), const),
        ] + cast_specs,
        out_specs=[
            pl.BlockSpec((tm, 2 * LRU_WIDTH), lambda i: (i, 0)),
            pl.BlockSpec((tm, SGU_WIDTH), lambda i: (i, 0)),
        ] + cast_specs,
        out_shape=[
            jax.ShapeDtypeStruct((m, 2 * LRU_WIDTH), F32),
            jax.ShapeDtypeStruct((m, SGU_WIDTH), F32),
        ] + [jax.ShapeDtypeStruct(later_weight.shape, BF16)],
        scratch_shapes=[pltpu.VMEM(w_in.shape, BF16)],
        compiler_params=pltpu.CompilerParams(
            dimension_semantics=("arbitrary",), vmem_limit_bytes=INPROJ_VMEM),
        name="inproj_sgu",
    )(xc, xl, mod, norm1, w_in, ws, bs, later_weight)


def _scan_pitch(seq):
    pitch = seq // SUBLANES
    return pitch + (4 - pitch % 8) % 8


def _lru_kernel(xr_ref, yr_ref, cw_ref, cb_ref, wa_ref, ba_ref, wx_ref, bx_ref, lam_ref, h0_ref, w_ref,
                o_ref, st_ref, w_b_ref, a_sc, b_sc, p_sc, h_sc, wg_ref, bg_ref, *,
                seq, nsb, nk, pitch, rc, unroll):
    sp_rows = SUBLANES * pitch
    n_rc = (seq * nsb) // rc
    seg = min(seq, rc)
    segs = rc // seg
    chunks_per_seq = seq // seg
    w_b_ref[...] = w_ref[...].astype(BF16)

    @pl.when(pl.program_id(1) == 0)
    def _():
        zeros = jnp.zeros((LRU_HEAD_DIM, LRU_HEAD_DIM), F32)
        for k in range(nk):
            blocks = []
            for d in range(2):
                for src in (wa_ref, wx_ref):
                    top = jnp.concatenate([src[d, 2 * k], zeros], axis=1)
                    bot = jnp.concatenate([zeros, src[d, 2 * k + 1]], axis=1)
                    blocks.append(jnp.concatenate([top, bot], axis=0))
            wg_ref[k] = (0.5 * jnp.concatenate(blocks, axis=1)).astype(BF16)
            lanes = slice(k * LANES, (k + 1) * LANES)
            bg_ref[k] = 0.5 * jnp.concatenate(
                [ba_ref[0:1, lanes], bx_ref[0:1, lanes], ba_ref[1:2, lanes], bx_ref[1:2, lanes]], axis=1)

    c4_all = (-0.5 * LRU_C) * jax.nn.softplus(-lam_ref[...])
    cw_all = cw_ref[...]
    cb_all = cb_ref[...]

    for s in range(nsb):
        lo, hi = s * sp_rows + seq, (s + 1) * sp_rows
        for slab in range(2 * nk):
            a_sc[slab, lo:hi, :] = jnp.ones((hi - lo, LANES), F32)
            b_sc[slab, lo:hi, :] = jnp.zeros((hi - lo, LANES), F32)

    def scratch_row(ci, k):
        if chunks_per_seq > 1:
            return pl.multiple_of((ci // chunks_per_seq) * sp_rows + (ci % chunks_per_seq) * seg,
                                  SUBLANES)
        return pl.multiple_of((ci * segs + k) * sp_rows, SUBLANES)

    def conv_segment(xs, prev, nxt, cw, cb):
        ext = jnp.concatenate([prev, xs, nxt], axis=0)
        n_ext = seg + 2 * SUBLANES
        xc = cb + cw[2:3] * xs
        xc = xc + cw[0:1] * pltpu.roll(ext, 2, axis=0)[SUBLANES:SUBLANES + seg]
        xc = xc + cw[1:2] * pltpu.roll(ext, 1, axis=0)[SUBLANES:SUBLANES + seg]
        return xc + cw[3:4] * pltpu.roll(ext, n_ext - 1, axis=0)[SUBLANES:SUBLANES + seg]

    def gates(ci, _):
        base = pl.multiple_of(ci * rc, rc)
        for k in range(nk):
            lanes = slice(k * LANES, (k + 1) * LANES)
            cw, cb, c4 = cw_all[:, lanes], cb_all[:, lanes], c4_all[:, lanes]
            xm = xr_ref[pl.ds(base, rc), lanes]
            if chunks_per_seq > 1:
                cs = ci % chunks_per_seq
                p0 = pl.multiple_of(jnp.maximum(base - SUBLANES, 0), SUBLANES)
                n0 = pl.multiple_of(jnp.minimum(base + rc, seq * nsb - SUBLANES), SUBLANES)
                prev = jnp.where(cs == 0, 0.0, xr_ref[pl.ds(p0, SUBLANES), lanes])
                nxt = jnp.where(cs == chunks_per_seq - 1, 0.0, xr_ref[pl.ds(n0, SUBLANES), lanes])
                xc = conv_segment(xm, prev, nxt, cw, cb)
            else:
                zeros = jnp.zeros((SUBLANES, LANES), F32)
                xc = jnp.concatenate(
                    [conv_segment(xm[j * seg:(j + 1) * seg], zeros, zeros, cw, cb) for j in range(segs)],
                    axis=0)
            g = jnp.dot(xc.astype(BF16), wg_ref[k], preferred_element_type=F32) + bg_ref[k]
            hxc = 0.5 * xc
            for d in range(2):
                th_r = jnp.tanh(g[:, (2 * d) * LANES:(2 * d + 1) * LANES])
                th_i = jnp.tanh(g[:, (2 * d + 1) * LANES:(2 * d + 2) * LANES])
                log_a = c4[d:d + 1] * th_r + c4[d:d + 1]
                a = jnp.exp(log_a)
                om = jnp.tanh(log_a) * (-1.0 - a * a)
                root = jnp.where(om > 0.0, om * lax.rsqrt(om), 0.0)
                b = root * ((th_i + 1.0) * hxc)
                for j in range(segs):
                    dst = scratch_row(ci, j)
                    a_sc[d * nk + k, pl.ds(dst, seg), :] = a[j * seg:(j + 1) * seg]
                    b_sc[d * nk + k, pl.ds(dst, seg), :] = b[j * seg:(j + 1) * seg]
        return 0

    lax.fori_loop(0, n_rc, gates, 0)

    def scan_sequence(s, _):
        off = s * sp_rows

        def rows(t):
            return pl.ds(off + t, SUBLANES, stride=pitch)

        def local(t, carry):
            out = []
            for slab, (h, p) in enumerate(carry):
                tt = t if slab < nk else pitch - 1 - t
                a = a_sc[slab, rows(tt), :]
                h = a * h + b_sc[slab, rows(tt), :]
                p = a * p
                h_sc[slab, rows(tt), :] = h
                p_sc[slab, rows(tt), :] = p
                out.append((h, p))
            return tuple(out)

        zero = jnp.zeros((SUBLANES, LANES), F32)
        one = jnp.ones((SUBLANES, LANES), F32)
        ends = lax.fori_loop(0, pitch, local, ((zero, one),) * (2 * nk), unroll=unroll)

        h0 = h0_ref[s]
        starts = []
        finals = [[None] * nk, [None] * nk]
        for slab, (h, p) in enumerate(ends):
            d, k = divmod(slab, nk)
            c = h0[d:d + 1, k * LANES:(k + 1) * LANES]
            cs = [None] * SUBLANES
            for j in (range(SUBLANES) if d == 0 else reversed(range(SUBLANES))):
                cs[j] = c
                c = h[j:j + 1] + p[j:j + 1] * c
            finals[d][k] = c
            starts.append(jnp.concatenate(cs, axis=0))
        st_ref[s] = jnp.concatenate(
            [jnp.concatenate(finals[0], axis=1), jnp.concatenate(finals[1], axis=1)], axis=0)

        def fix(t, _):
            for k in range(nk):
                a_sc[k, rows(t), :] = (h_sc[k, rows(t), :] + p_sc[k, rows(t), :] * starts[k]
                                       + h_sc[nk + k, rows(t), :] + p_sc[nk + k, rows(t), :] * starts[nk + k])
            return 0

        lax.fori_loop(0, pitch, fix, 0, unroll=unroll)
        return 0

    if nsb == 1:
        scan_sequence(0, 0)
    else:
        lax.fori_loop(0, nsb, scan_sequence, 0)

    def gate_out(ci, _):
        base = pl.multiple_of(ci * rc, rc)
        for k in range(nk):
            lanes = slice(k * LANES, (k + 1) * LANES)
            gy = _gelu(yr_ref[pl.ds(base, rc), lanes])
            for j in range(segs):
                src = scratch_row(ci, j)
                o_ref[pl.ds(base + j * seg, seg), lanes] = (gy[j * seg:(j + 1) * seg]
                                                            * a_sc[k, pl.ds(src, seg), :])
        return 0

    lax.fori_loop(0, n_rc, gate_out, 0)


def _lru(xy, conv_w, conv_b, wa, ba, wx, bx, lam, h0, later_weight, *, row_start, m, seq, nsb):
    nblk = m // (seq * nsb)
    assert row_start % (seq * nsb) == 0
    blk0 = row_start // (seq * nsb)
    nk = LRU_LANE_BLOCKS
    width = nk * LANES
    ncb = LRU_WIDTH // width
    pitch = _scan_pitch(seq)
    sc_rows = nsb * SUBLANES * pitch
    rows = seq * nsb
    cast_spec = _cast_slice_spec(later_weight.shape, (ncb, nblk))
    return pl.pallas_call(
        functools.partial(_lru_kernel, seq=seq, nsb=nsb, nk=nk, pitch=pitch,
                          rc=LRU_ROW_CHUNK, unroll=LRU_SCAN_UNROLL),
        grid=(ncb, nblk),
        in_specs=[
            pl.BlockSpec((rows, width), lambda j, b: (blk0 + b, j)),
            pl.BlockSpec((rows, width), lambda j, b: (blk0 + b, ncb + j)),
            pl.BlockSpec((4, width), lambda j, b: (0, j)),
            pl.BlockSpec((1, width), lambda j, b: (0, j)),
            pl.BlockSpec((2, 2 * nk, LRU_HEAD_DIM, LRU_HEAD_DIM), lambda j, b: (0, j, 0, 0)),
            pl.BlockSpec((2, width), lambda j, b: (0, j)),
            pl.BlockSpec((2, 2 * nk, LRU_HEAD_DIM, LRU_HEAD_DIM), lambda j, b: (0, j, 0, 0)),
            pl.BlockSpec((2, width), lambda j, b: (0, j)),
            pl.BlockSpec((2, width), lambda j, b: (0, j)),
            pl.BlockSpec((nsb, 2, width), lambda j, b: (b, 0, j)),
            cast_spec,
        ],
        out_specs=[
            pl.BlockSpec((rows, width), lambda j, b: (b, j)),
            pl.BlockSpec((nsb, 2, width), lambda j, b: (b, 0, j)),
            cast_spec,
        ],
        out_shape=[
            jax.ShapeDtypeStruct((m, LRU_WIDTH), F32),
            jax.ShapeDtypeStruct((m // seq, 2, LRU_WIDTH), F32),
            jax.ShapeDtypeStruct(later_weight.shape, BF16),
        ],
        scratch_shapes=[pltpu.VMEM((2 * nk, sc_rows, LANES), F32)] * 4 + [
            pltpu.VMEM((nk, LANES, 4 * LANES), BF16), pltpu.VMEM((nk, 1, 4 * LANES), F32)],
        compiler_params=pltpu.CompilerParams(
            dimension_semantics=("arbitrary", "arbitrary"), vmem_limit_bytes=LRU_VMEM),
        name="rglru",
    )(xy, xy, conv_w, conv_b, wa, ba, wx, bx, lam, h0, later_weight)


GROUP = SUBLANES * SUBLANES


def _swap_rows(slab_ref, val, r0):
    n = val.shape[0]
    for k in range(D_MODEL // LANES):
        slab_ref[k, r0:r0 + n, :] = val[:, k * LANES:(k + 1) * LANES]
    cols = []
    for k in range(D_MODEL // LANES):
        rows = [slab_ref[k, pl.ds(r0 + g * GROUP + t, SUBLANES, stride=SUBLANES), :]
                for g in range(n // GROUP) for t in range(SUBLANES)]
        cols.append(jnp.concatenate(rows, axis=0))
    return jnp.concatenate(cols, axis=1)


def _ffn_tile(x_ref, ol_ref, os_ref, mod_ref, gl_ref, gs_ref, wout_ref, n2_ref, up_ref,
              cw_ref, cb_ref, down_ref, fn_ref, y_ref, slab_sc, *, tm, fc, es, period, row):
    g1 = _mod_row(mod_ref, 2, row)
    sh2 = _mod_row(mod_ref, 3, row)
    sc2 = _mod_row(mod_ref, 4, row)
    g2 = _mod_row(mod_ref, 5, row)
    nl = _rms(ol_ref[...], gl_ref[...]).astype(BF16)
    ns = _rms(os_ref[...], gs_ref[...]).astype(BF16)
    o = (jnp.dot(nl, wout_ref[0:LRU_WIDTH, :], preferred_element_type=F32)
         + jnp.dot(ns, wout_ref[LRU_WIDTH:LRU_WIDTH + SGU_WIDTH, :], preferred_element_type=F32))
    x1 = x_ref[...] + g1 * o
    h2f = _rms(x1, n2_ref[...] * (1.0 + sc2)) + sh2
    h2 = _swap_rows(slab_sc, h2f, 0).astype(BF16)

    sub = lax.broadcasted_iota(jnp.int32, (SUBLANES, fc), 0)
    groups_per_period = period // GROUP
    n_groups = tm // GROUP

    def conv(z, c0, scale=None):
        w = cw_ref[:, c0:c0 + fc]
        b = cb_ref[:, c0:c0 + fc]
        if scale is not None:
            w, b = scale * w, scale * b
        w0, w1, w2 = w[0:1], w[1:2], w[2:3]
        z8 = [[z[g * GROUP + t * SUBLANES:g * GROUP + (t + 1) * SUBLANES] for t in range(SUBLANES)]
              for g in range(n_groups)]
        outs = []
        for g in range(n_groups):
            cur = z8[g]
            lo = pltpu.roll(cur[SUBLANES - 1], 1, axis=0)
            if g % groups_per_period == 0:
                lo = jnp.where(sub == 0, 0.0, lo)
            else:
                lo = jnp.where(sub == 0, pltpu.roll(z8[g - 1][SUBLANES - 1], 1, axis=0), lo)
            hi = pltpu.roll(cur[0], SUBLANES - 1, axis=0)
            if g % groups_per_period == groups_per_period - 1:
                hi = jnp.where(sub == SUBLANES - 1, 0.0, hi)
            else:
                hi = jnp.where(sub == SUBLANES - 1, pltpu.roll(z8[g + 1][0], SUBLANES - 1, axis=0), hi)
            zm = [lo] + cur[:SUBLANES - 1]
            zp = cur[1:] + [hi]
            for t in range(SUBLANES):
                outs.append(b + w0 * zm[t] + w1 * cur[t] + w2 * zp[t])
        return jnp.concatenate(outs, axis=0)

    def up(f0):
        zg = jnp.dot(h2, up_ref[:, f0:f0 + fc], preferred_element_type=F32)
        zv = jnp.dot(h2, up_ref[:, D_FF + f0:D_FF + f0 + fc], preferred_element_type=F32)
        return zg, zv

    acc = jnp.zeros((tm, D_MODEL), F32)
    nxt = up(0)
    for f0 in range(0, D_FF, fc):
        zg, zv = nxt
        if f0 + fc < D_FF:
            nxt = up(f0 + fc)
        hg = conv(zg, f0, scale=0.5)
        act = (hg * (jnp.tanh(hg) + 1.0) * conv(zv, D_FF + f0)).astype(BF16)
        if f0 + fc < D_FF:
            acc = acc + jnp.dot(act, down_ref[f0:f0 + fc, :], preferred_element_type=F32)
        else:
            for r0 in range(0, tm, es):
                a = acc[r0:r0 + es] + jnp.dot(act[r0:r0 + es], down_ref[f0:f0 + fc, :],
                                              preferred_element_type=F32)
                x2 = x1[r0:r0 + es] + g2 * _swap_rows(slab_sc, a, r0)
                y_ref[r0:r0 + es, :] = _rms(x2, fn_ref[...])


def _ffn_kernel(xc_ref, xl_ref, olc_ref, oll_ref, os_ref, mod_ref, gl_ref, gs_ref, wout_ref, n2_ref,
                up_ref, cw_ref, cb_ref, down_ref, fn_ref, yc_ref, yl_ref, slab_sc, *,
                nc, lat_mod_row, ctx_period, lat_period, **tile):
    i = pl.program_id(0)
    shared = (mod_ref, gl_ref, gs_ref, wout_ref, n2_ref, up_ref, cw_ref, cb_ref, down_ref, fn_ref)

    @pl.when(i < nc)
    def _():
        _ffn_tile(xc_ref, olc_ref, os_ref, *shared, yc_ref, slab_sc, period=ctx_period, row=0, **tile)

    @pl.when(i >= nc)
    def _():
        _ffn_tile(xl_ref, oll_ref, os_ref, *shared, yl_ref, slab_sc, period=lat_period,
                  row=lat_mod_row(i), **tile)


def _ffn(xc, xl, olru_c, olru_l, osgu, mod, g_lru, g_sgu, w_out, norm2, ffn_up, conv_w, conv_b,
         ffn_down, final_norm, *, tm, ctx_period, lat_period, lat_seq):
    nc, n, ctx_map, lat_map, lat_mod_row = _token_groups(tm, xc.shape[0], xl.shape[0], lat_seq)
    const = lambda i: (0, 0)
    resident = lambda shape: pl.BlockSpec(shape, const, pipeline_mode=pl.Buffered(1))
    return pl.pallas_call(
        functools.partial(_ffn_kernel, nc=nc, lat_mod_row=lat_mod_row, ctx_period=ctx_period,
                          lat_period=lat_period, tm=tm, fc=FFN_FC, es=FFN_ES),
        grid=(n,),
        in_specs=[
            pl.BlockSpec((tm, D_MODEL), ctx_map),
            pl.BlockSpec((tm, D_MODEL), lat_map),
            pl.BlockSpec((tm, LRU_WIDTH), ctx_map),
            pl.BlockSpec((tm, LRU_WIDTH), lat_map),
            pl.BlockSpec((tm, SGU_WIDTH), lambda i: (i, 0)),
            pl.BlockSpec((N_MOD, MOD_ROWS, D_MODEL), lambda i: (0, 0, 0)),
            pl.BlockSpec((1, LRU_WIDTH), const),
            pl.BlockSpec((1, SGU_WIDTH), const),
            resident((LRU_WIDTH + SGU_WIDTH, D_MODEL)),
            pl.BlockSpec((1, D_MODEL), const),
            resident((D_MODEL, 2 * D_FF)),
            pl.BlockSpec((3, 2 * D_FF), const),
            pl.BlockSpec((1, 2 * D_FF), const),
            resident((D_FF, D_MODEL)),
            pl.BlockSpec((1, D_MODEL), const),
        ],
        out_specs=[pl.BlockSpec((tm, D_MODEL), ctx_map), pl.BlockSpec((tm, D_MODEL), lat_map)],
        out_shape=[jax.ShapeDtypeStruct(xc.shape, F32), jax.ShapeDtypeStruct(xl.shape, F32)],
        scratch_shapes=[pltpu.VMEM((D_MODEL // LANES, tm, LANES), F32)],
        compiler_params=pltpu.CompilerParams(
            dimension_semantics=("arbitrary",), vmem_limit_bytes=FFN_VMEM),
        name="outproj_ffn",
    )(xc, xl, olru_c, olru_l, osgu, mod, g_lru, g_sgu, w_out, norm2, ffn_up, conv_w, conv_b, ffn_down,
      final_norm)


def kernel(x_prompt, x_sample, state_lru, c, c_ctx, norm1, norm2, w_ada, b_ada, w_in, lru_conv_w,
           lru_conv_b, lru_wa, lru_ba, lru_wx, lru_bx, lru_lam, sgu_ws, sgu_bs, g_lru, g_sgu, w_out,
           ffn_up, ffn_conv_w, ffn_conv_b, ffn_down, final_norm):
    batch, seq, _ = x_prompt.shape
    dec_batch, dec_seq, _ = x_sample.shape
    depth = norm1.shape[0]
    assert depth == 1, "the final norm is fused into the (single) layer's last kernel"
    l = 0

    xc = x_prompt.reshape(batch * seq, D_MODEL)
    xl = x_sample.reshape(dec_batch * dec_seq, D_MODEL)
    n_ctx, n_lat = xc.shape[0], xl.shape[0]
    zeros_state = jnp.zeros((batch, 2, LRU_WIDTH), x_prompt.dtype)

    mod = _modulation(c_ctx[None, :], c, w_ada[l], b_ada[l][None, :])
    lru_params = (lru_conv_w[l], lru_conv_b[l][None, :], lru_wa[l], lru_ba[l], lru_wx[l], lru_bx[l],
                  lru_lam[l])
    h0_lat = state_lru.reshape(dec_batch, 2, LRU_WIDTH)

    xy, osgu, w_out_b = _inproj(
        xc, xl, mod, norm1[l][None, :], w_in[l], sgu_ws[l], sgu_bs[l],
        w_out[l], tm=INPROJ_TM, lat_seq=dec_seq)
    olru_c, new_state, down_b = _lru(xy, *lru_params, zeros_state, ffn_down[l], row_start=0, m=n_ctx,
                                     seq=seq, nsb=LRU_CTX_SEQS)
    olru_l, _, up_b = _lru(xy, *lru_params, h0_lat, ffn_up[l], row_start=n_ctx, m=n_lat,
                           seq=dec_seq, nsb=1)
    yc, yl = _ffn(xc, xl, olru_c, olru_l, osgu, mod, g_lru[l][None, :], g_sgu[l][None, :],
                  w_out_b, norm2[l][None, :], up_b, ffn_conv_w[l],
                  ffn_conv_b[l][None, :], down_b, final_norm[None, :], tm=FFN_TM,
                  ctx_period=seq, lat_period=GRID_W, lat_seq=dec_seq)
    return (yc.reshape(batch, seq, D_MODEL), yl.reshape(dec_batch, dec_seq, D_MODEL),
            new_state[:, None])
```

```python
import functools
import math

import jax
import jax.numpy as jnp
from jax import lax
from jax.experimental import pallas as pl
from jax.experimental.pallas import tpu as pltpu

D_MODEL = 1024
LRU_HEADS = 8
LRU_WIDTH = 512
LRU_HEAD_DIM = 64
LRU_C = 8.0
SGU_GROUPS = 4
SGU_WIDTH = 512
CHUNK = 128
D_FF = 3072
N_MOD = 6
EPS = 1e-6
GRID_W = 64

LANES = 128
SUBLANES = 8
BF16_SUBLANES = 16
MOD_ROWS = 8
BF16 = jnp.bfloat16
F32 = jnp.float32

MIB = 1024 * 1024
INPROJ_TM = 1024
INPROJ_VMEM = 56 * MIB
LRU_ROW_CHUNK = 512
LRU_LANE_BLOCKS = 2
LRU_CTX_SEQS = 8
LRU_SCAN_UNROLL = True
LRU_VMEM = 56 * MIB
FFN_TM = 512
FFN_FC = 1536
FFN_ES = 256
FFN_VMEM = 60 * MIB

_GELU_C0 = math.sqrt(2.0 / math.pi)
_GELU_C1 = _GELU_C0 * 0.044715


def _rms(x, gain):
    return x * lax.rsqrt(jnp.mean(x * x, axis=-1, keepdims=True) + EPS) * gain


def _gelu(x):
    return (0.5 * x) * (1.0 + jnp.tanh(x * (_GELU_C0 + _GELU_C1 * (x * x))))


def _mod_row(mod_ref, k, row):
    return mod_ref[k, pl.ds(row, 1), :]


def _mod_kernel(cctx_ref, c_ref, w_ref, b_ref, o_ref):
    pad = jnp.zeros((MOD_ROWS - 1 - c_ref.shape[0], D_MODEL), F32)
    cc = jnp.concatenate([cctx_ref[...], c_ref[...], pad], axis=0)
    s = jax.nn.silu(cc).astype(BF16)
    o_ref[...] = jnp.dot(s, w_ref[...].astype(BF16), preferred_element_type=F32) + b_ref[...]


def _modulation(c_ctx, c, w_ada, b_ada):
    assert 1 + c.shape[0] <= MOD_ROWS
    return pl.pallas_call(
        _mod_kernel,
        grid=(N_MOD,),
        in_specs=[
            pl.BlockSpec((1, D_MODEL), lambda j: (0, 0)),
            pl.BlockSpec(c.shape, lambda j: (0, 0)),
            pl.BlockSpec((D_MODEL, D_MODEL), lambda j: (0, j)),
            pl.BlockSpec((1, D_MODEL), lambda j: (0, j)),
        ],
        out_specs=pl.BlockSpec((None, MOD_ROWS, D_MODEL), lambda j: (j, 0, 0)),
        out_shape=jax.ShapeDtypeStruct((N_MOD, MOD_ROWS, D_MODEL), F32),
        compiler_params=pltpu.CompilerParams(dimension_semantics=("arbitrary",)),
        name="modulation",
    )(c_ctx, c, w_ada, b_ada)


def _token_groups(tm, n_ctx, n_lat, lat_seq):
    assert n_ctx % tm == 0 and n_lat % tm == 0 and lat_seq % tm == 0
    nc = n_ctx // tm
    ctx_map = lambda i: (jnp.minimum(i, nc - 1), 0)
    lat_map = lambda i: (jnp.maximum(i - nc, 0), 0)
    lat_mod_row = lambda i: 1 + ((i - nc) * tm) // lat_seq
    return nc, nc + n_lat // tm, ctx_map, lat_map, lat_mod_row


def _inproj_kernel(xc_ref, xl_ref, mod_ref, n1_ref, win_f_ref, ws_ref, bs_ref, w_ref,
                   xy_ref, osgu_ref, w_b_ref, win_ref, *, tm, nc, lat_mod_row):
    i = pl.program_id(0)
    w_b_ref[...] = w_ref[...].astype(BF16)

    @pl.when(i == 0)
    def _():
        win_ref[...] = win_f_ref[...].astype(BF16)

    def body(x_ref, row):
        sh1 = _mod_row(mod_ref, 0, row)
        sc1 = _mod_row(mod_ref, 1, row)
        hb = (_rms(x_ref[...], n1_ref[...] * (1.0 + sc1)) + sh1).astype(BF16)
        uv = jnp.dot(hb, win_ref[:, 2 * LRU_WIDTH:], preferred_element_type=F32)
        xy_ref[...] = jnp.dot(hb, win_ref[:, 0:2 * LRU_WIDTH], preferred_element_type=F32)
        for g in range(SGU_GROUPS):
            lo = g * LANES
            gu = _gelu(uv[:, lo:lo + LANES])
            gv = _gelu(uv[:, SGU_WIDTH + lo:SGU_WIDTH + lo + LANES]).astype(BF16)
            bias = jnp.broadcast_to(bs_ref[g:g + 1, :], (LANES, CHUNK)).T
            ws = ws_ref[g].astype(BF16)
            for ck in range(tm // CHUNK):
                r0 = ck * CHUNK
                s = jnp.dot(ws, gv[r0:r0 + CHUNK], preferred_element_type=F32) + bias
                osgu_ref[r0:r0 + CHUNK, lo:lo + LANES] = gu[r0:r0 + CHUNK] * s

    @pl.when(i < nc)
    def _():
        body(xc_ref, 0)

    @pl.when(i >= nc)
    def _():
        body(xl_ref, lat_mod_row(i))


def _cast_slice_spec(shape, grid):
    n_steps = math.prod(grid)
    rows = shape[0]
    r = BF16_SUBLANES
    while rows % r or rows // r > n_steps:
        r += BF16_SUBLANES
    last = rows // r - 1

    def index_map(*idx):
        step = 0
        for i, n in zip(idx, grid):
            step = step * n + i
        return (jnp.minimum(step, last), 0)

    return pl.BlockSpec((r, shape[1]), index_map)


def _inproj(xc, xl, mod, norm1, w_in, ws, bs, later_weight, *, tm, lat_seq):
    nc, n, ctx_map, lat_map, lat_mod_row = _token_groups(tm, xc.shape[0], xl.shape[0], lat_seq)
    m = n * tm
    const = lambda i: (0, 0)
    cast_specs = [_cast_slice_spec(later_weight.shape, (n,))]
    return pl.pallas_call(
        functools.partial(_inproj_kernel, tm=tm, nc=nc, lat_mod_row=lat_mod_row),
        grid=(n,),
        in_specs=[
            pl.BlockSpec((tm, D_MODEL), ctx_map),
            pl.BlockSpec((tm, D_MODEL), lat_map),
            pl.BlockSpec((N_MOD, MOD_ROWS, D_MODEL), lambda i: (0, 0, 0)),
            pl.BlockSpec((1, D_MODEL), const),
            pl.BlockSpec((D_MODEL, 4 * LRU_WIDTH), const, pipeline_mode=pl.Buffered(1)),
            pl.BlockSpec((SGU_GROUPS, CHUNK, CHUNK), lambda i: (0, 0, 0)),
            pl.BlockSpec((SGU_GROUPS, CHUNK), const),
        ] + cast_specs,
        out_specs=[
            pl.BlockSpec((tm, 2 * LRU_WIDTH), lambda i: (i, 0)),
            pl.BlockSpec((tm, SGU_WIDTH), lambda i: (i, 0)),
        ] + cast_specs,
        out_shape=[
            jax.ShapeDtypeStruct((m, 2 * LRU_WIDTH), F32),
            jax.ShapeDtypeStruct((m, SGU_WIDTH), F32),
        ] + [jax.ShapeDtypeStruct(later_weight.shape, BF16)],
        scratch_shapes=[pltpu.VMEM(w_in.shape, BF16)],
        compiler_params=pltpu.CompilerParams(
            dimension_semantics=("arbitrary",), vmem_limit_bytes=INPROJ_VMEM),
        name="inproj_sgu",
    )(xc, xl, mod, norm1, w_in, ws, bs, later_weight)


def _scan_pitch(seq):
    pitch = seq // SUBLANES
    return pitch + (4 - pitch % 8) % 8


def _lru_kernel(xr_ref, yr_ref, cw_ref, cb_ref, wa_ref, ba_ref, wx_ref, bx_ref, lam_ref, h0_ref, w_ref,
                o_ref, st_ref, w_b_ref, a_sc, b_sc, p_sc, h_sc, wg_ref, bg_ref, *,
                seq, nsb, nk, pitch, rc, unroll):
    sp_rows = SUBLANES * pitch
    n_rc = (seq * nsb) // rc
    seg = min(seq, rc)
    segs = rc // seg
    chunks_per_seq = seq // seg
    w_b_ref[...] = w_ref[...].astype(BF16)

    @pl.when(pl.program_id(1) == 0)
    def _():
        zeros = jnp.zeros((LRU_HEAD_DIM, LRU_HEAD_DIM), F32)
        for k in range(nk):
            blocks = []
            for d in range(2):
                for src in (wa_ref, wx_ref):
                    top = jnp.concatenate([src[d, 2 * k], zeros], axis=1)
                    bot = jnp.concatenate([zeros, src[d, 2 * k + 1]], axis=1)
                    blocks.append(jnp.concatenate([top, bot], axis=0))
            wg_ref[k] = jnp.concatenate(blocks, axis=1).astype(BF16)
            lanes = slice(k * LANES, (k + 1) * LANES)
            bg_ref[k] = 0.5 * jnp.concatenate(
                [ba_ref[0:1, lanes], bx_ref[0:1, lanes], ba_ref[1:2, lanes], bx_ref[1:2, lanes]], axis=1)

    c4_all = (-0.5 * LRU_C) * jax.nn.softplus(-lam_ref[...])
    cw_all = 0.5 * cw_ref[...]
    cb_all = 0.5 * cb_ref[...]

    for s in range(nsb):
        lo, hi = s * sp_rows + seq, (s + 1) * sp_rows
        for slab in range(2 * nk):
            a_sc[slab, lo:hi, :] = jnp.ones((hi - lo, LANES), F32)
            b_sc[slab, lo:hi, :] = jnp.zeros((hi - lo, LANES), F32)

    def scratch_row(ci, k):
        if chunks_per_seq > 1:
            return pl.multiple_of((ci // chunks_per_seq) * sp_rows + (ci % chunks_per_seq) * seg,
                                  SUBLANES)
        return pl.multiple_of((ci * segs + k) * sp_rows, SUBLANES)

    def conv_segment(xs, prev, nxt, cw, cb):
        ext = jnp.concatenate([prev, xs, nxt], axis=0)
        n_ext = seg + 2 * SUBLANES
        xc = cb + cw[2:3] * xs
        xc = xc + cw[0:1] * pltpu.roll(ext, 2, axis=0)[SUBLANES:SUBLANES + seg]
        xc = xc + cw[1:2] * pltpu.roll(ext, 1, axis=0)[SUBLANES:SUBLANES + seg]
        return xc + cw[3:4] * pltpu.roll(ext, n_ext - 1, axis=0)[SUBLANES:SUBLANES + seg]

    def gates(ci, _):
        base = pl.multiple_of(ci * rc, rc)
        for k in range(nk):
            lanes = slice(k * LANES, (k + 1) * LANES)
            cw, cb, c4 = cw_all[:, lanes], cb_all[:, lanes], c4_all[:, lanes]
            xm = xr_ref[pl.ds(base, rc), lanes]
            if chunks_per_seq > 1:
                cs = ci % chunks_per_seq
                p0 = pl.multiple_of(jnp.maximum(base - SUBLANES, 0), SUBLANES)
                n0 = pl.multiple_of(jnp.minimum(base + rc, seq * nsb - SUBLANES), SUBLANES)
                prev = jnp.where(cs == 0, 0.0, xr_ref[pl.ds(p0, SUBLANES), lanes])
                nxt = jnp.where(cs == chunks_per_seq - 1, 0.0, xr_ref[pl.ds(n0, SUBLANES), lanes])
                hxc = conv_segment(xm, prev, nxt, cw, cb)
            else:
                zeros = jnp.zeros((SUBLANES, LANES), F32)
                hxc = jnp.concatenate(
                    [conv_segment(xm[j * seg:(j + 1) * seg], zeros, zeros, cw, cb) for j in range(segs)],
                    axis=0)
            g = jnp.dot(hxc.astype(BF16), wg_ref[k], preferred_element_type=F32) + bg_ref[k]
            for d in range(2):
                th_r = jnp.tanh(g[:, (2 * d) * LANES:(2 * d + 1) * LANES])
                th_i = jnp.tanh(g[:, (2 * d + 1) * LANES:(2 * d + 2) * LANES])
                log_a = c4[d:d + 1] * th_r + c4[d:d + 1]
                a = jnp.exp(log_a)
                om = jnp.tanh(log_a) * (-1.0 - a * a)
                root = jnp.where(om > 0.0, om * lax.rsqrt(om), 0.0)
                b = root * ((th_i + 1.0) * hxc)
                for j in range(segs):
                    dst = scratch_row(ci, j)
                    a_sc[d * nk + k, pl.ds(dst, seg), :] = a[j * seg:(j + 1) * seg]
                    b_sc[d * nk + k, pl.ds(dst, seg), :] = b[j * seg:(j + 1) * seg]
        return 0

    lax.fori_loop(0, n_rc, gates, 0)

    def scan_sequence(s, _):
        off = s * sp_rows

        def rows(t):
            return pl.ds(off + t, SUBLANES, stride=pitch)

        def local(t, carry):
            out = []
            for slab, (h, p) in enumerate(carry):
                tt = t if slab < nk else pitch - 1 - t
                a = a_sc[slab, rows(tt), :]
                h = a * h + b_sc[slab, rows(tt), :]
                p = a * p
                h_sc[slab, rows(tt), :] = h
                p_sc[slab, rows(tt), :] = p
                out.append((h, p))
            return tuple(out)

        zero = jnp.zeros((SUBLANES, LANES), F32)
        one = jnp.ones((SUBLANES, LANES), F32)
        ends = lax.fori_loop(0, pitch, local, ((zero, one),) * (2 * nk), unroll=unroll)

        h0 = h0_ref[s]
        starts = []
        finals = [[None] * nk, [None] * nk]
        for slab, (h, p) in enumerate(ends):
            d, k = divmod(slab, nk)
            c = h0[d:d + 1, k * LANES:(k + 1) * LANES]
            cs = [None] * SUBLANES
            for j in (range(SUBLANES) if d == 0 else reversed(range(SUBLANES))):
                cs[j] = c
                c = h[j:j + 1] + p[j:j + 1] * c
            finals[d][k] = c
            starts.append(jnp.concatenate(cs, axis=0))
        st_ref[s] = jnp.concatenate(
            [jnp.concatenate(finals[0], axis=1), jnp.concatenate(finals[1], axis=1)], axis=0)

        def fix(t, _):
            for k in range(nk):
                a_sc[k, rows(t), :] = (h_sc[k, rows(t), :] + p_sc[k, rows(t), :] * starts[k]
                                       + h_sc[nk + k, rows(t), :] + p_sc[nk + k, rows(t), :] * starts[nk + k])
            return 0

        lax.fori_loop(0, pitch, fix, 0, unroll=unroll)
        return 0

    if nsb == 1:
        scan_sequence(0, 0)
    else:
        lax.fori_loop(0, nsb, scan_sequence, 0)

    def gate_out(ci, _):
        base = pl.multiple_of(ci * rc, rc)
        for k in range(nk):
            lanes = slice(k * LANES, (k + 1) * LANES)
            gy = _gelu(yr_ref[pl.ds(base, rc), lanes])
            for j in range(segs):
                src = scratch_row(ci, j)
                o_ref[pl.ds(base + j * seg, seg), lanes] = (gy[j * seg:(j + 1) * seg]
                                                            * a_sc[k, pl.ds(src, seg), :])
        return 0

    lax.fori_loop(0, n_rc, gate_out, 0)


def _lru(xy, conv_w, conv_b, wa, ba, wx, bx, lam, h0, later_weight, *, row_start, m, seq, nsb):
    nblk = m // (seq * nsb)
    assert row_start % (seq * nsb) == 0
    blk0 = row_start // (seq * nsb)
    nk = LRU_LANE_BLOCKS
    width = nk * LANES
    ncb = LRU_WIDTH // width
    pitch = _scan_pitch(seq)
    sc_rows = nsb * SUBLANES * pitch
    rows = seq * nsb
    cast_spec = _cast_slice_spec(later_weight.shape, (ncb, nblk))
    return pl.pallas_call(
        functools.partial(_lru_kernel, seq=seq, nsb=nsb, nk=nk, pitch=pitch,
                          rc=LRU_ROW_CHUNK, unroll=LRU_SCAN_UNROLL),
        grid=(ncb, nblk),
        in_specs=[
            pl.BlockSpec((rows, width), lambda j, b: (blk0 + b, j)),
            pl.BlockSpec((rows, width), lambda j, b: (blk0 + b, ncb + j)),
            pl.BlockSpec((4, width), lambda j, b: (0, j)),
            pl.BlockSpec((1, width), lambda j, b: (0, j)),
            pl.BlockSpec((2, 2 * nk, LRU_HEAD_DIM, LRU_HEAD_DIM), lambda j, b: (0, j, 0, 0)),
            pl.BlockSpec((2, width), lambda j, b: (0, j)),
            pl.BlockSpec((2, 2 * nk, LRU_HEAD_DIM, LRU_HEAD_DIM), lambda j, b: (0, j, 0, 0)),
            pl.BlockSpec((2, width), lambda j, b: (0, j)),
            pl.BlockSpec((2, width), lambda j, b: (0, j)),
            pl.BlockSpec((nsb, 2, width), lambda j, b: (b, 0, j)),
            cast_spec,
        ],
        out_specs=[
            pl.BlockSpec((rows, width), lambda j, b: (b, j)),
            pl.BlockSpec((nsb, 2, width), lambda j, b: (b, 0, j)),
            cast_spec,
        ],
        out_shape=[
            jax.ShapeDtypeStruct((m, LRU_WIDTH), F32),
            jax.ShapeDtypeStruct((m // seq, 2, LRU_WIDTH), F32),
            jax.ShapeDtypeStruct(later_weight.shape, BF16),
        ],
        scratch_shapes=[pltpu.VMEM((2 * nk, sc_rows, LANES), F32)] * 4 + [
            pltpu.VMEM((nk, LANES, 4 * LANES), BF16), pltpu.VMEM((nk, 1, 4 * LANES), F32)],
        compiler_params=pltpu.CompilerParams(
            dimension_semantics=("arbitrary", "arbitrary"), vmem_limit_bytes=LRU_VMEM),
        name="rglru",
    )(xy, xy, conv_w, conv_b, wa, ba, wx, bx, lam, h0, later_weight)


GROUP = SUBLANES * SUBLANES


def _swap_rows(slab_ref, val, r0):
    n = val.shape[0]
    for k in range(D_MODEL // LANES):
        slab_ref[k, r0:r0 + n, :] = val[:, k * LANES:(k + 1) * LANES]
    cols = []
    for k in range(D_MODEL // LANES):
        rows = [slab_ref[k, pl.ds(r0 + g * GROUP + t, SUBLANES, stride=SUBLANES), :]
                for g in range(n // GROUP) for t in range(SUBLANES)]
        cols.append(jnp.concatenate(rows, axis=0))
    return jnp.concatenate(cols, axis=1)


def _ffn_tile(x_ref, ol_ref, os_ref, mod_ref, gl_ref, gs_ref, wout_ref, n2_ref, up_ref,
              cw_ref, cb_ref, down_ref, fn_ref, y_ref, slab_sc, *, tm, fc, es, period, row):
    g1 = _mod_row(mod_ref, 2, row)
    sh2 = _mod_row(mod_ref, 3, row)
    sc2 = _mod_row(mod_ref, 4, row)
    g2 = _mod_row(mod_ref, 5, row)
    nl = _rms(ol_ref[...], gl_ref[...]).astype(BF16)
    ns = _rms(os_ref[...], gs_ref[...]).astype(BF16)
    o = (jnp.dot(nl, wout_ref[0:LRU_WIDTH, :], preferred_element_type=F32)
         + jnp.dot(ns, wout_ref[LRU_WIDTH:LRU_WIDTH + SGU_WIDTH, :], preferred_element_type=F32))
    x1 = x_ref[...] + g1 * o
    h2f = _rms(x1, n2_ref[...] * (1.0 + sc2)) + sh2
    h2 = _swap_rows(slab_sc, h2f, 0).astype(BF16)

    sub = lax.broadcasted_iota(jnp.int32, (SUBLANES, fc), 0)
    groups_per_period = period // GROUP
    n_groups = tm // GROUP

    def conv(z, c0, scale=None):
        w = cw_ref[:, c0:c0 + fc]
        b = cb_ref[:, c0:c0 + fc]
        if scale is not None:
            w, b = scale * w, scale * b
        w0, w1, w2 = w[0:1], w[1:2], w[2:3]
        z8 = [[z[g * GROUP + t * SUBLANES:g * GROUP + (t + 1) * SUBLANES] for t in range(SUBLANES)]
              for g in range(n_groups)]
        outs = []
        for g in range(n_groups):
            cur = z8[g]
            lo = pltpu.roll(cur[SUBLANES - 1], 1, axis=0)
            if g % groups_per_period == 0:
                lo = jnp.where(sub == 0, 0.0, lo)
            else:
                lo = jnp.where(sub == 0, pltpu.roll(z8[g - 1][SUBLANES - 1], 1, axis=0), lo)
            hi = pltpu.roll(cur[0], SUBLANES - 1, axis=0)
            if g % groups_per_period == groups_per_period - 1:
                hi = jnp.where(sub == SUBLANES - 1, 0.0, hi)
            else:
                hi = jnp.where(sub == SUBLANES - 1, pltpu.roll(z8[g + 1][0], SUBLANES - 1, axis=0), hi)
            zm = [lo] + cur[:SUBLANES - 1]
            zp = cur[1:] + [hi]
            for t in range(SUBLANES):
                outs.append(b + w0 * zm[t] + w1 * cur[t] + w2 * zp[t])
        return jnp.concatenate(outs, axis=0)

    def up(f0):
        zg = jnp.dot(h2, up_ref[:, f0:f0 + fc], preferred_element_type=F32)
        zv = jnp.dot(h2, up_ref[:, D_FF + f0:D_FF + f0 + fc], preferred_element_type=F32)
        return zg, zv

    acc = jnp.zeros((tm, D_MODEL), F32)
    nxt = up(0)
    for f0 in range(0, D_FF, fc):
        zg, zv = nxt
        if f0 + fc < D_FF:
            nxt = up(f0 + fc)
        hg = conv(zg, f0, scale=0.5)
        act = (hg * (jnp.tanh(hg) + 1.0) * conv(zv, D_FF + f0)).astype(BF16)
        if f0 + fc < D_FF:
            acc = acc + jnp.dot(act, down_ref[f0:f0 + fc, :], preferred_element_type=F32)
        else:
            for r0 in range(0, tm, es):
                a = acc[r0:r0 + es] + jnp.dot(act[r0:r0 + es], down_ref[f0:f0 + fc, :],
                                              preferred_element_type=F32)
                x2 = x1[r0:r0 + es] + g2 * _swap_rows(slab_sc, a, r0)
                y_ref[r0:r0 + es, :] = _rms(x2, fn_ref[...])


def _ffn_kernel(xc_ref, xl_ref, olc_ref, oll_ref, os_ref, mod_ref, gl_ref, gs_ref, wout_ref, n2_ref,
                up_ref, cw_ref, cb_ref, down_ref, fn_ref, yc_ref, yl_ref, slab_sc, *,
                nc, lat_mod_row, ctx_period, lat_period, **tile):
    i = pl.program_id(0)
    shared = (mod_ref, gl_ref, gs_ref, wout_ref, n2_ref, up_ref, cw_ref, cb_ref, down_ref, fn_ref)

    @pl.when(i < nc)
    def _():
        _ffn_tile(xc_ref, olc_ref, os_ref, *shared, yc_ref, slab_sc, period=ctx_period, row=0, **tile)

    @pl.when(i >= nc)
    def _():
        _ffn_tile(xl_ref, oll_ref, os_ref, *shared, yl_ref, slab_sc, period=lat_period,
                  row=lat_mod_row(i), **tile)


def _ffn(xc, xl, olru_c, olru_l, osgu, mod, g_lru, g_sgu, w_out, norm2, ffn_up, conv_w, conv_b,
         ffn_down, final_norm, *, tm, ctx_period, lat_period, lat_seq):
    nc, n, ctx_map, lat_map, lat_mod_row = _token_groups(tm, xc.shape[0], xl.shape[0], lat_seq)
    const = lambda i: (0, 0)
    resident = lambda shape: pl.BlockSpec(shape, const, pipeline_mode=pl.Buffered(1))
    return pl.pallas_call(
        functools.partial(_ffn_kernel, nc=nc, lat_mod_row=lat_mod_row, ctx_period=ctx_period,
                          lat_period=lat_period, tm=tm, fc=FFN_FC, es=FFN_ES),
        grid=(n,),
        in_specs=[
            pl.BlockSpec((tm, D_MODEL), ctx_map),
            pl.BlockSpec((tm, D_MODEL), lat_map),
            pl.BlockSpec((tm, LRU_WIDTH), ctx_map),
            pl.BlockSpec((tm, LRU_WIDTH), lat_map),
            pl.BlockSpec((tm, SGU_WIDTH), lambda i: (i, 0)),
            pl.BlockSpec((N_MOD, MOD_ROWS, D_MODEL), lambda i: (0, 0, 0)),
            pl.BlockSpec((1, LRU_WIDTH), const),
            pl.BlockSpec((1, SGU_WIDTH), const),
            resident((LRU_WIDTH + SGU_WIDTH, D_MODEL)),
            pl.BlockSpec((1, D_MODEL), const),
            resident((D_MODEL, 2 * D_FF)),
            pl.BlockSpec((3, 2 * D_FF), const),
            pl.BlockSpec((1, 2 * D_FF), const),
            resident((D_FF, D_MODEL)),
            pl.BlockSpec((1, D_MODEL), const),
        ],
        out_specs=[pl.BlockSpec((tm, D_MODEL), ctx_map), pl.BlockSpec((tm, D_MODEL), lat_map)],
        out_shape=[jax.ShapeDtypeStruct(xc.shape, F32), jax.ShapeDtypeStruct(xl.shape, F32)],
        scratch_shapes=[pltpu.VMEM((D_MODEL // LANES, tm, LANES), F32)],
        compiler_params=pltpu.CompilerParams(
            dimension_semantics=("arbitrary",), vmem_limit_bytes=FFN_VMEM),
        name="outproj_ffn",
    )(xc, xl, olru_c, olru_l, osgu, mod, g_lru, g_sgu, w_out, norm2, ffn_up, conv_w, conv_b, ffn_down,
      final_norm)


def kernel(x_prompt, x_sample, state_lru, c, c_ctx, norm1, norm2, w_ada, b_ada, w_in, lru_conv_w,
           lru_conv_b, lru_wa, lru_ba, lru_wx, lru_bx, lru_lam, sgu_ws, sgu_bs, g_lru, g_sgu, w_out,
           ffn_up, ffn_conv_w, ffn_conv_b, ffn_down, final_norm):
    batch, seq, _ = x_prompt.shape
    dec_batch, dec_seq, _ = x_sample.shape
    depth = norm1.shape[0]
    assert depth == 1, "the final norm is fused into the (single) layer's last kernel"
    l = 0

    xc = x_prompt.reshape(batch * seq, D_MODEL)
    xl = x_sample.reshape(dec_batch * dec_seq, D_MODEL)
    n_ctx, n_lat = xc.shape[0], xl.shape[0]
    zeros_state = jnp.zeros((batch, 2, LRU_WIDTH), x_prompt.dtype)

    mod = _modulation(c_ctx[None, :], c, w_ada[l], b_ada[l][None, :])
    lru_params = (lru_conv_w[l], lru_conv_b[l][None, :], lru_wa[l], lru_ba[l], lru_wx[l], lru_bx[l],
                  lru_lam[l])
    h0_lat = state_lru.reshape(dec_batch, 2, LRU_WIDTH)

    xy, osgu, w_out_b = _inproj(
        xc, xl, mod, norm1[l][None, :], w_in[l], sgu_ws[l], sgu_bs[l],
        w_out[l], tm=INPROJ_TM, lat_seq=dec_seq)
    olru_c, new_state, down_b = _lru(xy, *lru_params, zeros_state, ffn_down[l], row_start=0, m=n_ctx,
                                     seq=seq, nsb=LRU_CTX_SEQS)
    olru_l, _, up_b = _lru(xy, *lru_params, h0_lat, ffn_up[l], row_start=n_ctx, m=n_lat,
                           seq=dec_seq, nsb=1)
    yc, yl = _ffn(xc, xl, olru_c, olru_l, osgu, mod, g_lru[l][None, :], g_sgu[l][None, :],
                  w_out_b, norm2[l][None, :], up_b, ffn_conv_w[l],
                  ffn_conv_b[l][None, :], down_b, final_norm[None, :], tm=FFN_TM,
                  ctx_period=seq, lat_period=GRID_W, lat_seq=dec_seq)
    return (yc.reshape(batch, seq, D_MODEL), yl.reshape(dec_batch, dec_seq, D_MODEL),
            new_state[:, None])
```

```python
import functools
import math

import jax
import jax.numpy as jnp
from jax import lax
from jax.experimental import pallas as pl
from jax.experimental.pallas import tpu as pltpu

D_MODEL = 1024
LRU_HEADS = 8
LRU_WIDTH = 512
LRU_HEAD_DIM = 64
LRU_C = 8.0
SGU_GROUPS = 4
SGU_WIDTH = 512
CHUNK = 128
D_FF = 3072
N_MOD = 6
EPS = 1e-6
GRID_W = 64

LANES = 128
SUBLANES = 8
BF16_SUBLANES = 16
MOD_ROWS = 8
BF16 = jnp.bfloat16
F32 = jnp.float32

MIB = 1024 * 1024
INPROJ_TM = 1024
INPROJ_VMEM = 56 * MIB
LRU_ROW_CHUNK = 512
LRU_LANE_BLOCKS = 2
LRU_CTX_SEQS = 8
LRU_SCAN_UNROLL = True
LRU_VMEM = 56 * MIB
FFN_TM = 512
FFN_FC = 1536
FFN_ES = 256
FFN_VMEM = 60 * MIB

_GELU_C0 = math.sqrt(2.0 / math.pi)
_GELU_C1 = _GELU_C0 * 0.044715


def _rms(x, gain):
    return x * lax.rsqrt(jnp.mean(x * x, axis=-1, keepdims=True) + EPS) * gain


def _gelu(x):
    return (0.5 * x) * (1.0 + jnp.tanh(x * (_GELU_C0 + _GELU_C1 * (x * x))))


def _mod_row(mod_ref, k, row):
    return mod_ref[k, pl.ds(row, 1), :]


def _mod_kernel(cctx_ref, c_ref, w_ref, b_ref, o_ref):
    pad = jnp.zeros((MOD_ROWS - 1 - c_ref.shape[0], D_MODEL), F32)
    cc = jnp.concatenate([cctx_ref[...], c_ref[...], pad], axis=0)
    s = jax.nn.silu(cc).astype(BF16)
    o_ref[...] = jnp.dot(s, w_ref[...].astype(BF16), preferred_element_type=F32) + b_ref[...]


def _modulation(c_ctx, c, w_ada, b_ada):
    assert 1 + c.shape[0] <= MOD_ROWS
    return pl.pallas_call(
        _mod_kernel,
        grid=(N_MOD,),
        in_specs=[
            pl.BlockSpec((1, D_MODEL), lambda j: (0, 0)),
            pl.BlockSpec(c.shape, lambda j: (0, 0)),
            pl.BlockSpec((D_MODEL, D_MODEL), lambda j: (0, j)),
            pl.BlockSpec((1, D_MODEL), lambda j: (0, j)),
        ],
        out_specs=pl.BlockSpec((None, MOD_ROWS, D_MODEL), lambda j: (j, 0, 0)),
        out_shape=jax.ShapeDtypeStruct((N_MOD, MOD_ROWS, D_MODEL), F32),
        compiler_params=pltpu.CompilerParams(dimension_semantics=("arbitrary",)),
        name="modulation",
    )(c_ctx, c, w_ada, b_ada)


def _token_groups(tm, n_ctx, n_lat, lat_seq):
    assert n_ctx % tm == 0 and n_lat % tm == 0 and lat_seq % tm == 0
    nc = n_ctx // tm
    ctx_map = lambda i: (jnp.minimum(i, nc - 1), 0)
    lat_map = lambda i: (jnp.maximum(i - nc, 0), 0)
    lat_mod_row = lambda i: 1 + ((i - nc) * tm) // lat_seq
    return nc, nc + n_lat // tm, ctx_map, lat_map, lat_mod_row


def _inproj_kernel(xc_ref, xl_ref, mod_ref, n1_ref, win_f_ref, ws_ref, bs_ref, w_ref,
                   xy_ref, osgu_ref, w_b_ref, win_ref, *, tm, nc, lat_mod_row):
    i = pl.program_id(0)
    w_b_ref[...] = w_ref[...].astype(BF16)

    @pl.when(i == 0)
    def _():
        win_ref[...] = win_f_ref[...].astype(BF16)

    def body(x_ref, row):
        sh1 = _mod_row(mod_ref, 0, row)
        sc1 = _mod_row(mod_ref, 1, row)
        hb = (_rms(x_ref[...], n1_ref[...] * (1.0 + sc1)) + sh1).astype(BF16)
        uv = jnp.dot(hb, win_ref[:, 2 * LRU_WIDTH:], preferred_element_type=F32)
        xy_ref[...] = jnp.dot(hb, win_ref[:, 0:2 * LRU_WIDTH], preferred_element_type=F32)
        for g in range(SGU_GROUPS):
            lo = g * LANES
            gu = _gelu(uv[:, lo:lo + LANES])
            gv = _gelu(uv[:, SGU_WIDTH + lo:SGU_WIDTH + lo + LANES]).astype(BF16)
            bias = jnp.broadcast_to(bs_ref[g:g + 1, :], (LANES, CHUNK)).T
            ws = ws_ref[g].astype(BF16)
            for ck in range(tm // CHUNK):
                r0 = ck * CHUNK
                s = jnp.dot(ws, gv[r0:r0 + CHUNK], preferred_element_type=F32) + bias
                osgu_ref[r0:r0 + CHUNK, lo:lo + LANES] = gu[r0:r0 + CHUNK] * s

    @pl.when(i < nc)
    def _():
        body(xc_ref, 0)

    @pl.when(i >= nc)
    def _():
        body(xl_ref, lat_mod_row(i))


def _cast_slice_spec(shape, grid):
    n_steps = math.prod(grid)
    rows = shape[0]
    r = BF16_SUBLANES
    while rows % r or rows // r > n_steps:
        r += BF16_SUBLANES
    last = rows // r - 1

    def index_map(*idx):
        step = 0
        for i, n in zip(idx, grid):
            step = step * n + i
        return (jnp.minimum(step, last), 0)

    return pl.BlockSpec((r, shape[1]), index_map)


def _inproj(xc, xl, mod, norm1, w_in, ws, bs, later_weight, *, tm, lat_seq):
    nc, n, ctx_map, lat_map, lat_mod_row = _token_groups(tm, xc.shape[0], xl.shape[0], lat_seq)
    m = n * tm
    const = lambda i: (0, 0)
    cast_specs = [_cast_slice_spec(later_weight.shape, (n,))]
    return pl.pallas_call(
        functools.partial(_inproj_kernel, tm=tm, nc=nc, lat_mod_row=lat_mod_row),
        grid=(n,),
        in_specs=[
            pl.BlockSpec((tm, D_MODEL), ctx_map),
            pl.BlockSpec((tm, D_MODEL), lat_map),
            pl.BlockSpec((N_MOD, MOD_ROWS, D_MODEL), lambda i: (0, 0, 0)),
            pl.BlockSpec((1, D_MODEL), const),
            pl.BlockSpec((D_MODEL, 4 * LRU_WIDTH), const, pipeline_mode=pl.Buffered(1)),
            pl.BlockSpec((SGU_GROUPS, CHUNK, CHUNK), lambda i: (0, 0, 0)),
            pl.BlockSpec((SGU_GROUPS, CHUNK), const),
        ] + cast_specs,
        out_specs=[
            pl.BlockSpec((tm, 2 * LRU_WIDTH), lambda i: (i, 0)),
            pl.BlockSpec((tm, SGU_WIDTH), lambda i: (i, 0)),
        ] + cast_specs,
        out_shape=[
            jax.ShapeDtypeStruct((m, 2 * LRU_WIDTH), F32),
            jax.ShapeDtypeStruct((m, SGU_WIDTH), F32),
        ] + [jax.ShapeDtypeStruct(later_weight.shape, BF16)],
        scratch_shapes=[pltpu.VMEM(w_in.shape, BF16)],
        compiler_params=pltpu.CompilerParams(
            dimension_semantics=("arbitrary",), vmem_limit_bytes=INPROJ_VMEM),
        name="inproj_sgu",
    )(xc, xl, mod, norm1, w_in, ws, bs, later_weight)


def _scan_pitch(seq):
    pitch = seq // SUBLANES
    return pitch + (4 - pitch % 8) % 8


def _lru_kernel(xr_ref, yr_ref, cw_ref, cb_ref, wa_ref, ba_ref, wx_ref, bx_ref, lam_ref, h0_ref, w_ref,
                o_ref, st_ref, w_b_ref, a_sc, b_sc, p_sc, h_sc, wg_ref, bg_ref, *,
                seq, nsb, nk, pitch, rc, unroll):
    sp_rows = SUBLANES * pitch
    n_rc = (seq * nsb) // rc
    seg = min(seq, rc)
    segs = rc // seg
    chunks_per_seq = seq // seg
    w_b_ref[...] = w_ref[...].astype(BF16)

    @pl.when(pl.program_id(1) == 0)
    def _():
        zeros = jnp.zeros((LRU_HEAD_DIM, LRU_HEAD_DIM), F32)
        for k in range(nk):
            blocks = []
            for d in range(2):
                for src in (wa_ref, wx_ref):
                    top = jnp.concatenate([src[d, 2 * k], zeros], axis=1)
                    bot = jnp.concatenate([zeros, src[d, 2 * k + 1]], axis=1)
                    blocks.append(jnp.concatenate([top, bot], axis=0))
            wg_ref[k] = jnp.concatenate(blocks, axis=1).astype(BF16)
            lanes = slice(k * LANES, (k + 1) * LANES)
            bg_ref[k] = 0.5 * jnp.concatenate(
                [ba_ref[0:1, lanes], bx_ref[0:1, lanes], ba_ref[1:2, lanes], bx_ref[1:2, lanes]], axis=1)

    c4_all = (-0.5 * LRU_C) * jax.nn.softplus(-lam_ref[...])
    cw_all = 0.5 * cw_ref[...]
    cb_all = 0.5 * cb_ref[...]

    for s in range(nsb):
        lo, hi = s * sp_rows + seq, (s + 1) * sp_rows
        for slab in range(2 * nk):
            a_sc[slab, lo:hi, :] = jnp.ones((hi - lo, LANES), F32)
            b_sc[slab, lo:hi, :] = jnp.zeros((hi - lo, LANES), F32)

    def scratch_row(ci, k):
        if chunks_per_seq > 1:
            return pl.multiple_of((ci // chunks_per_seq) * sp_rows + (ci % chunks_per_seq) * seg,
                                  SUBLANES)
        return pl.multiple_of((ci * segs + k) * sp_rows, SUBLANES)

    def conv_segment(xs, prev, nxt, cw, cb):
        ext = jnp.concatenate([prev, xs, nxt], axis=0)
        n_ext = seg + 2 * SUBLANES
        xc = cb + cw[2:3] * xs
        xc = xc + cw[0:1] * pltpu.roll(ext, 2, axis=0)[SUBLANES:SUBLANES + seg]
        xc = xc + cw[1:2] * pltpu.roll(ext, 1, axis=0)[SUBLANES:SUBLANES + seg]
        return xc + cw[3:4] * pltpu.roll(ext, n_ext - 1, axis=0)[SUBLANES:SUBLANES + seg]

    def gates(ci, _):
        base = pl.multiple_of(ci * rc, rc)
        for k in range(nk):
            lanes = slice(k * LANES, (k + 1) * LANES)
            cw, cb, c4 = cw_all[:, lanes], cb_all[:, lanes], c4_all[:, lanes]
            xm = xr_ref[pl.ds(base, rc), lanes]
            if chunks_per_seq > 1:
                cs = ci % chunks_per_seq
                p0 = pl.multiple_of(jnp.maximum(base - SUBLANES, 0), SUBLANES)
                n0 = pl.multiple_of(jnp.minimum(base + rc, seq * nsb - SUBLANES), SUBLANES)
                prev = jnp.where(cs == 0, 0.0, xr_ref[pl.ds(p0, SUBLANES), lanes])
                nxt = jnp.where(cs == chunks_per_seq - 1, 0.0, xr_ref[pl.ds(n0, SUBLANES), lanes])
                hxc = conv_segment(xm, prev, nxt, cw, cb)
            else:
                zeros = jnp.zeros((SUBLANES, LANES), F32)
                hxc = jnp.concatenate(
                    [conv_segment(xm[j * seg:(j + 1) * seg], zeros, zeros, cw, cb) for j in range(segs)],
                    axis=0)
            g = jnp.dot(hxc.astype(BF16), wg_ref[k], preferred_element_type=F32) + bg_ref[k]
            for d in range(2):
                th_r = jnp.tanh(g[:, (2 * d) * LANES:(2 * d + 1) * LANES])
                th_i = jnp.tanh(g[:, (2 * d + 1) * LANES:(2 * d + 2) * LANES])
                log_a = c4[d:d + 1] * th_r + c4[d:d + 1]
                a = jnp.exp(log_a)
                om = jnp.tanh(log_a) * (-1.0 - a * a)
                root = jnp.where(om > 0.0, om * lax.rsqrt(om), 0.0)
                b = root * ((th_i + 1.0) * hxc)
                for j in range(segs):
                    dst = scratch_row(ci, j)
                    a_sc[d * nk + k, pl.ds(dst, seg), :] = a[j * seg:(j + 1) * seg]
                    b_sc[d * nk + k, pl.ds(dst, seg), :] = b[j * seg:(j + 1) * seg]
        return 0

    lax.fori_loop(0, n_rc, gates, 0)

    def scan_sequence(s, _):
        off = s * sp_rows

        def rows(t):
            return pl.ds(off + t, SUBLANES, stride=pitch)

        def local(t, carry):
            out = []
            for slab, (h, p) in enumerate(carry):
                tt = t if slab < nk else pitch - 1 - t
                a = a_sc[slab, rows(tt), :]
                h = a * h + b_sc[slab, rows(tt), :]
                p = a * p
                h_sc[slab, rows(tt), :] = h
                p_sc[slab, rows(tt), :] = p
                out.append((h, p))
            return tuple(out)

        zero = jnp.zeros((SUBLANES, LANES), F32)
        one = jnp.ones((SUBLANES, LANES), F32)
        ends = lax.fori_loop(0, pitch, local, ((zero, one),) * (2 * nk), unroll=unroll)

        h0 = h0_ref[s]
        starts = []
        finals = [[None] * nk, [None] * nk]
        for slab, (h, p) in enumerate(ends):
            d, k = divmod(slab, nk)
            c = h0[d:d + 1, k * LANES:(k + 1) * LANES]
            cs = [None] * SUBLANES
            for j in (range(SUBLANES) if d == 0 else reversed(range(SUBLANES))):
                cs[j] = c
                c = h[j:j + 1] + p[j:j + 1] * c
            finals[d][k] = c
            starts.append(jnp.concatenate(cs, axis=0))
        st_ref[s] = jnp.concatenate(
            [jnp.concatenate(finals[0], axis=1), jnp.concatenate(finals[1], axis=1)], axis=0)

        def fix(t, _):
            for k in range(nk):
                a_sc[k, rows(t), :] = (h_sc[k, rows(t), :] + p_sc[k, rows(t), :] * starts[k]
                                       + h_sc[nk + k, rows(t), :] + p_sc[nk + k, rows(t), :] * starts[nk + k])
            return 0

        lax.fori_loop(0, pitch, fix, 0, unroll=unroll)
        return 0

    if nsb == 1:
        scan_sequence(0, 0)
    else:
        lax.fori_loop(0, nsb, scan_sequence, 0)

    def gate_out(ci, _):
        base = pl.multiple_of(ci * rc, rc)
        for k in range(nk):
            lanes = slice(k * LANES, (k + 1) * LANES)
            gy = _gelu(yr_ref[pl.ds(base, rc), lanes])
            for j in range(segs):
                src = scratch_row(ci, j)
                o_ref[pl.ds(base + j * seg, seg), lanes] = (gy[j * seg:(j + 1) * seg]
                                                            * a_sc[k, pl.ds(src, seg), :])
        return 0

    lax.fori_loop(0, n_rc, gate_out, 0)


def _lru(xy, conv_w, conv_b, wa, ba, wx, bx, lam, h0, later_weight, *, row_start, m, seq, nsb):
    nblk = m // (seq * nsb)
    assert row_start % (seq * nsb) == 0
    blk0 = row_start // (seq * nsb)
    nk = LRU_LANE_BLOCKS
    width = nk * LANES
    ncb = LRU_WIDTH // width
    pitch = _scan_pitch(seq)
    sc_rows = nsb * SUBLANES * pitch
    rows = seq * nsb
    cast_spec = _cast_slice_spec(later_weight.shape, (ncb, nblk))
    return pl.pallas_call(
        functools.partial(_lru_kernel, seq=seq, nsb=nsb, nk=nk, pitch=pitch,
                          rc=LRU_ROW_CHUNK, unroll=LRU_SCAN_UNROLL),
        grid=(ncb, nblk),
        in_specs=[
            pl.BlockSpec((rows, width), lambda j, b: (blk0 + b, j)),
            pl.BlockSpec((rows, width), lambda j, b: (blk0 + b, ncb + j)),
            pl.BlockSpec((4, width), lambda j, b: (0, j)),
            pl.BlockSpec((1, width), lambda j, b: (0, j)),
            pl.BlockSpec((2, 2 * nk, LRU_HEAD_DIM, LRU_HEAD_DIM), lambda j, b: (0, j, 0, 0)),
            pl.BlockSpec((2, width), lambda j, b: (0, j)),
            pl.BlockSpec((2, 2 * nk, LRU_HEAD_DIM, LRU_HEAD_DIM), lambda j, b: (0, j, 0, 0)),
            pl.BlockSpec((2, width), lambda j, b: (0, j)),
            pl.BlockSpec((2, width), lambda j, b: (0, j)),
            pl.BlockSpec((nsb, 2, width), lambda j, b: (b, 0, j)),
            cast_spec,
        ],
        out_specs=[
            pl.BlockSpec((rows, width), lambda j, b: (b, j)),
            pl.BlockSpec((nsb, 2, width), lambda j, b: (b, 0, j)),
            cast_spec,
        ],
        out_shape=[
            jax.ShapeDtypeStruct((m, LRU_WIDTH), F32),
            jax.ShapeDtypeStruct((m // seq, 2, LRU_WIDTH), F32),
            jax.ShapeDtypeStruct(later_weight.shape, BF16),
        ],
        scratch_shapes=[pltpu.VMEM((2 * nk, sc_rows, LANES), F32)] * 4 + [
            pltpu.VMEM((nk, LANES, 4 * LANES), BF16), pltpu.VMEM((nk, 1, 4 * LANES), F32)],
        compiler_params=pltpu.CompilerParams(
            dimension_semantics=("arbitrary", "arbitrary"), vmem_limit_bytes=LRU_VMEM),
        name="rglru",
    )(xy, xy, conv_w, conv_b, wa, ba, wx, bx, lam, h0, later_weight)


GROUP = SUBLANES * SUBLANES


def _swap_rows(slab_ref, val, r0):
    n = val.shape[0]
    for k in range(D_MODEL // LANES):
        slab_ref[k, r0:r0 + n, :] = val[:, k * LANES:(k + 1) * LANES]
    cols = []
    for k in range(D_MODEL // LANES):
        rows = [slab_ref[k, pl.ds(r0 + g * GROUP + t, SUBLANES, stride=SUBLANES), :]
                for g in range(n // GROUP) for t in range(SUBLANES)]
        cols.append(jnp.concatenate(rows, axis=0))
    return jnp.concatenate(cols, axis=1)


def _ffn_tile(x_ref, ol_ref, os_ref, mod_ref, gl_ref, gs_ref, wout_ref, n2_ref, up_ref,
              cw_ref, cb_ref, down_ref, fn_ref, y_ref, slab_sc, *, tm, fc, es, period, row):
    g1 = _mod_row(mod_ref, 2, row)
    sh2 = _mod_row(mod_ref, 3, row)
    sc2 = _mod_row(mod_ref, 4, row)
    g2 = _mod_row(mod_ref, 5, row)
    nl = _rms(ol_ref[...], gl_ref[...]).astype(BF16)
    ns = _rms(os_ref[...], gs_ref[...]).astype(BF16)
    o = (jnp.dot(nl, wout_ref[0:LRU_WIDTH, :], preferred_element_type=F32)
         + jnp.dot(ns, wout_ref[LRU_WIDTH:LRU_WIDTH + SGU_WIDTH, :], preferred_element_type=F32))
    x1 = x_ref[...] + g1 * o
    h2f = _rms(x1, n2_ref[...] * (1.0 + sc2)) + sh2
    h2 = _swap_rows(slab_sc, h2f, 0).astype(BF16)

    sub = lax.broadcasted_iota(jnp.int32, (SUBLANES, fc), 0)
    groups_per_period = period // GROUP
    n_groups = tm // GROUP

    def conv(z, c0, scale=None):
        w = cw_ref[:, c0:c0 + fc]
        b = cb_ref[:, c0:c0 + fc]
        if scale is not None:
            w, b = scale * w, scale * b
        w0, w1, w2 = w[0:1], w[1:2], w[2:3]
        z8 = [[z[g * GROUP + t * SUBLANES:g * GROUP + (t + 1) * SUBLANES] for t in range(SUBLANES)]
              for g in range(n_groups)]
        outs = []
        for g in range(n_groups):
            cur = z8[g]
            lo = pltpu.roll(cur[SUBLANES - 1], 1, axis=0)
            if g % groups_per_period == 0:
                lo = jnp.where(sub == 0, 0.0, lo)
            else:
                lo = jnp.where(sub == 0, pltpu.roll(z8[g - 1][SUBLANES - 1], 1, axis=0), lo)
            hi = pltpu.roll(cur[0], SUBLANES - 1, axis=0)
            if g % groups_per_period == groups_per_period - 1:
                hi = jnp.where(sub == SUBLANES - 1, 0.0, hi)
            else:
                hi = jnp.where(sub == SUBLANES - 1, pltpu.roll(z8[g + 1][0], SUBLANES - 1, axis=0), hi)
            zm = [lo] + cur[:SUBLANES - 1]
            zp = cur[1:] + [hi]
            for t in range(SUBLANES):
                outs.append(b + w0 * zm[t] + w1 * cur[t] + w2 * zp[t])
        return jnp.concatenate(outs, axis=0)

    def up(f0):
        zg = jnp.dot(h2, up_ref[:, f0:f0 + fc], preferred_element_type=F32)
        zv = jnp.dot(h2, up_ref[:, D_FF + f0:D_FF + f0 + fc], preferred_element_type=F32)
        return zg, zv

    acc = None
    nxt = up(0)
    for f0 in range(0, D_FF, fc):
        zg, zv = nxt
        if f0 + fc < D_FF:
            nxt = up(f0 + fc)
        hg = conv(zg, f0, scale=0.5)
        act = (hg * (jnp.tanh(hg) + 1.0) * conv(zv, D_FF + f0)).astype(BF16)
        if f0 + fc < D_FF:
            d = jnp.dot(act, down_ref[f0:f0 + fc, :], preferred_element_type=F32)
            acc = d if acc is None else acc + d
        else:
            for r0 in range(0, tm, es):
                a = jnp.dot(act[r0:r0 + es], down_ref[f0:f0 + fc, :], preferred_element_type=F32)
                if acc is not None:
                    a = acc[r0:r0 + es] + a
                x2 = x1[r0:r0 + es] + g2 * _swap_rows(slab_sc, a, r0)
                y_ref[r0:r0 + es, :] = _rms(x2, fn_ref[...])


def _ffn_kernel(xc_ref, xl_ref, olc_ref, oll_ref, os_ref, mod_ref, gl_ref, gs_ref, wout_ref, n2_ref,
                up_ref, cw_ref, cb_ref, down_ref, fn_ref, yc_ref, yl_ref, slab_sc, *,
                nc, lat_mod_row, ctx_period, lat_period, **tile):
    i = pl.program_id(0)
    shared = (mod_ref, gl_ref, gs_ref, wout_ref, n2_ref, up_ref, cw_ref, cb_ref, down_ref, fn_ref)

    @pl.when(i < nc)
    def _():
        _ffn_tile(xc_ref, olc_ref, os_ref, *shared, yc_ref, slab_sc, period=ctx_period, row=0, **tile)

    @pl.when(i >= nc)
    def _():
        _ffn_tile(xl_ref, oll_ref, os_ref, *shared, yl_ref, slab_sc, period=lat_period,
                  row=lat_mod_row(i), **tile)


def _ffn(xc, xl, olru_c, olru_l, osgu, mod, g_lru, g_sgu, w_out, norm2, ffn_up, conv_w, conv_b,
         ffn_down, final_norm, *, tm, ctx_period, lat_period, lat_seq):
    nc, n, ctx_map, lat_map, lat_mod_row = _token_groups(tm, xc.shape[0], xl.shape[0], lat_seq)
    const = lambda i: (0, 0)
    resident = lambda shape: pl.BlockSpec(shape, const, pipeline_mode=pl.Buffered(1))
    return pl.pallas_call(
        functools.partial(_ffn_kernel, nc=nc, lat_mod_row=lat_mod_row, ctx_period=ctx_period,
                          lat_period=lat_period, tm=tm, fc=FFN_FC, es=FFN_ES),
        grid=(n,),
        in_specs=[
            pl.BlockSpec((tm, D_MODEL), ctx_map),
            pl.BlockSpec((tm, D_MODEL), lat_map),
            pl.BlockSpec((tm, LRU_WIDTH), ctx_map),
            pl.BlockSpec((tm, LRU_WIDTH), lat_map),
            pl.BlockSpec((tm, SGU_WIDTH), lambda i: (i, 0)),
            pl.BlockSpec((N_MOD, MOD_ROWS, D_MODEL), lambda i: (0, 0, 0)),
            pl.BlockSpec((1, LRU_WIDTH), const),
            pl.BlockSpec((1, SGU_WIDTH), const),
            resident((LRU_WIDTH + SGU_WIDTH, D_MODEL)),
            pl.BlockSpec((1, D_MODEL), const),
            resident((D_MODEL, 2 * D_FF)),
            pl.BlockSpec((3, 2 * D_FF), const),
            pl.BlockSpec((1, 2 * D_FF), const),
            resident((D_FF, D_MODEL)),
            pl.BlockSpec((1, D_MODEL), const),
        ],
        out_specs=[pl.BlockSpec((tm, D_MODEL), ctx_map), pl.BlockSpec((tm, D_MODEL), lat_map)],
        out_shape=[jax.ShapeDtypeStruct(xc.shape, F32), jax.ShapeDtypeStruct(xl.shape, F32)],
        scratch_shapes=[pltpu.VMEM((D_MODEL // LANES, tm, LANES), F32)],
        compiler_params=pltpu.CompilerParams(
            dimension_semantics=("arbitrary",), vmem_limit_bytes=FFN_VMEM),
        name="outproj_ffn",
    )(xc, xl, olru_c, olru_l, osgu, mod, g_lru, g_sgu, w_out, norm2, ffn_up, conv_w, conv_b, ffn_down,
      final_norm)


def kernel(x_prompt, x_sample, state_lru, c, c_ctx, norm1, norm2, w_ada, b_ada, w_in, lru_conv_w,
           lru_conv_b, lru_wa, lru_ba, lru_wx, lru_bx, lru_lam, sgu_ws, sgu_bs, g_lru, g_sgu, w_out,
           ffn_up, ffn_conv_w, ffn_conv_b, ffn_down, final_norm):
    batch, seq, _ = x_prompt.shape
    dec_batch, dec_seq, _ = x_sample.shape
    depth = norm1.shape[0]
    assert depth == 1, "the final norm is fused into the (single) layer's last kernel"
    l = 0

    xc = x_prompt.reshape(batch * seq, D_MODEL)
    xl = x_sample.reshape(dec_batch * dec_seq, D_MODEL)
    n_ctx, n_lat = xc.shape[0], xl.shape[0]
    zeros_state = jnp.zeros((batch, 2, LRU_WIDTH), x_prompt.dtype)

    mod = _modulation(c_ctx[None, :], c, w_ada[l], b_ada[l][None, :])
    lru_params = (lru_conv_w[l], lru_conv_b[l][None, :], lru_wa[l], lru_ba[l], lru_wx[l], lru_bx[l],
                  lru_lam[l])
    h0_lat = state_lru.reshape(dec_batch, 2, LRU_WIDTH)

    xy, osgu, w_out_b = _inproj(
        xc, xl, mod, norm1[l][None, :], w_in[l], sgu_ws[l], sgu_bs[l],
        w_out[l], tm=INPROJ_TM, lat_seq=dec_seq)
    olru_c, new_state, down_b = _lru(xy, *lru_params, zeros_state, ffn_down[l], row_start=0, m=n_ctx,
                                     seq=seq, nsb=LRU_CTX_SEQS)
    olru_l, _, up_b = _lru(xy, *lru_params, h0_lat, ffn_up[l], row_start=n_ctx, m=n_lat,
                           seq=dec_seq, nsb=1)
    yc, yl = _ffn(xc, xl, olru_c, olru_l, osgu, mod, g_lru[l][None, :], g_sgu[l][None, :],
                  w_out_b, norm2[l][None, :], up_b, ffn_conv_w[l],
                  ffn_conv_b[l][None, :], down_b, final_norm[None, :], tm=FFN_TM,
                  ctx_period=seq, lat_period=GRID_W, lat_seq=dec_seq)
    return (yc.reshape(batch, seq, D_MODEL), yl.reshape(dec_batch, dec_seq, D_MODEL),
            new_state[:, None])
```

```python
import functools
import math

import jax
import jax.numpy as jnp
from jax import lax
from jax.experimental import pallas as pl
from jax.experimental.pallas import tpu as pltpu

D_MODEL = 1024
LRU_HEADS = 8
LRU_WIDTH = 512
LRU_HEAD_DIM = 64
LRU_C = 8.0
SGU_GROUPS = 4
SGU_WIDTH = 512
CHUNK = 128
D_FF = 3072
N_MOD = 6
EPS = 1e-6
GRID_W = 64

assert LRU_WIDTH == LRU_HEADS * LRU_HEAD_DIM and SGU_WIDTH == SGU_GROUPS * CHUNK

LANES = 128
SUBLANES = 8
BF16_SUBLANES = 16
MOD_ROWS = 8
BF16 = jnp.bfloat16
F32 = jnp.float32

MIB = 1024 * 1024
INPROJ_TM = 1024
INPROJ_VMEM = 56 * MIB
LRU_ROW_CHUNK = 512
LRU_LANE_BLOCKS = 2
LRU_CTX_SEQS = 8
LRU_SCAN_UNROLL = True
LRU_VMEM = 56 * MIB
FFN_TM = 512
FFN_FC = 1536
FFN_ES = 256
FFN_VMEM = 60 * MIB

_GELU_C0 = math.sqrt(2.0 / math.pi)
_GELU_C1 = _GELU_C0 * 0.044715


def _rms(x, gain):
    return x * lax.rsqrt(jnp.mean(x * x, axis=-1, keepdims=True) + EPS) * gain


def _gelu(x):
    return (0.5 * x) * (1.0 + jnp.tanh(x * (_GELU_C0 + _GELU_C1 * (x * x))))


def _mod_row(mod_ref, k, row):
    return mod_ref[k, pl.ds(row, 1), :]


def _mod_kernel(cctx_ref, c_ref, w_ref, b_ref, o_ref):
    pad = jnp.zeros((MOD_ROWS - 1 - c_ref.shape[0], D_MODEL), F32)
    cc = jnp.concatenate([cctx_ref[...], c_ref[...], pad], axis=0)
    s = jax.nn.silu(cc).astype(BF16)
    o_ref[...] = jnp.dot(s, w_ref[...].astype(BF16), preferred_element_type=F32) + b_ref[...]


def _modulation(c_ctx, c, w_ada, b_ada):
    assert 1 + c.shape[0] <= MOD_ROWS
    return pl.pallas_call(
        _mod_kernel,
        grid=(N_MOD,),
        in_specs=[
            pl.BlockSpec((1, D_MODEL), lambda j: (0, 0)),
            pl.BlockSpec(c.shape, lambda j: (0, 0)),
            pl.BlockSpec((D_MODEL, D_MODEL), lambda j: (0, j)),
            pl.BlockSpec((1, D_MODEL), lambda j: (0, j)),
        ],
        out_specs=pl.BlockSpec((None, MOD_ROWS, D_MODEL), lambda j: (j, 0, 0)),
        out_shape=jax.ShapeDtypeStruct((N_MOD, MOD_ROWS, D_MODEL), F32),
        compiler_params=pltpu.CompilerParams(dimension_semantics=("arbitrary",)),
        name="modulation",
    )(c_ctx, c, w_ada, b_ada)


def _token_groups(tm, n_ctx, n_lat, lat_seq):
    assert n_ctx % tm == 0 and n_lat % tm == 0 and lat_seq % tm == 0
    nc = n_ctx // tm
    ctx_map = lambda i: (jnp.minimum(i, nc - 1), 0)
    lat_map = lambda i: (jnp.maximum(i - nc, 0), 0)
    lat_mod_row = lambda i: 1 + ((i - nc) * tm) // lat_seq
    return nc, nc + n_lat // tm, ctx_map, lat_map, lat_mod_row


def _inproj_kernel(xc_ref, xl_ref, mod_ref, n1_ref, win_f_ref, ws_ref, bs_ref, w_ref,
                   xy_ref, osgu_ref, w_b_ref, win_ref, *, tm, nc, lat_mod_row):
    i = pl.program_id(0)
    w_b_ref[...] = w_ref[...].astype(BF16)

    @pl.when(i == 0)
    def _():
        win_ref[...] = win_f_ref[...].astype(BF16)

    def body(x_ref, row):
        sh1 = _mod_row(mod_ref, 0, row)
        sc1 = _mod_row(mod_ref, 1, row)
        hb = (_rms(x_ref[...], n1_ref[...] * (1.0 + sc1)) + sh1).astype(BF16)
        uv = jnp.dot(hb, win_ref[:, 2 * LRU_WIDTH:], preferred_element_type=F32)
        xy_ref[...] = jnp.dot(hb, win_ref[:, 0:2 * LRU_WIDTH], preferred_element_type=F32)
        for g in range(SGU_GROUPS):
            lo = g * LANES
            gu = _gelu(uv[:, lo:lo + LANES])
            gv = _gelu(uv[:, SGU_WIDTH + lo:SGU_WIDTH + lo + LANES]).astype(BF16)
            bias = jnp.broadcast_to(bs_ref[g:g + 1, :], (LANES, CHUNK)).T
            ws = ws_ref[g].astype(BF16)
            for ck in range(tm // CHUNK):
                r0 = ck * CHUNK
                s = jnp.dot(ws, gv[r0:r0 + CHUNK], preferred_element_type=F32) + bias
                osgu_ref[r0:r0 + CHUNK, lo:lo + LANES] = gu[r0:r0 + CHUNK] * s

    @pl.when(i < nc)
    def _():
        body(xc_ref, 0)

    @pl.when(i >= nc)
    def _():
        body(xl_ref, lat_mod_row(i))


def _cast_slice_spec(shape, grid):
    n_steps = math.prod(grid)
    rows = shape[0]
    r = BF16_SUBLANES
    while rows % r or rows // r > n_steps:
        r += BF16_SUBLANES
    last = rows // r - 1

    def index_map(*idx):
        step = 0
        for i, n in zip(idx, grid):
            step = step * n + i
        return (jnp.minimum(step, last), 0)

    return pl.BlockSpec((r, shape[1]), index_map)


def _inproj(xc, xl, mod, norm1, w_in, ws, bs, later_weight, *, tm, lat_seq):
    nc, n, ctx_map, lat_map, lat_mod_row = _token_groups(tm, xc.shape[0], xl.shape[0], lat_seq)
    m = n * tm
    const = lambda i: (0, 0)
    cast_specs = [_cast_slice_spec(later_weight.shape, (n,))]
    return pl.pallas_call(
        functools.partial(_inproj_kernel, tm=tm, nc=nc, lat_mod_row=lat_mod_row),
        grid=(n,),
        in_specs=[
            pl.BlockSpec((tm, D_MODEL), ctx_map),
            pl.BlockSpec((tm, D_MODEL), lat_map),
            pl.BlockSpec((N_MOD, MOD_ROWS, D_MODEL), lambda i: (0, 0, 0)),
            pl.BlockSpec((1, D_MODEL), const),
            pl.BlockSpec((D_MODEL, 4 * LRU_WIDTH), const, pipeline_mode=pl.Buffered(1)),
            pl.BlockSpec((SGU_GROUPS, CHUNK, CHUNK), lambda i: (0, 0, 0)),
            pl.BlockSpec((SGU_GROUPS, CHUNK), const),
        ] + cast_specs,
        out_specs=[
            pl.BlockSpec((tm, 2 * LRU_WIDTH), lambda i: (i, 0)),
            pl.BlockSpec((tm, SGU_WIDTH), lambda i: (i, 0)),
        ] + cast_specs,
        out_shape=[
            jax.ShapeDtypeStruct((m, 2 * LRU_WIDTH), F32),
            jax.ShapeDtypeStruct((m, SGU_WIDTH), F32),
        ] + [jax.ShapeDtypeStruct(later_weight.shape, BF16)],
        scratch_shapes=[pltpu.VMEM(w_in.shape, BF16)],
        compiler_params=pltpu.CompilerParams(
            dimension_semantics=("arbitrary",), vmem_limit_bytes=INPROJ_VMEM),
        name="inproj_sgu",
    )(xc, xl, mod, norm1, w_in, ws, bs, later_weight)


def _scan_pitch(seq):
    pitch = seq // SUBLANES
    return pitch + (4 - pitch % 8) % 8


def _lru_kernel(xr_ref, yr_ref, cw_ref, cb_ref, wa_ref, ba_ref, wx_ref, bx_ref, lam_ref, h0_ref, w_ref,
                o_ref, st_ref, w_b_ref, a_sc, b_sc, p_sc, h_sc, wg_ref, bg_ref, *,
                seq, nsb, nk, pitch, rc, unroll):
    sp_rows = SUBLANES * pitch
    n_rc = (seq * nsb) // rc
    seg = min(seq, rc)
    segs = rc // seg
    chunks_per_seq = seq // seg
    w_b_ref[...] = w_ref[...].astype(BF16)

    @pl.when(pl.program_id(1) == 0)
    def _():
        zeros = jnp.zeros((LRU_HEAD_DIM, LRU_HEAD_DIM), F32)
        for k in range(nk):
            blocks = []
            for d in range(2):
                for src in (wa_ref, wx_ref):
                    top = jnp.concatenate([src[d, 2 * k], zeros], axis=1)
                    bot = jnp.concatenate([zeros, src[d, 2 * k + 1]], axis=1)
                    blocks.append(jnp.concatenate([top, bot], axis=0))
            wg_ref[k] = jnp.concatenate(blocks, axis=1).astype(BF16)
            lanes = slice(k * LANES, (k + 1) * LANES)
            bg_ref[k] = 0.5 * jnp.concatenate(
                [ba_ref[0:1, lanes], bx_ref[0:1, lanes], ba_ref[1:2, lanes], bx_ref[1:2, lanes]], axis=1)

    c4_all = (-0.5 * LRU_C) * jax.nn.softplus(-lam_ref[...])
    cw_all = 0.5 * cw_ref[...]
    cb_all = 0.5 * cb_ref[...]

    for s in range(nsb):
        lo, hi = s * sp_rows + seq, (s + 1) * sp_rows
        for slab in range(2 * nk):
            a_sc[slab, lo:hi, :] = jnp.ones((hi - lo, LANES), F32)
            b_sc[slab, lo:hi, :] = jnp.zeros((hi - lo, LANES), F32)

    def scratch_row(ci, k):
        if chunks_per_seq > 1:
            return pl.multiple_of((ci // chunks_per_seq) * sp_rows + (ci % chunks_per_seq) * seg,
                                  SUBLANES)
        return pl.multiple_of((ci * segs + k) * sp_rows, SUBLANES)

    def conv_segment(xs, prev, nxt, cw, cb):
        ext = jnp.concatenate([prev, xs, nxt], axis=0)
        n_ext = seg + 2 * SUBLANES
        xc = cb + cw[2:3] * xs
        xc = xc + cw[0:1] * pltpu.roll(ext, 2, axis=0)[SUBLANES:SUBLANES + seg]
        xc = xc + cw[1:2] * pltpu.roll(ext, 1, axis=0)[SUBLANES:SUBLANES + seg]
        return xc + cw[3:4] * pltpu.roll(ext, n_ext - 1, axis=0)[SUBLANES:SUBLANES + seg]

    def gates(ci, _):
        base = pl.multiple_of(ci * rc, rc)
        for k in range(nk):
            lanes = slice(k * LANES, (k + 1) * LANES)
            cw, cb, c4 = cw_all[:, lanes], cb_all[:, lanes], c4_all[:, lanes]
            xm = xr_ref[pl.ds(base, rc), lanes]
            if chunks_per_seq > 1:
                cs = ci % chunks_per_seq
                p0 = pl.multiple_of(jnp.maximum(base - SUBLANES, 0), SUBLANES)
                n0 = pl.multiple_of(jnp.minimum(base + rc, seq * nsb - SUBLANES), SUBLANES)
                prev = jnp.where(cs == 0, 0.0, xr_ref[pl.ds(p0, SUBLANES), lanes])
                nxt = jnp.where(cs == chunks_per_seq - 1, 0.0, xr_ref[pl.ds(n0, SUBLANES), lanes])
                hxc = conv_segment(xm, prev, nxt, cw, cb)
            else:
                zeros = jnp.zeros((SUBLANES, LANES), F32)
                hxc = jnp.concatenate(
                    [conv_segment(xm[j * seg:(j + 1) * seg], zeros, zeros, cw, cb) for j in range(segs)],
                    axis=0)
            g = jnp.dot(hxc.astype(BF16), wg_ref[k], preferred_element_type=F32) + bg_ref[k]
            for d in range(2):
                th_r = jnp.tanh(g[:, (2 * d) * LANES:(2 * d + 1) * LANES])
                th_i = jnp.tanh(g[:, (2 * d + 1) * LANES:(2 * d + 2) * LANES])
                log_a = c4[d:d + 1] * th_r + c4[d:d + 1]
                a = jnp.exp(log_a)
                om = jnp.tanh(log_a) * (-1.0 - a * a)
                root = jnp.where(om > 0.0, om * lax.rsqrt(om), 0.0)
                b = root * ((th_i + 1.0) * hxc)
                for j in range(segs):
                    dst = scratch_row(ci, j)
                    a_sc[d * nk + k, pl.ds(dst, seg), :] = a[j * seg:(j + 1) * seg]
                    b_sc[d * nk + k, pl.ds(dst, seg), :] = b[j * seg:(j + 1) * seg]
        return 0

    lax.fori_loop(0, n_rc, gates, 0)

    def scan_sequence(s, _):
        off = s * sp_rows

        def rows(t):
            return pl.ds(off + t, SUBLANES, stride=pitch)

        def local(t, carry):
            out = []
            for slab, (h, p) in enumerate(carry):
                tt = t if slab < nk else pitch - 1 - t
                a = a_sc[slab, rows(tt), :]
                h = a * h + b_sc[slab, rows(tt), :]
                p = a * p
                h_sc[slab, rows(tt), :] = h
                p_sc[slab, rows(tt), :] = p
                out.append((h, p))
            return tuple(out)

        zero = jnp.zeros((SUBLANES, LANES), F32)
        one = jnp.ones((SUBLANES, LANES), F32)
        ends = lax.fori_loop(0, pitch, local, ((zero, one),) * (2 * nk), unroll=unroll)

        h0 = h0_ref[s]
        starts = []
        finals = [[None] * nk, [None] * nk]
        for slab, (h, p) in enumerate(ends):
            d, k = divmod(slab, nk)
            c = h0[d:d + 1, k * LANES:(k + 1) * LANES]
            cs = [None] * SUBLANES
            for j in (range(SUBLANES) if d == 0 else reversed(range(SUBLANES))):
                cs[j] = c
                c = h[j:j + 1] + p[j:j + 1] * c
            finals[d][k] = c
            starts.append(jnp.concatenate(cs, axis=0))
        st_ref[s] = jnp.concatenate(
            [jnp.concatenate(finals[0], axis=1), jnp.concatenate(finals[1], axis=1)], axis=0)

        def fix(t, _):
            for k in range(nk):
                a_sc[k, rows(t), :] = (h_sc[k, rows(t), :] + p_sc[k, rows(t), :] * starts[k]
                                       + h_sc[nk + k, rows(t), :] + p_sc[nk + k, rows(t), :] * starts[nk + k])
            return 0

        lax.fori_loop(0, pitch, fix, 0, unroll=unroll)
        return 0

    if nsb == 1:
        scan_sequence(0, 0)
    else:
        lax.fori_loop(0, nsb, scan_sequence, 0)

    def gate_out(ci, _):
        base = pl.multiple_of(ci * rc, rc)
        for k in range(nk):
            lanes = slice(k * LANES, (k + 1) * LANES)
            gy = _gelu(yr_ref[pl.ds(base, rc), lanes])
            for j in range(segs):
                src = scratch_row(ci, j)
                o_ref[pl.ds(base + j * seg, seg), lanes] = (gy[j * seg:(j + 1) * seg]
                                                            * a_sc[k, pl.ds(src, seg), :])
        return 0

    lax.fori_loop(0, n_rc, gate_out, 0)


def _lru(xy, conv_w, conv_b, wa, ba, wx, bx, lam, h0, later_weight, *, row_start, m, seq, nsb):
    nblk = m // (seq * nsb)
    assert row_start % (seq * nsb) == 0
    blk0 = row_start // (seq * nsb)
    nk = LRU_LANE_BLOCKS
    width = nk * LANES
    ncb = LRU_WIDTH // width
    pitch = _scan_pitch(seq)
    sc_rows = nsb * SUBLANES * pitch
    rows = seq * nsb
    cast_spec = _cast_slice_spec(later_weight.shape, (ncb, nblk))
    return pl.pallas_call(
        functools.partial(_lru_kernel, seq=seq, nsb=nsb, nk=nk, pitch=pitch,
                          rc=LRU_ROW_CHUNK, unroll=LRU_SCAN_UNROLL),
        grid=(ncb, nblk),
        in_specs=[
            pl.BlockSpec((rows, width), lambda j, b: (blk0 + b, j)),
            pl.BlockSpec((rows, width), lambda j, b: (blk0 + b, ncb + j)),
            pl.BlockSpec((4, width), lambda j, b: (0, j)),
            pl.BlockSpec((1, width), lambda j, b: (0, j)),
            pl.BlockSpec((2, 2 * nk, LRU_HEAD_DIM, LRU_HEAD_DIM), lambda j, b: (0, j, 0, 0)),
            pl.BlockSpec((2, width), lambda j, b: (0, j)),
            pl.BlockSpec((2, 2 * nk, LRU_HEAD_DIM, LRU_HEAD_DIM), lambda j, b: (0, j, 0, 0)),
            pl.BlockSpec((2, width), lambda j, b: (0, j)),
            pl.BlockSpec((2, width), lambda j, b: (0, j)),
            pl.BlockSpec((nsb, 2, width), lambda j, b: (b, 0, j)),
            cast_spec,
        ],
        out_specs=[
            pl.BlockSpec((rows, width), lambda j, b: (b, j)),
            pl.BlockSpec((nsb, 2, width), lambda j, b: (b, 0, j)),
            cast_spec,
        ],
        out_shape=[
            jax.ShapeDtypeStruct((m, LRU_WIDTH), F32),
            jax.ShapeDtypeStruct((m // seq, 2, LRU_WIDTH), F32),
            jax.ShapeDtypeStruct(later_weight.shape, BF16),
        ],
        scratch_shapes=[pltpu.VMEM((2 * nk, sc_rows, LANES), F32)] * 4 + [
            pltpu.VMEM((nk, LANES, 4 * LANES), BF16), pltpu.VMEM((nk, 1, 4 * LANES), F32)],
        compiler_params=pltpu.CompilerParams(
            dimension_semantics=("arbitrary", "arbitrary"), vmem_limit_bytes=LRU_VMEM),
        name="rglru",
    )(xy, xy, conv_w, conv_b, wa, ba, wx, bx, lam, h0, later_weight)


GROUP = SUBLANES * SUBLANES


def _swap_rows(slab_ref, val, r0):
    n = val.shape[0]
    for k in range(D_MODEL // LANES):
        slab_ref[k, r0:r0 + n, :] = val[:, k * LANES:(k + 1) * LANES]
    cols = []
    for k in range(D_MODEL // LANES):
        rows = [slab_ref[k, pl.ds(r0 + g * GROUP + t, SUBLANES, stride=SUBLANES), :]
                for g in range(n // GROUP) for t in range(SUBLANES)]
        cols.append(jnp.concatenate(rows, axis=0))
    return jnp.concatenate(cols, axis=1)


def _ffn_tile(x_ref, ol_ref, os_ref, mod_ref, gl_ref, gs_ref, wout_ref, n2_ref, up_ref,
              cw_ref, cb_ref, down_ref, fn_ref, y_ref, slab_sc, *, tm, fc, es, period, row):
    g1 = _mod_row(mod_ref, 2, row)
    sh2 = _mod_row(mod_ref, 3, row)
    sc2 = _mod_row(mod_ref, 4, row)
    g2 = _mod_row(mod_ref, 5, row)
    nl = _rms(ol_ref[...], gl_ref[...]).astype(BF16)
    ns = _rms(os_ref[...], gs_ref[...]).astype(BF16)
    o = (jnp.dot(nl, wout_ref[0:LRU_WIDTH, :], preferred_element_type=F32)
         + jnp.dot(ns, wout_ref[LRU_WIDTH:LRU_WIDTH + SGU_WIDTH, :], preferred_element_type=F32))
    x1 = x_ref[...] + g1 * o
    h2f = _rms(x1, n2_ref[...] * (1.0 + sc2)) + sh2
    h2 = _swap_rows(slab_sc, h2f, 0).astype(BF16)

    sub = lax.broadcasted_iota(jnp.int32, (SUBLANES, fc), 0)
    groups_per_period = period // GROUP
    n_groups = tm // GROUP

    def conv(z, c0, scale=None):
        w = cw_ref[:, c0:c0 + fc]
        b = cb_ref[:, c0:c0 + fc]
        if scale is not None:
            w, b = scale * w, scale * b
        w0, w1, w2 = w[0:1], w[1:2], w[2:3]
        z8 = [[z[g * GROUP + t * SUBLANES:g * GROUP + (t + 1) * SUBLANES] for t in range(SUBLANES)]
              for g in range(n_groups)]
        outs = []
        for g in range(n_groups):
            cur = z8[g]
            lo = pltpu.roll(cur[SUBLANES - 1], 1, axis=0)
            if g % groups_per_period == 0:
                lo = jnp.where(sub == 0, 0.0, lo)
            else:
                lo = jnp.where(sub == 0, pltpu.roll(z8[g - 1][SUBLANES - 1], 1, axis=0), lo)
            hi = pltpu.roll(cur[0], SUBLANES - 1, axis=0)
            if g % groups_per_period == groups_per_period - 1:
                hi = jnp.where(sub == SUBLANES - 1, 0.0, hi)
            else:
                hi = jnp.where(sub == SUBLANES - 1, pltpu.roll(z8[g + 1][0], SUBLANES - 1, axis=0), hi)
            zm = [lo] + cur[:SUBLANES - 1]
            zp = cur[1:] + [hi]
            for t in range(SUBLANES):
                outs.append(b + w0 * zm[t] + w1 * cur[t] + w2 * zp[t])
        return jnp.concatenate(outs, axis=0)

    def up(f0):
        zg = jnp.dot(h2, up_ref[:, f0:f0 + fc], preferred_element_type=F32)
        zv = jnp.dot(h2, up_ref[:, D_FF + f0:D_FF + f0 + fc], preferred_element_type=F32)
        return zg, zv

    acc = None
    nxt = up(0)
    for f0 in range(0, D_FF, fc):
        zg, zv = nxt
        if f0 + fc < D_FF:
            nxt = up(f0 + fc)
        hg = conv(zg, f0, scale=0.5)
        act = (hg * (jnp.tanh(hg) + 1.0) * conv(zv, D_FF + f0)).astype(BF16)
        if f0 + fc < D_FF:
            d = jnp.dot(act, down_ref[f0:f0 + fc, :], preferred_element_type=F32)
            acc = d if acc is None else acc + d
        else:
            for r0 in range(0, tm, es):
                a = jnp.dot(act[r0:r0 + es], down_ref[f0:f0 + fc, :], preferred_element_type=F32)
                if acc is not None:
                    a = acc[r0:r0 + es] + a
                x2 = x1[r0:r0 + es] + g2 * _swap_rows(slab_sc, a, r0)
                y_ref[r0:r0 + es, :] = _rms(x2, fn_ref[...])


def _ffn_kernel(xc_ref, xl_ref, olc_ref, oll_ref, os_ref, mod_ref, gl_ref, gs_ref, wout_ref, n2_ref,
                up_ref, cw_ref, cb_ref, down_ref, fn_ref, yc_ref, yl_ref, slab_sc, *,
                nc, lat_mod_row, ctx_period, lat_period, **tile):
    i = pl.program_id(0)
    shared = (mod_ref, gl_ref, gs_ref, wout_ref, n2_ref, up_ref, cw_ref, cb_ref, down_ref, fn_ref)

    @pl.when(i < nc)
    def _():
        _ffn_tile(xc_ref, olc_ref, os_ref, *shared, yc_ref, slab_sc, period=ctx_period, row=0, **tile)

    @pl.when(i >= nc)
    def _():
        _ffn_tile(xl_ref, oll_ref, os_ref, *shared, yl_ref, slab_sc, period=lat_period,
                  row=lat_mod_row(i), **tile)


def _ffn(xc, xl, olru_c, olru_l, osgu, mod, g_lru, g_sgu, w_out, norm2, ffn_up, conv_w, conv_b,
         ffn_down, final_norm, *, tm, ctx_period, lat_period, lat_seq):
    nc, n, ctx_map, lat_map, lat_mod_row = _token_groups(tm, xc.shape[0], xl.shape[0], lat_seq)
    const = lambda i: (0, 0)
    resident = lambda shape: pl.BlockSpec(shape, const, pipeline_mode=pl.Buffered(1))
    return pl.pallas_call(
        functools.partial(_ffn_kernel, nc=nc, lat_mod_row=lat_mod_row, ctx_period=ctx_period,
                          lat_period=lat_period, tm=tm, fc=FFN_FC, es=FFN_ES),
        grid=(n,),
        in_specs=[
            pl.BlockSpec((tm, D_MODEL), ctx_map),
            pl.BlockSpec((tm, D_MODEL), lat_map),
            pl.BlockSpec((tm, LRU_WIDTH), ctx_map),
            pl.BlockSpec((tm, LRU_WIDTH), lat_map),
            pl.BlockSpec((tm, SGU_WIDTH), lambda i: (i, 0)),
            pl.BlockSpec((N_MOD, MOD_ROWS, D_MODEL), lambda i: (0, 0, 0)),
            pl.BlockSpec((1, LRU_WIDTH), const),
            pl.BlockSpec((1, SGU_WIDTH), const),
            resident((LRU_WIDTH + SGU_WIDTH, D_MODEL)),
            pl.BlockSpec((1, D_MODEL), const),
            resident((D_MODEL, 2 * D_FF)),
            pl.BlockSpec((3, 2 * D_FF), const),
            pl.BlockSpec((1, 2 * D_FF), const),
            resident((D_FF, D_MODEL)),
            pl.BlockSpec((1, D_MODEL), const),
        ],
        out_specs=[pl.BlockSpec((tm, D_MODEL), ctx_map), pl.BlockSpec((tm, D_MODEL), lat_map)],
        out_shape=[jax.ShapeDtypeStruct(xc.shape, F32), jax.ShapeDtypeStruct(xl.shape, F32)],
        scratch_shapes=[pltpu.VMEM((D_MODEL // LANES, tm, LANES), F32)],
        compiler_params=pltpu.CompilerParams(
            dimension_semantics=("arbitrary",), vmem_limit_bytes=FFN_VMEM),
        name="outproj_ffn",
    )(xc, xl, olru_c, olru_l, osgu, mod, g_lru, g_sgu, w_out, norm2, ffn_up, conv_w, conv_b, ffn_down,
      final_norm)


def kernel(x_prompt, x_sample, state_lru, c, c_ctx, norm1, norm2, w_ada, b_ada, w_in, lru_conv_w,
           lru_conv_b, lru_wa, lru_ba, lru_wx, lru_bx, lru_lam, sgu_ws, sgu_bs, g_lru, g_sgu, w_out,
           ffn_up, ffn_conv_w, ffn_conv_b, ffn_down, final_norm):
    batch, seq, _ = x_prompt.shape
    dec_batch, dec_seq, _ = x_sample.shape
    depth = norm1.shape[0]
    assert depth == 1, "the final norm is fused into the (single) layer's last kernel"
    l = 0

    xc = x_prompt.reshape(batch * seq, D_MODEL)
    xl = x_sample.reshape(dec_batch * dec_seq, D_MODEL)
    n_ctx, n_lat = xc.shape[0], xl.shape[0]
    zeros_state = jnp.zeros((batch, 2, LRU_WIDTH), x_prompt.dtype)

    mod = _modulation(c_ctx[None, :], c, w_ada[l], b_ada[l][None, :])
    lru_params = (lru_conv_w[l], lru_conv_b[l][None, :], lru_wa[l], lru_ba[l], lru_wx[l], lru_bx[l],
                  lru_lam[l])
    h0_lat = state_lru.reshape(dec_batch, 2, LRU_WIDTH)

    xy, osgu, w_out_b = _inproj(
        xc, xl, mod, norm1[l][None, :], w_in[l], sgu_ws[l], sgu_bs[l],
        w_out[l], tm=INPROJ_TM, lat_seq=dec_seq)
    olru_c, new_state, down_b = _lru(xy, *lru_params, zeros_state, ffn_down[l], row_start=0, m=n_ctx,
                                     seq=seq, nsb=LRU_CTX_SEQS)
    olru_l, _, up_b = _lru(xy, *lru_params, h0_lat, ffn_up[l], row_start=n_ctx, m=n_lat,
                           seq=dec_seq, nsb=1)
    yc, yl = _ffn(xc, xl, olru_c, olru_l, osgu, mod, g_lru[l][None, :], g_sgu[l][None, :],
                  w_out_b, norm2[l][None, :], up_b, ffn_conv_w[l],
                  ffn_conv_b[l][None, :], down_b, final_norm[None, :], tm=FFN_TM,
                  ctx_period=seq, lat_period=GRID_W, lat_seq=dec_seq)
    return (yc.reshape(batch, seq, D_MODEL), yl.reshape(dec_batch, dec_seq, D_MODEL),
            new_state[:, None])
```

```python
import functools
import math

import jax
import jax.numpy as jnp
from jax import lax
from jax.experimental import pallas as pl
from jax.experimental.pallas import tpu as pltpu

D_MODEL = 1024
LRU_HEADS = 8
LRU_WIDTH = 512
LRU_HEAD_DIM = 64
LRU_C = 8.0
SGU_GROUPS = 4
SGU_WIDTH = 512
CHUNK = 128
D_FF = 3072
N_MOD = 6
EPS = 1e-6
GRID_W = 64

LANES = 128
SUBLANES = 8
BF16_SUBLANES = 16
MOD_ROWS = 8
BF16 = jnp.bfloat16
F32 = jnp.float32

MIB = 1024 * 1024
INPROJ_TM = 1024
INPROJ_VMEM = 56 * MIB
LRU_ROW_CHUNK = 1024
LRU_LANE_BLOCKS = 2
LRU_CTX_SEQS = 8
LRU_SCAN_UNROLL = True
LRU_VMEM = 56 * MIB
FFN_TM = 512
FFN_FC = 1536
FFN_ES = 256
FFN_VMEM = 60 * MIB

_GELU_C0 = math.sqrt(2.0 / math.pi)
_GELU_C1 = _GELU_C0 * 0.044715


def _rms(x, gain):
    return x * lax.rsqrt(jnp.mean(x * x, axis=-1, keepdims=True) + EPS) * gain


def _gelu(x):
    return (0.5 * x) * (1.0 + jnp.tanh(x * (_GELU_C0 + _GELU_C1 * (x * x))))


def _mod_row(mod_ref, k, row):
    return mod_ref[k, pl.ds(row, 1), :]


def _mod_kernel(cctx_ref, c_ref, w_ref, b_ref, o_ref):
    pad = jnp.zeros((MOD_ROWS - 1 - c_ref.shape[0], D_MODEL), F32)
    cc = jnp.concatenate([cctx_ref[...], c_ref[...], pad], axis=0)
    s = jax.nn.silu(cc).astype(BF16)
    o_ref[...] = jnp.dot(s, w_ref[...].astype(BF16), preferred_element_type=F32) + b_ref[...]


def _modulation(c_ctx, c, w_ada, b_ada):
    assert 1 + c.shape[0] <= MOD_ROWS
    return pl.pallas_call(
        _mod_kernel,
        grid=(N_MOD,),
        in_specs=[
            pl.BlockSpec((1, D_MODEL), lambda j: (0, 0)),
            pl.BlockSpec(c.shape, lambda j: (0, 0)),
            pl.BlockSpec((D_MODEL, D_MODEL), lambda j: (0, j)),
            pl.BlockSpec((1, D_MODEL), lambda j: (0, j)),
        ],
        out_specs=pl.BlockSpec((None, MOD_ROWS, D_MODEL), lambda j: (j, 0, 0)),
        out_shape=jax.ShapeDtypeStruct((N_MOD, MOD_ROWS, D_MODEL), F32),
        compiler_params=pltpu.CompilerParams(dimension_semantics=("arbitrary",)),
        name="modulation",
    )(c_ctx, c, w_ada, b_ada)


def _token_groups(tm, n_ctx, n_lat, lat_seq):
    assert n_ctx % tm == 0 and n_lat % tm == 0 and lat_seq % tm == 0
    nc = n_ctx // tm
    ctx_map = lambda i: (jnp.minimum(i, nc - 1), 0)
    lat_map = lambda i: (jnp.maximum(i - nc, 0), 0)
    lat_mod_row = lambda i: 1 + ((i - nc) * tm) // lat_seq
    return nc, nc + n_lat // tm, ctx_map, lat_map, lat_mod_row


def _inproj_kernel(xc_ref, xl_ref, mod_ref, n1_ref, win_f_ref, ws_ref, bs_ref, w_ref,
                   xy_ref, osgu_ref, w_b_ref, win_ref, *, tm, nc, lat_mod_row):
    i = pl.program_id(0)
    w_b_ref[...] = w_ref[...].astype(BF16)

    @pl.when(i == 0)
    def _():
        win_ref[...] = win_f_ref[...].astype(BF16)

    def body(x_ref, row):
        sh1 = _mod_row(mod_ref, 0, row)
        sc1 = _mod_row(mod_ref, 1, row)
        hb = (_rms(x_ref[...], n1_ref[...] * (1.0 + sc1)) + sh1).astype(BF16)
        uv = jnp.dot(hb, win_ref[:, 2 * LRU_WIDTH:], preferred_element_type=F32)
        xy_ref[...] = jnp.dot(hb, win_ref[:, 0:2 * LRU_WIDTH], preferred_element_type=F32)
        for g in range(SGU_GROUPS):
            lo = g * LANES
            gu = _gelu(uv[:, lo:lo + LANES])
            gv = _gelu(uv[:, SGU_WIDTH + lo:SGU_WIDTH + lo + LANES]).astype(BF16)
            bias = jnp.broadcast_to(bs_ref[g:g + 1, :], (LANES, CHUNK)).T
            ws = ws_ref[g].astype(BF16)
            for ck in range(tm // CHUNK):
                r0 = ck * CHUNK
                s = jnp.dot(ws, gv[r0:r0 + CHUNK], preferred_element_type=F32) + bias
                osgu_ref[r0:r0 + CHUNK, lo:lo + LANES] = gu[r0:r0 + CHUNK] * s

    @pl.when(i < nc)
    def _():
        body(xc_ref, 0)

    @pl.when(i >= nc)
    def _():
        body(xl_ref, lat_mod_row(i))


def _cast_slice_spec(shape, grid):
    n_steps = math.prod(grid)
    rows = shape[0]
    r = BF16_SUBLANES
    while rows % r or rows // r > n_steps:
        r += BF16_SUBLANES
    last = rows // r - 1

    def index_map(*idx):
        step = 0
        for i, n in zip(idx, grid):
            step = step * n + i
        return (jnp.minimum(step, last), 0)

    return pl.BlockSpec((r, shape[1]), index_map)


def _inproj(xc, xl, mod, norm1, w_in, ws, bs, later_weight, *, tm, lat_seq):
    nc, n, ctx_map, lat_map, lat_mod_row = _token_groups(tm, xc.shape[0], xl.shape[0], lat_seq)
    m = n * tm
    const = lambda i: (0, 0)
    cast_specs = [_cast_slice_spec(later_weight.shape, (n,))]
    return pl.pallas_call(
        functools.partial(_inproj_kernel, tm=tm, nc=nc, lat_mod_row=lat_mod_row),
        grid=(n,),
        in_specs=[
            pl.BlockSpec((tm, D_MODEL), ctx_map),
            pl.BlockSpec((tm, D_MODEL), lat_map),
            pl.BlockSpec((N_MOD, MOD_ROWS, D_MODEL), lambda i: (0, 0, 0)),
            pl.BlockSpec((1, D_MODEL), const),
            pl.BlockSpec((D_MODEL, 4 * LRU_WIDTH), const, pipeline_mode=pl.Buffered(1)),
            pl.BlockSpec((SGU_GROUPS, CHUNK, CHUNK), lambda i: (0, 0, 0)),
            pl.BlockSpec((SGU_GROUPS, CHUNK), const),
        ] + cast_specs,
        out_specs=[
            pl.BlockSpec((tm, 2 * LRU_WIDTH), lambda i: (i, 0)),
            pl.BlockSpec((tm, SGU_WIDTH), lambda i: (i, 0)),
        ] + cast_specs,
        out_shape=[
            jax.ShapeDtypeStruct((m, 2 * LRU_WIDTH), F32),
            jax.ShapeDtypeStruct((m, SGU_WIDTH), F32),
        ] + [jax.ShapeDtypeStruct(later_weight.shape, BF16)],
        scratch_shapes=[pltpu.VMEM(w_in.shape, BF16)],
        compiler_params=pltpu.CompilerParams(
            dimension_semantics=("arbitrary",), vmem_limit_bytes=INPROJ_VMEM),
        name="inproj_sgu",
    )(xc, xl, mod, norm1, w_in, ws, bs, later_weight)


def _scan_pitch(seq):
    pitch = seq // SUBLANES
    return pitch + (4 - pitch % 8) % 8


def _lru_kernel(xr_ref, yr_ref, cw_ref, cb_ref, wa_ref, ba_ref, wx_ref, bx_ref, lam_ref, h0_ref, w_ref,
                o_ref, st_ref, w_b_ref, a_sc, b_sc, p_sc, h_sc, wg_ref, bg_ref, *,
                seq, nsb, nk, pitch, rc, unroll):
    sp_rows = SUBLANES * pitch
    n_rc = (seq * nsb) // rc
    seg = min(seq, rc)
    segs = rc // seg
    chunks_per_seq = seq // seg
    w_b_ref[...] = w_ref[...].astype(BF16)

    @pl.when(pl.program_id(1) == 0)
    def _():
        zeros = jnp.zeros((LRU_HEAD_DIM, LRU_HEAD_DIM), F32)
        for k in range(nk):
            blocks = []
            for d in range(2):
                for src in (wa_ref, wx_ref):
                    top = jnp.concatenate([src[d, 2 * k], zeros], axis=1)
                    bot = jnp.concatenate([zeros, src[d, 2 * k + 1]], axis=1)
                    blocks.append(jnp.concatenate([top, bot], axis=0))
            wg_ref[k] = jnp.concatenate(blocks, axis=1).astype(BF16)
            lanes = slice(k * LANES, (k + 1) * LANES)
            bg_ref[k] = 0.5 * jnp.concatenate(
                [ba_ref[0:1, lanes], bx_ref[0:1, lanes], ba_ref[1:2, lanes], bx_ref[1:2, lanes]], axis=1)

    c4_all = (-0.5 * LRU_C) * jax.nn.softplus(-lam_ref[...])
    cw_all = 0.5 * cw_ref[...]
    cb_all = 0.5 * cb_ref[...]

    for s in range(nsb):
        lo, hi = s * sp_rows + seq, (s + 1) * sp_rows
        for slab in range(2 * nk):
            a_sc[slab, lo:hi, :] = jnp.ones((hi - lo, LANES), F32)
            b_sc[slab, lo:hi, :] = jnp.zeros((hi - lo, LANES), F32)

    def scratch_row(ci, k):
        if chunks_per_seq > 1:
            return pl.multiple_of((ci // chunks_per_seq) * sp_rows + (ci % chunks_per_seq) * seg,
                                  SUBLANES)
        return pl.multiple_of((ci * segs + k) * sp_rows, SUBLANES)

    def conv_segment(xs, prev, nxt, cw, cb):
        ext = jnp.concatenate([prev, xs, nxt], axis=0)
        n_ext = seg + 2 * SUBLANES
        xc = cb + cw[2:3] * xs
        xc = xc + cw[0:1] * pltpu.roll(ext, 2, axis=0)[SUBLANES:SUBLANES + seg]
        xc = xc + cw[1:2] * pltpu.roll(ext, 1, axis=0)[SUBLANES:SUBLANES + seg]
        return xc + cw[3:4] * pltpu.roll(ext, n_ext - 1, axis=0)[SUBLANES:SUBLANES + seg]

    def gates(ci, _):
        base = pl.multiple_of(ci * rc, rc)
        for k in range(nk):
            lanes = slice(k * LANES, (k + 1) * LANES)
            cw, cb, c4 = cw_all[:, lanes], cb_all[:, lanes], c4_all[:, lanes]
            xm = xr_ref[pl.ds(base, rc), lanes]
            if chunks_per_seq > 1:
                cs = ci % chunks_per_seq
                p0 = pl.multiple_of(jnp.maximum(base - SUBLANES, 0), SUBLANES)
                n0 = pl.multiple_of(jnp.minimum(base + rc, seq * nsb - SUBLANES), SUBLANES)
                prev = jnp.where(cs == 0, 0.0, xr_ref[pl.ds(p0, SUBLANES), lanes])
                nxt = jnp.where(cs == chunks_per_seq - 1, 0.0, xr_ref[pl.ds(n0, SUBLANES), lanes])
                hxc = conv_segment(xm, prev, nxt, cw, cb)
            else:
                zeros = jnp.zeros((SUBLANES, LANES), F32)
                hxc = jnp.concatenate(
                    [conv_segment(xm[j * seg:(j + 1) * seg], zeros, zeros, cw, cb) for j in range(segs)],
                    axis=0)
            g = jnp.dot(hxc.astype(BF16), wg_ref[k], preferred_element_type=F32) + bg_ref[k]
            for d in range(2):
                th_r = jnp.tanh(g[:, (2 * d) * LANES:(2 * d + 1) * LANES])
                th_i = jnp.tanh(g[:, (2 * d + 1) * LANES:(2 * d + 2) * LANES])
                log_a = c4[d:d + 1] * th_r + c4[d:d + 1]
                a = jnp.exp(log_a)
                om = jnp.tanh(log_a) * (-1.0 - a * a)
                root = jnp.where(om > 0.0, om * lax.rsqrt(om), 0.0)
                b = root * ((th_i + 1.0) * hxc)
                for j in range(segs):
                    dst = scratch_row(ci, j)
                    a_sc[d * nk + k, pl.ds(dst, seg), :] = a[j * seg:(j + 1) * seg]
                    b_sc[d * nk + k, pl.ds(dst, seg), :] = b[j * seg:(j + 1) * seg]
        return 0

    lax.fori_loop(0, n_rc, gates, 0)

    def scan_sequence(s, _):
        off = s * sp_rows

        def rows(t):
            return pl.ds(off + t, SUBLANES, stride=pitch)

        def local(t, carry):
            out = []
            for slab, (h, p) in enumerate(carry):
                tt = t if slab < nk else pitch - 1 - t
                a = a_sc[slab, rows(tt), :]
                h = a * h + b_sc[slab, rows(tt), :]
                p = a * p
                h_sc[slab, rows(tt), :] = h
                p_sc[slab, rows(tt), :] = p
                out.append((h, p))
            return tuple(out)

        zero = jnp.zeros((SUBLANES, LANES), F32)
        one = jnp.ones((SUBLANES, LANES), F32)
        ends = lax.fori_loop(0, pitch, local, ((zero, one),) * (2 * nk), unroll=unroll)

        h0 = h0_ref[s]
        starts = []
        finals = [[None] * nk, [None] * nk]
        for slab, (h, p) in enumerate(ends):
            d, k = divmod(slab, nk)
            c = h0[d:d + 1, k * LANES:(k + 1) * LANES]
            cs = [None] * SUBLANES
            for j in (range(SUBLANES) if d == 0 else reversed(range(SUBLANES))):
                cs[j] = c
                c = h[j:j + 1] + p[j:j + 1] * c
            finals[d][k] = c
            starts.append(jnp.concatenate(cs, axis=0))
        st_ref[s] = jnp.concatenate(
            [jnp.concatenate(finals[0], axis=1), jnp.concatenate(finals[1], axis=1)], axis=0)

        def fix(t, _):
            for k in range(nk):
                a_sc[k, rows(t), :] = (h_sc[k, rows(t), :] + p_sc[k, rows(t), :] * starts[k]
                                       + h_sc[nk + k, rows(t), :] + p_sc[nk + k, rows(t), :] * starts[nk + k])
            return 0

        lax.fori_loop(0, pitch, fix, 0, unroll=unroll)
        return 0

    if nsb == 1:
        scan_sequence(0, 0)
    else:
        lax.fori_loop(0, nsb, scan_sequence, 0)

    def gate_out(ci, _):
        base = pl.multiple_of(ci * rc, rc)
        for k in range(nk):
            lanes = slice(k * LANES, (k + 1) * LANES)
            gy = _gelu(yr_ref[pl.ds(base, rc), lanes])
            for j in range(segs):
                src = scratch_row(ci, j)
                o_ref[pl.ds(base + j * seg, seg), lanes] = (gy[j * seg:(j + 1) * seg]
                                                            * a_sc[k, pl.ds(src, seg), :])
        return 0

    lax.fori_loop(0, n_rc, gate_out, 0)


def _lru(xy, conv_w, conv_b, wa, ba, wx, bx, lam, h0, later_weight, *, row_start, m, seq, nsb):
    nblk = m // (seq * nsb)
    assert row_start % (seq * nsb) == 0
    blk0 = row_start // (seq * nsb)
    nk = LRU_LANE_BLOCKS
    width = nk * LANES
    ncb = LRU_WIDTH // width
    pitch = _scan_pitch(seq)
    sc_rows = nsb * SUBLANES * pitch
    rows = seq * nsb
    cast_spec = _cast_slice_spec(later_weight.shape, (ncb, nblk))
    return pl.pallas_call(
        functools.partial(_lru_kernel, seq=seq, nsb=nsb, nk=nk, pitch=pitch,
                          rc=LRU_ROW_CHUNK, unroll=LRU_SCAN_UNROLL),
        grid=(ncb, nblk),
        in_specs=[
            pl.BlockSpec((rows, width), lambda j, b: (blk0 + b, j)),
            pl.BlockSpec((rows, width), lambda j, b: (blk0 + b, ncb + j)),
            pl.BlockSpec((4, width), lambda j, b: (0, j)),
            pl.BlockSpec((1, width), lambda j, b: (0, j)),
            pl.BlockSpec((2, 2 * nk, LRU_HEAD_DIM, LRU_HEAD_DIM), lambda j, b: (0, j, 0, 0)),
            pl.BlockSpec((2, width), lambda j, b: (0, j)),
            pl.BlockSpec((2, 2 * nk, LRU_HEAD_DIM, LRU_HEAD_DIM), lambda j, b: (0, j, 0, 0)),
            pl.BlockSpec((2, width), lambda j, b: (0, j)),
            pl.BlockSpec((2, width), lambda j, b: (0, j)),
            pl.BlockSpec((nsb, 2, width), lambda j, b: (b, 0, j)),
            cast_spec,
        ],
        out_specs=[
            pl.BlockSpec((rows, width), lambda j, b: (b, j)),
            pl.BlockSpec((nsb, 2, width), lambda j, b: (b, 0, j)),
            cast_spec,
        ],
        out_shape=[
            jax.ShapeDtypeStruct((m, LRU_WIDTH), F32),
            jax.ShapeDtypeStruct((m // seq, 2, LRU_WIDTH), F32),
            jax.ShapeDtypeStruct(later_weight.shape, BF16),
        ],
        scratch_shapes=[pltpu.VMEM((2 * nk, sc_rows, LANES), F32)] * 4 + [
            pltpu.VMEM((nk, LANES, 4 * LANES), BF16), pltpu.VMEM((nk, 1, 4 * LANES), F32)],
        compiler_params=pltpu.CompilerParams(
            dimension_semantics=("arbitrary", "arbitrary"), vmem_limit_bytes=LRU_VMEM),
        name="rglru",
    )(xy, xy, conv_w, conv_b, wa, ba, wx, bx, lam, h0, later_weight)


GROUP = SUBLANES * SUBLANES


def _swap_rows(slab_ref, val, r0):
    n = val.shape[0]
    for k in range(D_MODEL // LANES):
        slab_ref[k, r0:r0 + n, :] = val[:, k * LANES:(k + 1) * LANES]
    cols = []
    for k in range(D_MODEL // LANES):
        rows = [slab_ref[k, pl.ds(r0 + g * GROUP + t, SUBLANES, stride=SUBLANES), :]
                for g in range(n // GROUP) for t in range(SUBLANES)]
        cols.append(jnp.concatenate(rows, axis=0))
    return jnp.concatenate(cols, axis=1)


def _ffn_tile(x_ref, ol_ref, os_ref, mod_ref, gl_ref, gs_ref, wout_ref, n2_ref, up_ref,
              cw_ref, cb_ref, down_ref, fn_ref, y_ref, slab_sc, *, tm, fc, es, period, row):
    g1 = _mod_row(mod_ref, 2, row)
    sh2 = _mod_row(mod_ref, 3, row)
    sc2 = _mod_row(mod_ref, 4, row)
    g2 = _mod_row(mod_ref, 5, row)
    nl = _rms(ol_ref[...], gl_ref[...]).astype(BF16)
    ns = _rms(os_ref[...], gs_ref[...]).astype(BF16)
    o = (jnp.dot(nl, wout_ref[0:LRU_WIDTH, :], preferred_element_type=F32)
         + jnp.dot(ns, wout_ref[LRU_WIDTH:LRU_WIDTH + SGU_WIDTH, :], preferred_element_type=F32))
    x1 = x_ref[...] + g1 * o
    h2f = _rms(x1, n2_ref[...] * (1.0 + sc2)) + sh2
    h2 = _swap_rows(slab_sc, h2f, 0).astype(BF16)

    sub = lax.broadcasted_iota(jnp.int32, (SUBLANES, fc), 0)
    groups_per_period = period // GROUP
    n_groups = tm // GROUP

    def conv(z, c0, scale=None):
        w = cw_ref[:, c0:c0 + fc]
        b = cb_ref[:, c0:c0 + fc]
        if scale is not None:
            w, b = scale * w, scale * b
        w0, w1, w2 = w[0:1], w[1:2], w[2:3]
        z8 = [[z[g * GROUP + t * SUBLANES:g * GROUP + (t + 1) * SUBLANES] for t in range(SUBLANES)]
              for g in range(n_groups)]
        outs = []
        for g in range(n_groups):
            cur = z8[g]
            lo = pltpu.roll(cur[SUBLANES - 1], 1, axis=0)
            if g % groups_per_period == 0:
                lo = jnp.where(sub == 0, 0.0, lo)
            else:
                lo = jnp.where(sub == 0, pltpu.roll(z8[g - 1][SUBLANES - 1], 1, axis=0), lo)
            hi = pltpu.roll(cur[0], SUBLANES - 1, axis=0)
            if g % groups_per_period == groups_per_period - 1:
                hi = jnp.where(sub == SUBLANES - 1, 0.0, hi)
            else:
                hi = jnp.where(sub == SUBLANES - 1, pltpu.roll(z8[g + 1][0], SUBLANES - 1, axis=0), hi)
            zm = [lo] + cur[:SUBLANES - 1]
            zp = cur[1:] + [hi]
            for t in range(SUBLANES):
                outs.append(b + w0 * zm[t] + w1 * cur[t] + w2 * zp[t])
        return jnp.concatenate(outs, axis=0)

    def up(f0):
        zg = jnp.dot(h2, up_ref[:, f0:f0 + fc], preferred_element_type=F32)
        zv = jnp.dot(h2, up_ref[:, D_FF + f0:D_FF + f0 + fc], preferred_element_type=F32)
        return zg, zv

    acc = None
    nxt = up(0)
    for f0 in range(0, D_FF, fc):
        zg, zv = nxt
        if f0 + fc < D_FF:
            nxt = up(f0 + fc)
        hg = conv(zg, f0, scale=0.5)
        act = (hg * (jnp.tanh(hg) + 1.0) * conv(zv, D_FF + f0)).astype(BF16)
        if f0 + fc < D_FF:
            d = jnp.dot(act, down_ref[f0:f0 + fc, :], preferred_element_type=F32)
            acc = d if acc is None else acc + d
        else:
            for r0 in range(0, tm, es):
                a = jnp.dot(act[r0:r0 + es], down_ref[f0:f0 + fc, :], preferred_element_type=F32)
                if acc is not None:
                    a = acc[r0:r0 + es] + a
                x2 = x1[r0:r0 + es] + g2 * _swap_rows(slab_sc, a, r0)
                y_ref[r0:r0 + es, :] = _rms(x2, fn_ref[...])


def _ffn_kernel(xc_ref, xl_ref, olc_ref, oll_ref, os_ref, mod_ref, gl_ref, gs_ref, wout_ref, n2_ref,
                up_ref, cw_ref, cb_ref, down_ref, fn_ref, yc_ref, yl_ref, slab_sc, *,
                nc, lat_mod_row, ctx_period, lat_period, **tile):
    i = pl.program_id(0)
    shared = (mod_ref, gl_ref, gs_ref, wout_ref, n2_ref, up_ref, cw_ref, cb_ref, down_ref, fn_ref)

    @pl.when(i < nc)
    def _():
        _ffn_tile(xc_ref, olc_ref, os_ref, *shared, yc_ref, slab_sc, period=ctx_period, row=0, **tile)

    @pl.when(i >= nc)
    def _():
        _ffn_tile(xl_ref, oll_ref, os_ref, *shared, yl_ref, slab_sc, period=lat_period,
                  row=lat_mod_row(i), **tile)


def _ffn(xc, xl, olru_c, olru_l, osgu, mod, g_lru, g_sgu, w_out, norm2, ffn_up, conv_w, conv_b,
         ffn_down, final_norm, *, tm, ctx_period, lat_period, lat_seq):
    nc, n, ctx_map, lat_map, lat_mod_row = _token_groups(tm, xc.shape[0], xl.shape[0], lat_seq)
    const = lambda i: (0, 0)
    resident = lambda shape: pl.BlockSpec(shape, const, pipeline_mode=pl.Buffered(1))
    return pl.pallas_call(
        functools.partial(_ffn_kernel, nc=nc, lat_mod_row=lat_mod_row, ctx_period=ctx_period,
                          lat_period=lat_period, tm=tm, fc=FFN_FC, es=FFN_ES),
        grid=(n,),
        in_specs=[
            pl.BlockSpec((tm, D_MODEL), ctx_map),
            pl.BlockSpec((tm, D_MODEL), lat_map),
            pl.BlockSpec((tm, LRU_WIDTH), ctx_map),
            pl.BlockSpec((tm, LRU_WIDTH), lat_map),
            pl.BlockSpec((tm, SGU_WIDTH), lambda i: (i, 0)),
            pl.BlockSpec((N_MOD, MOD_ROWS, D_MODEL), lambda i: (0, 0, 0)),
            pl.BlockSpec((1, LRU_WIDTH), const),
            pl.BlockSpec((1, SGU_WIDTH), const),
            resident((LRU_WIDTH + SGU_WIDTH, D_MODEL)),
            pl.BlockSpec((1, D_MODEL), const),
            resident((D_MODEL, 2 * D_FF)),
            pl.BlockSpec((3, 2 * D_FF), const),
            pl.BlockSpec((1, 2 * D_FF), const),
            resident((D_FF, D_MODEL)),
            pl.BlockSpec((1, D_MODEL), const),
        ],
        out_specs=[pl.BlockSpec((tm, D_MODEL), ctx_map), pl.BlockSpec((tm, D_MODEL), lat_map)],
        out_shape=[jax.ShapeDtypeStruct(xc.shape, F32), jax.ShapeDtypeStruct(xl.shape, F32)],
        scratch_shapes=[pltpu.VMEM((D_MODEL // LANES, tm, LANES), F32)],
        compiler_params=pltpu.CompilerParams(
            dimension_semantics=("arbitrary",), vmem_limit_bytes=FFN_VMEM),
        name="outproj_ffn",
    )(xc, xl, olru_c, olru_l, osgu, mod, g_lru, g_sgu, w_out, norm2, ffn_up, conv_w, conv_b, ffn_down,
      final_norm)


def kernel(x_prompt, x_sample, state_lru, c, c_ctx, norm1, norm2, w_ada, b_ada, w_in, lru_conv_w,
           lru_conv_b, lru_wa, lru_ba, lru_wx, lru_bx, lru_lam, sgu_ws, sgu_bs, g_lru, g_sgu, w_out,
           ffn_up, ffn_conv_w, ffn_conv_b, ffn_down, final_norm):
    batch, seq, _ = x_prompt.shape
    dec_batch, dec_seq, _ = x_sample.shape
    depth = norm1.shape[0]
    assert depth == 1, "the final norm is fused into the (single) layer's last kernel"
    l = 0

    xc = x_prompt.reshape(batch * seq, D_MODEL)
    xl = x_sample.reshape(dec_batch * dec_seq, D_MODEL)
    n_ctx, n_lat = xc.shape[0], xl.shape[0]
    zeros_state = jnp.zeros((batch, 2, LRU_WIDTH), x_prompt.dtype)

    mod = _modulation(c_ctx[None, :], c, w_ada[l], b_ada[l][None, :])
    lru_params = (lru_conv_w[l], lru_conv_b[l][None, :], lru_wa[l], lru_ba[l], lru_wx[l], lru_bx[l],
                  lru_lam[l])
    h0_lat = state_lru.reshape(dec_batch, 2, LRU_WIDTH)

    xy, osgu, w_out_b = _inproj(
        xc, xl, mod, norm1[l][None, :], w_in[l], sgu_ws[l], sgu_bs[l],
        w_out[l], tm=INPROJ_TM, lat_seq=dec_seq)
    olru_c, new_state, down_b = _lru(xy, *lru_params, zeros_state, ffn_down[l], row_start=0, m=n_ctx,
                                     seq=seq, nsb=LRU_CTX_SEQS)
    olru_l, _, up_b = _lru(xy, *lru_params, h0_lat, ffn_up[l], row_start=n_ctx, m=n_lat,
                           seq=dec_seq, nsb=1)
    yc, yl = _ffn(xc, xl, olru_c, olru_l, osgu, mod, g_lru[l][None, :], g_sgu[l][None, :],
                  w_out_b, norm2[l][None, :], up_b, ffn_conv_w[l],
                  ffn_conv_b[l][None, :], down_b, final_norm[None, :], tm=FFN_TM,
                  ctx_period=seq, lat_period=GRID_W, lat_seq=dec_seq)
    return (yc.reshape(batch, seq, D_MODEL), yl.reshape(dec_batch, dec_seq, D_MODEL),
            new_state[:, None])
```
